```python
import math
import jax, jax.numpy as jnp
from jax import lax
import numpy as np

D_MODEL = 1024
BATCH = 16
SEQ = 256
DEPTH = 2
DEC_BATCH = 4
DEC_SEQ = 2048
PAST_LEN = 256

GRID_W = 64
N_MIXERS = 2
N_ATTN_LAYERS = (DEPTH + 1) // 2
N_HGRN_LAYERS = DEPTH // 2
N_MOD = 9
EPS = 1e-6
DA_HEADS = D_MODEL // 128
DA_DK = 64
DA_DV = 128
ROPE_THETA = 10000.0
Q_BLOCK = 128
HG_HEADS = D_MODEL // 128
HG_DK = 128
HG_DV = D_MODEL // HG_HEADS
CHUNK = 32
FFN_HIDDEN = ((8 * D_MODEL // 3 + 127) // 128) * 128

kernel_name = "diff_hgrn2_macaron_prefix_dit"


def rmsnorm(x, g):
    xf = x.astype(jnp.float32)
    y = xf * lax.rsqrt(jnp.mean(xf * xf, axis=-1, keepdims=True) + EPS)
    return (y * g.astype(jnp.float32)).astype(x.dtype)


def ada_pre(x, mod, j, g):
    shift = mod[:, 3 * j][:, None, :]
    scale = mod[:, 3 * j + 1][:, None, :]
    return rmsnorm(x, g) * (1 + scale) + shift


def ada_post(x, y, mod, j, g, res_w):
    gate = mod[:, 3 * j + 2][:, None, :]
    return x + res_w * gate * rmsnorm(y, g)


def swiglu(h, w_in, w_out):
    a, b = jnp.split(h @ w_in, 2, axis=-1)
    return (jax.nn.silu(a) * b) @ w_out


def ffn_sublayer(x, mod, j, g_pre, g_post, w_in, w_out):
    return ada_post(x, swiglu(ada_pre(x, mod, j, g_pre), w_in, w_out), mod, j, g_post, 0.5)


def axial_rope(x):
    L = x.shape[1]
    rows = L // GRID_W
    row = jnp.repeat(jnp.arange(rows, dtype=jnp.float32), GRID_W)
    col = jnp.tile(jnp.arange(GRID_W, dtype=jnp.float32), rows)
    nf = DA_DK // 4
    inv = ROPE_THETA ** (-jnp.arange(nf, dtype=jnp.float32) / nf)
    ang = jnp.stack([row[:, None] * inv, col[:, None] * inv], axis=1)
    cos = jnp.cos(ang)[None, :, None, None]
    sin = jnp.sin(ang)[None, :, None, None]
    xr = x.astype(jnp.float32).reshape(*x.shape[:-1], 2, 2, nf)
    x1 = xr[..., 0, :]
    x2 = xr[..., 1, :]
    out = jnp.stack([x1 * cos - x2 * sin, x1 * sin + x2 * cos], axis=-2)
    return out.reshape(x.shape).astype(x.dtype)


def diff_qkv(h, w_in):
    B, L, _ = h.shape
    z = h @ w_in
    qd = DA_HEADS * 2 * DA_DK
    q = z[..., :qd].reshape(B, L, DA_HEADS, 2, DA_DK)
    k = z[..., qd:2 * qd].reshape(B, L, DA_HEADS, 2, DA_DK)
    v = z[..., 2 * qd:].reshape(B, L, DA_HEADS, DA_DV)
    return q, k, v


def diff_lambda(lam_p, lam_init):
    lp = lam_p.astype(jnp.float32)
    return jnp.exp(jnp.sum(lp[0] * lp[1])) - jnp.exp(jnp.sum(lp[2] * lp[3])) + lam_init


def diff_attention(q, k, v, lam):
    B, Lq = q.shape[:2]
    nb = Lq // Q_BLOCK
    qb = q.reshape(B, nb, Q_BLOCK, DA_HEADS, 2, DA_DK).transpose(1, 0, 2, 3, 4, 5)
    kf = k.astype(jnp.float32)
    vf = v.astype(jnp.float32)
    scale = DA_DK ** -0.5

    def block(qi):
        s = jnp.einsum('bqhcd,bkhcd->bhcqk', qi.astype(jnp.float32), kf) * scale
        p = jax.nn.softmax(s, axis=-1)
        a = p[:, :, 0] - lam * p[:, :, 1]
        return jnp.einsum('bhqk,bkhv->bqhv', a, vf)

    o = lax.map(block, qb)
    return o.transpose(1, 0, 2, 3, 4).reshape(B, Lq, DA_HEADS, DA_DV)


def diff_out(o, lam_init, gain, w_out, dtype):
    B, L = o.shape[:2]
    o = rmsnorm(o, gain) * (1 - lam_init)
    return o.reshape(B, L, DA_HEADS * DA_DV).astype(dtype) @ w_out


def hgrn_proj(h, w_in, lb):
    B, L, _ = h.shape
    z = (h @ w_in).astype(jnp.float32)
    q, ff, fb, i, g = [a.reshape(B, L, HG_HEADS, -1) for a in jnp.split(z, 5, axis=-1)]
    f_f = lb[0] + (1 - lb[0]) * jax.nn.sigmoid(ff)
    f_b = lb[1] + (1 - lb[1]) * jax.nn.sigmoid(fb)
    return q, i, g, (jnp.log(f_f), 1 - f_f), (jnp.log(f_b), 1 - f_b)


def gla_chunked(q, k, v, logf, S0):
    B, L, H, DK = q.shape
    DV = v.shape[-1]
    nc = L // CHUNK

    def chunks(a):
        return a.reshape(B, nc, CHUNK, H, a.shape[-1]).transpose(1, 0, 3, 2, 4)

    mask = jnp.tril(jnp.ones((CHUNK, CHUNK), dtype=bool))[:, :, None]

    def step(S, inp):
        qc, kc, vc, gc = inp
        b = jnp.cumsum(gc, axis=2)
        o_inter = jnp.einsum('bhtk,bhkv->bhtv', qc * jnp.exp(b), S)
        diff = b[:, :, :, None, :] - b[:, :, None, :, :]
        dec = jnp.exp(jnp.where(mask, diff, -jnp.inf))
        A = jnp.einsum('bhtk,bhtsk,bhsk->bhts', qc, dec, kc)
        o_intra = jnp.einsum('bhts,bhsv->bhtv', A, vc)
        b_last = b[:, :, -1]
        S_new = jnp.exp(b_last)[..., None] * S + jnp.einsum(
            'bhsk,bhsv->bhkv', kc * jnp.exp(b_last[:, :, None] - b), vc)
        return S_new, o_inter + o_intra

    S_fin, o = lax.scan(step, S0, (chunks(q), chunks(k), chunks(v), chunks(logf)))
    return o.transpose(1, 0, 3, 2, 4).reshape(B, L, H, DV), S_fin


def hgrn_bidir(q, i, dir_f, dir_b, S0f, S0b):
    o_f, S_f = gla_chunked(q, dir_f[1], i, dir_f[0], S0f)
    fl = lambda a: a[:, ::-1]
    o_b, S_b = gla_chunked(fl(q), fl(dir_b[1]), fl(i), fl(dir_b[0]), S0b)
    return o_f + fl(o_b), S_f, S_b


def hgrn_out(o, g, gain, w_out, dtype):
    B, L = o.shape[:2]
    o = rmsnorm(o, gain) * jax.nn.sigmoid(g)
    return o.reshape(B, L, HG_HEADS * HG_DV).astype(dtype) @ w_out


def setup_inputs(seed: int = 0) -> dict:
    key = jax.random.key(seed)
    ks = jax.random.split(key, 24)
    f32 = jnp.float32

    def nrm(k, shape, scale):
        return jax.random.normal(k, shape, f32) * scale

    D = D_MODEL
    return {
        "x_prompt": nrm(ks[0], (BATCH, SEQ, D), 1.0),
        "x_sample": nrm(ks[1], (DEC_BATCH, DEC_SEQ, D), 1.0),
        "cache_k": nrm(ks[2], (DEC_BATCH, N_ATTN_LAYERS, PAST_LEN, DA_HEADS, 2, DA_DK), 1.0),
        "cache_v": nrm(ks[3], (DEC_BATCH, N_ATTN_LAYERS, PAST_LEN, DA_HEADS, DA_DV), 1.0),
        "state_hgrn": nrm(ks[4], (DEC_BATCH, N_HGRN_LAYERS, 2, HG_HEADS, HG_DK, HG_DV), 0.3),
        "c": nrm(ks[5], (DEC_BATCH, D), 1.0),
        "c_ctx": nrm(ks[6], (D,), 1.0),
        "w_mod": nrm(ks[7], (DEPTH, D, N_MOD * D), 0.5 * D ** -0.5),
        "b_mod": nrm(ks[8], (DEPTH, N_MOD * D), 0.02),
        "norm_g": 1.0 + nrm(ks[9], (DEPTH, 6, D), 0.02),
        "ffn_w_in": nrm(ks[10], (DEPTH, 2, D, 2 * FFN_HIDDEN), D ** -0.5),
        "ffn_w_out": nrm(ks[11], (DEPTH, 2, FFN_HIDDEN, D), FFN_HIDDEN ** -0.5),
        "attn_w_in": nrm(ks[12], (N_ATTN_LAYERS, D, 2 * DA_HEADS * 2 * DA_DK + DA_HEADS * DA_DV), D ** -0.5),
        "attn_w_out": nrm(ks[13], (N_ATTN_LAYERS, DA_HEADS * DA_DV, D), (DA_HEADS * DA_DV) ** -0.5),
        "attn_lambda": nrm(ks[14], (N_ATTN_LAYERS, 4, DA_DK), 0.1),
        "attn_subln": 1.0 + nrm(ks[15], (N_ATTN_LAYERS, DA_DV), 0.02),
        "hgrn_w_in": nrm(ks[16], (N_HGRN_LAYERS, D, 5 * HG_HEADS * HG_DK), D ** -0.5),
        "hgrn_w_out": nrm(ks[17], (N_HGRN_LAYERS, HG_HEADS * HG_DV, D), (HG_HEADS * HG_DV) ** -0.5),
        "hgrn_lower_bounds": nrm(ks[18], (DEPTH, 2, HG_HEADS * HG_DK), 0.5),
        "hgrn_gnorm": 1.0 + nrm(ks[19], (N_HGRN_LAYERS, HG_DV), 0.02),
    }


def reference(x_prompt, x_sample, cache_k, cache_v, state_hgrn, c, c_ctx, w_mod, b_mod, norm_g,
              ffn_w_in, ffn_w_out, attn_w_in, attn_w_out, attn_lambda, attn_subln,
              hgrn_w_in, hgrn_w_out, hgrn_lower_bounds, hgrn_gnorm):
    lb_soft = jax.nn.softmax(hgrn_lower_bounds.astype(jnp.float32), axis=0)
    lb_all = jnp.cumsum(lb_soft, axis=0) - lb_soft[0]
    mod_ctx_all = jnp.einsum('d,lde->le', jax.nn.silu(c_ctx), w_mod) + b_mod
    mod_lat_all = jnp.einsum('bd,lde->lbe', jax.nn.silu(c), w_mod) + b_mod[:, None]

    x = x_prompt
    Bp = x.shape[0]
    ks_list, vs_list, st_list = [], [], []
    for l in range(DEPTH):
        mod = mod_ctx_all[l].reshape(1, N_MOD, D_MODEL)
        x = ffn_sublayer(x, mod, 0, norm_g[l, 0], norm_g[l, 1], ffn_w_in[l, 0], ffn_w_out[l, 0])
        h = ada_pre(x, mod, 1, norm_g[l, 2])
        if l % N_MIXERS == 0:
            a = l // N_MIXERS
            lam_init = 0.8 - 0.6 * math.exp(-0.3 * l)
            lam = diff_lambda(attn_lambda[a], lam_init)
            q, k, v = diff_qkv(h, attn_w_in[a])
            o = diff_attention(q, k, v, lam)
            y = diff_out(o, lam_init, attn_subln[a], attn_w_out[a], h.dtype)
            ks_list.append(k)
            vs_list.append(v)
        else:
            r = l // N_MIXERS
            lb = lb_all[l].reshape(2, HG_HEADS, HG_DK)
            q, i, g, df, db = hgrn_proj(h, hgrn_w_in[r], lb)
            zero = jnp.zeros((Bp, HG_HEADS, HG_DK, HG_DV), jnp.float32)
            o, S_f, S_b = hgrn_bidir(q, i, df, db, zero, zero)
            y = hgrn_out(o, g, hgrn_gnorm[r], hgrn_w_out[r], h.dtype)
            st_list.append(jnp.stack([S_f, S_b], axis=1).astype(x.dtype))
        x = ada_post(x, y, mod, 1, norm_g[l, 3], 1.0)
        x = ffn_sublayer(x, mod, 2, norm_g[l, 4], norm_g[l, 5], ffn_w_in[l, 1], ffn_w_out[l, 1])
    y_prompt = x
    new_cache_k = jnp.stack(ks_list, axis=1)
    new_cache_v = jnp.stack(vs_list, axis=1)
    new_state_hgrn = jnp.stack(st_list, axis=1)

    x = x_sample
    Bs = x.shape[0]
    for l in range(DEPTH):
        mod = mod_lat_all[l].reshape(Bs, N_MOD, D_MODEL)
        x = ffn_sublayer(x, mod, 0, norm_g[l, 0], norm_g[l, 1], ffn_w_in[l, 0], ffn_w_out[l, 0])
        h = ada_pre(x, mod, 1, norm_g[l, 2])
        if l % N_MIXERS == 0:
            a = l // N_MIXERS
            lam_init = 0.8 - 0.6 * math.exp(-0.3 * l)
            lam = diff_lambda(attn_lambda[a], lam_init)
            q, k, v = diff_qkv(h, attn_w_in[a])
            q = axial_rope(q)
            k = axial_rope(k)
            k_all = jnp.concatenate([k, cache_k[:, a].astype(k.dtype)], axis=1)
            v_all = jnp.concatenate([v, cache_v[:, a].astype(v.dtype)], axis=1)
            o = diff_attention(q, k_all, v_all, lam)
            y = diff_out(o, lam_init, attn_subln[a], attn_w_out[a], h.dtype)
        else:
            r = l // N_MIXERS
            lb = lb_all[l].reshape(2, HG_HEADS, HG_DK)
            q, i, g, df, db = hgrn_proj(h, hgrn_w_in[r], lb)
            S0 = state_hgrn[:, r].astype(jnp.float32)
            o, _, _ = hgrn_bidir(q, i, df, db, S0[:, 0], S0[:, 1])
            y = hgrn_out(o, g, hgrn_gnorm[r], hgrn_w_out[r], h.dtype)
        x = ada_post(x, y, mod, 1, norm_g[l, 3], 1.0)
        x = ffn_sublayer(x, mod, 2, norm_g[l, 4], norm_g[l, 5], ffn_w_in[l, 1], ffn_w_out[l, 1])
    y_sample = x

    return (y_prompt, y_sample, new_cache_k, new_cache_v, new_state_hgrn)
```

```python
import functools
import math

import numpy as np
import jax
import jax.numpy as jnp
from jax import lax
from jax.experimental import pallas as pl
from jax.experimental.pallas import tpu as pltpu

D = 1024
BATCH = 16
SEQ = 256
DEPTH = 2
DEC_BATCH = 4
DEC_SEQ = 2048
PAST = 256
GRID_W = 64
N_MOD = 9
EPS = 1e-6
HEADS = 8
HD = 128
DA_DK = 64
ROPE_THETA = 10000.0
FFN_H = 2816

N_CTX = BATCH * SEQ
N_SMP = DEC_BATCH * DEC_SEQ
N_TOK = N_CTX + N_SMP
N_GRP = 8

VMEM_LIMIT = 56 * 1024 * 1024
LANES = 128
SUB = 8

TM = 512
FFN_HC = 256
TQ = 256
CH = 128
N_SLAB = CH // SUB
N_LEVEL = 7


def _cparams(sem):
    return pltpu.CompilerParams(dimension_semantics=sem, vmem_limit_bytes=VMEM_LIMIT)


def _grp_of_block(i, rows):
    nctx = N_CTX // rows
    per = DEC_SEQ // rows
    return jnp.where(i < nctx, 0, 1 + (i - nctx) // per)


def _rms(x, g):
    return (x * lax.rsqrt(jnp.mean(x * x, axis=-1, keepdims=True) + EPS)) * g


def _dot(a, b):
    return jnp.dot(a, b, preferred_element_type=jnp.float32)


def _dot_nt(a, b):
    return lax.dot_general(a, b, (((1,), (1,)), ((), ())), preferred_element_type=jnp.float32)


def _dot_tn(a, b):
    return lax.dot_general(a, b, (((0,), (0,)), ((), ())), preferred_element_type=jnp.float32)


def _const_spec(shape):
    nd = len(shape)
    return pl.BlockSpec(shape, lambda *_: (0,) * nd, pipeline_mode=pl.Buffered(1))


def _mod_kernel(c_ref, w_ref, b_ref, o_ref):
    c = c_ref[...]
    s = (c * jax.nn.sigmoid(c)).astype(jnp.bfloat16)
    o_ref[...] = _dot(s, w_ref[...].astype(jnp.bfloat16)) + b_ref[...]


def _modulation(cc, w_mod, b_mod):
    tn = 1024
    nt = (N_MOD * D) // tn
    return pl.pallas_call(
        _mod_kernel,
        grid=(DEPTH, nt),
        in_specs=[
            pl.BlockSpec((N_GRP, D), lambda l, j: (0, 0)),
            pl.BlockSpec((None, D, tn), lambda l, j: (l, 0, j)),
            pl.BlockSpec((None, 1, tn), lambda l, j: (l, 0, j)),
        ],
        out_specs=pl.BlockSpec((None, N_GRP, tn), lambda l, j: (l, 0, j)),
        out_shape=jax.ShapeDtypeStruct((DEPTH, N_GRP, N_MOD * D), jnp.float32),
        compiler_params=_cparams(("parallel", "parallel")),
        name="modulation",
    )(cc, w_mod, b_mod.reshape(DEPTH, 1, N_MOD * D))


def _ada_pre(x, mod_ref, g_ref, j, gi):
    shift = mod_ref[3 * j:3 * j + 1, :]
    scale = mod_ref[3 * j + 1:3 * j + 2, :]
    return _rms(x, g_ref[gi:gi + 1, :]) * (1.0 + scale) + shift


def _ada_post(x, y, mod_ref, g_ref, j, gi, res_w):
    gate = mod_ref[3 * j + 2:3 * j + 3, :]
    return x + res_w * gate * _rms(y, g_ref[gi:gi + 1, :])


def _tok_specs():
    return [
        pl.BlockSpec((TM, D), lambda i: (i, 0)),
        pl.BlockSpec((None, N_MOD, D), lambda i: (_grp_of_block(i, TM), 0, 0)),
        _const_spec((6, D)),
    ]


def _ffn_kernel(x_ref, mod_ref, g_ref, win_ref, wout_ref, o_ref, acc_ref, *, j):
    x = x_ref[...]
    h = _ada_pre(x, mod_ref, g_ref, j, 2 * j).astype(jnp.bfloat16)
    for c in range(FFN_H // FFN_HC):
        lo = c * FFN_HC
        a = _dot(h, win_ref[:, lo:lo + FFN_HC])
        b = _dot(h, win_ref[:, FFN_H + lo:FFN_H + lo + FFN_HC])
        u = ((a * jax.nn.sigmoid(a)) * b).astype(jnp.bfloat16)
        part = _dot(u, wout_ref[lo:lo + FFN_HC, :])
        if c == 0:
            acc_ref[...] = part
        else:
            acc_ref[...] += part
    o_ref[...] = _ada_post(x, acc_ref[...], mod_ref, g_ref, j, 2 * j + 1, 0.5)


def _ffn(x, mod_l, g_l, w_in, w_out, j):
    return pl.pallas_call(
        functools.partial(_ffn_kernel, j=j),
        grid=(N_TOK // TM,),
        in_specs=_tok_specs() + [_const_spec((D, 2 * FFN_H)), _const_spec((FFN_H, D))],
        out_specs=pl.BlockSpec((TM, D), lambda i: (i, 0)),
        out_shape=jax.ShapeDtypeStruct((N_TOK, D), jnp.float32),
        scratch_shapes=[pltpu.VMEM((TM, D), jnp.float32)],
        compiler_params=_cparams(("parallel",)),
        name=f"ffn{j}",
    )(x, mod_l, g_l, w_in, w_out)


def _rope(x, cos, sin_signed):
    lane = lax.broadcasted_iota(jnp.int32, x.shape, 1)
    partner = jnp.where((lane % 32) < 16, pltpu.roll(x, LANES - 16, 1), pltpu.roll(x, 16, 1))
    return x * cos + partner * sin_signed


def _attn_pre_kernel(x_ref, mod_ref, g_ref, w_ref, cos_ref, sin_ref, q_ref, k_ref, v_ref, *, rope):
    x = x_ref[...]
    h = _ada_pre(x, mod_ref, g_ref, 1, 2).astype(jnp.bfloat16)
    z = _dot(h, w_ref[...])
    scale = DA_DK ** -0.5
    for hd in range(HEADS):
        sl = slice(hd * HD, (hd + 1) * HD)
        q = z[:, hd * HD:(hd + 1) * HD]
        k = z[:, D + hd * HD:D + (hd + 1) * HD]
        if rope:
            q = _rope(q, cos_ref[...], sin_ref[...])
            k = _rope(k, cos_ref[...], sin_ref[...])
        q_ref[:, sl] = (q * scale).astype(q_ref.dtype)
        k_ref[:, sl] = k.astype(k_ref.dtype)
    v_ref[...] = z[:, 2 * D:].astype(v_ref.dtype)


def _attn_pre(x, mod_l, g_l, w_in, cos_t, sin_t, *, ctx):
    rows = N_CTX if ctx else N_SMP
    base = 0 if ctx else N_CTX // TM
    per = DEC_SEQ // TM
    kv_dtype = jnp.float32 if ctx else jnp.bfloat16
    tok = _tok_specs()
    tok[0] = pl.BlockSpec((TM, D), lambda i: (i + base, 0))
    tok[1] = pl.BlockSpec((None, N_MOD, D), lambda i: (_grp_of_block(i + base, TM), 0, 0))
    out_spec = pl.BlockSpec((TM, D), lambda i: (i, 0))
    return pl.pallas_call(
        functools.partial(_attn_pre_kernel, rope=not ctx),
        grid=(rows // TM,),
        in_specs=tok + [
            _const_spec((D, 3 * D)),
            pl.BlockSpec((TM, HD), lambda i: (i % per, 0)),
            pl.BlockSpec((TM, HD), lambda i: (i % per, 0)),
        ],
        out_specs=[out_spec, out_spec, out_spec],
        out_shape=[
            jax.ShapeDtypeStruct((rows, D), jnp.bfloat16),
            jax.ShapeDtypeStruct((rows, D), kv_dtype),
            jax.ShapeDtypeStruct((rows, D), kv_dtype),
        ],
        compiler_params=_cparams(("parallel",)),
        name="attn_pre_ctx" if ctx else "attn_pre_smp",
    )(x, mod_l, g_l, w_in, cos_t, sin_t)


def _attn_kernel(*refs, lam_init, cache):
    if cache:
        (x_ref, mod_ref, g_ref, q_ref, k_ref, v_ref, kc_ref, vc_ref,
         lam_ref, sub_ref, w_ref, o_ref, oh_ref) = refs
    else:
        (x_ref, mod_ref, g_ref, q_ref, k_ref, v_ref,
         lam_ref, sub_ref, w_ref, o_ref, oh_ref) = refs
    lp = lam_ref[...]
    lam = (jnp.exp(jnp.sum(lp[0:1] * lp[1:2], axis=-1, keepdims=True))
           - jnp.exp(jnp.sum(lp[2:3] * lp[3:4], axis=-1, keepdims=True)) + lam_init)
    lane = lax.broadcasted_iota(jnp.int32, (q_ref.shape[0], HD), 1)
    bf = jnp.bfloat16
    for hd in range(HEADS):
        sl = slice(hd * HD, (hd + 1) * HD)
        qh = q_ref[:, sl]
        qs = (jnp.where(lane < DA_DK, qh, jnp.zeros_like(qh)), jnp.where(lane >= DA_DK, qh, jnp.zeros_like(qh)))
        kh = k_ref[:, sl].astype(bf)
        vh = v_ref[:, sl].astype(bf)
        if cache:
            kch = kc_ref[:, sl].astype(bf)
            vch = vc_ref[:, sl].astype(bf)
        es, ecs, invs = [], [], []
        for comp in range(2):
            s = _dot_nt(qs[comp], kh)
            m = jnp.max(s, axis=-1, keepdims=True)
            if cache:
                sc = _dot_nt(qs[comp], kch)
                m = jnp.maximum(m, jnp.max(sc, axis=-1, keepdims=True))
                ec = jnp.exp(sc - m)
            e = jnp.exp(s - m)
            tot = jnp.sum(e, axis=-1, keepdims=True)
            if cache:
                tot = tot + jnp.sum(ec, axis=-1, keepdims=True)
                ecs.append(ec)
            es.append(e)
            invs.append(1.0 / tot)
        w2 = lam * invs[1]
        o = _dot((es[0] * invs[0] - es[1] * w2).astype(bf), vh)
        if cache:
            o = o + _dot((ecs[0] * invs[0] - ecs[1] * w2).astype(bf), vch)
        o = _rms(o, sub_ref[...]) * (1.0 - lam_init)
        oh_ref[:, sl] = o.astype(bf)
    y = _dot(oh_ref[...], w_ref[...])
    o_ref[...] = _ada_post(x_ref[...], y, mod_ref, g_ref, 1, 3, 1.0)


def _attn(x, mod_l, g_l, q, k, v, kc, vc, lam_p, subln, w_out, *, ctx, lam_init):
    nb, seq = (BATCH, SEQ) if ctx else (DEC_BATCH, DEC_SEQ)
    rows = nb * seq
    base = 0 if ctx else N_CTX // TQ
    nq = seq // TQ
    row_spec = pl.BlockSpec((TQ, D), lambda b, i: (b * nq + i, 0))
    kv_spec = pl.BlockSpec((seq, D), lambda b, i: (b, 0))
    in_specs = [
        pl.BlockSpec((TQ, D), lambda b, i: (base + b * nq + i, 0)),
        pl.BlockSpec((None, N_MOD, D), lambda b, i: (_grp_of_block(base + b * nq + i, TQ), 0, 0)),
        pl.BlockSpec((6, D), lambda b, i: (0, 0)),
        row_spec, kv_spec, kv_spec,
    ]
    args = [x, mod_l, g_l, q, k, v]
    if not ctx:
        c_spec = pl.BlockSpec((None, PAST, D), lambda b, i: (b, 0, 0))
        in_specs += [c_spec, c_spec]
        args += [kc, vc]
    in_specs += [
        pl.BlockSpec((4, DA_DK), lambda b, i: (0, 0)),
        pl.BlockSpec((1, HD), lambda b, i: (0, 0)),
        pl.BlockSpec((D, D), lambda b, i: (0, 0), pipeline_mode=pl.Buffered(1)),
    ]
    args += [lam_p, subln, w_out]
    return pl.pallas_call(
        functools.partial(_attn_kernel, lam_init=lam_init, cache=not ctx),
        grid=(nb, nq),
        in_specs=in_specs,
        out_specs=row_spec,
        out_shape=jax.ShapeDtypeStruct((rows, D), jnp.float32),
        scratch_shapes=[pltpu.VMEM((TQ, D), jnp.bfloat16)],
        compiler_params=_cparams(("parallel", "parallel")),
        name="attn_ctx" if ctx else "attn_smp",
    )(*args)


def _hgrn_pre_kernel(x_ref, mod_ref, g_ref, w_ref, lb_ref, q_ref, i_ref, gt_ref, lf_ref, *, layer):
    x = x_ref[...]
    h = _ada_pre(x, mod_ref, g_ref, 1, 2).astype(jnp.bfloat16)
    z = _dot(h, w_ref[...])
    q_ref[...] = z[:, 0:D].astype(q_ref.dtype)
    i_ref[...] = z[:, 3 * D:4 * D].astype(i_ref.dtype)
    gt_ref[...] = z[:, 4 * D:5 * D].astype(gt_ref.dtype)
    for d in range(2):
        raw = [lb_ref[l, d:d + 1, :] for l in range(DEPTH)]
        m = functools.reduce(jnp.maximum, raw)
        ex = [jnp.exp(r - m) for r in raw]
        tot = functools.reduce(lambda a, b: a + b, ex)
        soft = [e / tot for e in ex]
        lb = functools.reduce(lambda a, b: a + b, soft[:layer + 1]) - soft[0]
        ff = z[:, (1 + d) * D:(2 + d) * D]
        f = lb + (1.0 - lb) * jax.nn.sigmoid(ff)
        lf_ref[d] = jnp.log(f)


def _hgrn_pre(x, mod_l, g_l, w_in, lb_raw, layer):
    row = pl.BlockSpec((TM, D), lambda i: (i, 0))
    return pl.pallas_call(
        functools.partial(_hgrn_pre_kernel, layer=layer),
        grid=(N_TOK // TM,),
        in_specs=_tok_specs() + [_const_spec((D, 5 * D)), _const_spec((DEPTH, 2, D))],
        out_specs=[row, row, row, pl.BlockSpec((2, TM, D), lambda i: (0, i, 0))],
        out_shape=[
            jax.ShapeDtypeStruct((N_TOK, D), jnp.bfloat16),
            jax.ShapeDtypeStruct((N_TOK, D), jnp.bfloat16),
            jax.ShapeDtypeStruct((N_TOK, D), jnp.bfloat16),
            jax.ShapeDtypeStruct((2, N_TOK, D), jnp.float32),
        ],
        compiler_params=_cparams(("parallel",)),
        name="hgrn_pre",
    )(x, mod_l, g_l, w_in, lb_raw)


def _level_masks(rev):
    t = np.arange(CH)[:, None]
    s = np.arange(CH)[None, :]
    out = []
    for p in range(N_LEVEL):
        same = (t >> (p + 1)) == (s >> (p + 1))
        tb, sb = (t >> p) & 1, (s >> p) & 1
        out.append(same & ((tb == 0) & (sb == 1) if rev else (tb == 1) & (sb == 0)))
    out.append(t == s)
    return np.stack(out).astype(np.float32)


def _row_bcast(x, r):
    return jnp.broadcast_to(x[r:r + 1, :], x.shape)


def _rec_kernel(blk_ref, sid_ref, first_ref, last_ref,
                q_ref, i_ref, lf_ref, s0_ref, msk_ref, o_ref, sfin_ref,
                st_ref, b_ref, br_ref, qt_ref, kt_ref, *, rev):
    step = pl.program_id(0)
    bf = jnp.bfloat16

    @pl.when(first_ref[step] == 1)
    def _():
        for hd in range(HEADS):
            st_ref[hd] = s0_ref[hd].T

    sub = lax.broadcasted_iota(jnp.int32, (SUB, D), 0)
    order = range(N_SLAB - 1, -1, -1) if rev else range(N_SLAB)
    edge = 0 if rev else SUB - 1
    carry = None
    for v in order:
        rows = slice(v * SUB, (v + 1) * SUB)
        c = lf_ref[rows, :]
        for k in (1, 2, 4):
            if rev:
                c = c + jnp.where(sub < SUB - k, pltpu.roll(c, SUB - k, 0), 0.0)
            else:
                c = c + jnp.where(sub >= k, pltpu.roll(c, k, 0), 0.0)
        if carry is not None:
            c = c + carry
        carry = _row_bcast(c, edge)
        b_ref[rows, :] = c
        br_ref[rows, :] = carry
    b_last = carry

    for v2 in range(N_SLAB // 2):
        rows = slice(v2 * 2 * SUB, (v2 + 1) * 2 * SUB)
        g = lf_ref[rows, :]
        f = jnp.exp(g)
        k = 1.0 - f
        q = q_ref[rows, :].astype(jnp.float32)
        b = b_ref[rows, :]
        qt_ref[0, rows, :] = (q * f).astype(bf)
        kt_ref[0, rows, :] = k.astype(bf)
        for p in range(1, N_LEVEL):
            halves = []
            for v in (2 * v2, 2 * v2 + 1):
                bv = b[(v - 2 * v2) * SUB:(v - 2 * v2 + 1) * SUB, :]
                if p >= 3:
                    w = p - 3
                    hi = v & ~((1 << (w + 1)) - 1)
                    vr = hi | (1 << w) if rev else hi | ((1 << w) - 1)
                    r = br_ref[vr * SUB:(vr + 1) * SUB, :]
                elif p == 2:
                    r = _row_bcast(bv, 4 if rev else 3)
                else:
                    lo, hi_r = (2, 6) if rev else (1, 5)
                    r = jnp.where(sub < 4, _row_bcast(bv, lo), _row_bcast(bv, hi_r))
                halves.append(jnp.exp(-jnp.abs(bv - r)))
            e = jnp.concatenate(halves, axis=0)
            qt_ref[p, rows, :] = (q * e).astype(bf)
            kt_ref[p, rows, :] = (k * e).astype(bf)
        bl = jnp.concatenate([b_last, b_last], axis=0)
        qt_ref[N_LEVEL, rows, :] = (q * jnp.exp(b)).astype(bf)
        kt_ref[N_LEVEL, rows, :] = (k * jnp.exp(bl - b)).astype(bf)
    decay = jnp.exp(b_last[0:1, :])

    for hd in range(HEADS):
        sl = slice(hd * HD, (hd + 1) * HD)
        a = msk_ref[N_LEVEL] * _dot_nt(q_ref[:, sl], kt_ref[0, :, sl])
        for p in range(N_LEVEL):
            a = a + msk_ref[p] * _dot_nt(qt_ref[p, :, sl], kt_ref[p, :, sl])
        vh = i_ref[:, sl]
        st = st_ref[hd]
        o = _dot(a.astype(bf), vh) + _dot_nt(qt_ref[N_LEVEL, :, sl], st.astype(bf))
        o_ref[:, sl] = o
        st_ref[hd] = st * decay[:, sl] + _dot_tn(vh, kt_ref[N_LEVEL, :, sl])

    @pl.when(last_ref[step] == 1)
    def _():
        for hd in range(HEADS):
            sfin_ref[hd] = st_ref[hd].T


def _rec_tables(rev):
    blk, sid, first, last = [], [], [], []
    seqs = [(b * (SEQ // CH), SEQ // CH) for b in range(BATCH)]
    seqs += [(N_CTX // CH + b * (DEC_SEQ // CH), DEC_SEQ // CH) for b in range(DEC_BATCH)]
    for n, (start, nc) in enumerate(seqs):
        for c in range(nc):
            blk.append(start + (nc - 1 - c if rev else c))
            sid.append(n)
            first.append(int(c == 0))
            last.append(int(c == nc - 1))
    return [jnp.asarray(np.asarray(a, np.int32)) for a in (blk, sid, first, last)]


def _rec(q, i, lf, s0, *, rev):
    tables = _rec_tables(rev)
    d = 1 if rev else 0
    nseq = BATCH + DEC_BATCH
    row = pl.BlockSpec((CH, D), lambda s, blk, sid, fi, la: (blk[s], 0))
    st_spec = pl.BlockSpec((None, HEADS, HD, HD), lambda s, blk, sid, fi, la: (sid[s], 0, 0, 0))
    grid_spec = pltpu.PrefetchScalarGridSpec(
        num_scalar_prefetch=4,
        grid=(N_TOK // CH,),
        in_specs=[
            row, row,
            pl.BlockSpec((None, CH, D), lambda s, blk, sid, fi, la: (d, blk[s], 0)),
            st_spec,
            pl.BlockSpec((N_LEVEL + 1, CH, CH), lambda s, *_: (0, 0, 0)),
        ],
        out_specs=[row, st_spec],
        scratch_shapes=[
            pltpu.VMEM((HEADS, HD, HD), jnp.float32),
            pltpu.VMEM((CH, D), jnp.float32),
            pltpu.VMEM((CH, D), jnp.float32),
            pltpu.VMEM((N_LEVEL + 1, CH, D), jnp.bfloat16),
            pltpu.VMEM((N_LEVEL + 1, CH, D), jnp.bfloat16),
        ],
    )
    return pl.pallas_call(
        functools.partial(_rec_kernel, rev=rev),
        grid_spec=grid_spec,
        out_shape=[
            jax.ShapeDtypeStruct((N_TOK, D), jnp.float32),
            jax.ShapeDtypeStruct((nseq, HEADS, HD, HD), jnp.float32),
        ],
        compiler_params=_cparams(("arbitrary",)),
        name="hgrn_rec_bwd" if rev else "hgrn_rec_fwd",
    )(*tables, q, i, lf, s0, jnp.asarray(_level_masks(rev)))


def _hgrn_out_kernel(x_ref, mod_ref, g_ref, of_ref, ob_ref, gt_ref, gn_ref, w_ref, o_ref, oh_ref):
    for hd in range(HEADS):
        sl = slice(hd * HD, (hd + 1) * HD)
        o = of_ref[:, sl] + ob_ref[:, sl]
        o = _rms(o, gn_ref[...]) * jax.nn.sigmoid(gt_ref[:, sl].astype(jnp.float32))
        oh_ref[:, sl] = o.astype(jnp.bfloat16)
    y = _dot(oh_ref[...], w_ref[...])
    o_ref[...] = _ada_post(x_ref[...], y, mod_ref, g_ref, 1, 3, 1.0)


def _hgrn_out(x, mod_l, g_l, o_f, o_b, gt, gnorm, w_out):
    row = pl.BlockSpec((TM, D), lambda i: (i, 0))
    return pl.pallas_call(
        _hgrn_out_kernel,
        grid=(N_TOK // TM,),
        in_specs=_tok_specs() + [row, row, row, _const_spec((1, HD)), _const_spec((D, D))],
        out_specs=row,
        out_shape=jax.ShapeDtypeStruct((N_TOK, D), jnp.float32),
        scratch_shapes=[pltpu.VMEM((TM, D), jnp.bfloat16)],
        compiler_params=_cparams(("parallel",)),
        name="hgrn_out",
    )(x, mod_l, g_l, o_f, o_b, gt, gnorm, w_out)


def _rope_tables():
    pos = np.arange(DEC_SEQ)
    row = (pos // GRID_W).astype(np.float64)
    col = (pos % GRID_W).astype(np.float64)
    nf = DA_DK // 4
    inv = ROPE_THETA ** (-np.arange(nf, dtype=np.float64) / nf)
    lane = np.arange(HD)
    axis = (lane % DA_DK) // (2 * nf)
    ang = np.where(axis[None, :] == 0, row[:, None], col[:, None]) * inv[lane % nf][None, :]
    sign = np.where((lane % (2 * nf)) < nf, -1.0, 1.0)
    return (jnp.asarray(np.cos(ang), jnp.float32), jnp.asarray(np.sin(ang) * sign[None, :], jnp.float32))


def kernel(x_prompt, x_sample, cache_k, cache_v, state_hgrn, c, c_ctx, w_mod, b_mod, norm_g,
           ffn_w_in, ffn_w_out, attn_w_in, attn_w_out, attn_lambda, attn_subln,
           hgrn_w_in, hgrn_w_out, hgrn_lower_bounds, hgrn_gnorm):
    bf = jnp.bfloat16
    x = jnp.concatenate([x_prompt.reshape(N_CTX, D), x_sample.reshape(N_SMP, D)], axis=0)
    cc = jnp.concatenate([c_ctx[None, :], c, jnp.zeros((N_GRP - 1 - DEC_BATCH, D), jnp.float32)], axis=0)
    mod = _modulation(cc, w_mod, b_mod).reshape(DEPTH, N_GRP, N_MOD, D)
    cos_t, sin_t = _rope_tables()
    ffn_w_in = ffn_w_in.astype(bf)
    ffn_w_out = ffn_w_out.astype(bf)

    ks, vs, states = [], [], []
    for l in range(DEPTH):
        mod_l, g_l = mod[l], norm_g[l]
        x = _ffn(x, mod_l, g_l, ffn_w_in[l, 0], ffn_w_out[l, 0], 0)
        if l % 2 == 0:
            a = l // 2
            lam_init = 0.8 - 0.6 * math.exp(-0.3 * l)
            w_in = attn_w_in[a].astype(bf)
            w_out = attn_w_out[a].astype(bf)
            subln = attn_subln[a].reshape(1, HD)
            qc, kc_new, vc_new = _attn_pre(x, mod_l, g_l, w_in, cos_t, sin_t, ctx=True)
            qs, ksm, vsm = _attn_pre(x, mod_l, g_l, w_in, cos_t, sin_t, ctx=False)
            xc = _attn(x, mod_l, g_l, qc, kc_new, vc_new, None, None, attn_lambda[a], subln, w_out,
                       ctx=True, lam_init=lam_init)
            xs = _attn(x, mod_l, g_l, qs, ksm, vsm,
                       cache_k[:, a].reshape(DEC_BATCH, PAST, D), cache_v[:, a].reshape(DEC_BATCH, PAST, D),
                       attn_lambda[a], subln, w_out, ctx=False, lam_init=lam_init)
            x = jnp.concatenate([xc, xs], axis=0)
            ks.append(kc_new.reshape(BATCH, SEQ, HEADS, 2, DA_DK))
            vs.append(vc_new.reshape(BATCH, SEQ, HEADS, HD))
        else:
            r = l // 2
            q, i, gt, lf = _hgrn_pre(x, mod_l, g_l, hgrn_w_in[r].astype(bf), hgrn_lower_bounds, l)
            zero = jnp.zeros((BATCH, HEADS, HD, HD), jnp.float32)
            s0 = state_hgrn[:, r].astype(jnp.float32)
            o_f, s_f = _rec(q, i, lf, jnp.concatenate([zero, s0[:, 0]], axis=0), rev=False)
            o_b, s_b = _rec(q, i, lf, jnp.concatenate([zero, s0[:, 1]], axis=0), rev=True)
            x = _hgrn_out(x, mod_l, g_l, o_f, o_b, gt, hgrn_gnorm[r].reshape(1, HD), hgrn_w_out[r].astype(bf))
            states.append(jnp.stack([s_f[:BATCH], s_b[:BATCH]], axis=1))
        x = _ffn(x, mod_l, g_l, ffn_w_in[l, 1], ffn_w_out[l, 1], 2)

    y_prompt = x[:N_CTX].reshape(BATCH, SEQ, D)
    y_sample = x[N_CTX:].reshape(DEC_BATCH, DEC_SEQ, D)
    return (y_prompt, y_sample, jnp.stack(ks, axis=1), jnp.stack(vs, axis=1), jnp.stack(states, axis=1))
```

```python
import functools
import math

import numpy as np
import jax
import jax.numpy as jnp
from jax import lax
from jax.experimental import pallas as pl
from jax.experimental.pallas import tpu as pltpu

D = 1024
BATCH = 16
SEQ = 256
DEPTH = 2
DEC_BATCH = 4
DEC_SEQ = 2048
PAST = 256
GRID_W = 64
N_MOD = 9
EPS = 1e-6
HEADS = 8
HD = 128
DA_DK = 64
ROPE_THETA = 10000.0
FFN_H = 2816
LOG2E = 1.4426950408889634

N_CTX = BATCH * SEQ
N_SMP = DEC_BATCH * DEC_SEQ
N_TOK = N_CTX + N_SMP
N_GRP = 8

VMEM_LIMIT = 56 * 1024 * 1024
LANES = 128
SUB = 8

TM = 512
FFN_HC = 256
TQ = 256
CH = 128
N_SLAB = CH // SUB
N_LEVEL = 7


def _cparams(sem):
    return pltpu.CompilerParams(dimension_semantics=sem, vmem_limit_bytes=VMEM_LIMIT)


def _grp_of_block(i, rows):
    nctx = N_CTX // rows
    per = DEC_SEQ // rows
    return jnp.where(i < nctx, 0, 1 + (i - nctx) // per)


def _rms(x, g):
    return (x * lax.rsqrt(jnp.mean(x * x, axis=-1, keepdims=True) + EPS)) * g


def _dot(a, b):
    return jnp.dot(a, b, preferred_element_type=jnp.float32)


def _dot_nt(a, b):
    return lax.dot_general(a, b, (((1,), (1,)), ((), ())), preferred_element_type=jnp.float32)


def _dot_tn(a, b):
    return lax.dot_general(a, b, (((0,), (0,)), ((), ())), preferred_element_type=jnp.float32)


def _const_spec(shape):
    nd = len(shape)
    return pl.BlockSpec(shape, lambda *_: (0,) * nd, pipeline_mode=pl.Buffered(1))


def _mod_kernel(c_ref, w_ref, b_ref, o_ref):
    c = c_ref[...]
    s = (c * jax.nn.sigmoid(c)).astype(jnp.bfloat16)
    o_ref[...] = _dot(s, w_ref[...].astype(jnp.bfloat16)) + b_ref[...]


def _modulation(cc, w_mod, b_mod):
    tn = 1024
    nt = (N_MOD * D) // tn
    return pl.pallas_call(
        _mod_kernel,
        grid=(DEPTH, nt),
        in_specs=[
            pl.BlockSpec((N_GRP, D), lambda l, j: (0, 0)),
            pl.BlockSpec((None, D, tn), lambda l, j: (l, 0, j)),
            pl.BlockSpec((None, 1, tn), lambda l, j: (l, 0, j)),
        ],
        out_specs=pl.BlockSpec((None, N_GRP, tn), lambda l, j: (l, 0, j)),
        out_shape=jax.ShapeDtypeStruct((DEPTH, N_GRP, N_MOD * D), jnp.float32),
        compiler_params=_cparams(("parallel", "parallel")),
        name="modulation",
    )(cc, w_mod, b_mod.reshape(DEPTH, 1, N_MOD * D))


def _ada_pre(x, mod_ref, g_ref, j, gi):
    shift = mod_ref[3 * j:3 * j + 1, :]
    scale = mod_ref[3 * j + 1:3 * j + 2, :]
    return _rms(x, g_ref[gi:gi + 1, :]) * (1.0 + scale) + shift


def _ada_post(x, y, mod_ref, g_ref, j, gi, res_w):
    gate = mod_ref[3 * j + 2:3 * j + 3, :]
    return x + res_w * gate * _rms(y, g_ref[gi:gi + 1, :])


def _tok_specs():
    return [
        pl.BlockSpec((TM, D), lambda i: (i, 0)),
        pl.BlockSpec((None, N_MOD, D), lambda i: (_grp_of_block(i, TM), 0, 0)),
        _const_spec((6, D)),
    ]


def _ffn_kernel(*refs, j, pair):
    if pair:
        xa_ref, xb_ref, mod_ref, g_ref, win_ref, wout_ref, o_ref, acc_ref = refs
        x = jnp.where(pl.program_id(0) < N_CTX // TM, xa_ref[...], xb_ref[...])
    else:
        x_ref, mod_ref, g_ref, win_ref, wout_ref, o_ref, acc_ref = refs
        x = x_ref[...]
    h = _ada_pre(x, mod_ref, g_ref, j, 2 * j).astype(jnp.bfloat16)
    for c in range(FFN_H // FFN_HC):
        lo = c * FFN_HC
        a = _dot(h, win_ref[:, lo:lo + FFN_HC])
        b = _dot(h, win_ref[:, FFN_H + lo:FFN_H + lo + FFN_HC])
        u = ((a * jax.nn.sigmoid(a)) * b).astype(jnp.bfloat16)
        part = _dot(u, wout_ref[lo:lo + FFN_HC, :])
        if c == 0:
            acc_ref[...] = part
        else:
            acc_ref[...] += part
    o_ref[...] = _ada_post(x, acc_ref[...], mod_ref, g_ref, j, 2 * j + 1, 0.5)


def _ffn(x, mod_l, g_l, w_in, w_out, j, *, base=0, rows=N_TOK):
    pair = isinstance(x, tuple)
    nctx = N_CTX // TM
    b0 = base // TM
    if pair:
        assert base == 0 and rows == N_TOK
        x_specs = [pl.BlockSpec((TM, D), lambda i: (jnp.minimum(i, nctx - 1), 0)),
                   pl.BlockSpec((TM, D), lambda i: (jnp.maximum(i - nctx, 0), 0))]
        x_args = list(x)
    else:
        x_specs = [pl.BlockSpec((TM, D), lambda i: (i + b0, 0))]
        x_args = [x]
    return pl.pallas_call(
        functools.partial(_ffn_kernel, j=j, pair=pair),
        grid=(rows // TM,),
        in_specs=x_specs + [
            pl.BlockSpec((None, N_MOD, D), lambda i: (_grp_of_block(i + b0, TM), 0, 0)),
            _const_spec((6, D)), _const_spec((D, 2 * FFN_H)), _const_spec((FFN_H, D))],
        out_specs=pl.BlockSpec((TM, D), lambda i: (i, 0)),
        out_shape=jax.ShapeDtypeStruct((rows, D), jnp.float32),
        scratch_shapes=[pltpu.VMEM((TM, D), jnp.float32)],
        compiler_params=_cparams(("parallel",)),
        name=f"ffn{j}",
    )(*x_args, mod_l, g_l, w_in, w_out)


def _rope_lanes(x, cos, sin_signed):
    lane = lax.broadcasted_iota(jnp.int32, x.shape, 1)
    partner = jnp.where((lane % 32) < 16, pltpu.roll(x, LANES - 16, 1), pltpu.roll(x, 16, 1))
    return x * cos + partner * sin_signed


def _rope_rows(x, cos_t, sin_signed_t):
    parts = []
    for g in range(HD // 32):
        parts += [x[g * 32 + 16:g * 32 + 32, :], x[g * 32:g * 32 + 16, :]]
    return x * cos_t + jnp.concatenate(parts, axis=0) * sin_signed_t


def _attn_pre_kernel(x_ref, mod_ref, g_ref, wqv_ref, wkt_ref, cos_ref, sin_ref, cost_ref, sint_ref,
                     q_ref, kt_ref, v_ref, *, rope):
    x = x_ref[...]
    h = _ada_pre(x, mod_ref, g_ref, 1, 2).astype(jnp.bfloat16)
    z = _dot(h, wqv_ref[...])
    kt = _dot_nt(wkt_ref[...], h)
    qscale = DA_DK ** -0.5 * LOG2E
    for hd in range(HEADS):
        sl = slice(hd * HD, (hd + 1) * HD)
        q = z[:, hd * HD:(hd + 1) * HD]
        k = kt[hd * HD:(hd + 1) * HD, :]
        if rope:
            q = _rope_lanes(q, cos_ref[...], sin_ref[...])
            k = _rope_rows(k, cost_ref[...], sint_ref[...])
        q_ref[:, sl] = (q * qscale).astype(q_ref.dtype)
        kt_ref[sl, :] = k.astype(kt_ref.dtype)
    v_ref[...] = z[:, D:].astype(v_ref.dtype)


def _attn_pre(x, mod_l, g_l, w_qv, w_kt, tables, *, ctx):
    nb, seq = (BATCH, SEQ) if ctx else (DEC_BATCH, DEC_SEQ)
    t = min(TM, seq)
    rows = nb * seq
    base = 0 if ctx else N_CTX // t
    per = seq // t
    ptab = DEC_SEQ // t
    kv_dtype = jnp.float32 if ctx else jnp.bfloat16
    out_spec = pl.BlockSpec((t, D), lambda i: (i, 0))
    return pl.pallas_call(
        functools.partial(_attn_pre_kernel, rope=not ctx),
        grid=(rows // t,),
        in_specs=[
            pl.BlockSpec((t, D), lambda i: (i + base, 0)),
            pl.BlockSpec((None, N_MOD, D), lambda i: (_grp_of_block(i + base, t), 0, 0)),
            _const_spec((6, D)), _const_spec((D, 2 * D)), _const_spec((D, D)),
            pl.BlockSpec((t, HD), lambda i: (i % ptab, 0)),
            pl.BlockSpec((t, HD), lambda i: (i % ptab, 0)),
            pl.BlockSpec((HD, t), lambda i: (0, i % ptab)),
            pl.BlockSpec((HD, t), lambda i: (0, i % ptab)),
        ],
        out_specs=[out_spec, pl.BlockSpec((None, D, t), lambda i: (i // per, 0, i % per)), out_spec],
        out_shape=[
            jax.ShapeDtypeStruct((rows, D), jnp.bfloat16),
            jax.ShapeDtypeStruct((nb, D, seq), kv_dtype),
            jax.ShapeDtypeStruct((rows, D), kv_dtype),
        ],
        compiler_params=_cparams(("parallel",)),
        name="attn_pre_ctx" if ctx else "attn_pre_smp",
    )(x, mod_l, g_l, w_qv, w_kt, *tables)


def _attn_kernel(*refs, lam_init, cache):
    if cache:
        (x_ref, mod_ref, g_ref, q_ref, k_ref, v_ref, kc_ref, vc_ref,
         lam_ref, sub_ref, w_ref, o_ref, oh_ref) = refs
    else:
        (x_ref, mod_ref, g_ref, q_ref, k_ref, v_ref,
         lam_ref, sub_ref, w_ref, o_ref, oh_ref) = refs
    lp = lam_ref[...]
    lam = (jnp.exp(jnp.sum(lp[0:1] * lp[1:2], axis=-1, keepdims=True))
           - jnp.exp(jnp.sum(lp[2:3] * lp[3:4], axis=-1, keepdims=True)) + lam_init)
    lane = lax.broadcasted_iota(jnp.int32, (q_ref.shape[0], HD), 1)
    bf = jnp.bfloat16
    for hd in range(HEADS):
        sl = slice(hd * HD, (hd + 1) * HD)
        qh = q_ref[:, sl]
        qs = (jnp.where(lane < DA_DK, qh, jnp.zeros_like(qh)), jnp.where(lane >= DA_DK, qh, jnp.zeros_like(qh)))
        kh = k_ref[sl, :].astype(bf)
        vh = v_ref[:, sl].astype(bf)
        if cache:
            kch = kc_ref[sl, :].astype(bf)
            vch = vc_ref[:, sl].astype(bf)
        es, ecs, invs = [], [], []
        for comp in range(2):
            s = _dot(qs[comp], kh)
            m = jnp.max(s, axis=-1, keepdims=True)
            if cache:
                sc = _dot(qs[comp], kch)
                m = jnp.maximum(m, jnp.max(sc, axis=-1, keepdims=True))
                ec = jnp.exp2(sc - m)
            e = jnp.exp2(s - m)
            tot = jnp.sum(e, axis=-1, keepdims=True)
            if cache:
                tot = tot + jnp.sum(ec, axis=-1, keepdims=True)
                ecs.append(ec)
            es.append(e)
            invs.append(1.0 / tot)
        w2 = lam * invs[1]
        o = _dot((es[0] * invs[0] - es[1] * w2).astype(bf), vh)
        if cache:
            o = o + _dot((ecs[0] * invs[0] - ecs[1] * w2).astype(bf), vch)
        o = _rms(o, sub_ref[...]) * (1.0 - lam_init)
        oh_ref[:, sl] = o.astype(bf)
    y = _dot(oh_ref[...], w_ref[...])
    o_ref[...] = _ada_post(x_ref[...], y, mod_ref, g_ref, 1, 3, 1.0)


def _attn(x, mod_l, g_l, q, k, v, kc, vc, lam_p, subln, w_out, *, ctx, lam_init):
    nb, seq = (BATCH, SEQ) if ctx else (DEC_BATCH, DEC_SEQ)
    rows = nb * seq
    base = 0 if ctx else N_CTX // TQ
    nq = seq // TQ
    row_spec = pl.BlockSpec((TQ, D), lambda b, i: (b * nq + i, 0))
    in_specs = [
        pl.BlockSpec((TQ, D), lambda b, i: (base + b * nq + i, 0)),
        pl.BlockSpec((None, N_MOD, D), lambda b, i: (_grp_of_block(base + b * nq + i, TQ), 0, 0)),
        pl.BlockSpec((6, D), lambda b, i: (0, 0)),
        row_spec,
        pl.BlockSpec((None, D, seq), lambda b, i: (b, 0, 0)),
        pl.BlockSpec((seq, D), lambda b, i: (b, 0)),
    ]
    args = [x, mod_l, g_l, q, k, v]
    if not ctx:
        in_specs += [pl.BlockSpec((None, D, PAST), lambda b, i: (b, 0, 0)),
                     pl.BlockSpec((None, PAST, D), lambda b, i: (b, 0, 0))]
        args += [kc, vc]
    in_specs += [
        pl.BlockSpec((4, DA_DK), lambda b, i: (0, 0)),
        pl.BlockSpec((1, HD), lambda b, i: (0, 0)),
        pl.BlockSpec((D, D), lambda b, i: (0, 0), pipeline_mode=pl.Buffered(1)),
    ]
    args += [lam_p, subln, w_out]
    return pl.pallas_call(
        functools.partial(_attn_kernel, lam_init=lam_init, cache=not ctx),
        grid=(nb, nq),
        in_specs=in_specs,
        out_specs=row_spec,
        out_shape=jax.ShapeDtypeStruct((rows, D), jnp.float32),
        scratch_shapes=[pltpu.VMEM((TQ, D), jnp.bfloat16)],
        compiler_params=_cparams(("parallel", "parallel")),
        name="attn_ctx" if ctx else "attn_smp",
    )(*args)


def _hgrn_pre_kernel(x_ref, mod_ref, g_ref, w_ref, lb_ref, q_ref, i_ref, gt_ref, lf_ref, *, layer):
    x = x_ref[...]
    h = _ada_pre(x, mod_ref, g_ref, 1, 2).astype(jnp.bfloat16)
    z = _dot(h, w_ref[...])
    q_ref[...] = z[:, 0:D].astype(q_ref.dtype)
    i_ref[...] = z[:, 3 * D:4 * D].astype(i_ref.dtype)
    gt_ref[...] = z[:, 4 * D:5 * D].astype(gt_ref.dtype)
    for d in range(2):
        raw = [lb_ref[l, d:d + 1, :] for l in range(DEPTH)]
        m = functools.reduce(jnp.maximum, raw)
        ex = [jnp.exp(r - m) for r in raw]
        tot = functools.reduce(lambda a, b: a + b, ex)
        soft = [e / tot for e in ex]
        lb = functools.reduce(lambda a, b: a + b, soft[:layer + 1]) - soft[0]
        ff = z[:, (1 + d) * D:(2 + d) * D]
        f = lb + (1.0 - lb) * jax.nn.sigmoid(ff)
        lf_ref[d] = jnp.log(f)


def _hgrn_pre(x, mod_l, g_l, w_in, lb_raw, layer):
    row = pl.BlockSpec((TM, D), lambda i: (i, 0))
    return pl.pallas_call(
        functools.partial(_hgrn_pre_kernel, layer=layer),
        grid=(N_TOK // TM,),
        in_specs=_tok_specs() + [_const_spec((D, 5 * D)), _const_spec((DEPTH, 2, D))],
        out_specs=[row, row, row, pl.BlockSpec((2, TM, D), lambda i: (0, i, 0))],
        out_shape=[
            jax.ShapeDtypeStruct((N_TOK, D), jnp.bfloat16),
            jax.ShapeDtypeStruct((N_TOK, D), jnp.bfloat16),
            jax.ShapeDtypeStruct((N_TOK, D), jnp.bfloat16),
            jax.ShapeDtypeStruct((2, N_TOK, D), jnp.float32),
        ],
        compiler_params=_cparams(("parallel",)),
        name="hgrn_pre",
    )(x, mod_l, g_l, w_in, lb_raw)


def _level_masks(rev):
    t = np.arange(CH)[:, None]
    s = np.arange(CH)[None, :]
    out = []
    for p in range(N_LEVEL):
        same = (t >> (p + 1)) == (s >> (p + 1))
        tb, sb = (t >> p) & 1, (s >> p) & 1
        out.append(same & ((tb == 0) & (sb == 1) if rev else (tb == 1) & (sb == 0)))
    out.append(t == s)
    return np.stack(out).astype(np.float32)


def _row_bcast(x, r):
    return jnp.broadcast_to(x[r:r + 1, :], x.shape)


def _rec_kernel(blk_ref, sid_ref, first_ref, last_ref,
                q_ref, i_ref, lf_ref, s0_ref, msk_ref, o_ref, sfin_ref,
                st_ref, b_ref, br_ref, qt_ref, kt_ref, *, rev):
    step = pl.program_id(0)
    bf = jnp.bfloat16

    @pl.when(first_ref[step] == 1)
    def _():
        st_ref[...] = jnp.zeros_like(st_ref)

    @pl.when(first_ref[step] == 2)
    def _():
        for hd in range(HEADS):
            st_ref[hd] = s0_ref[hd].T

    sub = lax.broadcasted_iota(jnp.int32, (SUB, D), 0)
    order = range(N_SLAB - 1, -1, -1) if rev else range(N_SLAB)
    edge = 0 if rev else SUB - 1
    carry = None
    for v in order:
        rows = slice(v * SUB, (v + 1) * SUB)
        c = lf_ref[rows, :]
        for k in (1, 2, 4):
            if rev:
                c = c + jnp.where(sub < SUB - k, pltpu.roll(c, SUB - k, 0), 0.0)
            else:
                c = c + jnp.where(sub >= k, pltpu.roll(c, k, 0), 0.0)
        if carry is not None:
            c = c + carry
        carry = _row_bcast(c, edge)
        b_ref[rows, :] = c
        br_ref[rows, :] = carry
    b_last = carry

    for v2 in range(N_SLAB // 2):
        rows = slice(v2 * 2 * SUB, (v2 + 1) * 2 * SUB)
        g = lf_ref[rows, :]
        f = jnp.exp(g)
        k = 1.0 - f
        q = q_ref[rows, :].astype(jnp.float32)
        b = b_ref[rows, :]
        qt_ref[0, rows, :] = (q * f).astype(bf)
        kt_ref[0, rows, :] = k.astype(bf)
        for p in range(1, N_LEVEL):
            halves = []
            for v in (2 * v2, 2 * v2 + 1):
                bv = b[(v - 2 * v2) * SUB:(v - 2 * v2 + 1) * SUB, :]
                if p >= 3:
                    w = p - 3
                    hi = v & ~((1 << (w + 1)) - 1)
                    vr = hi | (1 << w) if rev else hi | ((1 << w) - 1)
                    r = br_ref[vr * SUB:(vr + 1) * SUB, :]
                elif p == 2:
                    r = _row_bcast(bv, 4 if rev else 3)
                else:
                    lo, hi_r = (2, 6) if rev else (1, 5)
                    r = jnp.where(sub < 4, _row_bcast(bv, lo), _row_bcast(bv, hi_r))
                halves.append(jnp.exp(-jnp.abs(bv - r)))
            e = jnp.concatenate(halves, axis=0)
            qt_ref[p, rows, :] = (q * e).astype(bf)
            kt_ref[p, rows, :] = (k * e).astype(bf)
        bl = jnp.concatenate([b_last, b_last], axis=0)
        qt_ref[N_LEVEL, rows, :] = (q * jnp.exp(b)).astype(bf)
        kt_ref[N_LEVEL, rows, :] = (k * jnp.exp(bl - b)).astype(bf)
    decay = jnp.exp(b_last[0:1, :])

    for hd in range(HEADS):
        sl = slice(hd * HD, (hd + 1) * HD)
        a = msk_ref[N_LEVEL] * _dot_nt(q_ref[:, sl], kt_ref[0, :, sl])
        for p in range(N_LEVEL):
            a = a + msk_ref[p] * _dot_nt(qt_ref[p, :, sl], kt_ref[p, :, sl])
        vh = i_ref[:, sl]
        st = st_ref[hd]
        o = _dot(a.astype(bf), vh) + _dot_nt(qt_ref[N_LEVEL, :, sl], st.astype(bf))
        o_ref[:, sl] = o
        st_ref[hd] = st * decay[:, sl] + _dot_tn(vh, kt_ref[N_LEVEL, :, sl])

    @pl.when(last_ref[step] == 1)
    def _():
        for hd in range(HEADS):
            sfin_ref[hd] = st_ref[hd].T


def _rec_tables(rev):
    blk, sid, first, last = [], [], [], []
    seqs = [(b * (SEQ // CH), SEQ // CH) for b in range(BATCH)]
    seqs += [(N_CTX // CH + b * (DEC_SEQ // CH), DEC_SEQ // CH) for b in range(DEC_BATCH)]
    for n, (start, nc) in enumerate(seqs):
        for c in range(nc):
            blk.append(start + (nc - 1 - c if rev else c))
            sid.append(n)
            first.append((1 if n < BATCH else 2) if c == 0 else 0)
            last.append(int(c == nc - 1))
    return [jnp.asarray(np.asarray(a, np.int32)) for a in (blk, sid, first, last)]


def _rec(q, i, lf, s0, *, rev):
    tables = _rec_tables(rev)
    d = 1 if rev else 0
    nseq = BATCH + DEC_BATCH
    row = pl.BlockSpec((CH, D), lambda s, blk, sid, fi, la: (blk[s], 0))
    st_spec = pl.BlockSpec((None, HEADS, HD, HD), lambda s, blk, sid, fi, la: (sid[s], 0, 0, 0))
    s0_spec = pl.BlockSpec((None, None, HEADS, HD, HD),
                           lambda s, blk, sid, fi, la: (jnp.maximum(sid[s] - BATCH, 0), d, 0, 0, 0))
    grid_spec = pltpu.PrefetchScalarGridSpec(
        num_scalar_prefetch=4,
        grid=(N_TOK // CH,),
        in_specs=[
            row, row,
            pl.BlockSpec((None, CH, D), lambda s, blk, sid, fi, la: (d, blk[s], 0)),
            s0_spec,
            pl.BlockSpec((N_LEVEL + 1, CH, CH), lambda s, *_: (0, 0, 0)),
        ],
        out_specs=[row, st_spec],
        scratch_shapes=[
            pltpu.VMEM((HEADS, HD, HD), jnp.float32),
            pltpu.VMEM((CH, D), jnp.float32),
            pltpu.VMEM((CH, D), jnp.float32),
            pltpu.VMEM((N_LEVEL + 1, CH, D), jnp.bfloat16),
            pltpu.VMEM((N_LEVEL + 1, CH, D), jnp.bfloat16),
        ],
    )
    return pl.pallas_call(
        functools.partial(_rec_kernel, rev=rev),
        grid_spec=grid_spec,
        out_shape=[
            jax.ShapeDtypeStruct((N_TOK, D), jnp.float32),
            jax.ShapeDtypeStruct((nseq, HEADS, HD, HD), jnp.float32),
        ],
        compiler_params=_cparams(("arbitrary",)),
        name="hgrn_rec_bwd" if rev else "hgrn_rec_fwd",
    )(*tables, q, i, lf, s0, jnp.asarray(_level_masks(rev)))


def _hgrn_out_kernel(x_ref, mod_ref, g_ref, of_ref, ob_ref, gt_ref, gn_ref, w_ref, o_ref, oh_ref):
    for hd in range(HEADS):
        sl = slice(hd * HD, (hd + 1) * HD)
        o = of_ref[:, sl] + ob_ref[:, sl]
        o = _rms(o, gn_ref[...]) * jax.nn.sigmoid(gt_ref[:, sl].astype(jnp.float32))
        oh_ref[:, sl] = o.astype(jnp.bfloat16)
    y = _dot(oh_ref[...], w_ref[...])
    o_ref[...] = _ada_post(x_ref[...], y, mod_ref, g_ref, 1, 3, 1.0)


def _hgrn_out(x, mod_l, g_l, o_f, o_b, gt, gnorm, w_out):
    row = pl.BlockSpec((TM, D), lambda i: (i, 0))
    return pl.pallas_call(
        _hgrn_out_kernel,
        grid=(N_TOK // TM,),
        in_specs=_tok_specs() + [row, row, row, _const_spec((1, HD)), _const_spec((D, D))],
        out_specs=row,
        out_shape=jax.ShapeDtypeStruct((N_TOK, D), jnp.float32),
        scratch_shapes=[pltpu.VMEM((TM, D), jnp.bfloat16)],
        compiler_params=_cparams(("parallel",)),
        name="hgrn_out",
    )(x, mod_l, g_l, o_f, o_b, gt, gnorm, w_out)


def _rope_tables():
    pos = np.arange(DEC_SEQ)
    row = (pos // GRID_W).astype(np.float64)
    col = (pos % GRID_W).astype(np.float64)
    nf = DA_DK // 4
    inv = ROPE_THETA ** (-np.arange(nf, dtype=np.float64) / nf)
    lane = np.arange(HD)
    axis = (lane % DA_DK) // (2 * nf)
    ang = np.where(axis[None, :] == 0, row[:, None], col[:, None]) * inv[lane % nf][None, :]
    sign = np.where((lane % (2 * nf)) < nf, -1.0, 1.0)
    cos, sin = np.cos(ang), np.sin(ang) * sign[None, :]
    return tuple(jnp.asarray(a, jnp.float32) for a in (cos, sin, cos.T, sin.T))


def kernel(x_prompt, x_sample, cache_k, cache_v, state_hgrn, c, c_ctx, w_mod, b_mod, norm_g,
           ffn_w_in, ffn_w_out, attn_w_in, attn_w_out, attn_lambda, attn_subln,
           hgrn_w_in, hgrn_w_out, hgrn_lower_bounds, hgrn_gnorm):
    bf = jnp.bfloat16
    x = (x_prompt.reshape(N_CTX, D), x_sample.reshape(N_SMP, D))
    cc = jnp.concatenate([c_ctx[None, :], c, jnp.zeros((N_GRP - 1 - DEC_BATCH, D), jnp.float32)], axis=0)
    mod = _modulation(cc, w_mod, b_mod).reshape(DEPTH, N_GRP, N_MOD, D)
    tables = _rope_tables()
    ffn_w_in = ffn_w_in.astype(bf)
    ffn_w_out = ffn_w_out.astype(bf)

    ks, vs, states = [], [], []
    for l in range(DEPTH):
        mod_l, g_l = mod[l], norm_g[l]
        x = _ffn(x, mod_l, g_l, ffn_w_in[l, 0], ffn_w_out[l, 0], 0)
        if l % 2 == 0:
            a = l // 2
            lam_init = 0.8 - 0.6 * math.exp(-0.3 * l)
            w_qv = jnp.concatenate([attn_w_in[a][:, :D], attn_w_in[a][:, 2 * D:]], axis=1).astype(bf)
            w_kt = attn_w_in[a][:, D:2 * D].T.astype(bf)
            w_out = attn_w_out[a].astype(bf)
            subln = attn_subln[a].reshape(1, HD)
            qc, kt_new, vc_new = _attn_pre(x, mod_l, g_l, w_qv, w_kt, tables, ctx=True)
            qs, kt_smp, v_smp = _attn_pre(x, mod_l, g_l, w_qv, w_kt, tables, ctx=False)
            kct = jnp.transpose(cache_k[:, a], (0, 2, 3, 4, 1)).reshape(DEC_BATCH, D, PAST)
            xc = _attn(x, mod_l, g_l, qc, kt_new, vc_new, None, None, attn_lambda[a], subln, w_out,
                       ctx=True, lam_init=lam_init)
            xs = _attn(x, mod_l, g_l, qs, kt_smp, v_smp, kct, cache_v[:, a].reshape(DEC_BATCH, PAST, D),
                       attn_lambda[a], subln, w_out, ctx=False, lam_init=lam_init)
            x = (xc, xs)
            ks.append(jnp.transpose(kt_new.reshape(BATCH, HEADS, 2, DA_DK, SEQ), (0, 4, 1, 2, 3)))
            vs.append(vc_new.reshape(BATCH, SEQ, HEADS, HD))
        else:
            r = l // 2
            q, i, gt, lf = _hgrn_pre(x, mod_l, g_l, hgrn_w_in[r].astype(bf), hgrn_lower_bounds, l)
            s0 = state_hgrn[:, r].astype(jnp.float32)
            o_f, s_f = _rec(q, i, lf, s0, rev=False)
            o_b, s_b = _rec(q, i, lf, s0, rev=True)
            x = _hgrn_out(x, mod_l, g_l, o_f, o_b, gt, hgrn_gnorm[r].reshape(1, HD), hgrn_w_out[r].astype(bf))
            states.append(jnp.stack([s_f[:BATCH], s_b[:BATCH]], axis=1))
        if l < DEPTH - 1:
            x = _ffn(x, mod_l, g_l, ffn_w_in[l, 1], ffn_w_out[l, 1], 2)

    assert not isinstance(x, tuple)
    y_prompt = _ffn(x, mod_l, g_l, ffn_w_in[-1, 1], ffn_w_out[-1, 1], 2, base=0, rows=N_CTX)
    y_sample = _ffn(x, mod_l, g_l, ffn_w_in[-1, 1], ffn_w_out[-1, 1], 2, base=N_CTX, rows=N_SMP)
    return (y_prompt.reshape(BATCH, SEQ, D), y_sample.reshape(DEC_BATCH, DEC_SEQ, D),
            jnp.stack(ks, axis=1), jnp.stack(vs, axis=1), jnp.stack(states, axis=1))
```

```python
import functools
import math

import numpy as np
import jax
import jax.numpy as jnp
from jax import lax
from jax.experimental import pallas as pl
from jax.experimental.pallas import tpu as pltpu

D = 1024
BATCH = 16
SEQ = 256
DEPTH = 2
DEC_BATCH = 4
DEC_SEQ = 2048
PAST = 256
GRID_W = 64
N_MOD = 9
EPS = 1e-6
HEADS = 8
HD = 128
DA_DK = 64
ROPE_THETA = 10000.0
FFN_H = 2816
LOG2E = 1.4426950408889634

N_CTX = BATCH * SEQ
N_SMP = DEC_BATCH * DEC_SEQ
N_TOK = N_CTX + N_SMP
N_GRP = 8

VMEM_LIMIT = 56 * 1024 * 1024
LANES = 128
SUB = 8

TM = 512
FFN_HC = 256
TQ = 256
CH = 128
N_SLAB = CH // SUB
N_LEVEL = 7


def _cparams(sem):
    return pltpu.CompilerParams(dimension_semantics=sem, vmem_limit_bytes=VMEM_LIMIT)


def _grp_of_block(i, rows):
    nctx = N_CTX // rows
    per = DEC_SEQ // rows
    return jnp.where(i < nctx, 0, 1 + (i - nctx) // per)


def _rms(x, g):
    return (x * lax.rsqrt(jnp.mean(x * x, axis=-1, keepdims=True) + EPS)) * g


def _dot(a, b):
    return jnp.dot(a, b, preferred_element_type=jnp.float32)


def _dot_nt(a, b):
    return lax.dot_general(a, b, (((1,), (1,)), ((), ())), preferred_element_type=jnp.float32)


def _dot_tn(a, b):
    return lax.dot_general(a, b, (((0,), (0,)), ((), ())), preferred_element_type=jnp.float32)


def _const_spec(shape):
    nd = len(shape)
    return pl.BlockSpec(shape, lambda *_: (0,) * nd, pipeline_mode=pl.Buffered(1))


def _mod_kernel(c_ref, w_ref, b_ref, o_ref):
    c = c_ref[...]
    s = (c * jax.nn.sigmoid(c)).astype(jnp.bfloat16)
    o_ref[...] = _dot(s, w_ref[...].astype(jnp.bfloat16)) + b_ref[...]


def _modulation(cc, w_mod, b_mod):
    tn = 1024
    nt = (N_MOD * D) // tn
    return pl.pallas_call(
        _mod_kernel,
        grid=(DEPTH, nt),
        in_specs=[
            pl.BlockSpec((N_GRP, D), lambda l, j: (0, 0)),
            pl.BlockSpec((None, D, tn), lambda l, j: (l, 0, j)),
            pl.BlockSpec((None, 1, tn), lambda l, j: (l, 0, j)),
        ],
        out_specs=pl.BlockSpec((None, N_GRP, tn), lambda l, j: (l, 0, j)),
        out_shape=jax.ShapeDtypeStruct((DEPTH, N_GRP, N_MOD * D), jnp.float32),
        compiler_params=_cparams(("parallel", "parallel")),
        name="modulation",
    )(cc, w_mod, b_mod.reshape(DEPTH, 1, N_MOD * D))


def _ada_pre(x, mod_ref, g_ref, j, gi):
    shift = mod_ref[3 * j:3 * j + 1, :]
    scale = mod_ref[3 * j + 1:3 * j + 2, :]
    return _rms(x, g_ref[gi:gi + 1, :]) * (1.0 + scale) + shift


def _ada_post(x, y, mod_ref, g_ref, j, gi, res_w):
    gate = mod_ref[3 * j + 2:3 * j + 3, :]
    return x + res_w * gate * _rms(y, g_ref[gi:gi + 1, :])


def _tok_specs():
    return [
        pl.BlockSpec((TM, D), lambda i: (i, 0)),
        pl.BlockSpec((None, N_MOD, D), lambda i: (_grp_of_block(i, TM), 0, 0)),
        _const_spec((6, D)),
    ]


def _ffn_kernel(*refs, j, pair):
    if pair:
        xa_ref, xb_ref, mod_ref, g_ref, win_ref, wout_ref, o_ref, acc_ref = refs
        x = jnp.where(pl.program_id(0) < N_CTX // TM, xa_ref[...], xb_ref[...])
    else:
        x_ref, mod_ref, g_ref, win_ref, wout_ref, o_ref, acc_ref = refs
        x = x_ref[...]
    h = _ada_pre(x, mod_ref, g_ref, j, 2 * j).astype(jnp.bfloat16)
    for c in range(FFN_H // FFN_HC):
        lo = c * FFN_HC
        a = _dot(h, win_ref[:, lo:lo + FFN_HC])
        b = _dot(h, win_ref[:, FFN_H + lo:FFN_H + lo + FFN_HC])
        u = ((a * jax.nn.sigmoid(a)) * b).astype(jnp.bfloat16)
        part = _dot(u, wout_ref[lo:lo + FFN_HC, :])
        if c == 0:
            acc_ref[...] = part
        else:
            acc_ref[...] += part
    o_ref[...] = _ada_post(x, acc_ref[...], mod_ref, g_ref, j, 2 * j + 1, 0.5)


def _ffn(x, mod_l, g_l, w_in, w_out, j, *, base=0, rows=N_TOK):
    pair = isinstance(x, tuple)
    nctx = N_CTX // TM
    b0 = base // TM
    if pair:
        assert base == 0 and rows == N_TOK
        x_specs = [pl.BlockSpec((TM, D), lambda i: (jnp.minimum(i, nctx - 1), 0)),
                   pl.BlockSpec((TM, D), lambda i: (jnp.maximum(i - nctx, 0), 0))]
        x_args = list(x)
    else:
        x_specs = [pl.BlockSpec((TM, D), lambda i: (i + b0, 0))]
        x_args = [x]
    return pl.pallas_call(
        functools.partial(_ffn_kernel, j=j, pair=pair),
        grid=(rows // TM,),
        in_specs=x_specs + [
            pl.BlockSpec((None, N_MOD, D), lambda i: (_grp_of_block(i + b0, TM), 0, 0)),
            _const_spec((6, D)), _const_spec((D, 2 * FFN_H)), _const_spec((FFN_H, D))],
        out_specs=pl.BlockSpec((TM, D), lambda i: (i, 0)),
        out_shape=jax.ShapeDtypeStruct((rows, D), jnp.float32),
        scratch_shapes=[pltpu.VMEM((TM, D), jnp.float32)],
        compiler_params=_cparams(("parallel",)),
        name=f"ffn{j}",
    )(*x_args, mod_l, g_l, w_in, w_out)


def _rope_lanes(x, cos, sin_signed):
    lane = lax.broadcasted_iota(jnp.int32, x.shape, 1)
    partner = jnp.where((lane % 32) < 16, pltpu.roll(x, LANES - 16, 1), pltpu.roll(x, 16, 1))
    return x * cos + partner * sin_signed


def _rope_rows(x, cos_t, sin_signed_t):
    parts = []
    for g in range(HD // 32):
        parts += [x[g * 32 + 16:g * 32 + 32, :], x[g * 32:g * 32 + 16, :]]
    return x * cos_t + jnp.concatenate(parts, axis=0) * sin_signed_t


def _attn_pre_kernel(x_ref, mod_ref, g_ref, wqv_ref, wkt_ref, cos_ref, sin_ref, cost_ref, sint_ref,
                     q_ref, kt_ref, v_ref, *, rope):
    x = x_ref[...]
    h = _ada_pre(x, mod_ref, g_ref, 1, 2).astype(jnp.bfloat16)
    z = _dot(h, wqv_ref[...])
    kt = _dot_nt(wkt_ref[...], h)
    qscale = DA_DK ** -0.5 * LOG2E
    for hd in range(HEADS):
        sl = slice(hd * HD, (hd + 1) * HD)
        q = z[:, hd * HD:(hd + 1) * HD]
        k = kt[hd * HD:(hd + 1) * HD, :]
        if rope:
            q = _rope_lanes(q, cos_ref[...], sin_ref[...])
            k = _rope_rows(k, cost_ref[...], sint_ref[...])
        q_ref[:, sl] = (q * qscale).astype(q_ref.dtype)
        kt_ref[sl, :] = k.astype(kt_ref.dtype)
    v_ref[...] = z[:, D:].astype(v_ref.dtype)


def _attn_pre(x, mod_l, g_l, w_qv, w_kt, tables, *, ctx):
    nb, seq = (BATCH, SEQ) if ctx else (DEC_BATCH, DEC_SEQ)
    t = min(TM, seq)
    rows = nb * seq
    base = 0 if ctx else N_CTX // t
    per = seq // t
    ptab = DEC_SEQ // t
    kv_dtype = jnp.float32 if ctx else jnp.bfloat16
    out_spec = pl.BlockSpec((t, D), lambda i: (i, 0))
    return pl.pallas_call(
        functools.partial(_attn_pre_kernel, rope=not ctx),
        grid=(rows // t,),
        in_specs=[
            pl.BlockSpec((t, D), lambda i: (i + base, 0)),
            pl.BlockSpec((None, N_MOD, D), lambda i: (_grp_of_block(i + base, t), 0, 0)),
            _const_spec((6, D)), _const_spec((D, 2 * D)), _const_spec((D, D)),
            pl.BlockSpec((t, HD), lambda i: (i % ptab, 0)),
            pl.BlockSpec((t, HD), lambda i: (i % ptab, 0)),
            pl.BlockSpec((HD, t), lambda i: (0, i % ptab)),
            pl.BlockSpec((HD, t), lambda i: (0, i % ptab)),
        ],
        out_specs=[out_spec, pl.BlockSpec((None, D, t), lambda i: (i // per, 0, i % per)), out_spec],
        out_shape=[
            jax.ShapeDtypeStruct((rows, D), jnp.bfloat16),
            jax.ShapeDtypeStruct((nb, D, seq), kv_dtype),
            jax.ShapeDtypeStruct((rows, D), kv_dtype),
        ],
        compiler_params=_cparams(("parallel",)),
        name="attn_pre_ctx" if ctx else "attn_pre_smp",
    )(x, mod_l, g_l, w_qv, w_kt, *tables)


def _attn_kernel(*refs, lam_init, cache):
    if cache:
        (x_ref, mod_ref, g_ref, q_ref, k_ref, v_ref, kc_ref, vc_ref,
         lam_ref, sub_ref, w_ref, o_ref, oh_ref) = refs
    else:
        (x_ref, mod_ref, g_ref, q_ref, k_ref, v_ref,
         lam_ref, sub_ref, w_ref, o_ref, oh_ref) = refs
    lp = lam_ref[...]
    lam = (jnp.exp(jnp.sum(lp[0:1] * lp[1:2], axis=-1, keepdims=True))
           - jnp.exp(jnp.sum(lp[2:3] * lp[3:4], axis=-1, keepdims=True)) + lam_init)
    lane = lax.broadcasted_iota(jnp.int32, (q_ref.shape[0], HD), 1)
    bf = jnp.bfloat16
    for hd in range(HEADS):
        sl = slice(hd * HD, (hd + 1) * HD)
        qh = q_ref[:, sl]
        qs = (jnp.where(lane < DA_DK, qh, jnp.zeros_like(qh)), jnp.where(lane >= DA_DK, qh, jnp.zeros_like(qh)))
        kh = k_ref[sl, :].astype(bf)
        vh = v_ref[:, sl].astype(bf)
        if cache:
            kch = kc_ref[sl, :].astype(bf)
            vch = vc_ref[:, sl].astype(bf)
        es, ecs, invs = [], [], []
        for comp in range(2):
            s = _dot(qs[comp], kh)
            m = jnp.max(s, axis=-1, keepdims=True)
            if cache:
                sc = _dot(qs[comp], kch)
                m = jnp.maximum(m, jnp.max(sc, axis=-1, keepdims=True))
                ec = jnp.exp2(sc - m)
            e = jnp.exp2(s - m)
            tot = jnp.sum(e, axis=-1, keepdims=True)
            if cache:
                tot = tot + jnp.sum(ec, axis=-1, keepdims=True)
                ecs.append(ec)
            es.append(e)
            invs.append(1.0 / tot)
        w2 = lam * invs[1]
        o = _dot((es[0] * invs[0] - es[1] * w2).astype(bf), vh)
        if cache:
            o = o + _dot((ecs[0] * invs[0] - ecs[1] * w2).astype(bf), vch)
        o = _rms(o, sub_ref[...]) * (1.0 - lam_init)
        oh_ref[:, sl] = o.astype(bf)
    y = _dot(oh_ref[...], w_ref[...])
    o_ref[...] = _ada_post(x_ref[...], y, mod_ref, g_ref, 1, 3, 1.0)


def _attn(x, mod_l, g_l, q, k, v, kc, vc, lam_p, subln, w_out, *, ctx, lam_init):
    nb, seq = (BATCH, SEQ) if ctx else (DEC_BATCH, DEC_SEQ)
    rows = nb * seq
    base = 0 if ctx else N_CTX // TQ
    nq = seq // TQ
    row_spec = pl.BlockSpec((TQ, D), lambda b, i: (b * nq + i, 0))
    in_specs = [
        pl.BlockSpec((TQ, D), lambda b, i: (base + b * nq + i, 0)),
        pl.BlockSpec((None, N_MOD, D), lambda b, i: (_grp_of_block(base + b * nq + i, TQ), 0, 0)),
        pl.BlockSpec((6, D), lambda b, i: (0, 0)),
        row_spec,
        pl.BlockSpec((None, D, seq), lambda b, i: (b, 0, 0)),
        pl.BlockSpec((seq, D), lambda b, i: (b, 0)),
    ]
    args = [x, mod_l, g_l, q, k, v]
    if not ctx:
        in_specs += [pl.BlockSpec((None, D, PAST), lambda b, i: (b, 0, 0)),
                     pl.BlockSpec((None, PAST, D), lambda b, i: (b, 0, 0))]
        args += [kc, vc]
    in_specs += [
        pl.BlockSpec((4, DA_DK), lambda b, i: (0, 0)),
        pl.BlockSpec((1, HD), lambda b, i: (0, 0)),
        pl.BlockSpec((D, D), lambda b, i: (0, 0), pipeline_mode=pl.Buffered(1)),
    ]
    args += [lam_p, subln, w_out]
    return pl.pallas_call(
        functools.partial(_attn_kernel, lam_init=lam_init, cache=not ctx),
        grid=(nb, nq),
        in_specs=in_specs,
        out_specs=row_spec,
        out_shape=jax.ShapeDtypeStruct((rows, D), jnp.float32),
        scratch_shapes=[pltpu.VMEM((TQ, D), jnp.bfloat16)],
        compiler_params=_cparams(("parallel", "parallel")),
        name="attn_ctx" if ctx else "attn_smp",
    )(*args)


def _hgrn_pre_kernel(x_ref, mod_ref, g_ref, w_ref, lb_ref, q_ref, i_ref, gt_ref, lf_ref, *, layer):
    x = x_ref[...]
    h = _ada_pre(x, mod_ref, g_ref, 1, 2).astype(jnp.bfloat16)
    z = _dot(h, w_ref[...])
    q_ref[...] = z[:, 0:D].astype(q_ref.dtype)
    i_ref[...] = z[:, 3 * D:4 * D].astype(i_ref.dtype)
    gt_ref[...] = z[:, 4 * D:5 * D].astype(gt_ref.dtype)
    for d in range(2):
        raw = [lb_ref[l, d:d + 1, :] for l in range(DEPTH)]
        m = functools.reduce(jnp.maximum, raw)
        ex = [jnp.exp(r - m) for r in raw]
        tot = functools.reduce(lambda a, b: a + b, ex)
        soft = [e / tot for e in ex]
        lb = functools.reduce(lambda a, b: a + b, soft[:layer + 1]) - soft[0]
        ff = z[:, (1 + d) * D:(2 + d) * D]
        f = lb + (1.0 - lb) * jax.nn.sigmoid(ff)
        lf_ref[d] = jnp.log(f)


def _hgrn_pre(x, mod_l, g_l, w_in, lb_raw, layer):
    row = pl.BlockSpec((TM, D), lambda i: (i, 0))
    return pl.pallas_call(
        functools.partial(_hgrn_pre_kernel, layer=layer),
        grid=(N_TOK // TM,),
        in_specs=_tok_specs() + [_const_spec((D, 5 * D)), _const_spec((DEPTH, 2, D))],
        out_specs=[row, row, row, pl.BlockSpec((2, TM, D), lambda i: (0, i, 0))],
        out_shape=[
            jax.ShapeDtypeStruct((N_TOK, D), jnp.bfloat16),
            jax.ShapeDtypeStruct((N_TOK, D), jnp.bfloat16),
            jax.ShapeDtypeStruct((N_TOK, D), jnp.bfloat16),
            jax.ShapeDtypeStruct((2, N_TOK, D), jnp.float32),
        ],
        compiler_params=_cparams(("parallel",)),
        name="hgrn_pre",
    )(x, mod_l, g_l, w_in, lb_raw)


def _level_masks(rev):
    t = np.arange(CH)[:, None]
    s = np.arange(CH)[None, :]
    out = []
    for p in range(N_LEVEL):
        same = (t >> (p + 1)) == (s >> (p + 1))
        tb, sb = (t >> p) & 1, (s >> p) & 1
        out.append(same & ((tb == 0) & (sb == 1) if rev else (tb == 1) & (sb == 0)))
    out.append(t == s)
    return np.stack(out).astype(np.float32)


def _row_bcast(x, r):
    return jnp.broadcast_to(x[r:r + 1, :], x.shape)


def _neg_abs(x):
    bits = lax.bitcast_convert_type(x, jnp.uint32) | jnp.uint32(0x80000000)
    return lax.bitcast_convert_type(bits, jnp.float32)


def _rec_prepare(lf_ref, q_ref, b_ref, br_ref, qt_ref, kt_ref, rev):
    bf = jnp.bfloat16
    sub = lax.broadcasted_iota(jnp.int32, (SUB, D), 0)
    order = range(N_SLAB - 1, -1, -1) if rev else range(N_SLAB)
    edge = 0 if rev else SUB - 1
    carry = None
    for v in order:
        rows = slice(v * SUB, (v + 1) * SUB)
        c = lf_ref[rows, :] * LOG2E
        for k in (1, 2, 4):
            if rev:
                c = c + jnp.where(sub < SUB - k, pltpu.roll(c, SUB - k, 0), 0.0)
            else:
                c = c + jnp.where(sub >= k, pltpu.roll(c, k, 0), 0.0)
        if carry is not None:
            c = c + carry
        carry = _row_bcast(c, edge)
        b_ref[rows, :] = c
        br_ref[rows, :] = carry
    b_last = carry

    zeros = jnp.zeros((SUB, D), jnp.float32)
    for v2 in range(N_SLAB // 2):
        rows = slice(v2 * 2 * SUB, (v2 + 1) * 2 * SUB)
        f = jnp.exp2(lf_ref[rows, :] * LOG2E)
        k = 1.0 - f
        q = q_ref[rows, :].astype(jnp.float32)
        b = b_ref[rows, :]
        qt_ref[0, rows, :] = (q * f).astype(bf)
        kt_ref[0, rows, :] = k.astype(bf)
        half = (slice(0, SUB), slice(SUB, 2 * SUB))
        for p in range(1, N_LEVEL):
            if p >= 3:
                w = p - 3
                es, qside = [], []
                for n, v in enumerate((2 * v2, 2 * v2 + 1)):
                    hi = v & ~((1 << (w + 1)) - 1)
                    vr = hi | (1 << w) if rev else hi | ((1 << w) - 1)
                    r = br_ref[vr * SUB:(vr + 1) * SUB, :]
                    qside.append(((v >> w) & 1) == (0 if rev else 1))
                    es.append(jnp.exp2(b[half[n], :] - r) if qside[n] else jnp.exp2(r - b[half[n], :]))
                if qside[0] == qside[1]:
                    e = jnp.concatenate(es, axis=0)
                    if qside[0]:
                        qt_ref[p, rows, :] = (q * e).astype(bf)
                    else:
                        kt_ref[p, rows, :] = (k * e).astype(bf)
                else:
                    qe = [q[half[n], :] * es[n] if qside[n] else zeros for n in range(2)]
                    ke = [zeros if qside[n] else k[half[n], :] * es[n] for n in range(2)]
                    qt_ref[p, rows, :] = jnp.concatenate(qe, axis=0).astype(bf)
                    kt_ref[p, rows, :] = jnp.concatenate(ke, axis=0).astype(bf)
            else:
                es = []
                for n in range(2):
                    bv = b[half[n], :]
                    if p == 2:
                        r = _row_bcast(bv, 4 if rev else 3)
                    else:
                        lo, hi_r = (2, 6) if rev else (1, 5)
                        r = jnp.where(sub < 4, _row_bcast(bv, lo), _row_bcast(bv, hi_r))
                    es.append(jnp.exp2(_neg_abs(bv - r)))
                e = jnp.concatenate(es, axis=0)
                qt_ref[p, rows, :] = (q * e).astype(bf)
                kt_ref[p, rows, :] = (k * e).astype(bf)
        bl = jnp.concatenate([b_last, b_last], axis=0)
        qt_ref[N_LEVEL, rows, :] = (q * jnp.exp2(b)).astype(bf)
        kt_ref[N_LEVEL, rows, :] = (k * jnp.exp2(bl - b)).astype(bf)
    return jnp.exp2(b_last[0:1, :])


def _rec_heads(q_ref, i_ref, qt_ref, kt_ref, msk_ref, st_ref, o_ref, decay):
    bf = jnp.bfloat16
    top = N_LEVEL - 1
    for hd in range(HEADS):
        sl = slice(hd * HD, (hd + 1) * HD)
        a = _dot_nt(qt_ref[top, :, sl], kt_ref[top, :, sl])
        a = a + msk_ref[N_LEVEL] * _dot_nt(q_ref[:, sl], kt_ref[0, :, sl])
        for p in range(top):
            a = a + msk_ref[p] * _dot_nt(qt_ref[p, :, sl], kt_ref[p, :, sl])
        vh = i_ref[:, sl]
        st = st_ref[hd]
        o_ref[:, sl] = _dot(a.astype(bf), vh) + _dot_nt(qt_ref[N_LEVEL, :, sl], st.astype(bf))
        st_ref[hd] = st * decay[:, sl] + _dot_tn(vh, kt_ref[N_LEVEL, :, sl])


def _rec_kernel(blkf_ref, blkb_ref, sid_ref, first_ref, last_ref,
                qf_ref, if_ref, lff_ref, qb_ref, ib_ref, lfb_ref, s0_ref, mskf_ref, mskb_ref,
                of_ref, ob_ref, sfin_ref,
                st_ref, bf_ref, brf_ref, qtf_ref, ktf_ref, bb_ref, brb_ref, qtb_ref, ktb_ref):
    step = pl.program_id(0)

    @pl.when(step == 0)
    def _():
        for ref in (qtf_ref, ktf_ref, qtb_ref, ktb_ref):
            ref[...] = jnp.zeros_like(ref)

    @pl.when(first_ref[step] == 1)
    def _():
        st_ref[...] = jnp.zeros_like(st_ref)

    @pl.when(first_ref[step] == 2)
    def _():
        for d in range(2):
            for hd in range(HEADS):
                st_ref[d, hd] = s0_ref[d, hd].T

    decay_f = _rec_prepare(lff_ref, qf_ref, bf_ref, brf_ref, qtf_ref, ktf_ref, False)
    decay_b = _rec_prepare(lfb_ref, qb_ref, bb_ref, brb_ref, qtb_ref, ktb_ref, True)
    _rec_heads(qf_ref, if_ref, qtf_ref, ktf_ref, mskf_ref, st_ref.at[0], of_ref, decay_f)
    _rec_heads(qb_ref, ib_ref, qtb_ref, ktb_ref, mskb_ref, st_ref.at[1], ob_ref, decay_b)

    @pl.when(last_ref[step] == 1)
    def _():
        for d in range(2):
            for hd in range(HEADS):
                sfin_ref[d, hd] = st_ref[d, hd].T


def _rec_tables():
    blkf, blkb, sid, first, last = [], [], [], [], []
    seqs = [(b * (SEQ // CH), SEQ // CH) for b in range(BATCH)]
    seqs += [(N_CTX // CH + b * (DEC_SEQ // CH), DEC_SEQ // CH) for b in range(DEC_BATCH)]
    for n, (start, nc) in enumerate(seqs):
        ctx = n < BATCH
        for c in range(nc):
            blkf.append(start + c)
            blkb.append(start + nc - 1 - c)
            sid.append(n)
            first.append((1 if ctx else 2) if c == 0 else 0)
            last.append(int(ctx and c == nc - 1))
    return [jnp.asarray(np.asarray(a, np.int32)) for a in (blkf, blkb, sid, first, last)]


def _rec(q, i, lf, s0):
    row_f = pl.BlockSpec((CH, D), lambda s, bkf, bkb, sid, fi, la: (bkf[s], 0))
    row_b = pl.BlockSpec((CH, D), lambda s, bkf, bkb, sid, fi, la: (bkb[s], 0))
    st_shape = (None, 2, HEADS, HD, HD)
    scratch = [pltpu.VMEM((2, HEADS, HD, HD), jnp.float32)]
    for _ in range(2):
        scratch += [
            pltpu.VMEM((CH, D), jnp.float32),
            pltpu.VMEM((CH, D), jnp.float32),
            pltpu.VMEM((N_LEVEL + 1, CH, D), jnp.bfloat16),
            pltpu.VMEM((N_LEVEL + 1, CH, D), jnp.bfloat16),
        ]
    msk_spec = pl.BlockSpec((N_LEVEL + 1, CH, CH), lambda s, *_: (0, 0, 0))
    grid_spec = pltpu.PrefetchScalarGridSpec(
        num_scalar_prefetch=5,
        grid=(N_TOK // CH,),
        in_specs=[
            row_f, row_f,
            pl.BlockSpec((None, CH, D), lambda s, bkf, bkb, sid, fi, la: (0, bkf[s], 0)),
            row_b, row_b,
            pl.BlockSpec((None, CH, D), lambda s, bkf, bkb, sid, fi, la: (1, bkb[s], 0)),
            pl.BlockSpec(st_shape, lambda s, bkf, bkb, sid, fi, la: (jnp.maximum(sid[s] - BATCH, 0), 0, 0, 0, 0)),
            msk_spec, msk_spec,
        ],
        out_specs=[
            row_f, row_b,
            pl.BlockSpec(st_shape, lambda s, bkf, bkb, sid, fi, la: (jnp.minimum(sid[s], BATCH - 1), 0, 0, 0, 0)),
        ],
        scratch_shapes=scratch,
    )
    return pl.pallas_call(
        _rec_kernel,
        grid_spec=grid_spec,
        out_shape=[
            jax.ShapeDtypeStruct((N_TOK, D), jnp.float32),
            jax.ShapeDtypeStruct((N_TOK, D), jnp.float32),
            jax.ShapeDtypeStruct((BATCH, 2, HEADS, HD, HD), jnp.float32),
        ],
        compiler_params=_cparams(("arbitrary",)),
        name="hgrn_rec",
    )(*_rec_tables(), q, i, lf, q, i, lf, s0, jnp.asarray(_level_masks(False)), jnp.asarray(_level_masks(True)))


def _hgrn_out_kernel(x_ref, mod_ref, g_ref, of_ref, ob_ref, gt_ref, gn_ref, w_ref, o_ref, oh_ref):
    for hd in range(HEADS):
        sl = slice(hd * HD, (hd + 1) * HD)
        o = of_ref[:, sl] + ob_ref[:, sl]
        o = _rms(o, gn_ref[...]) * jax.nn.sigmoid(gt_ref[:, sl].astype(jnp.float32))
        oh_ref[:, sl] = o.astype(jnp.bfloat16)
    y = _dot(oh_ref[...], w_ref[...])
    o_ref[...] = _ada_post(x_ref[...], y, mod_ref, g_ref, 1, 3, 1.0)


def _hgrn_out(x, mod_l, g_l, o_f, o_b, gt, gnorm, w_out):
    row = pl.BlockSpec((TM, D), lambda i: (i, 0))
    return pl.pallas_call(
        _hgrn_out_kernel,
        grid=(N_TOK // TM,),
        in_specs=_tok_specs() + [row, row, row, _const_spec((1, HD)), _const_spec((D, D))],
        out_specs=row,
        out_shape=jax.ShapeDtypeStruct((N_TOK, D), jnp.float32),
        scratch_shapes=[pltpu.VMEM((TM, D), jnp.bfloat16)],
        compiler_params=_cparams(("parallel",)),
        name="hgrn_out",
    )(x, mod_l, g_l, o_f, o_b, gt, gnorm, w_out)


def _rope_tables():
    pos = np.arange(DEC_SEQ)
    row = (pos // GRID_W).astype(np.float64)
    col = (pos % GRID_W).astype(np.float64)
    nf = DA_DK // 4
    inv = ROPE_THETA ** (-np.arange(nf, dtype=np.float64) / nf)
    lane = np.arange(HD)
    axis = (lane % DA_DK) // (2 * nf)
    ang = np.where(axis[None, :] == 0, row[:, None], col[:, None]) * inv[lane % nf][None, :]
    sign = np.where((lane % (2 * nf)) < nf, -1.0, 1.0)
    cos, sin = np.cos(ang), np.sin(ang) * sign[None, :]
    return tuple(jnp.asarray(a, jnp.float32) for a in (cos, sin, cos.T, sin.T))


def kernel(x_prompt, x_sample, cache_k, cache_v, state_hgrn, c, c_ctx, w_mod, b_mod, norm_g,
           ffn_w_in, ffn_w_out, attn_w_in, attn_w_out, attn_lambda, attn_subln,
           hgrn_w_in, hgrn_w_out, hgrn_lower_bounds, hgrn_gnorm):
    bf = jnp.bfloat16
    x = (x_prompt.reshape(N_CTX, D), x_sample.reshape(N_SMP, D))
    cc = jnp.concatenate([c_ctx[None, :], c, jnp.zeros((N_GRP - 1 - DEC_BATCH, D), jnp.float32)], axis=0)
    mod = _modulation(cc, w_mod, b_mod).reshape(DEPTH, N_GRP, N_MOD, D)
    tables = _rope_tables()

    ks, vs, states = [], [], []
    for l in range(DEPTH):
        mod_l, g_l = mod[l], norm_g[l]
        x = _ffn(x, mod_l, g_l, ffn_w_in[l, 0].astype(bf), ffn_w_out[l, 0].astype(bf), 0)
        if l % 2 == 0:
            a = l // 2
            lam_init = 0.8 - 0.6 * math.exp(-0.3 * l)
            w_qv = jnp.concatenate([attn_w_in[a][:, :D], attn_w_in[a][:, 2 * D:]], axis=1).astype(bf)
            w_kt = attn_w_in[a][:, D:2 * D].T.astype(bf)
            w_out = attn_w_out[a].astype(bf)
            subln = attn_subln[a].reshape(1, HD)
            qc, kt_new, vc_new = _attn_pre(x, mod_l, g_l, w_qv, w_kt, tables, ctx=True)
            qs, kt_smp, v_smp = _attn_pre(x, mod_l, g_l, w_qv, w_kt, tables, ctx=False)
            kct = jnp.transpose(cache_k[:, a], (0, 2, 3, 4, 1)).reshape(DEC_BATCH, D, PAST)
            xc = _attn(x, mod_l, g_l, qc, kt_new, vc_new, None, None, attn_lambda[a], subln, w_out,
                       ctx=True, lam_init=lam_init)
            xs = _attn(x, mod_l, g_l, qs, kt_smp, v_smp, kct, cache_v[:, a].reshape(DEC_BATCH, PAST, D),
                       attn_lambda[a], subln, w_out, ctx=False, lam_init=lam_init)
            x = (xc, xs)
            ks.append(jnp.transpose(kt_new.reshape(BATCH, HEADS, 2, DA_DK, SEQ), (0, 4, 1, 2, 3)))
            vs.append(vc_new.reshape(BATCH, SEQ, HEADS, HD))
        else:
            r = l // 2
            q, i, gt, lf = _hgrn_pre(x, mod_l, g_l, hgrn_w_in[r].astype(bf), hgrn_lower_bounds, l)
            s0 = state_hgrn[:, r].astype(jnp.float32)
            o_f, o_b, s_fin = _rec(q, i, lf, s0)
            x = _hgrn_out(x, mod_l, g_l, o_f, o_b, gt, hgrn_gnorm[r].reshape(1, HD), hgrn_w_out[r].astype(bf))
            states.append(s_fin)
        if l < DEPTH - 1:
            x = _ffn(x, mod_l, g_l, ffn_w_in[l, 1].astype(bf), ffn_w_out[l, 1].astype(bf), 2)

    assert not isinstance(x, tuple)
    w_in, w_out = ffn_w_in[-1, 1].astype(bf), ffn_w_out[-1, 1].astype(bf)
    y_prompt = _ffn(x, mod_l, g_l, w_in, w_out, 2, base=0, rows=N_CTX)
    y_sample = _ffn(x, mod_l, g_l, w_in, w_out, 2, base=N_CTX, rows=N_SMP)
    return (y_prompt.reshape(BATCH, SEQ, D), y_sample.reshape(DEC_BATCH, DEC_SEQ, D),
            jnp.stack(ks, axis=1), jnp.stack(vs, axis=1), jnp.stack(states, axis=1))
```

```python
import functools
import math

import numpy as np
import jax
import jax.numpy as jnp
from jax import lax
from jax.experimental import pallas as pl
from jax.experimental.pallas import tpu as pltpu

D = 1024
BATCH = 16
SEQ = 256
DEPTH = 2
DEC_BATCH = 4
DEC_SEQ = 2048
PAST = 256
GRID_W = 64
N_MOD = 9
EPS = 1e-6
HEADS = 8
HD = 128
DA_DK = 64
ROPE_THETA = 10000.0
FFN_H = 2816
LOG2E = 1.4426950408889634

N_CTX = BATCH * SEQ
N_SMP = DEC_BATCH * DEC_SEQ
N_TOK = N_CTX + N_SMP
N_GRP = 8

VMEM_LIMIT = 56 * 1024 * 1024
LANES = 128
SUB = 8

TM = 512
FFN_HC = 256
TQ = 256
CH = 128
N_SLAB = CH // SUB
N_LEVEL = 7
MAX_FAST_LOG2_DECAY = 200.0


def _cparams(sem):
    return pltpu.CompilerParams(dimension_semantics=sem, vmem_limit_bytes=VMEM_LIMIT)


def _grp_of_block(i, rows):
    nctx = N_CTX // rows
    per = DEC_SEQ // rows
    return jnp.where(i < nctx, 0, 1 + (i - nctx) // per)


def _rms(x, g):
    return (x * lax.rsqrt(jnp.mean(x * x, axis=-1, keepdims=True) + EPS)) * g


def _dot(a, b):
    return jnp.dot(a, b, preferred_element_type=jnp.float32)


def _dot_nt(a, b):
    return lax.dot_general(a, b, (((1,), (1,)), ((), ())), preferred_element_type=jnp.float32)


def _dot_tn(a, b):
    return lax.dot_general(a, b, (((0,), (0,)), ((), ())), preferred_element_type=jnp.float32)


def _const_spec(shape):
    nd = len(shape)
    return pl.BlockSpec(shape, lambda *_: (0,) * nd, pipeline_mode=pl.Buffered(1))


def _mod_kernel(c_ref, w_ref, b_ref, o_ref):
    c = c_ref[...]
    s = (c * jax.nn.sigmoid(c)).astype(jnp.bfloat16)
    o_ref[...] = _dot(s, w_ref[...].astype(jnp.bfloat16)) + b_ref[...]


def _modulation(cc, w_mod, b_mod):
    tn = 1024
    nt = (N_MOD * D) // tn
    return pl.pallas_call(
        _mod_kernel,
        grid=(DEPTH, nt),
        in_specs=[
            pl.BlockSpec((N_GRP, D), lambda l, j: (0, 0)),
            pl.BlockSpec((None, D, tn), lambda l, j: (l, 0, j)),
            pl.BlockSpec((None, 1, tn), lambda l, j: (l, 0, j)),
        ],
        out_specs=pl.BlockSpec((None, N_GRP, tn), lambda l, j: (l, 0, j)),
        out_shape=jax.ShapeDtypeStruct((DEPTH, N_GRP, N_MOD * D), jnp.float32),
        compiler_params=_cparams(("parallel", "parallel")),
        name="modulation",
    )(cc, w_mod, b_mod.reshape(DEPTH, 1, N_MOD * D))


def _ada_pre(x, mod_ref, g_ref, j, gi):
    shift = mod_ref[3 * j:3 * j + 1, :]
    scale = mod_ref[3 * j + 1:3 * j + 2, :]
    return _rms(x, g_ref[gi:gi + 1, :]) * (1.0 + scale) + shift


def _ada_post(x, y, mod_ref, g_ref, j, gi, res_w):
    gate = mod_ref[3 * j + 2:3 * j + 3, :]
    return x + res_w * gate * _rms(y, g_ref[gi:gi + 1, :])


def _tok_specs():
    return [
        pl.BlockSpec((TM, D), lambda i: (i, 0)),
        pl.BlockSpec((None, N_MOD, D), lambda i: (_grp_of_block(i, TM), 0, 0)),
        _const_spec((6, D)),
    ]


def _ffn_kernel(*refs, j, pair):
    if pair:
        xa_ref, xb_ref, mod_ref, g_ref, win_ref, wout_ref, o_ref, acc_ref = refs
        x = jnp.where(pl.program_id(0) < N_CTX // TM, xa_ref[...], xb_ref[...])
    else:
        x_ref, mod_ref, g_ref, win_ref, wout_ref, o_ref, acc_ref = refs
        x = x_ref[...]
    h = _ada_pre(x, mod_ref, g_ref, j, 2 * j).astype(jnp.bfloat16)
    for c in range(FFN_H // FFN_HC):
        lo = c * FFN_HC
        a = _dot(h, win_ref[:, lo:lo + FFN_HC])
        b = _dot(h, win_ref[:, FFN_H + lo:FFN_H + lo + FFN_HC])
        u = ((a * jax.nn.sigmoid(a)) * b).astype(jnp.bfloat16)
        part = _dot(u, wout_ref[lo:lo + FFN_HC, :])
        if c == 0:
            acc_ref[...] = part
        else:
            acc_ref[...] += part
    o_ref[...] = _ada_post(x, acc_ref[...], mod_ref, g_ref, j, 2 * j + 1, 0.5)


def _ffn(x, mod_l, g_l, w_in, w_out, j, *, base=0, rows=N_TOK):
    pair = isinstance(x, tuple)
    nctx = N_CTX // TM
    b0 = base // TM
    if pair:
        assert base == 0 and rows == N_TOK
        x_specs = [pl.BlockSpec((TM, D), lambda i: (jnp.minimum(i, nctx - 1), 0)),
                   pl.BlockSpec((TM, D), lambda i: (jnp.maximum(i - nctx, 0), 0))]
        x_args = list(x)
    else:
        x_specs = [pl.BlockSpec((TM, D), lambda i: (i + b0, 0))]
        x_args = [x]
    return pl.pallas_call(
        functools.partial(_ffn_kernel, j=j, pair=pair),
        grid=(rows // TM,),
        in_specs=x_specs + [
            pl.BlockSpec((None, N_MOD, D), lambda i: (_grp_of_block(i + b0, TM), 0, 0)),
            _const_spec((6, D)), _const_spec((D, 2 * FFN_H)), _const_spec((FFN_H, D))],
        out_specs=pl.BlockSpec((TM, D), lambda i: (i, 0)),
        out_shape=jax.ShapeDtypeStruct((rows, D), jnp.float32),
        scratch_shapes=[pltpu.VMEM((TM, D), jnp.float32)],
        compiler_params=_cparams(("parallel",)),
        name=f"ffn{j}",
    )(*x_args, mod_l, g_l, w_in, w_out)


def _rope_lanes(x, cos, sin_signed):
    lane = lax.broadcasted_iota(jnp.int32, x.shape, 1)
    partner = jnp.where((lane % 32) < 16, pltpu.roll(x, LANES - 16, 1), pltpu.roll(x, 16, 1))
    return x * cos + partner * sin_signed


def _rope_rows(x, cos_t, sin_signed_t):
    parts = []
    for g in range(HD // 32):
        parts += [x[g * 32 + 16:g * 32 + 32, :], x[g * 32:g * 32 + 16, :]]
    return x * cos_t + jnp.concatenate(parts, axis=0) * sin_signed_t


def _attn_pre_kernel(x_ref, mod_ref, g_ref, wqv_ref, wkt_ref, cos_ref, sin_ref, cost_ref, sint_ref,
                     q_ref, kt_ref, v_ref, *, rope):
    x = x_ref[...]
    h = _ada_pre(x, mod_ref, g_ref, 1, 2).astype(jnp.bfloat16)
    z = _dot(h, wqv_ref[...])
    kt = _dot_nt(wkt_ref[...], h)
    qscale = DA_DK ** -0.5 * LOG2E
    for hd in range(HEADS):
        sl = slice(hd * HD, (hd + 1) * HD)
        q = z[:, hd * HD:(hd + 1) * HD]
        k = kt[hd * HD:(hd + 1) * HD, :]
        if rope:
            q = _rope_lanes(q, cos_ref[...], sin_ref[...])
            k = _rope_rows(k, cost_ref[...], sint_ref[...])
        q_ref[:, sl] = (q * qscale).astype(q_ref.dtype)
        kt_ref[sl, :] = k.astype(kt_ref.dtype)
    v_ref[...] = z[:, D:].astype(v_ref.dtype)


def _attn_pre(x, mod_l, g_l, w_qv, w_kt, tables, *, ctx):
    nb, seq = (BATCH, SEQ) if ctx else (DEC_BATCH, DEC_SEQ)
    t = min(TM, seq)
    rows = nb * seq
    base = 0 if ctx else N_CTX // t
    per = seq // t
    ptab = DEC_SEQ // t
    kv_dtype = jnp.float32 if ctx else jnp.bfloat16
    out_spec = pl.BlockSpec((t, D), lambda i: (i, 0))
    return pl.pallas_call(
        functools.partial(_attn_pre_kernel, rope=not ctx),
        grid=(rows // t,),
        in_specs=[
            pl.BlockSpec((t, D), lambda i: (i + base, 0)),
            pl.BlockSpec((None, N_MOD, D), lambda i: (_grp_of_block(i + base, t), 0, 0)),
            _const_spec((6, D)), _const_spec((D, 2 * D)), _const_spec((D, D)),
            pl.BlockSpec((t, HD), lambda i: (i % ptab, 0)),
            pl.BlockSpec((t, HD), lambda i: (i % ptab, 0)),
            pl.BlockSpec((HD, t), lambda i: (0, i % ptab)),
            pl.BlockSpec((HD, t), lambda i: (0, i % ptab)),
        ],
        out_specs=[out_spec, pl.BlockSpec((None, D, t), lambda i: (i // per, 0, i % per)), out_spec],
        out_shape=[
            jax.ShapeDtypeStruct((rows, D), jnp.bfloat16),
            jax.ShapeDtypeStruct((nb, D, seq), kv_dtype),
            jax.ShapeDtypeStruct((rows, D), kv_dtype),
        ],
        compiler_params=_cparams(("parallel",)),
        name="attn_pre_ctx" if ctx else "attn_pre_smp",
    )(x, mod_l, g_l, w_qv, w_kt, *tables)


def _attn_kernel(*refs, lam_init, cache):
    if cache:
        (x_ref, mod_ref, g_ref, q_ref, k_ref, v_ref, kc_ref, vc_ref,
         lam_ref, sub_ref, w_ref, o_ref, oh_ref) = refs
    else:
        (x_ref, mod_ref, g_ref, q_ref, k_ref, v_ref,
         lam_ref, sub_ref, w_ref, o_ref, oh_ref) = refs
    lp = lam_ref[...]
    lam = (jnp.exp(jnp.sum(lp[0:1] * lp[1:2], axis=-1, keepdims=True))
           - jnp.exp(jnp.sum(lp[2:3] * lp[3:4], axis=-1, keepdims=True)) + lam_init)
    lane = lax.broadcasted_iota(jnp.int32, (q_ref.shape[0], HD), 1)
    bf = jnp.bfloat16
    for hd in range(HEADS):
        sl = slice(hd * HD, (hd + 1) * HD)
        qh = q_ref[:, sl]
        qs = (jnp.where(lane < DA_DK, qh, jnp.zeros_like(qh)), jnp.where(lane >= DA_DK, qh, jnp.zeros_like(qh)))
        kh = k_ref[sl, :].astype(bf)
        vh = v_ref[:, sl].astype(bf)
        if cache:
            kch = kc_ref[sl, :].astype(bf)
            vch = vc_ref[:, sl].astype(bf)
        es, ecs, invs = [], [], []
        for comp in range(2):
            s = _dot(qs[comp], kh)
            m = jnp.max(s, axis=-1, keepdims=True)
            if cache:
                sc = _dot(qs[comp], kch)
                m = jnp.maximum(m, jnp.max(sc, axis=-1, keepdims=True))
                ec = jnp.exp2(sc - m)
            e = jnp.exp2(s - m)
            tot = jnp.sum(e, axis=-1, keepdims=True)
            if cache:
                tot = tot + jnp.sum(ec, axis=-1, keepdims=True)
                ecs.append(ec)
            es.append(e)
            invs.append(1.0 / tot)
        w2 = lam * invs[1]
        o = _dot((es[0] * invs[0] - es[1] * w2).astype(bf), vh)
        if cache:
            o = o + _dot((ecs[0] * invs[0] - ecs[1] * w2).astype(bf), vch)
        o = _rms(o, sub_ref[...]) * (1.0 - lam_init)
        oh_ref[:, sl] = o.astype(bf)
    y = _dot(oh_ref[...], w_ref[...])
    o_ref[...] = _ada_post(x_ref[...], y, mod_ref, g_ref, 1, 3, 1.0)


def _attn(x, mod_l, g_l, q, k, v, kc, vc, lam_p, subln, w_out, *, ctx, lam_init):
    nb, seq = (BATCH, SEQ) if ctx else (DEC_BATCH, DEC_SEQ)
    rows = nb * seq
    base = 0 if ctx else N_CTX // TQ
    nq = seq // TQ
    row_spec = pl.BlockSpec((TQ, D), lambda b, i: (b * nq + i, 0))
    in_specs = [
        pl.BlockSpec((TQ, D), lambda b, i: (base + b * nq + i, 0)),
        pl.BlockSpec((None, N_MOD, D), lambda b, i: (_grp_of_block(base + b * nq + i, TQ), 0, 0)),
        pl.BlockSpec((6, D), lambda b, i: (0, 0)),
        row_spec,
        pl.BlockSpec((None, D, seq), lambda b, i: (b, 0, 0)),
        pl.BlockSpec((seq, D), lambda b, i: (b, 0)),
    ]
    args = [x, mod_l, g_l, q, k, v]
    if not ctx:
        in_specs += [pl.BlockSpec((None, D, PAST), lambda b, i: (b, 0, 0)),
                     pl.BlockSpec((None, PAST, D), lambda b, i: (b, 0, 0))]
        args += [kc, vc]
    in_specs += [
        pl.BlockSpec((4, DA_DK), lambda b, i: (0, 0)),
        pl.BlockSpec((1, HD), lambda b, i: (0, 0)),
        pl.BlockSpec((D, D), lambda b, i: (0, 0), pipeline_mode=pl.Buffered(1)),
    ]
    args += [lam_p, subln, w_out]
    return pl.pallas_call(
        functools.partial(_attn_kernel, lam_init=lam_init, cache=not ctx),
        grid=(nb, nq),
        in_specs=in_specs,
        out_specs=row_spec,
        out_shape=jax.ShapeDtypeStruct((rows, D), jnp.float32),
        scratch_shapes=[pltpu.VMEM((TQ, D), jnp.bfloat16)],
        compiler_params=_cparams(("parallel", "parallel")),
        name="attn_ctx" if ctx else "attn_smp",
    )(*args)


def _hgrn_pre_kernel(x_ref, mod_ref, g_ref, w_ref, lb_ref, q_ref, i_ref, gt_ref, lf_ref, *, layer):
    x = x_ref[...]
    h = _ada_pre(x, mod_ref, g_ref, 1, 2).astype(jnp.bfloat16)
    z = _dot(h, w_ref[...])
    q_ref[...] = z[:, 0:D].astype(q_ref.dtype)
    i_ref[...] = z[:, 3 * D:4 * D].astype(i_ref.dtype)
    gt_ref[...] = z[:, 4 * D:5 * D].astype(gt_ref.dtype)
    for d in range(2):
        raw = [lb_ref[l, d:d + 1, :] for l in range(DEPTH)]
        m = functools.reduce(jnp.maximum, raw)
        ex = [jnp.exp(r - m) for r in raw]
        tot = functools.reduce(lambda a, b: a + b, ex)
        soft = [e / tot for e in ex]
        lb = functools.reduce(lambda a, b: a + b, soft[:layer + 1]) - soft[0]
        ff = z[:, (1 + d) * D:(2 + d) * D]
        f = lb + (1.0 - lb) * jax.nn.sigmoid(ff)
        lf_ref[d] = jnp.log(f)


def _hgrn_pre(x, mod_l, g_l, w_in, lb_raw, layer):
    row = pl.BlockSpec((TM, D), lambda i: (i, 0))
    return pl.pallas_call(
        functools.partial(_hgrn_pre_kernel, layer=layer),
        grid=(N_TOK // TM,),
        in_specs=_tok_specs() + [_const_spec((D, 5 * D)), _const_spec((DEPTH, 2, D))],
        out_specs=[row, row, row, pl.BlockSpec((2, TM, D), lambda i: (0, i, 0))],
        out_shape=[
            jax.ShapeDtypeStruct((N_TOK, D), jnp.bfloat16),
            jax.ShapeDtypeStruct((N_TOK, D), jnp.bfloat16),
            jax.ShapeDtypeStruct((N_TOK, D), jnp.bfloat16),
            jax.ShapeDtypeStruct((2, N_TOK, D), jnp.float32),
        ],
        compiler_params=_cparams(("parallel",)),
        name="hgrn_pre",
    )(x, mod_l, g_l, w_in, lb_raw)


def _level_masks(rev):
    t = np.arange(CH)[:, None]
    s = np.arange(CH)[None, :]
    out = []
    for p in range(N_LEVEL):
        same = (t >> (p + 1)) == (s >> (p + 1))
        tb, sb = (t >> p) & 1, (s >> p) & 1
        out.append(same & ((tb == 0) & (sb == 1) if rev else (tb == 1) & (sb == 0)))
    out.append(t == s)
    return np.stack(out).astype(np.float32)


def _row_bcast(x, r):
    return jnp.broadcast_to(x[r:r + 1, :], x.shape)


def _rec_cumsum(lf_ref, b_ref, br_ref, rev):
    sub = lax.broadcasted_iota(jnp.int32, (SUB, D), 0)
    order = range(N_SLAB - 1, -1, -1) if rev else range(N_SLAB)
    edge = 0 if rev else SUB - 1
    carry = None
    for v in order:
        rows = slice(v * SUB, (v + 1) * SUB)
        c = lf_ref[rows, :] * LOG2E
        for k in (1, 2, 4):
            if rev:
                c = c + jnp.where(sub < SUB - k, pltpu.roll(c, SUB - k, 0), 0.0)
            else:
                c = c + jnp.where(sub >= k, pltpu.roll(c, k, 0), 0.0)
        if carry is not None:
            c = c + carry
        carry = _row_bcast(c, edge)
        b_ref[rows, :] = c
        br_ref[rows, :] = carry
    return carry


def _rec_fast(lf_ref, q_ref, i_ref, b_ref, b_last, qs_ref, ks_ref, st_ref, o_ref, rev):
    bf = jnp.bfloat16
    half = 0.5 * b_last
    half2 = jnp.concatenate([half, half], axis=0)
    for v2 in range(N_SLAB // 2):
        rows = slice(v2 * 2 * SUB, (v2 + 1) * 2 * SUB)
        k = 1.0 - jnp.exp2(lf_ref[rows, :] * LOG2E)
        d = b_ref[rows, :] - half2
        qs_ref[rows, :] = (q_ref[rows, :].astype(jnp.float32) * jnp.exp2(d)).astype(bf)
        ks_ref[rows, :] = (k * jnp.exp2(-d)).astype(bf)
    scale = jnp.exp2(half[0:1, :])
    decay = scale * scale
    t = lax.broadcasted_iota(jnp.int32, (CH, CH), 0)
    s = lax.broadcasted_iota(jnp.int32, (CH, CH), 1)
    seen = (s >= t) if rev else (s <= t)
    for hd in range(HEADS):
        sl = slice(hd * HD, (hd + 1) * HD)
        qs, ks, vh = qs_ref[:, sl], ks_ref[:, sl], i_ref[:, sl]
        a = jnp.where(seen, _dot_nt(qs, ks), 0.0)
        st = st_ref[hd]
        o_ref[:, sl] = _dot(a.astype(bf), vh) + _dot_nt(qs, (st * scale[:, sl]).astype(bf))
        st_ref[hd] = st * decay[:, sl] + _dot_tn(vh, ks) * scale[:, sl]


def _rec_prepare(lf_ref, q_ref, b_ref, br_ref, b_last, qt_ref, kt_ref, rev):
    bf = jnp.bfloat16
    sub = lax.broadcasted_iota(jnp.int32, (SUB, D), 0)
    zeros = jnp.zeros((SUB, D), jnp.float32)
    for v2 in range(N_SLAB // 2):
        rows = slice(v2 * 2 * SUB, (v2 + 1) * 2 * SUB)
        f = jnp.exp2(lf_ref[rows, :] * LOG2E)
        k = 1.0 - f
        q = q_ref[rows, :].astype(jnp.float32)
        b = b_ref[rows, :]
        qt_ref[0, rows, :] = (q * f).astype(bf)
        kt_ref[0, rows, :] = k.astype(bf)
        half = (slice(0, SUB), slice(SUB, 2 * SUB))
        for p in range(1, N_LEVEL):
            if p >= 3:
                w = p - 3
                es, qside = [], []
                for n, v in enumerate((2 * v2, 2 * v2 + 1)):
                    hi = v & ~((1 << (w + 1)) - 1)
                    vr = hi | (1 << w) if rev else hi | ((1 << w) - 1)
                    r = br_ref[vr * SUB:(vr + 1) * SUB, :]
                    qside.append(((v >> w) & 1) == (0 if rev else 1))
                    es.append(jnp.exp2(b[half[n], :] - r) if qside[n] else jnp.exp2(r - b[half[n], :]))
                if qside[0] == qside[1]:
                    e = jnp.concatenate(es, axis=0)
                    if qside[0]:
                        qt_ref[p, rows, :] = (q * e).astype(bf)
                    else:
                        kt_ref[p, rows, :] = (k * e).astype(bf)
                else:
                    qe = [q[half[n], :] * es[n] if qside[n] else zeros for n in range(2)]
                    ke = [zeros if qside[n] else k[half[n], :] * es[n] for n in range(2)]
                    qt_ref[p, rows, :] = jnp.concatenate(qe, axis=0).astype(bf)
                    kt_ref[p, rows, :] = jnp.concatenate(ke, axis=0).astype(bf)
            else:
                es = []
                for n in range(2):
                    bv = b[half[n], :]
                    if p == 2:
                        r = _row_bcast(bv, 4 if rev else 3)
                    else:
                        lo, hi_r = (2, 6) if rev else (1, 5)
                        r = jnp.where(sub < 4, _row_bcast(bv, lo), _row_bcast(bv, hi_r))
                    es.append(jnp.exp2(-jnp.abs(bv - r)))
                e = jnp.concatenate(es, axis=0)
                qt_ref[p, rows, :] = (q * e).astype(bf)
                kt_ref[p, rows, :] = (k * e).astype(bf)
        bl = jnp.concatenate([b_last, b_last], axis=0)
        qt_ref[N_LEVEL, rows, :] = (q * jnp.exp2(b)).astype(bf)
        kt_ref[N_LEVEL, rows, :] = (k * jnp.exp2(bl - b)).astype(bf)
    return jnp.exp2(b_last[0:1, :])


def _rec_heads(q_ref, i_ref, qt_ref, kt_ref, msk_ref, st_ref, o_ref, decay):
    bf = jnp.bfloat16
    top = N_LEVEL - 1
    for hd in range(HEADS):
        sl = slice(hd * HD, (hd + 1) * HD)
        a = _dot_nt(qt_ref[top, :, sl], kt_ref[top, :, sl])
        a = a + msk_ref[N_LEVEL] * _dot_nt(q_ref[:, sl], kt_ref[0, :, sl])
        for p in range(top):
            a = a + msk_ref[p] * _dot_nt(qt_ref[p, :, sl], kt_ref[p, :, sl])
        vh = i_ref[:, sl]
        st = st_ref[hd]
        o_ref[:, sl] = _dot(a.astype(bf), vh) + _dot_nt(qt_ref[N_LEVEL, :, sl], st.astype(bf))
        st_ref[hd] = st * decay[:, sl] + _dot_tn(vh, kt_ref[N_LEVEL, :, sl])


def _rec_kernel(blkf_ref, blkb_ref, sid_ref, first_ref, last_ref,
                qf_ref, if_ref, lff_ref, qb_ref, ib_ref, lfb_ref, s0_ref, mskf_ref, mskb_ref,
                of_ref, ob_ref, sfin_ref,
                st_ref, bf_ref, brf_ref, qtf_ref, ktf_ref, bb_ref, brb_ref, qtb_ref, ktb_ref):
    step = pl.program_id(0)

    @pl.when(step == 0)
    def _():
        for ref in (qtf_ref, ktf_ref, qtb_ref, ktb_ref):
            ref[...] = jnp.zeros_like(ref)

    @pl.when(first_ref[step] == 1)
    def _():
        st_ref[...] = jnp.zeros_like(st_ref)

    @pl.when(first_ref[step] == 2)
    def _():
        for d in range(2):
            for hd in range(HEADS):
                st_ref[d, hd] = s0_ref[d, hd].T

    last_f = _rec_cumsum(lff_ref, bf_ref, brf_ref, False)
    last_b = _rec_cumsum(lfb_ref, bb_ref, brb_ref, True)
    moderate = jnp.min(jnp.minimum(last_f, last_b)) >= -MAX_FAST_LOG2_DECAY

    @pl.when(moderate)
    def _():
        _rec_fast(lff_ref, qf_ref, if_ref, bf_ref, last_f, qtf_ref.at[N_LEVEL], ktf_ref.at[N_LEVEL],
                  st_ref.at[0], of_ref, False)
        _rec_fast(lfb_ref, qb_ref, ib_ref, bb_ref, last_b, qtb_ref.at[N_LEVEL], ktb_ref.at[N_LEVEL],
                  st_ref.at[1], ob_ref, True)

    @pl.when(jnp.logical_not(moderate))
    def _():
        decay_f = _rec_prepare(lff_ref, qf_ref, bf_ref, brf_ref, last_f, qtf_ref, ktf_ref, False)
        decay_b = _rec_prepare(lfb_ref, qb_ref, bb_ref, brb_ref, last_b, qtb_ref, ktb_ref, True)
        _rec_heads(qf_ref, if_ref, qtf_ref, ktf_ref, mskf_ref, st_ref.at[0], of_ref, decay_f)
        _rec_heads(qb_ref, ib_ref, qtb_ref, ktb_ref, mskb_ref, st_ref.at[1], ob_ref, decay_b)

    @pl.when(last_ref[step] == 1)
    def _():
        for d in range(2):
            for hd in range(HEADS):
                sfin_ref[d, hd] = st_ref[d, hd].T


def _rec_tables():
    blkf, blkb, sid, first, last = [], [], [], [], []
    seqs = [(b * (SEQ // CH), SEQ // CH) for b in range(BATCH)]
    seqs += [(N_CTX // CH + b * (DEC_SEQ // CH), DEC_SEQ // CH) for b in range(DEC_BATCH)]
    for n, (start, nc) in enumerate(seqs):
        ctx = n < BATCH
        for c in range(nc):
            blkf.append(start + c)
            blkb.append(start + nc - 1 - c)
            sid.append(n)
            first.append((1 if ctx else 2) if c == 0 else 0)
            last.append(int(ctx and c == nc - 1))
    return [jnp.asarray(np.asarray(a, np.int32)) for a in (blkf, blkb, sid, first, last)]


def _rec(q, i, lf, s0):
    row_f = pl.BlockSpec((CH, D), lambda s, bkf, bkb, sid, fi, la: (bkf[s], 0))
    row_b = pl.BlockSpec((CH, D), lambda s, bkf, bkb, sid, fi, la: (bkb[s], 0))
    st_shape = (None, 2, HEADS, HD, HD)
    scratch = [pltpu.VMEM((2, HEADS, HD, HD), jnp.float32)]
    for _ in range(2):
        scratch += [
            pltpu.VMEM((CH, D), jnp.float32),
            pltpu.VMEM((CH, D), jnp.float32),
            pltpu.VMEM((N_LEVEL + 1, CH, D), jnp.bfloat16),
            pltpu.VMEM((N_LEVEL + 1, CH, D), jnp.bfloat16),
        ]
    msk_spec = pl.BlockSpec((N_LEVEL + 1, CH, CH), lambda s, *_: (0, 0, 0))
    grid_spec = pltpu.PrefetchScalarGridSpec(
        num_scalar_prefetch=5,
        grid=(N_TOK // CH,),
        in_specs=[
            row_f, row_f,
            pl.BlockSpec((None, CH, D), lambda s, bkf, bkb, sid, fi, la: (0, bkf[s], 0)),
            row_b, row_b,
            pl.BlockSpec((None, CH, D), lambda s, bkf, bkb, sid, fi, la: (1, bkb[s], 0)),
            pl.BlockSpec(st_shape, lambda s, bkf, bkb, sid, fi, la: (jnp.maximum(sid[s] - BATCH, 0), 0, 0, 0, 0)),
            msk_spec, msk_spec,
        ],
        out_specs=[
            row_f, row_b,
            pl.BlockSpec(st_shape, lambda s, bkf, bkb, sid, fi, la: (jnp.minimum(sid[s], BATCH - 1), 0, 0, 0, 0)),
        ],
        scratch_shapes=scratch,
    )
    return pl.pallas_call(
        _rec_kernel,
        grid_spec=grid_spec,
        out_shape=[
            jax.ShapeDtypeStruct((N_TOK, D), jnp.float32),
            jax.ShapeDtypeStruct((N_TOK, D), jnp.float32),
            jax.ShapeDtypeStruct((BATCH, 2, HEADS, HD, HD), jnp.float32),
        ],
        compiler_params=_cparams(("arbitrary",)),
        name="hgrn_rec",
    )(*_rec_tables(), q, i, lf, q, i, lf, s0, jnp.asarray(_level_masks(False)), jnp.asarray(_level_masks(True)))


def _hgrn_out_kernel(x_ref, mod_ref, g_ref, of_ref, ob_ref, gt_ref, gn_ref, w_ref, o_ref, oh_ref):
    for hd in range(HEADS):
        sl = slice(hd * HD, (hd + 1) * HD)
        o = of_ref[:, sl] + ob_ref[:, sl]
        o = _rms(o, gn_ref[...]) * jax.nn.sigmoid(gt_ref[:, sl].astype(jnp.float32))
        oh_ref[:, sl] = o.astype(jnp.bfloat16)
    y = _dot(oh_ref[...], w_ref[...])
    o_ref[...] = _ada_post(x_ref[...], y, mod_ref, g_ref, 1, 3, 1.0)


def _hgrn_out(x, mod_l, g_l, o_f, o_b, gt, gnorm, w_out):
    row = pl.BlockSpec((TM, D), lambda i: (i, 0))
    return pl.pallas_call(
        _hgrn_out_kernel,
        grid=(N_TOK // TM,),
        in_specs=_tok_specs() + [row, row, row, _const_spec((1, HD)), _const_spec((D, D))],
        out_specs=row,
        out_shape=jax.ShapeDtypeStruct((N_TOK, D), jnp.float32),
        scratch_shapes=[pltpu.VMEM((TM, D), jnp.bfloat16)],
        compiler_params=_cparams(("parallel",)),
        name="hgrn_out",
    )(x, mod_l, g_l, o_f, o_b, gt, gnorm, w_out)


def _rope_tables():
    pos = np.arange(DEC_SEQ)
    row = (pos // GRID_W).astype(np.float64)
    col = (pos % GRID_W).astype(np.float64)
    nf = DA_DK // 4
    inv = ROPE_THETA ** (-np.arange(nf, dtype=np.float64) / nf)
    lane = np.arange(HD)
    axis = (lane % DA_DK) // (2 * nf)
    ang = np.where(axis[None, :] == 0, row[:, None], col[:, None]) * inv[lane % nf][None, :]
    sign = np.where((lane % (2 * nf)) < nf, -1.0, 1.0)
    cos, sin = np.cos(ang), np.sin(ang) * sign[None, :]
    return tuple(jnp.asarray(a, jnp.float32) for a in (cos, sin, cos.T, sin.T))


def kernel(x_prompt, x_sample, cache_k, cache_v, state_hgrn, c, c_ctx, w_mod, b_mod, norm_g,
           ffn_w_in, ffn_w_out, attn_w_in, attn_w_out, attn_lambda, attn_subln,
           hgrn_w_in, hgrn_w_out, hgrn_lower_bounds, hgrn_gnorm):
    bf = jnp.bfloat16
    x = (x_prompt.reshape(N_CTX, D), x_sample.reshape(N_SMP, D))
    cc = jnp.concatenate([c_ctx[None, :], c, jnp.zeros((N_GRP - 1 - DEC_BATCH, D), jnp.float32)], axis=0)
    mod = _modulation(cc, w_mod, b_mod).reshape(DEPTH, N_GRP, N_MOD, D)
    tables = _rope_tables()

    ks, vs, states = [], [], []
    for l in range(DEPTH):
        mod_l, g_l = mod[l], norm_g[l]
        x = _ffn(x, mod_l, g_l, ffn_w_in[l, 0].astype(bf), ffn_w_out[l, 0].astype(bf), 0)
        if l % 2 == 0:
            a = l // 2
            lam_init = 0.8 - 0.6 * math.exp(-0.3 * l)
            w_qv = jnp.concatenate([attn_w_in[a][:, :D], attn_w_in[a][:, 2 * D:]], axis=1).astype(bf)
            w_kt = attn_w_in[a][:, D:2 * D].T.astype(bf)
            w_out = attn_w_out[a].astype(bf)
            subln = attn_subln[a].reshape(1, HD)
            qc, kt_new, vc_new = _attn_pre(x, mod_l, g_l, w_qv, w_kt, tables, ctx=True)
            qs, kt_smp, v_smp = _attn_pre(x, mod_l, g_l, w_qv, w_kt, tables, ctx=False)
            kct = jnp.transpose(cache_k[:, a], (0, 2, 3, 4, 1)).reshape(DEC_BATCH, D, PAST)
            xc = _attn(x, mod_l, g_l, qc, kt_new, vc_new, None, None, attn_lambda[a], subln, w_out,
                       ctx=True, lam_init=lam_init)
            xs = _attn(x, mod_l, g_l, qs, kt_smp, v_smp, kct, cache_v[:, a].reshape(DEC_BATCH, PAST, D),
                       attn_lambda[a], subln, w_out, ctx=False, lam_init=lam_init)
            x = (xc, xs)
            ks.append(jnp.transpose(kt_new.reshape(BATCH, HEADS, 2, DA_DK, SEQ), (0, 4, 1, 2, 3)))
            vs.append(vc_new.reshape(BATCH, SEQ, HEADS, HD))
        else:
            r = l // 2
            q, i, gt, lf = _hgrn_pre(x, mod_l, g_l, hgrn_w_in[r].astype(bf), hgrn_lower_bounds, l)
            s0 = state_hgrn[:, r].astype(jnp.float32)
            o_f, o_b, s_fin = _rec(q, i, lf, s0)
            x = _hgrn_out(x, mod_l, g_l, o_f, o_b, gt, hgrn_gnorm[r].reshape(1, HD), hgrn_w_out[r].astype(bf))
            states.append(s_fin)
        if l < DEPTH - 1:
            x = _ffn(x, mod_l, g_l, ffn_w_in[l, 1].astype(bf), ffn_w_out[l, 1].astype(bf), 2)

    assert not isinstance(x, tuple)
    w_in, w_out = ffn_w_in[-1, 1].astype(bf), ffn_w_out[-1, 1].astype(bf)
    y_prompt = _ffn(x, mod_l, g_l, w_in, w_out, 2, base=0, rows=N_CTX)
    y_sample = _ffn(x, mod_l, g_l, w_in, w_out, 2, base=N_CTX, rows=N_SMP)
    return (y_prompt.reshape(BATCH, SEQ, D), y_sample.reshape(DEC_BATCH, DEC_SEQ, D),
            jnp.stack(ks, axis=1), jnp.stack(vs, axis=1), jnp.stack(states, axis=1))
```

```python
import functools
import math

import numpy as np
import jax
import jax.numpy as jnp
from jax import lax
from jax.experimental import pallas as pl
from jax.experimental.pallas import tpu as pltpu

D = 1024
BATCH = 16
SEQ = 256
DEPTH = 2
DEC_BATCH = 4
DEC_SEQ = 2048
PAST = 256
GRID_W = 64
N_MOD = 9
EPS = 1e-6
HEADS = 8
HD = 128
DA_DK = 64
ROPE_THETA = 10000.0
FFN_H = 2816
LOG2E = 1.4426950408889634

N_CTX = BATCH * SEQ
N_SMP = DEC_BATCH * DEC_SEQ
N_TOK = N_CTX + N_SMP
N_GRP = 8

VMEM_LIMIT = 56 * 1024 * 1024
LANES = 128
SUB = 8

TM = 512
FFN_HC = 256
TQ = 256
KB = 256
CH = 128
N_SLAB = CH // SUB
N_LEVEL = 7
MAX_FAST_LOG2_DECAY = 200.0


def _cparams(sem):
    return pltpu.CompilerParams(dimension_semantics=sem, vmem_limit_bytes=VMEM_LIMIT)


def _grp_of_block(i, rows):
    nctx = N_CTX // rows
    per = DEC_SEQ // rows
    return jnp.where(i < nctx, 0, 1 + (i - nctx) // per)


def _rms(x, g):
    return (x * lax.rsqrt(jnp.mean(x * x, axis=-1, keepdims=True) + EPS)) * g


def _dot(a, b):
    return jnp.dot(a, b, preferred_element_type=jnp.float32)


def _dot_nt(a, b):
    return lax.dot_general(a, b, (((1,), (1,)), ((), ())), preferred_element_type=jnp.float32)


def _dot_tn(a, b):
    return lax.dot_general(a, b, (((0,), (0,)), ((), ())), preferred_element_type=jnp.float32)


def _const_spec(shape):
    nd = len(shape)
    return pl.BlockSpec(shape, lambda *_: (0,) * nd, pipeline_mode=pl.Buffered(1))


def _mod_kernel(c_ref, w_ref, b_ref, o_ref):
    c = c_ref[...]
    s = (c * jax.nn.sigmoid(c)).astype(jnp.bfloat16)
    o_ref[...] = _dot(s, w_ref[...].astype(jnp.bfloat16)) + b_ref[...]


def _modulation(cc, w_mod, b_mod):
    tn = 1024
    nt = (N_MOD * D) // tn
    return pl.pallas_call(
        _mod_kernel,
        grid=(DEPTH, nt),
        in_specs=[
            pl.BlockSpec((N_GRP, D), lambda l, j: (0, 0)),
            pl.BlockSpec((None, D, tn), lambda l, j: (l, 0, j)),
            pl.BlockSpec((None, 1, tn), lambda l, j: (l, 0, j)),
        ],
        out_specs=pl.BlockSpec((None, N_GRP, tn), lambda l, j: (l, 0, j)),
        out_shape=jax.ShapeDtypeStruct((DEPTH, N_GRP, N_MOD * D), jnp.float32),
        compiler_params=_cparams(("parallel", "parallel")),
        name="modulation",
    )(cc, w_mod, b_mod.reshape(DEPTH, 1, N_MOD * D))


def _ada_pre(x, mod_ref, g_ref, j, gi):
    shift = mod_ref[3 * j:3 * j + 1, :]
    scale = mod_ref[3 * j + 1:3 * j + 2, :]
    return _rms(x, g_ref[gi:gi + 1, :]) * (1.0 + scale) + shift


def _ada_post(x, y, mod_ref, g_ref, j, gi, res_w):
    gate = mod_ref[3 * j + 2:3 * j + 3, :]
    return x + res_w * gate * _rms(y, g_ref[gi:gi + 1, :])


def _tok_specs():
    return [
        pl.BlockSpec((TM, D), lambda i: (i, 0)),
        pl.BlockSpec((None, N_MOD, D), lambda i: (_grp_of_block(i, TM), 0, 0)),
        _const_spec((6, D)),
    ]


def _ffn_kernel(*refs, j, pair):
    if pair:
        xa_ref, xb_ref, mod_ref, g_ref, win_ref, wout_ref, o_ref, acc_ref = refs
        x = jnp.where(pl.program_id(0) < N_CTX // TM, xa_ref[...], xb_ref[...])
    else:
        x_ref, mod_ref, g_ref, win_ref, wout_ref, o_ref, acc_ref = refs
        x = x_ref[...]
    h = _ada_pre(x, mod_ref, g_ref, j, 2 * j).astype(jnp.bfloat16)
    for c in range(FFN_H // FFN_HC):
        lo = c * FFN_HC
        a = _dot(h, win_ref[:, lo:lo + FFN_HC])
        b = _dot(h, win_ref[:, FFN_H + lo:FFN_H + lo + FFN_HC])
        u = ((a * jax.nn.sigmoid(a)) * b).astype(jnp.bfloat16)
        part = _dot(u, wout_ref[lo:lo + FFN_HC, :])
        if c == 0:
            acc_ref[...] = part
        else:
            acc_ref[...] += part
    o_ref[...] = _ada_post(x, acc_ref[...], mod_ref, g_ref, j, 2 * j + 1, 0.5)


def _ffn(x, mod_l, g_l, w_in, w_out, l, j, *, base=0, rows=N_TOK):
    pair = isinstance(x, tuple)
    nctx = N_CTX // TM
    b0 = base // TM
    w_idx = (l, j // 2, 0, 0)
    if pair:
        assert base == 0 and rows == N_TOK
        x_specs = [pl.BlockSpec((TM, D), lambda i: (jnp.minimum(i, nctx - 1), 0)),
                   pl.BlockSpec((TM, D), lambda i: (jnp.maximum(i - nctx, 0), 0))]
        x_args = list(x)
    else:
        x_specs = [pl.BlockSpec((TM, D), lambda i: (i + b0, 0))]
        x_args = [x]
    return pl.pallas_call(
        functools.partial(_ffn_kernel, j=j, pair=pair),
        grid=(rows // TM,),
        in_specs=x_specs + [
            pl.BlockSpec((None, N_MOD, D), lambda i: (_grp_of_block(i + b0, TM), 0, 0)),
            _const_spec((6, D)),
            pl.BlockSpec((None, None, D, 2 * FFN_H), lambda i: w_idx, pipeline_mode=pl.Buffered(1)),
            pl.BlockSpec((None, None, FFN_H, D), lambda i: w_idx, pipeline_mode=pl.Buffered(1))],
        out_specs=pl.BlockSpec((TM, D), lambda i: (i, 0)),
        out_shape=jax.ShapeDtypeStruct((rows, D), jnp.float32),
        scratch_shapes=[pltpu.VMEM((TM, D), jnp.float32)],
        compiler_params=_cparams(("parallel",)),
        name=f"ffn{j}",
    )(*x_args, mod_l, g_l, w_in, w_out)


def _rope_lanes(x, cos, sin_signed):
    lane = lax.broadcasted_iota(jnp.int32, x.shape, 1)
    partner = jnp.where((lane % 32) < 16, pltpu.roll(x, LANES - 16, 1), pltpu.roll(x, 16, 1))
    return x * cos + partner * sin_signed


def _rope_rows(x, cos_t, sin_signed_t):
    parts = []
    for g in range(HD // 32):
        parts += [x[g * 32 + 16:g * 32 + 32, :], x[g * 32:g * 32 + 16, :]]
    return x * cos_t + jnp.concatenate(parts, axis=0) * sin_signed_t


def _attn_pre_kernel(x_ref, mod_ref, g_ref, wqv_ref, wkt_ref, cos_ref, sin_ref, cost_ref, sint_ref,
                     q_ref, kt_ref, v_ref, *, rope):
    x = x_ref[...]
    h = _ada_pre(x, mod_ref, g_ref, 1, 2).astype(jnp.bfloat16)
    z = _dot(h, wqv_ref[...])
    kt = _dot_nt(wkt_ref[...], h)
    qscale = DA_DK ** -0.5 * LOG2E
    for hd in range(HEADS):
        sl = slice(hd * HD, (hd + 1) * HD)
        q = z[:, hd * HD:(hd + 1) * HD]
        k = kt[hd * HD:(hd + 1) * HD, :]
        if rope:
            q = _rope_lanes(q, cos_ref[...], sin_ref[...])
            k = _rope_rows(k, cost_ref[...], sint_ref[...])
        q_ref[:, sl] = (q * qscale).astype(q_ref.dtype)
        kt_ref[sl, :] = k.astype(kt_ref.dtype)
    v_ref[...] = z[:, D:].astype(v_ref.dtype)


def _attn_pre(x, mod_l, g_l, w_qv, w_kt, tables, *, ctx):
    nb, seq = (BATCH, SEQ) if ctx else (DEC_BATCH, DEC_SEQ)
    t = min(TM, seq)
    rows = nb * seq
    base = 0 if ctx else N_CTX // t
    per = seq // t
    ptab = DEC_SEQ // t
    kv_dtype = jnp.float32 if ctx else jnp.bfloat16
    out_spec = pl.BlockSpec((t, D), lambda i: (i, 0))
    return pl.pallas_call(
        functools.partial(_attn_pre_kernel, rope=not ctx),
        grid=(rows // t,),
        in_specs=[
            pl.BlockSpec((t, D), lambda i: (i + base, 0)),
            pl.BlockSpec((None, N_MOD, D), lambda i: (_grp_of_block(i + base, t), 0, 0)),
            _const_spec((6, D)), _const_spec((D, 2 * D)), _const_spec((D, D)),
            pl.BlockSpec((t, HD), lambda i: (i % ptab, 0)),
            pl.BlockSpec((t, HD), lambda i: (i % ptab, 0)),
            pl.BlockSpec((HD, t), lambda i: (0, i % ptab)),
            pl.BlockSpec((HD, t), lambda i: (0, i % ptab)),
        ],
        out_specs=[out_spec, pl.BlockSpec((None, D, t), lambda i: (i // per, 0, i % per)), out_spec],
        out_shape=[
            jax.ShapeDtypeStruct((rows, D), jnp.bfloat16),
            jax.ShapeDtypeStruct((nb, D, seq), kv_dtype),
            jax.ShapeDtypeStruct((rows, D), kv_dtype),
        ],
        compiler_params=_cparams(("parallel",)),
        name="attn_pre_ctx" if ctx else "attn_pre_smp",
    )(x, mod_l, g_l, w_qv, w_kt, *tables)


def _attn_kernel(*refs, lam_init, cache):
    if cache:
        (x_ref, mod_ref, g_ref, q_ref, k_ref, v_ref, kc_ref, vc_ref,
         lam_ref, sub_ref, w_ref, o_ref, oh_ref, s_ref, e_ref) = refs
    else:
        (x_ref, mod_ref, g_ref, q_ref, k_ref, v_ref,
         lam_ref, sub_ref, w_ref, o_ref, oh_ref, s_ref, e_ref) = refs
    lp = lam_ref[...]
    lam = (jnp.exp(jnp.sum(lp[0:1] * lp[1:2], axis=-1, keepdims=True))
           - jnp.exp(jnp.sum(lp[2:3] * lp[3:4], axis=-1, keepdims=True)) + lam_init)
    tq = q_ref.shape[0]
    lane = lax.broadcasted_iota(jnp.int32, (tq, HD), 1)
    bf = jnp.bfloat16
    n_own = k_ref.shape[1] // KB
    n_kb = n_own + (1 if cache else 0)

    def keys_t(sl, kb):
        if kb < n_own:
            return k_ref[sl, kb * KB:(kb + 1) * KB].astype(bf)
        return kc_ref[sl, :].astype(bf)

    def values(sl, kb):
        if kb < n_own:
            return v_ref[kb * KB:(kb + 1) * KB, sl].astype(bf)
        return vc_ref[:, sl].astype(bf)

    def head_slice(hd):
        return slice(hd * HD, (hd + 1) * HD)

    def kcols(kb):
        return slice(kb * KB, (kb + 1) * KB)

    def stacked_q(hd):
        qh = q_ref[:, head_slice(hd)]
        return jnp.concatenate([jnp.where(lane < DA_DK, qh, jnp.zeros_like(qh)),
                                jnp.where(lane >= DA_DK, qh, jnp.zeros_like(qh))], axis=0)

    st = [dict() for _ in range(HEADS)]
    for slot in range(HEADS + 2):
        h1, h2, h3 = slot, slot - 1, slot - 2
        if h1 < HEADS:
            st[h1]["qq"] = stacked_q(h1)
        if 0 <= h3 < HEADS:
            tot = st[h3]["tot"]
            st[h3]["c"] = jnp.broadcast_to(lam * tot[:tq] / tot[tq:], (tq, KB)).astype(bf)
        for kb in range(n_kb):
            if h1 < HEADS:
                s = _dot(st[h1]["qq"], keys_t(head_slice(h1), kb))
                s_ref[h1 % 2, :, kcols(kb)] = s
                mp = jnp.maximum(s[:, :LANES], s[:, LANES:])
                st[h1]["m"] = mp if kb == 0 else jnp.maximum(st[h1]["m"], mp)
            if 0 <= h2 < HEADS:
                e = jnp.exp2(s_ref[h2 % 2, :, kcols(kb)] - st[h2]["mrow"])
                ep = e[:, :LANES] + e[:, LANES:]
                st[h2]["l"] = ep if kb == 0 else st[h2]["l"] + ep
                e_ref[h2 % 2, :, kcols(kb)] = e.astype(bf)
            if 0 <= h3 < HEADS:
                a = e_ref[h3 % 2, :tq, kcols(kb)] - st[h3]["c"] * e_ref[h3 % 2, tq:, kcols(kb)]
                part = _dot(a, values(head_slice(h3), kb))
                st[h3]["o"] = part if kb == 0 else st[h3]["o"] + part
        if h1 < HEADS:
            st[h1]["mrow"] = jnp.max(st[h1]["m"], axis=-1, keepdims=True)
        if 0 <= h2 < HEADS:
            st[h2]["tot"] = jnp.sum(st[h2]["l"], axis=-1, keepdims=True)
        if 0 <= h3 < HEADS:
            o = st[h3]["o"] * (1.0 / st[h3]["tot"][:tq])
            o = _rms(o, sub_ref[...]) * (1.0 - lam_init)
            oh_ref[:, head_slice(h3)] = o.astype(bf)
            st[h3].clear()
    y = _dot(oh_ref[...], w_ref[...])
    o_ref[...] = _ada_post(x_ref[...], y, mod_ref, g_ref, 1, 3, 1.0)


def _attn(x, mod_l, g_l, q, k, v, kc, vc, lam_p, subln, w_out, *, ctx, lam_init):
    nb, seq = (BATCH, SEQ) if ctx else (DEC_BATCH, DEC_SEQ)
    rows = nb * seq
    base = 0 if ctx else N_CTX // TQ
    nq = seq // TQ
    assert seq % KB == 0 and PAST == KB
    n_keys = seq if ctx else seq + PAST
    row_spec = pl.BlockSpec((TQ, D), lambda b, i: (b * nq + i, 0))
    in_specs = [
        pl.BlockSpec((TQ, D), lambda b, i: (base + b * nq + i, 0)),
        pl.BlockSpec((None, N_MOD, D), lambda b, i: (_grp_of_block(base + b * nq + i, TQ), 0, 0)),
        pl.BlockSpec((6, D), lambda b, i: (0, 0)),
        row_spec,
        pl.BlockSpec((None, D, seq), lambda b, i: (b, 0, 0)),
        pl.BlockSpec((seq, D), lambda b, i: (b, 0)),
    ]
    args = [x, mod_l, g_l, q, k, v]
    if not ctx:
        in_specs += [pl.BlockSpec((None, D, PAST), lambda b, i: (b, 0, 0)),
                     pl.BlockSpec((None, PAST, D), lambda b, i: (b, 0, 0))]
        args += [kc, vc]
    in_specs += [
        pl.BlockSpec((4, DA_DK), lambda b, i: (0, 0)),
        pl.BlockSpec((1, HD), lambda b, i: (0, 0)),
        pl.BlockSpec((D, D), lambda b, i: (0, 0), pipeline_mode=pl.Buffered(1)),
    ]
    args += [lam_p, subln, w_out]
    return pl.pallas_call(
        functools.partial(_attn_kernel, lam_init=lam_init, cache=not ctx),
        grid=(nb, nq),
        in_specs=in_specs,
        out_specs=row_spec,
        out_shape=jax.ShapeDtypeStruct((rows, D), jnp.float32),
        scratch_shapes=[pltpu.VMEM((TQ, D), jnp.bfloat16),
                        pltpu.VMEM((2, 2 * TQ, n_keys), jnp.float32),
                        pltpu.VMEM((2, 2 * TQ, n_keys), jnp.bfloat16)],
        compiler_params=_cparams(("parallel", "parallel")),
        name="attn_ctx" if ctx else "attn_smp",
    )(*args)


def _hgrn_pre_kernel(x_ref, mod_ref, g_ref, w_ref, lb_ref, q_ref, i_ref, gt_ref, lf_ref, *, layer):
    x = x_ref[...]
    h = _ada_pre(x, mod_ref, g_ref, 1, 2).astype(jnp.bfloat16)
    z = _dot(h, w_ref[...])
    q_ref[...] = z[:, 0:D].astype(q_ref.dtype)
    i_ref[...] = z[:, 3 * D:4 * D].astype(i_ref.dtype)
    gt_ref[...] = z[:, 4 * D:5 * D].astype(gt_ref.dtype)
    for d in range(2):
        raw = [lb_ref[l, d:d + 1, :] for l in range(DEPTH)]
        m = functools.reduce(jnp.maximum, raw)
        ex = [jnp.exp(r - m) for r in raw]
        tot = functools.reduce(lambda a, b: a + b, ex)
        soft = [e / tot for e in ex]
        lb = functools.reduce(lambda a, b: a + b, soft[:layer + 1]) - soft[0]
        ff = z[:, (1 + d) * D:(2 + d) * D]
        f = lb + (1.0 - lb) * jax.nn.sigmoid(ff)
        lf_ref[d] = jnp.log(f)


def _hgrn_pre(x, mod_l, g_l, w_in, lb_raw, layer):
    row = pl.BlockSpec((TM, D), lambda i: (i, 0))
    return pl.pallas_call(
        functools.partial(_hgrn_pre_kernel, layer=layer),
        grid=(N_TOK // TM,),
        in_specs=_tok_specs() + [_const_spec((D, 5 * D)), _const_spec((DEPTH, 2, D))],
        out_specs=[row, row, row, pl.BlockSpec((2, TM, D), lambda i: (0, i, 0))],
        out_shape=[
            jax.ShapeDtypeStruct((N_TOK, D), jnp.bfloat16),
            jax.ShapeDtypeStruct((N_TOK, D), jnp.bfloat16),
            jax.ShapeDtypeStruct((N_TOK, D), jnp.bfloat16),
            jax.ShapeDtypeStruct((2, N_TOK, D), jnp.float32),
        ],
        compiler_params=_cparams(("parallel",)),
        name="hgrn_pre",
    )(x, mod_l, g_l, w_in, lb_raw)


def _level_masks(rev):
    t = np.arange(CH)[:, None]
    s = np.arange(CH)[None, :]
    out = []
    for p in range(N_LEVEL):
        same = (t >> (p + 1)) == (s >> (p + 1))
        tb, sb = (t >> p) & 1, (s >> p) & 1
        out.append(same & ((tb == 0) & (sb == 1) if rev else (tb == 1) & (sb == 0)))
    out.append(t == s)
    return np.stack(out).astype(np.float32)


def _row_bcast(x, r):
    return jnp.broadcast_to(x[r:r + 1, :], x.shape)


def _rec_cumsum(lf_ref, b_ref, br_ref, rev):
    sub = lax.broadcasted_iota(jnp.int32, (SUB, D), 0)
    order = range(N_SLAB - 1, -1, -1) if rev else range(N_SLAB)
    edge = 0 if rev else SUB - 1
    carry = None
    for v in order:
        rows = slice(v * SUB, (v + 1) * SUB)
        c = lf_ref[rows, :] * LOG2E
        for k in (1, 2, 4):
            if rev:
                c = c + jnp.where(sub < SUB - k, pltpu.roll(c, SUB - k, 0), 0.0)
            else:
                c = c + jnp.where(sub >= k, pltpu.roll(c, k, 0), 0.0)
        if carry is not None:
            c = c + carry
        carry = _row_bcast(c, edge)
        b_ref[rows, :] = c
        br_ref[rows, :] = carry
    return carry


def _rec_fast(lf_ref, q_ref, i_ref, b_ref, b_last, qs_ref, ks_ref, st_ref, o_ref, rev):
    bf = jnp.bfloat16
    half = 0.5 * b_last
    half2 = jnp.concatenate([half, half], axis=0)
    for v2 in range(N_SLAB // 2):
        rows = slice(v2 * 2 * SUB, (v2 + 1) * 2 * SUB)
        k = 1.0 - jnp.exp2(lf_ref[rows, :] * LOG2E)
        d = b_ref[rows, :] - half2
        qs_ref[rows, :] = (q_ref[rows, :].astype(jnp.float32) * jnp.exp2(d)).astype(bf)
        ks_ref[rows, :] = (k * jnp.exp2(-d)).astype(bf)
    scale = jnp.exp2(half[0:1, :])
    decay = scale * scale
    t = lax.broadcasted_iota(jnp.int32, (CH, CH), 0)
    s = lax.broadcasted_iota(jnp.int32, (CH, CH), 1)
    seen = (s >= t) if rev else (s <= t)
    for hd in range(HEADS):
        sl = slice(hd * HD, (hd + 1) * HD)
        qs, ks, vh = qs_ref[:, sl], ks_ref[:, sl], i_ref[:, sl]
        a = jnp.where(seen, _dot_nt(qs, ks), 0.0)
        st = st_ref[hd]
        o_ref[:, sl] = _dot(a.astype(bf), vh) + _dot_nt(qs, (st * scale[:, sl]).astype(bf))
        st_ref[hd] = st * decay[:, sl] + _dot_tn(vh, ks) * scale[:, sl]


def _rec_prepare(lf_ref, q_ref, b_ref, br_ref, b_last, qt_ref, kt_ref, rev):
    bf = jnp.bfloat16
    sub = lax.broadcasted_iota(jnp.int32, (SUB, D), 0)
    zeros = jnp.zeros((SUB, D), jnp.float32)
    for v2 in range(N_SLAB // 2):
        rows = slice(v2 * 2 * SUB, (v2 + 1) * 2 * SUB)
        f = jnp.exp2(lf_ref[rows, :] * LOG2E)
        k = 1.0 - f
        q = q_ref[rows, :].astype(jnp.float32)
        b = b_ref[rows, :]
        qt_ref[0, rows, :] = (q * f).astype(bf)
        kt_ref[0, rows, :] = k.astype(bf)
        half = (slice(0, SUB), slice(SUB, 2 * SUB))
        for p in range(1, N_LEVEL):
            if p >= 3:
                w = p - 3
                es, qside = [], []
                for n, v in enumerate((2 * v2, 2 * v2 + 1)):
                    hi = v & ~((1 << (w + 1)) - 1)
                    vr = hi | (1 << w) if rev else hi | ((1 << w) - 1)
                    r = br_ref[vr * SUB:(vr + 1) * SUB, :]
                    qside.append(((v >> w) & 1) == (0 if rev else 1))
                    es.append(jnp.exp2(b[half[n], :] - r) if qside[n] else jnp.exp2(r - b[half[n], :]))
                if qside[0] == qside[1]:
                    e = jnp.concatenate(es, axis=0)
                    if qside[0]:
                        qt_ref[p, rows, :] = (q * e).astype(bf)
                    else:
                        kt_ref[p, rows, :] = (k * e).astype(bf)
                else:
                    qe = [q[half[n], :] * es[n] if qside[n] else zeros for n in range(2)]
                    ke = [zeros if qside[n] else k[half[n], :] * es[n] for n in range(2)]
                    qt_ref[p, rows, :] = jnp.concatenate(qe, axis=0).astype(bf)
                    kt_ref[p, rows, :] = jnp.concatenate(ke, axis=0).astype(bf)
            else:
                es = []
                for n in range(2):
                    bv = b[half[n], :]
                    if p == 2:
                        r = _row_bcast(bv, 4 if rev else 3)
                    else:
                        lo, hi_r = (2, 6) if rev else (1, 5)
                        r = jnp.where(sub < 4, _row_bcast(bv, lo), _row_bcast(bv, hi_r))
                    es.append(jnp.exp2(-jnp.abs(bv - r)))
                e = jnp.concatenate(es, axis=0)
                qt_ref[p, rows, :] = (q * e).astype(bf)
                kt_ref[p, rows, :] = (k * e).astype(bf)
        bl = jnp.concatenate([b_last, b_last], axis=0)
        qt_ref[N_LEVEL, rows, :] = (q * jnp.exp2(b)).astype(bf)
        kt_ref[N_LEVEL, rows, :] = (k * jnp.exp2(bl - b)).astype(bf)
    return jnp.exp2(b_last[0:1, :])


def _rec_heads(q_ref, i_ref, qt_ref, kt_ref, msk_ref, st_ref, o_ref, decay):
    bf = jnp.bfloat16
    top = N_LEVEL - 1
    for hd in range(HEADS):
        sl = slice(hd * HD, (hd + 1) * HD)
        a = _dot_nt(qt_ref[top, :, sl], kt_ref[top, :, sl])
        a = a + msk_ref[N_LEVEL] * _dot_nt(q_ref[:, sl], kt_ref[0, :, sl])
        for p in range(top):
            a = a + msk_ref[p] * _dot_nt(qt_ref[p, :, sl], kt_ref[p, :, sl])
        vh = i_ref[:, sl]
        st = st_ref[hd]
        o_ref[:, sl] = _dot(a.astype(bf), vh) + _dot_nt(qt_ref[N_LEVEL, :, sl], st.astype(bf))
        st_ref[hd] = st * decay[:, sl] + _dot_tn(vh, kt_ref[N_LEVEL, :, sl])


def _rec_kernel(blkf_ref, blkb_ref, sid_ref, first_ref, last_ref,
                qf_ref, if_ref, lff_ref, qb_ref, ib_ref, lfb_ref, s0_ref, mskf_ref, mskb_ref,
                of_ref, ob_ref, sfin_ref,
                st_ref, bf_ref, brf_ref, qtf_ref, ktf_ref, bb_ref, brb_ref, qtb_ref, ktb_ref):
    step = pl.program_id(0)

    @pl.when(step == 0)
    def _():
        for ref in (qtf_ref, ktf_ref, qtb_ref, ktb_ref):
            ref[...] = jnp.zeros_like(ref)

    @pl.when(first_ref[step] == 1)
    def _():
        st_ref[...] = jnp.zeros_like(st_ref)

    @pl.when(first_ref[step] == 2)
    def _():
        for d in range(2):
            for hd in range(HEADS):
                st_ref[d, hd] = s0_ref[d, hd].T

    last_f = _rec_cumsum(lff_ref, bf_ref, brf_ref, False)
    last_b = _rec_cumsum(lfb_ref, bb_ref, brb_ref, True)
    moderate = jnp.min(jnp.minimum(last_f, last_b)) >= -MAX_FAST_LOG2_DECAY

    @pl.when(moderate)
    def _():
        _rec_fast(lff_ref, qf_ref, if_ref, bf_ref, last_f, qtf_ref.at[N_LEVEL], ktf_ref.at[N_LEVEL],
                  st_ref.at[0], of_ref, False)
        _rec_fast(lfb_ref, qb_ref, ib_ref, bb_ref, last_b, qtb_ref.at[N_LEVEL], ktb_ref.at[N_LEVEL],
                  st_ref.at[1], ob_ref, True)

    @pl.when(jnp.logical_not(moderate))
    def _():
        decay_f = _rec_prepare(lff_ref, qf_ref, bf_ref, brf_ref, last_f, qtf_ref, ktf_ref, False)
        decay_b = _rec_prepare(lfb_ref, qb_ref, bb_ref, brb_ref, last_b, qtb_ref, ktb_ref, True)
        _rec_heads(qf_ref, if_ref, qtf_ref, ktf_ref, mskf_ref, st_ref.at[0], of_ref, decay_f)
        _rec_heads(qb_ref, ib_ref, qtb_ref, ktb_ref, mskb_ref, st_ref.at[1], ob_ref, decay_b)

    @pl.when(last_ref[step] == 1)
    def _():
        for d in range(2):
            for hd in range(HEADS):
                sfin_ref[d, hd] = st_ref[d, hd].T


def _rec_tables():
    blkf, blkb, sid, first, last = [], [], [], [], []
    seqs = [(b * (SEQ // CH), SEQ // CH) for b in range(BATCH)]
    seqs += [(N_CTX // CH + b * (DEC_SEQ // CH), DEC_SEQ // CH) for b in range(DEC_BATCH)]
    for n, (start, nc) in enumerate(seqs):
        ctx = n < BATCH
        for c in range(nc):
            blkf.append(start + c)
            blkb.append(start + nc - 1 - c)
            sid.append(n)
            first.append((1 if ctx else 2) if c == 0 else 0)
            last.append(int(ctx and c == nc - 1))
    return [jnp.asarray(np.asarray(a, np.int32)) for a in (blkf, blkb, sid, first, last)]


def _rec(q, i, lf, s0):
    row_f = pl.BlockSpec((CH, D), lambda s, bkf, bkb, sid, fi, la: (bkf[s], 0))
    row_b = pl.BlockSpec((CH, D), lambda s, bkf, bkb, sid, fi, la: (bkb[s], 0))
    st_shape = (None, 2, HEADS, HD, HD)
    scratch = [pltpu.VMEM((2, HEADS, HD, HD), jnp.float32)]
    for _ in range(2):
        scratch += [
            pltpu.VMEM((CH, D), jnp.float32),
            pltpu.VMEM((CH, D), jnp.float32),
            pltpu.VMEM((N_LEVEL + 1, CH, D), jnp.bfloat16),
            pltpu.VMEM((N_LEVEL + 1, CH, D), jnp.bfloat16),
        ]
    msk_spec = pl.BlockSpec((N_LEVEL + 1, CH, CH), lambda s, *_: (0, 0, 0))
    grid_spec = pltpu.PrefetchScalarGridSpec(
        num_scalar_prefetch=5,
        grid=(N_TOK // CH,),
        in_specs=[
            row_f, row_f,
            pl.BlockSpec((None, CH, D), lambda s, bkf, bkb, sid, fi, la: (0, bkf[s], 0)),
            row_b, row_b,
            pl.BlockSpec((None, CH, D), lambda s, bkf, bkb, sid, fi, la: (1, bkb[s], 0)),
            pl.BlockSpec(st_shape, lambda s, bkf, bkb, sid, fi, la: (jnp.maximum(sid[s] - BATCH, 0), 0, 0, 0, 0)),
            msk_spec, msk_spec,
        ],
        out_specs=[
            row_f, row_b,
            pl.BlockSpec(st_shape, lambda s, bkf, bkb, sid, fi, la: (jnp.minimum(sid[s], BATCH - 1), 0, 0, 0, 0)),
        ],
        scratch_shapes=scratch,
    )
    return pl.pallas_call(
        _rec_kernel,
        grid_spec=grid_spec,
        out_shape=[
            jax.ShapeDtypeStruct((N_TOK, D), jnp.float32),
            jax.ShapeDtypeStruct((N_TOK, D), jnp.float32),
            jax.ShapeDtypeStruct((BATCH, 2, HEADS, HD, HD), jnp.float32),
        ],
        compiler_params=_cparams(("arbitrary",)),
        name="hgrn_rec",
    )(*_rec_tables(), q, i, lf, q, i, lf, s0, jnp.asarray(_level_masks(False)), jnp.asarray(_level_masks(True)))


def _hgrn_out_kernel(x_ref, mod_ref, g_ref, of_ref, ob_ref, gt_ref, gn_ref, w_ref, o_ref, oh_ref):
    for hd in range(HEADS):
        sl = slice(hd * HD, (hd + 1) * HD)
        o = of_ref[:, sl] + ob_ref[:, sl]
        o = _rms(o, gn_ref[...]) * jax.nn.sigmoid(gt_ref[:, sl].astype(jnp.float32))
        oh_ref[:, sl] = o.astype(jnp.bfloat16)
    y = _dot(oh_ref[...], w_ref[...])
    o_ref[...] = _ada_post(x_ref[...], y, mod_ref, g_ref, 1, 3, 1.0)


def _hgrn_out(x, mod_l, g_l, o_f, o_b, gt, gnorm, w_out):
    row = pl.BlockSpec((TM, D), lambda i: (i, 0))
    return pl.pallas_call(
        _hgrn_out_kernel,
        grid=(N_TOK // TM,),
        in_specs=_tok_specs() + [row, row, row, _const_spec((1, HD)), _const_spec((D, D))],
        out_specs=row,
        out_shape=jax.ShapeDtypeStruct((N_TOK, D), jnp.float32),
        scratch_shapes=[pltpu.VMEM((TM, D), jnp.bfloat16)],
        compiler_params=_cparams(("parallel",)),
        name="hgrn_out",
    )(x, mod_l, g_l, o_f, o_b, gt, gnorm, w_out)


def _rope_tables():
    pos = np.arange(DEC_SEQ)
    row = (pos // GRID_W).astype(np.float64)
    col = (pos % GRID_W).astype(np.float64)
    nf = DA_DK // 4
    inv = ROPE_THETA ** (-np.arange(nf, dtype=np.float64) / nf)
    lane = np.arange(HD)
    axis = (lane % DA_DK) // (2 * nf)
    ang = np.where(axis[None, :] == 0, row[:, None], col[:, None]) * inv[lane % nf][None, :]
    sign = np.where((lane % (2 * nf)) < nf, -1.0, 1.0)
    cos, sin = np.cos(ang), np.sin(ang) * sign[None, :]
    return tuple(jnp.asarray(a, jnp.float32) for a in (cos, sin, cos.T, sin.T))


def kernel(x_prompt, x_sample, cache_k, cache_v, state_hgrn, c, c_ctx, w_mod, b_mod, norm_g,
           ffn_w_in, ffn_w_out, attn_w_in, attn_w_out, attn_lambda, attn_subln,
           hgrn_w_in, hgrn_w_out, hgrn_lower_bounds, hgrn_gnorm):
    bf = jnp.bfloat16
    x = (x_prompt.reshape(N_CTX, D), x_sample.reshape(N_SMP, D))
    cc = jnp.concatenate([c_ctx[None, :], c, jnp.zeros((N_GRP - 1 - DEC_BATCH, D), jnp.float32)], axis=0)
    mod = _modulation(cc, w_mod, b_mod).reshape(DEPTH, N_GRP, N_MOD, D)
    tables = _rope_tables()
    ffn_w_in = ffn_w_in.astype(bf)
    ffn_w_out = ffn_w_out.astype(bf)

    ks, vs, states = [], [], []
    for l in range(DEPTH):
        mod_l, g_l = mod[l], norm_g[l]
        x = _ffn(x, mod_l, g_l, ffn_w_in, ffn_w_out, l, 0)
        if l % 2 == 0:
            a = l // 2
            lam_init = 0.8 - 0.6 * math.exp(-0.3 * l)
            w_qv = jnp.concatenate([attn_w_in[a][:, :D], attn_w_in[a][:, 2 * D:]], axis=1).astype(bf)
            w_kt = attn_w_in[a][:, D:2 * D].T.astype(bf)
            w_out = attn_w_out[a].astype(bf)
            subln = attn_subln[a].reshape(1, HD)
            qc, kt_new, vc_new = _attn_pre(x, mod_l, g_l, w_qv, w_kt, tables, ctx=True)
            qs, kt_smp, v_smp = _attn_pre(x, mod_l, g_l, w_qv, w_kt, tables, ctx=False)
            kct = jnp.transpose(cache_k[:, a], (0, 2, 3, 4, 1)).reshape(DEC_BATCH, D, PAST)
            xc = _attn(x, mod_l, g_l, qc, kt_new, vc_new, None, None, attn_lambda[a], subln, w_out,
                       ctx=True, lam_init=lam_init)
            xs = _attn(x, mod_l, g_l, qs, kt_smp, v_smp, kct, cache_v[:, a].reshape(DEC_BATCH, PAST, D),
                       attn_lambda[a], subln, w_out, ctx=False, lam_init=lam_init)
            x = (xc, xs)
            ks.append(jnp.transpose(kt_new.reshape(BATCH, HEADS, 2, DA_DK, SEQ), (0, 4, 1, 2, 3)))
            vs.append(vc_new.reshape(BATCH, SEQ, HEADS, HD))
        else:
            r = l // 2
            q, i, gt, lf = _hgrn_pre(x, mod_l, g_l, hgrn_w_in[r].astype(bf), hgrn_lower_bounds, l)
            s0 = state_hgrn[:, r].astype(jnp.float32)
            o_f, o_b, s_fin = _rec(q, i, lf, s0)
            x = _hgrn_out(x, mod_l, g_l, o_f, o_b, gt, hgrn_gnorm[r].reshape(1, HD), hgrn_w_out[r].astype(bf))
            states.append(s_fin)
        if l < DEPTH - 1:
            x = _ffn(x, mod_l, g_l, ffn_w_in, ffn_w_out, l, 2)

    assert not isinstance(x, tuple)
    y_prompt = _ffn(x, mod_l, g_l, ffn_w_in, ffn_w_out, DEPTH - 1, 2, base=0, rows=N_CTX)
    y_sample = _ffn(x, mod_l, g_l, ffn_w_in, ffn_w_out, DEPTH - 1, 2, base=N_CTX, rows=N_SMP)
    return (y_prompt.reshape(BATCH, SEQ, D), y_sample.reshape(DEC_BATCH, DEC_SEQ, D),
            jnp.stack(ks, axis=1), jnp.stack(vs, axis=1), jnp.stack(states, axis=1))
```

```python
import functools
import math

import numpy as np
import jax
import jax.numpy as jnp
from jax import lax
from jax.experimental import pallas as pl
from jax.experimental.pallas import tpu as pltpu

D = 1024
BATCH = 16
SEQ = 256
DEPTH = 2
DEC_BATCH = 4
DEC_SEQ = 2048
PAST = 256
GRID_W = 64
N_MOD = 9
EPS = 1e-6
HEADS = 8
HD = 128
DA_DK = 64
ROPE_THETA = 10000.0
FFN_H = 2816
LOG2E = 1.4426950408889634

N_CTX = BATCH * SEQ
N_SMP = DEC_BATCH * DEC_SEQ
N_TOK = N_CTX + N_SMP
N_GRP = 8

VMEM_LIMIT = 56 * 1024 * 1024
LANES = 128
SUB = 8

TM = 512
FFN_HC = 256
TQ = 256
KB = 256
CH = 128
N_SLAB = CH // SUB
N_LEVEL = 7
MAX_FAST_LOG2_DECAY = 200.0


def _cparams(sem):
    return pltpu.CompilerParams(dimension_semantics=sem, vmem_limit_bytes=VMEM_LIMIT)


def _grp_of_block(i, rows):
    nctx = N_CTX // rows
    per = DEC_SEQ // rows
    return jnp.where(i < nctx, 0, 1 + (i - nctx) // per)


def _rms(x, g):
    return (x * lax.rsqrt(jnp.mean(x * x, axis=-1, keepdims=True) + EPS)) * g


def _dot(a, b):
    return lax.dot_general(a, b, (((1,), (0,)), ((), ())), preferred_element_type=jnp.float32)


def _dot_nt(a, b):
    return lax.dot_general(a, b, (((1,), (1,)), ((), ())), preferred_element_type=jnp.float32)


def _dot_tn(a, b):
    return lax.dot_general(a, b, (((0,), (0,)), ((), ())), preferred_element_type=jnp.float32)


def _const_spec(shape):
    nd = len(shape)
    return pl.BlockSpec(shape, lambda *_: (0,) * nd, pipeline_mode=pl.Buffered(1))


def _mod_kernel(c_ref, w_ref, b_ref, o_ref):
    c = c_ref[...]
    s = (c * jax.nn.sigmoid(c)).astype(jnp.bfloat16)
    o_ref[...] = _dot(s, w_ref[...].astype(jnp.bfloat16)) + b_ref[...]


def _modulation(cc, w_mod, b_mod):
    tn = 1024
    nt = (N_MOD * D) // tn
    return pl.pallas_call(
        _mod_kernel,
        grid=(DEPTH, nt),
        in_specs=[
            pl.BlockSpec((N_GRP, D), lambda l, j: (0, 0)),
            pl.BlockSpec((None, D, tn), lambda l, j: (l, 0, j)),
            pl.BlockSpec((None, 1, tn), lambda l, j: (l, 0, j)),
        ],
        out_specs=pl.BlockSpec((None, N_GRP, tn), lambda l, j: (l, 0, j)),
        out_shape=jax.ShapeDtypeStruct((DEPTH, N_GRP, N_MOD * D), jnp.float32),
        compiler_params=_cparams(("parallel", "parallel")),
        name="modulation",
    )(cc, w_mod, b_mod.reshape(DEPTH, 1, N_MOD * D))


def _ada_pre(x, mod_ref, g_ref, j, gi):
    shift = mod_ref[3 * j:3 * j + 1, :]
    scale = mod_ref[3 * j + 1:3 * j + 2, :]
    return _rms(x, g_ref[gi:gi + 1, :]) * (1.0 + scale) + shift


def _ada_post(x, y, mod_ref, g_ref, j, gi, res_w):
    gate = mod_ref[3 * j + 2:3 * j + 3, :]
    return x + res_w * gate * _rms(y, g_ref[gi:gi + 1, :])


def _tok_specs():
    return [
        pl.BlockSpec((TM, D), lambda i: (i, 0)),
        pl.BlockSpec((None, N_MOD, D), lambda i: (_grp_of_block(i, TM), 0, 0)),
        _const_spec((6, D)),
    ]


def _ffn_kernel(*refs, j, pair):
    if pair:
        xa_ref, xb_ref, mod_ref, g_ref, win_ref, wout_ref, o_ref, acc_ref = refs
        x = jnp.where(pl.program_id(0) < N_CTX // TM, xa_ref[...], xb_ref[...])
    else:
        x_ref, mod_ref, g_ref, win_ref, wout_ref, o_ref, acc_ref = refs
        x = x_ref[...]
    h = _ada_pre(x, mod_ref, g_ref, j, 2 * j).astype(jnp.bfloat16)
    for c in range(FFN_H // FFN_HC):
        lo = c * FFN_HC
        a = _dot(h, win_ref[:, lo:lo + FFN_HC])
        b = _dot(h, win_ref[:, FFN_H + lo:FFN_H + lo + FFN_HC])
        u = ((a * jax.nn.sigmoid(a)) * b).astype(jnp.bfloat16)
        part = _dot(u, wout_ref[lo:lo + FFN_HC, :])
        if c == 0:
            acc_ref[...] = part
        else:
            acc_ref[...] += part
    o_ref[...] = _ada_post(x, acc_ref[...], mod_ref, g_ref, j, 2 * j + 1, 0.5)


def _ffn(x, mod_l, g_l, w_in, w_out, l, j, *, base=0, rows=N_TOK):
    pair = isinstance(x, tuple)
    nctx = N_CTX // TM
    b0 = base // TM
    w_idx = (l, j // 2, 0, 0)
    if pair:
        assert base == 0 and rows == N_TOK
        x_specs = [pl.BlockSpec((TM, D), lambda i: (jnp.minimum(i, nctx - 1), 0)),
                   pl.BlockSpec((TM, D), lambda i: (jnp.maximum(i - nctx, 0), 0))]
        x_args = list(x)
    else:
        x_specs = [pl.BlockSpec((TM, D), lambda i: (i + b0, 0))]
        x_args = [x]
    return pl.pallas_call(
        functools.partial(_ffn_kernel, j=j, pair=pair),
        grid=(rows // TM,),
        in_specs=x_specs + [
            pl.BlockSpec((None, N_MOD, D), lambda i: (_grp_of_block(i + b0, TM), 0, 0)),
            _const_spec((6, D)),
            pl.BlockSpec((None, None, D, 2 * FFN_H), lambda i: w_idx, pipeline_mode=pl.Buffered(1)),
            pl.BlockSpec((None, None, FFN_H, D), lambda i: w_idx, pipeline_mode=pl.Buffered(1))],
        out_specs=pl.BlockSpec((TM, D), lambda i: (i, 0)),
        out_shape=jax.ShapeDtypeStruct((rows, D), jnp.float32),
        scratch_shapes=[pltpu.VMEM((TM, D), jnp.float32)],
        compiler_params=_cparams(("parallel",)),
        name=f"ffn{j}",
    )(*x_args, mod_l, g_l, w_in, w_out)


def _rope_lanes(x, cos, sin_signed):
    lane = lax.broadcasted_iota(jnp.int32, x.shape, 1)
    partner = jnp.where((lane % 32) < 16, pltpu.roll(x, LANES - 16, 1), pltpu.roll(x, 16, 1))
    return x * cos + partner * sin_signed


def _rope_rows(x, cos_t, sin_signed_t):
    parts = []
    for g in range(HD // 32):
        parts += [x[g * 32 + 16:g * 32 + 32, :], x[g * 32:g * 32 + 16, :]]
    return x * cos_t + jnp.concatenate(parts, axis=0) * sin_signed_t


def _attn_pre_kernel(x_ref, mod_ref, g_ref, w_ref, wkt_ref, cos_ref, sin_ref, cost_ref, sint_ref,
                     q_ref, kt_ref, v_ref, *, rope):
    x = x_ref[...]
    h = _ada_pre(x, mod_ref, g_ref, 1, 2).astype(jnp.bfloat16)
    qscale = DA_DK ** -0.5 * LOG2E
    pair = 2 * HD
    for j in range(D // pair):
        cols = slice(j * pair, (j + 1) * pair)
        zq = _dot(h, w_ref[:, cols])
        zkt = _dot_nt(wkt_ref[cols, :], h)
        v_ref[:, cols] = _dot(h, w_ref[:, 2 * D + j * pair:2 * D + (j + 1) * pair]).astype(v_ref.dtype)
        for n in range(2):
            sl = slice(j * pair + n * HD, j * pair + (n + 1) * HD)
            q = zq[:, n * HD:(n + 1) * HD]
            k = zkt[n * HD:(n + 1) * HD, :]
            if rope:
                q = _rope_lanes(q, cos_ref[...], sin_ref[...])
                k = _rope_rows(k, cost_ref[...], sint_ref[...])
            q_ref[:, sl] = (q * qscale).astype(q_ref.dtype)
            kt_ref[sl, :] = k.astype(kt_ref.dtype)


def _attn_pre(x, mod_l, g_l, w_in, w_kt, tables, *, ctx):
    nb, seq = (BATCH, SEQ) if ctx else (DEC_BATCH, DEC_SEQ)
    t = min(TM, seq)
    rows = nb * seq
    base = 0 if ctx else N_CTX // t
    per = seq // t
    ptab = DEC_SEQ // t
    kv_dtype = jnp.float32 if ctx else jnp.bfloat16
    out_spec = pl.BlockSpec((t, D), lambda i: (i, 0))
    return pl.pallas_call(
        functools.partial(_attn_pre_kernel, rope=not ctx),
        grid=(rows // t,),
        in_specs=[
            pl.BlockSpec((t, D), lambda i: (i + base, 0)),
            pl.BlockSpec((None, N_MOD, D), lambda i: (_grp_of_block(i + base, t), 0, 0)),
            _const_spec((6, D)), _const_spec((D, 3 * D)), _const_spec((D, D)),
            pl.BlockSpec((t, HD), lambda i: (i % ptab, 0)),
            pl.BlockSpec((t, HD), lambda i: (i % ptab, 0)),
            pl.BlockSpec((HD, t), lambda i: (0, i % ptab)),
            pl.BlockSpec((HD, t), lambda i: (0, i % ptab)),
        ],
        out_specs=[out_spec, pl.BlockSpec((None, D, t), lambda i: (i // per, 0, i % per)), out_spec],
        out_shape=[
            jax.ShapeDtypeStruct((rows, D), jnp.bfloat16),
            jax.ShapeDtypeStruct((nb, D, seq), kv_dtype),
            jax.ShapeDtypeStruct((rows, D), kv_dtype),
        ],
        compiler_params=_cparams(("parallel",)),
        name="attn_pre_ctx" if ctx else "attn_pre_smp",
    )(x, mod_l, g_l, w_in, w_kt, *tables)


def _attn_kernel(*refs, lam_init, cache):
    if cache:
        (x_ref, mod_ref, g_ref, q_ref, k_ref, v_ref, kc_ref, vc_ref,
         lam_ref, sub_ref, w_ref, o_ref, oh_ref, s_ref, e_ref) = refs
    else:
        (x_ref, mod_ref, g_ref, q_ref, k_ref, v_ref,
         lam_ref, sub_ref, w_ref, o_ref, oh_ref, s_ref, e_ref) = refs
    lp = lam_ref[...]
    lam = (jnp.exp(jnp.sum(lp[0:1] * lp[1:2], axis=-1, keepdims=True))
           - jnp.exp(jnp.sum(lp[2:3] * lp[3:4], axis=-1, keepdims=True)) + lam_init)
    tq = q_ref.shape[0]
    lane = lax.broadcasted_iota(jnp.int32, (tq, HD), 1)
    bf = jnp.bfloat16
    n_own = k_ref.shape[1] // KB
    n_kb = n_own + (1 if cache else 0)

    def keys_t(sl, kb):
        if kb < n_own:
            return k_ref[sl, kb * KB:(kb + 1) * KB].astype(bf)
        return kc_ref[sl, :].astype(bf)

    def values(sl, kb):
        if kb < n_own:
            return v_ref[kb * KB:(kb + 1) * KB, sl].astype(bf)
        return vc_ref[:, sl].astype(bf)

    def head_slice(hd):
        return slice(hd * HD, (hd + 1) * HD)

    def kcols(kb):
        return slice(kb * KB, (kb + 1) * KB)

    def stacked_q(hd):
        qh = q_ref[:, head_slice(hd)]
        return jnp.concatenate([jnp.where(lane < DA_DK, qh, jnp.zeros_like(qh)),
                                jnp.where(lane >= DA_DK, qh, jnp.zeros_like(qh))], axis=0)

    st = [dict() for _ in range(HEADS)]
    for slot in range(HEADS + 2):
        h1, h2, h3 = slot, slot - 1, slot - 2
        if h1 < HEADS:
            st[h1]["qq"] = stacked_q(h1)
        if 0 <= h3 < HEADS:
            tot = st[h3]["tot"]
            st[h3]["c"] = jnp.broadcast_to(lam * tot[:tq] / tot[tq:], (tq, KB)).astype(bf)
        for kb in range(n_kb):
            if h1 < HEADS:
                s = _dot(st[h1]["qq"], keys_t(head_slice(h1), kb))
                s_ref[h1 % 2, :, kcols(kb)] = s
                mp = jnp.maximum(s[:, :LANES], s[:, LANES:])
                st[h1]["m"] = mp if kb == 0 else jnp.maximum(st[h1]["m"], mp)
            if 0 <= h2 < HEADS:
                e = jnp.exp2(s_ref[h2 % 2, :, kcols(kb)] - st[h2]["mrow"])
                ep = e[:, :LANES] + e[:, LANES:]
                st[h2]["l"] = ep if kb == 0 else st[h2]["l"] + ep
                e_ref[h2 % 2, :, kcols(kb)] = e.astype(bf)
            if 0 <= h3 < HEADS:
                a = e_ref[h3 % 2, :tq, kcols(kb)] - st[h3]["c"] * e_ref[h3 % 2, tq:, kcols(kb)]
                part = _dot(a, values(head_slice(h3), kb))
                st[h3]["o"] = part if kb == 0 else st[h3]["o"] + part
        if h1 < HEADS:
            st[h1]["mrow"] = jnp.max(st[h1]["m"], axis=-1, keepdims=True)
        if 0 <= h2 < HEADS:
            st[h2]["tot"] = jnp.sum(st[h2]["l"], axis=-1, keepdims=True)
        if 0 <= h3 < HEADS:
            o = st[h3]["o"] * (1.0 / st[h3]["tot"][:tq])
            o = _rms(o, sub_ref[...]) * (1.0 - lam_init)
            oh_ref[:, head_slice(h3)] = o.astype(bf)
            st[h3].clear()
    y = _dot(oh_ref[...], w_ref[...])
    o_ref[...] = _ada_post(x_ref[...], y, mod_ref, g_ref, 1, 3, 1.0)


def _attn(x, mod_l, g_l, q, k, v, kc, vc, lam_p, subln, w_out, *, ctx, lam_init):
    nb, seq = (BATCH, SEQ) if ctx else (DEC_BATCH, DEC_SEQ)
    rows = nb * seq
    base = 0 if ctx else N_CTX // TQ
    nq = seq // TQ
    assert seq % KB == 0 and PAST == KB
    n_keys = seq if ctx else seq + PAST
    row_spec = pl.BlockSpec((TQ, D), lambda b, i: (b * nq + i, 0))
    in_specs = [
        pl.BlockSpec((TQ, D), lambda b, i: (base + b * nq + i, 0)),
        pl.BlockSpec((None, N_MOD, D), lambda b, i: (_grp_of_block(base + b * nq + i, TQ), 0, 0)),
        pl.BlockSpec((6, D), lambda b, i: (0, 0)),
        row_spec,
        pl.BlockSpec((None, D, seq), lambda b, i: (b, 0, 0)),
        pl.BlockSpec((seq, D), lambda b, i: (b, 0)),
    ]
    args = [x, mod_l, g_l, q, k, v]
    if not ctx:
        in_specs += [pl.BlockSpec((None, D, PAST), lambda b, i: (b, 0, 0)),
                     pl.BlockSpec((None, PAST, D), lambda b, i: (b, 0, 0))]
        args += [kc, vc]
    in_specs += [
        pl.BlockSpec((4, DA_DK), lambda b, i: (0, 0)),
        pl.BlockSpec((1, HD), lambda b, i: (0, 0)),
        pl.BlockSpec((D, D), lambda b, i: (0, 0), pipeline_mode=pl.Buffered(1)),
    ]
    args += [lam_p, subln, w_out]
    return pl.pallas_call(
        functools.partial(_attn_kernel, lam_init=lam_init, cache=not ctx),
        grid=(nb, nq),
        in_specs=in_specs,
        out_specs=row_spec,
        out_shape=jax.ShapeDtypeStruct((rows, D), jnp.float32),
        scratch_shapes=[pltpu.VMEM((TQ, D), jnp.bfloat16),
                        pltpu.VMEM((2, 2 * TQ, n_keys), jnp.float32),
                        pltpu.VMEM((2, 2 * TQ, n_keys), jnp.bfloat16)],
        compiler_params=_cparams(("parallel", "parallel")),
        name="attn_ctx" if ctx else "attn_smp",
    )(*args)


def _hgrn_pre_kernel(x_ref, mod_ref, g_ref, w_ref, lb_ref, q_ref, i_ref, gt_ref, lf_ref, *, layer):
    x = x_ref[...]
    h = _ada_pre(x, mod_ref, g_ref, 1, 2).astype(jnp.bfloat16)

    def proj(n):
        return _dot(h, w_ref[:, n * D:(n + 1) * D])

    def log_gate(d, ff):
        raw = [lb_ref[l, d:d + 1, :] for l in range(DEPTH)]
        m = functools.reduce(jnp.maximum, raw)
        ex = [jnp.exp(r - m) for r in raw]
        tot = functools.reduce(lambda a, b: a + b, ex)
        soft = [e / tot for e in ex]
        lb = functools.reduce(lambda a, b: a + b, soft[:layer + 1]) - soft[0]
        return jnp.log(lb + (1.0 - lb) * jax.nn.sigmoid(ff))

    ff = proj(1)
    q_ref[...] = proj(0).astype(q_ref.dtype)
    lf_ref[0] = log_gate(0, ff)
    fb = proj(2)
    i_ref[...] = proj(3).astype(i_ref.dtype)
    lf_ref[1] = log_gate(1, fb)
    gt_ref[...] = proj(4).astype(gt_ref.dtype)


def _hgrn_pre(x, mod_l, g_l, w_in, lb_raw, layer):
    row = pl.BlockSpec((TM, D), lambda i: (i, 0))
    return pl.pallas_call(
        functools.partial(_hgrn_pre_kernel, layer=layer),
        grid=(N_TOK // TM,),
        in_specs=_tok_specs() + [_const_spec((D, 5 * D)), _const_spec((DEPTH, 2, D))],
        out_specs=[row, row, row, pl.BlockSpec((2, TM, D), lambda i: (0, i, 0))],
        out_shape=[
            jax.ShapeDtypeStruct((N_TOK, D), jnp.bfloat16),
            jax.ShapeDtypeStruct((N_TOK, D), jnp.bfloat16),
            jax.ShapeDtypeStruct((N_TOK, D), jnp.bfloat16),
            jax.ShapeDtypeStruct((2, N_TOK, D), jnp.float32),
        ],
        compiler_params=_cparams(("parallel",)),
        name="hgrn_pre",
    )(x, mod_l, g_l, w_in, lb_raw)


def _level_masks(rev):
    t = np.arange(CH)[:, None]
    s = np.arange(CH)[None, :]
    out = []
    for p in range(N_LEVEL):
        same = (t >> (p + 1)) == (s >> (p + 1))
        tb, sb = (t >> p) & 1, (s >> p) & 1
        out.append(same & ((tb == 0) & (sb == 1) if rev else (tb == 1) & (sb == 0)))
    out.append(t == s)
    return np.stack(out).astype(np.float32)


def _row_bcast(x, r):
    return jnp.broadcast_to(x[r:r + 1, :], x.shape)


def _rec_cumsum(lf_ref, b_ref, br_ref, rev):
    sub = lax.broadcasted_iota(jnp.int32, (SUB, D), 0)
    order = range(N_SLAB - 1, -1, -1) if rev else range(N_SLAB)
    edge = 0 if rev else SUB - 1
    carry = None
    for v in order:
        rows = slice(v * SUB, (v + 1) * SUB)
        c = lf_ref[rows, :] * LOG2E
        for k in (1, 2, 4):
            if rev:
                c = c + jnp.where(sub < SUB - k, pltpu.roll(c, SUB - k, 0), 0.0)
            else:
                c = c + jnp.where(sub >= k, pltpu.roll(c, k, 0), 0.0)
        if carry is not None:
            c = c + carry
        carry = _row_bcast(c, edge)
        b_ref[rows, :] = c
        br_ref[rows, :] = carry
    return carry


def _rec_fast(lf_ref, q_ref, i_ref, b_ref, b_last, qs_ref, ks_ref, st_ref, o_ref, rev):
    bf = jnp.bfloat16
    half = 0.5 * b_last
    half2 = jnp.concatenate([half, half], axis=0)
    for v2 in range(N_SLAB // 2):
        rows = slice(v2 * 2 * SUB, (v2 + 1) * 2 * SUB)
        k = 1.0 - jnp.exp2(lf_ref[rows, :] * LOG2E)
        d = b_ref[rows, :] - half2
        qs_ref[rows, :] = (q_ref[rows, :].astype(jnp.float32) * jnp.exp2(d)).astype(bf)
        ks_ref[rows, :] = (k * jnp.exp2(-d)).astype(bf)
    scale = jnp.exp2(half[0:1, :])
    decay = scale * scale
    t = lax.broadcasted_iota(jnp.int32, (CH, CH), 0)
    s = lax.broadcasted_iota(jnp.int32, (CH, CH), 1)
    seen = (s >= t) if rev else (s <= t)
    for hd in range(HEADS):
        sl = slice(hd * HD, (hd + 1) * HD)
        qs, ks, vh = qs_ref[:, sl], ks_ref[:, sl], i_ref[:, sl]
        a = jnp.where(seen, _dot_nt(qs, ks), 0.0)
        st = st_ref[hd]
        o_ref[:, sl] = _dot(a.astype(bf), vh) + _dot_nt(qs, (st * scale[:, sl]).astype(bf))
        st_ref[hd] = st * decay[:, sl] + _dot_tn(vh, ks) * scale[:, sl]


def _rec_prepare(lf_ref, q_ref, b_ref, br_ref, b_last, qt_ref, kt_ref, rev):
    bf = jnp.bfloat16
    sub = lax.broadcasted_iota(jnp.int32, (SUB, D), 0)
    zeros = jnp.zeros((SUB, D), jnp.float32)
    for v2 in range(N_SLAB // 2):
        rows = slice(v2 * 2 * SUB, (v2 + 1) * 2 * SUB)
        f = jnp.exp2(lf_ref[rows, :] * LOG2E)
        k = 1.0 - f
        q = q_ref[rows, :].astype(jnp.float32)
        b = b_ref[rows, :]
        qt_ref[0, rows, :] = (q * f).astype(bf)
        kt_ref[0, rows, :] = k.astype(bf)
        half = (slice(0, SUB), slice(SUB, 2 * SUB))
        for p in range(1, N_LEVEL):
            if p >= 3:
                w = p - 3
                es, qside = [], []
                for n, v in enumerate((2 * v2, 2 * v2 + 1)):
                    hi = v & ~((1 << (w + 1)) - 1)
                    vr = hi | (1 << w) if rev else hi | ((1 << w) - 1)
                    r = br_ref[vr * SUB:(vr + 1) * SUB, :]
                    qside.append(((v >> w) & 1) == (0 if rev else 1))
                    es.append(jnp.exp2(b[half[n], :] - r) if qside[n] else jnp.exp2(r - b[half[n], :]))
                if qside[0] == qside[1]:
                    e = jnp.concatenate(es, axis=0)
                    if qside[0]:
                        qt_ref[p, rows, :] = (q * e).astype(bf)
                    else:
                        kt_ref[p, rows, :] = (k * e).astype(bf)
                else:
                    qe = [q[half[n], :] * es[n] if qside[n] else zeros for n in range(2)]
                    ke = [zeros if qside[n] else k[half[n], :] * es[n] for n in range(2)]
                    qt_ref[p, rows, :] = jnp.concatenate(qe, axis=0).astype(bf)
                    kt_ref[p, rows, :] = jnp.concatenate(ke, axis=0).astype(bf)
            else:
                es = []
                for n in range(2):
                    bv = b[half[n], :]
                    if p == 2:
                        r = _row_bcast(bv, 4 if rev else 3)
                    else:
                        lo, hi_r = (2, 6) if rev else (1, 5)
                        r = jnp.where(sub < 4, _row_bcast(bv, lo), _row_bcast(bv, hi_r))
                    es.append(jnp.exp2(-jnp.abs(bv - r)))
                e = jnp.concatenate(es, axis=0)
                qt_ref[p, rows, :] = (q * e).astype(bf)
                kt_ref[p, rows, :] = (k * e).astype(bf)
        bl = jnp.concatenate([b_last, b_last], axis=0)
        qt_ref[N_LEVEL, rows, :] = (q * jnp.exp2(b)).astype(bf)
        kt_ref[N_LEVEL, rows, :] = (k * jnp.exp2(bl - b)).astype(bf)
    return jnp.exp2(b_last[0:1, :])


def _rec_heads(q_ref, i_ref, qt_ref, kt_ref, msk_ref, st_ref, o_ref, decay):
    bf = jnp.bfloat16
    top = N_LEVEL - 1
    for hd in range(HEADS):
        sl = slice(hd * HD, (hd + 1) * HD)
        a = _dot_nt(qt_ref[top, :, sl], kt_ref[top, :, sl])
        a = a + msk_ref[N_LEVEL] * _dot_nt(q_ref[:, sl], kt_ref[0, :, sl])
        for p in range(top):
            a = a + msk_ref[p] * _dot_nt(qt_ref[p, :, sl], kt_ref[p, :, sl])
        vh = i_ref[:, sl]
        st = st_ref[hd]
        o_ref[:, sl] = _dot(a.astype(bf), vh) + _dot_nt(qt_ref[N_LEVEL, :, sl], st.astype(bf))
        st_ref[hd] = st * decay[:, sl] + _dot_tn(vh, kt_ref[N_LEVEL, :, sl])


def _rec_kernel(blkf_ref, blkb_ref, sid_ref, first_ref, last_ref,
                qf_ref, if_ref, lff_ref, qb_ref, ib_ref, lfb_ref, s0_ref, mskf_ref, mskb_ref,
                of_ref, ob_ref, sfin_ref,
                st_ref, bf_ref, brf_ref, qtf_ref, ktf_ref, bb_ref, brb_ref, qtb_ref, ktb_ref):
    step = pl.program_id(0)

    @pl.when(step == 0)
    def _():
        for ref in (qtf_ref, ktf_ref, qtb_ref, ktb_ref):
            ref[...] = jnp.zeros_like(ref)

    @pl.when(first_ref[step] == 1)
    def _():
        st_ref[...] = jnp.zeros_like(st_ref)

    @pl.when(first_ref[step] == 2)
    def _():
        for d in range(2):
            for hd in range(HEADS):
                st_ref[d, hd] = s0_ref[d, hd].T

    last_f = _rec_cumsum(lff_ref, bf_ref, brf_ref, False)
    last_b = _rec_cumsum(lfb_ref, bb_ref, brb_ref, True)
    moderate = jnp.min(jnp.minimum(last_f, last_b)) >= -MAX_FAST_LOG2_DECAY

    @pl.when(moderate)
    def _():
        _rec_fast(lff_ref, qf_ref, if_ref, bf_ref, last_f, qtf_ref.at[N_LEVEL], ktf_ref.at[N_LEVEL],
                  st_ref.at[0], of_ref, False)
        _rec_fast(lfb_ref, qb_ref, ib_ref, bb_ref, last_b, qtb_ref.at[N_LEVEL], ktb_ref.at[N_LEVEL],
                  st_ref.at[1], ob_ref, True)

    @pl.when(jnp.logical_not(moderate))
    def _():
        decay_f = _rec_prepare(lff_ref, qf_ref, bf_ref, brf_ref, last_f, qtf_ref, ktf_ref, False)
        decay_b = _rec_prepare(lfb_ref, qb_ref, bb_ref, brb_ref, last_b, qtb_ref, ktb_ref, True)
        _rec_heads(qf_ref, if_ref, qtf_ref, ktf_ref, mskf_ref, st_ref.at[0], of_ref, decay_f)
        _rec_heads(qb_ref, ib_ref, qtb_ref, ktb_ref, mskb_ref, st_ref.at[1], ob_ref, decay_b)

    @pl.when(last_ref[step] == 1)
    def _():
        for d in range(2):
            for hd in range(HEADS):
                sfin_ref[d, hd] = st_ref[d, hd].T


def _rec_tables():
    blkf, blkb, sid, first, last = [], [], [], [], []
    seqs = [(b * (SEQ // CH), SEQ // CH) for b in range(BATCH)]
    seqs += [(N_CTX // CH + b * (DEC_SEQ // CH), DEC_SEQ // CH) for b in range(DEC_BATCH)]
    for n, (start, nc) in enumerate(seqs):
        ctx = n < BATCH
        for c in range(nc):
            blkf.append(start + c)
            blkb.append(start + nc - 1 - c)
            sid.append(n)
            first.append((1 if ctx else 2) if c == 0 else 0)
            last.append(int(ctx and c == nc - 1))
    return [jnp.asarray(np.asarray(a, np.int32)) for a in (blkf, blkb, sid, first, last)]


def _rec(q, i, lf, s0):
    row_f = pl.BlockSpec((CH, D), lambda s, bkf, bkb, sid, fi, la: (bkf[s], 0))
    row_b = pl.BlockSpec((CH, D), lambda s, bkf, bkb, sid, fi, la: (bkb[s], 0))
    st_shape = (None, 2, HEADS, HD, HD)
    scratch = [pltpu.VMEM((2, HEADS, HD, HD), jnp.float32)]
    for _ in range(2):
        scratch += [
            pltpu.VMEM((CH, D), jnp.float32),
            pltpu.VMEM((CH, D), jnp.float32),
            pltpu.VMEM((N_LEVEL + 1, CH, D), jnp.bfloat16),
            pltpu.VMEM((N_LEVEL + 1, CH, D), jnp.bfloat16),
        ]
    msk_spec = pl.BlockSpec((N_LEVEL + 1, CH, CH), lambda s, *_: (0, 0, 0))
    grid_spec = pltpu.PrefetchScalarGridSpec(
        num_scalar_prefetch=5,
        grid=(N_TOK // CH,),
        in_specs=[
            row_f, row_f,
            pl.BlockSpec((None, CH, D), lambda s, bkf, bkb, sid, fi, la: (0, bkf[s], 0)),
            row_b, row_b,
            pl.BlockSpec((None, CH, D), lambda s, bkf, bkb, sid, fi, la: (1, bkb[s], 0)),
            pl.BlockSpec(st_shape, lambda s, bkf, bkb, sid, fi, la: (jnp.maximum(sid[s] - BATCH, 0), 0, 0, 0, 0)),
            msk_spec, msk_spec,
        ],
        out_specs=[
            row_f, row_b,
            pl.BlockSpec(st_shape, lambda s, bkf, bkb, sid, fi, la: (jnp.minimum(sid[s], BATCH - 1), 0, 0, 0, 0)),
        ],
        scratch_shapes=scratch,
    )
    return pl.pallas_call(
        _rec_kernel,
        grid_spec=grid_spec,
        out_shape=[
            jax.ShapeDtypeStruct((N_TOK, D), jnp.float32),
            jax.ShapeDtypeStruct((N_TOK, D), jnp.float32),
            jax.ShapeDtypeStruct((BATCH, 2, HEADS, HD, HD), jnp.float32),
        ],
        compiler_params=_cparams(("arbitrary",)),
        name="hgrn_rec",
    )(*_rec_tables(), q, i, lf, q, i, lf, s0, jnp.asarray(_level_masks(False)), jnp.asarray(_level_masks(True)))


def _hgrn_out_kernel(x_ref, mod_ref, g_ref, of_ref, ob_ref, gt_ref, gn_ref, w_ref, o_ref, oh_ref):
    for hd in range(HEADS):
        sl = slice(hd * HD, (hd + 1) * HD)
        o = of_ref[:, sl] + ob_ref[:, sl]
        o = _rms(o, gn_ref[...]) * jax.nn.sigmoid(gt_ref[:, sl].astype(jnp.float32))
        oh_ref[:, sl] = o.astype(jnp.bfloat16)
    y = _dot(oh_ref[...], w_ref[...])
    o_ref[...] = _ada_post(x_ref[...], y, mod_ref, g_ref, 1, 3, 1.0)


def _hgrn_out(x, mod_l, g_l, o_f, o_b, gt, gnorm, w_out):
    row = pl.BlockSpec((TM, D), lambda i: (i, 0))
    return pl.pallas_call(
        _hgrn_out_kernel,
        grid=(N_TOK // TM,),
        in_specs=_tok_specs() + [row, row, row, _const_spec((1, HD)), _const_spec((D, D))],
        out_specs=row,
        out_shape=jax.ShapeDtypeStruct((N_TOK, D), jnp.float32),
        scratch_shapes=[pltpu.VMEM((TM, D), jnp.bfloat16)],
        compiler_params=_cparams(("parallel",)),
        name="hgrn_out",
    )(x, mod_l, g_l, o_f, o_b, gt, gnorm, w_out)


def _rope_tables():
    pos = np.arange(DEC_SEQ)
    row = (pos // GRID_W).astype(np.float64)
    col = (pos % GRID_W).astype(np.float64)
    nf = DA_DK // 4
    inv = ROPE_THETA ** (-np.arange(nf, dtype=np.float64) / nf)
    lane = np.arange(HD)
    axis = (lane % DA_DK) // (2 * nf)
    ang = np.where(axis[None, :] == 0, row[:, None], col[:, None]) * inv[lane % nf][None, :]
    sign = np.where((lane % (2 * nf)) < nf, -1.0, 1.0)
    cos, sin = np.cos(ang), np.sin(ang) * sign[None, :]
    return tuple(jnp.asarray(a, jnp.float32) for a in (cos, sin, cos.T, sin.T))


def kernel(x_prompt, x_sample, cache_k, cache_v, state_hgrn, c, c_ctx, w_mod, b_mod, norm_g,
           ffn_w_in, ffn_w_out, attn_w_in, attn_w_out, attn_lambda, attn_subln,
           hgrn_w_in, hgrn_w_out, hgrn_lower_bounds, hgrn_gnorm):
    bf = jnp.bfloat16
    x = (x_prompt.reshape(N_CTX, D), x_sample.reshape(N_SMP, D))
    cc = jnp.concatenate([c_ctx[None, :], c, jnp.zeros((N_GRP - 1 - DEC_BATCH, D), jnp.float32)], axis=0)
    mod = _modulation(cc, w_mod, b_mod).reshape(DEPTH, N_GRP, N_MOD, D)
    tables = _rope_tables()

    ks, vs, states = [], [], []
    for l in range(DEPTH):
        mod_l, g_l = mod[l], norm_g[l]
        x = _ffn(x, mod_l, g_l, ffn_w_in, ffn_w_out, l, 0)
        if l % 2 == 0:
            a = l // 2
            lam_init = 0.8 - 0.6 * math.exp(-0.3 * l)
            w_kt = attn_w_in[a][:, D:2 * D].T
            w_out = attn_w_out[a]
            subln = attn_subln[a].reshape(1, HD)
            qc, kt_new, vc_new = _attn_pre(x, mod_l, g_l, attn_w_in[a], w_kt, tables, ctx=True)
            qs, kt_smp, v_smp = _attn_pre(x, mod_l, g_l, attn_w_in[a], w_kt, tables, ctx=False)
            kct = jnp.transpose(cache_k[:, a], (0, 2, 3, 4, 1)).reshape(DEC_BATCH, D, PAST)
            xc = _attn(x, mod_l, g_l, qc, kt_new, vc_new, None, None, attn_lambda[a], subln, w_out,
                       ctx=True, lam_init=lam_init)
            xs = _attn(x, mod_l, g_l, qs, kt_smp, v_smp, kct, cache_v[:, a].reshape(DEC_BATCH, PAST, D),
                       attn_lambda[a], subln, w_out, ctx=False, lam_init=lam_init)
            x = (xc, xs)
            ks.append(jnp.transpose(kt_new.reshape(BATCH, HEADS, 2, DA_DK, SEQ), (0, 4, 1, 2, 3)))
            vs.append(vc_new.reshape(BATCH, SEQ, HEADS, HD))
        else:
            r = l // 2
            q, i, gt, lf = _hgrn_pre(x, mod_l, g_l, hgrn_w_in[r], hgrn_lower_bounds, l)
            s0 = state_hgrn[:, r].astype(jnp.float32)
            o_f, o_b, s_fin = _rec(q, i, lf, s0)
            x = _hgrn_out(x, mod_l, g_l, o_f, o_b, gt, hgrn_gnorm[r].reshape(1, HD), hgrn_w_out[r])
            states.append(s_fin)
        if l < DEPTH - 1:
            x = _ffn(x, mod_l, g_l, ffn_w_in, ffn_w_out, l, 2)

    assert not isinstance(x, tuple)
    y_prompt = _ffn(x, mod_l, g_l, ffn_w_in, ffn_w_out, DEPTH - 1, 2, base=0, rows=N_CTX)
    y_sample = _ffn(x, mod_l, g_l, ffn_w_in, ffn_w_out, DEPTH - 1, 2, base=N_CTX, rows=N_SMP)
    return (y_prompt.reshape(BATCH, SEQ, D), y_sample.reshape(DEC_BATCH, DEC_SEQ, D),
            jnp.stack(ks, axis=1), jnp.stack(vs, axis=1), jnp.stack(states, axis=1))
```

```python
import functools
import math

import numpy as np
import jax
import jax.numpy as jnp
from jax import lax
from jax.experimental import pallas as pl
from jax.experimental.pallas import tpu as pltpu

D = 1024
BATCH = 16
SEQ = 256
DEPTH = 2
DEC_BATCH = 4
DEC_SEQ = 2048
PAST = 256
GRID_W = 64
N_MOD = 9
EPS = 1e-6
HEADS = 8
HD = 128
DA_DK = 64
ROPE_THETA = 10000.0
FFN_H = 2816
LOG2E = 1.4426950408889634

N_CTX = BATCH * SEQ
N_SMP = DEC_BATCH * DEC_SEQ
N_TOK = N_CTX + N_SMP
N_GRP = 8

VMEM_LIMIT = 56 * 1024 * 1024
LANES = 128
SUB = 8

TM = 512
FFN_HC = 256
TQ = 256
KB = 256
CH = 128
N_SLAB = CH // SUB
N_LEVEL = 7
MAX_FAST_LOG2_DECAY = 200.0


def _cparams(sem):
    return pltpu.CompilerParams(dimension_semantics=sem, vmem_limit_bytes=VMEM_LIMIT)


def _grp_of_block(i, rows):
    nctx = N_CTX // rows
    per = DEC_SEQ // rows
    return jnp.where(i < nctx, 0, 1 + (i - nctx) // per)


def _rms(x, g):
    return (x * lax.rsqrt(jnp.mean(x * x, axis=-1, keepdims=True) + EPS)) * g


def _dot(a, b):
    return lax.dot_general(a, b, (((1,), (0,)), ((), ())), preferred_element_type=jnp.float32)


def _dot_nt(a, b):
    return lax.dot_general(a, b, (((1,), (1,)), ((), ())), preferred_element_type=jnp.float32)


def _dot_tn(a, b):
    return lax.dot_general(a, b, (((0,), (0,)), ((), ())), preferred_element_type=jnp.float32)


def _const_spec(shape):
    nd = len(shape)
    return pl.BlockSpec(shape, lambda *_: (0,) * nd, pipeline_mode=pl.Buffered(1))


def _mod_kernel(c_ref, w_ref, b_ref, o_ref):
    c = c_ref[...]
    s = (c * jax.nn.sigmoid(c)).astype(jnp.bfloat16)
    o_ref[...] = _dot(s, w_ref[...].astype(jnp.bfloat16)) + b_ref[...]


def _modulation(cc, w_mod, b_mod):
    tn = 1024
    nt = (N_MOD * D) // tn
    return pl.pallas_call(
        _mod_kernel,
        grid=(DEPTH, nt),
        in_specs=[
            pl.BlockSpec((N_GRP, D), lambda l, j: (0, 0)),
            pl.BlockSpec((None, D, tn), lambda l, j: (l, 0, j)),
            pl.BlockSpec((None, 1, tn), lambda l, j: (l, 0, j)),
        ],
        out_specs=pl.BlockSpec((None, N_GRP, tn), lambda l, j: (l, 0, j)),
        out_shape=jax.ShapeDtypeStruct((DEPTH, N_GRP, N_MOD * D), jnp.float32),
        compiler_params=_cparams(("parallel", "parallel")),
        name="modulation",
    )(cc, w_mod, b_mod.reshape(DEPTH, 1, N_MOD * D))


def _ada_pre(x, mod_ref, g_ref, j, gi):
    shift = mod_ref[3 * j:3 * j + 1, :]
    scale = mod_ref[3 * j + 1:3 * j + 2, :]
    return _rms(x, g_ref[gi:gi + 1, :]) * (1.0 + scale) + shift


def _ada_post(x, y, mod_ref, g_ref, j, gi, res_w):
    gate = mod_ref[3 * j + 2:3 * j + 3, :]
    return x + res_w * gate * _rms(y, g_ref[gi:gi + 1, :])


def _tok_specs():
    return [
        pl.BlockSpec((TM, D), lambda i: (i, 0)),
        pl.BlockSpec((None, N_MOD, D), lambda i: (_grp_of_block(i, TM), 0, 0)),
        _const_spec((6, D)),
    ]


def _ffn_kernel(*refs, j, pair):
    if pair:
        xa_ref, xb_ref, mod_ref, g_ref, win_ref, wout_ref, o_ref, acc_ref = refs
        x = jnp.where(pl.program_id(0) < N_CTX // TM, xa_ref[...], xb_ref[...])
    else:
        x_ref, mod_ref, g_ref, win_ref, wout_ref, o_ref, acc_ref = refs
        x = x_ref[...]
    h = _ada_pre(x, mod_ref, g_ref, j, 2 * j).astype(jnp.bfloat16)
    for c in range(FFN_H // FFN_HC):
        lo = c * FFN_HC
        a = _dot(h, win_ref[:, lo:lo + FFN_HC])
        b = _dot(h, win_ref[:, FFN_H + lo:FFN_H + lo + FFN_HC])
        u = ((a * jax.nn.sigmoid(a)) * b).astype(jnp.bfloat16)
        part = _dot(u, wout_ref[lo:lo + FFN_HC, :])
        if c == 0:
            acc_ref[...] = part
        else:
            acc_ref[...] += part
    o_ref[...] = _ada_post(x, acc_ref[...], mod_ref, g_ref, j, 2 * j + 1, 0.5)


def _ffn(x, mod_l, g_l, w_in, w_out, l, j, *, base=0, rows=N_TOK):
    pair = isinstance(x, tuple)
    nctx = N_CTX // TM
    b0 = base // TM
    w_idx = (l, j // 2, 0, 0)
    if pair:
        assert base == 0 and rows == N_TOK
        x_specs = [pl.BlockSpec((TM, D), lambda i: (jnp.minimum(i, nctx - 1), 0)),
                   pl.BlockSpec((TM, D), lambda i: (jnp.maximum(i - nctx, 0), 0))]
        x_args = list(x)
    else:
        x_specs = [pl.BlockSpec((TM, D), lambda i: (i + b0, 0))]
        x_args = [x]
    return pl.pallas_call(
        functools.partial(_ffn_kernel, j=j, pair=pair),
        grid=(rows // TM,),
        in_specs=x_specs + [
            pl.BlockSpec((None, N_MOD, D), lambda i: (_grp_of_block(i + b0, TM), 0, 0)),
            _const_spec((6, D)),
            pl.BlockSpec((None, None, D, 2 * FFN_H), lambda i: w_idx, pipeline_mode=pl.Buffered(1)),
            pl.BlockSpec((None, None, FFN_H, D), lambda i: w_idx, pipeline_mode=pl.Buffered(1))],
        out_specs=pl.BlockSpec((TM, D), lambda i: (i, 0)),
        out_shape=jax.ShapeDtypeStruct((rows, D), jnp.float32),
        scratch_shapes=[pltpu.VMEM((TM, D), jnp.float32)],
        compiler_params=_cparams(("parallel",)),
        name=f"ffn{j}",
    )(*x_args, mod_l, g_l, w_in, w_out)


def _rope_lanes(x, cos, sin_signed):
    lane = lax.broadcasted_iota(jnp.int32, x.shape, 1)
    partner = jnp.where((lane % 32) < 16, pltpu.roll(x, LANES - 16, 1), pltpu.roll(x, 16, 1))
    return x * cos + partner * sin_signed


def _rope_rows(x, cos_t, sin_signed_t):
    parts = []
    for g in range(HD // 32):
        parts += [x[g * 32 + 16:g * 32 + 32, :], x[g * 32:g * 32 + 16, :]]
    return x * cos_t + jnp.concatenate(parts, axis=0) * sin_signed_t


def _attn_pre_kernel(x_ref, mod_ref, g_ref, w_ref, wkt_ref, cos_ref, sin_ref, cost_ref, sint_ref,
                     q_ref, kt_ref, v_ref, *, rope):
    x = x_ref[...]
    h = _ada_pre(x, mod_ref, g_ref, 1, 2).astype(jnp.bfloat16)
    qscale = DA_DK ** -0.5 * LOG2E
    zq = _dot(h, w_ref[:, 0:D])
    zkt = _dot_nt(wkt_ref[...], h)
    v_ref[...] = _dot(h, w_ref[:, 2 * D:3 * D]).astype(v_ref.dtype)
    for hd in range(HEADS):
        sl = slice(hd * HD, (hd + 1) * HD)
        q = zq[:, sl]
        k = zkt[sl, :]
        if rope:
            q = _rope_lanes(q, cos_ref[...], sin_ref[...])
            k = _rope_rows(k, cost_ref[...], sint_ref[...])
        q_ref[:, sl] = (q * qscale).astype(q_ref.dtype)
        kt_ref[sl, :] = k.astype(kt_ref.dtype)


def _attn_pre(x, mod_l, g_l, w_in, w_kt, tables, *, ctx):
    nb, seq = (BATCH, SEQ) if ctx else (DEC_BATCH, DEC_SEQ)
    t = min(TM, seq)
    rows = nb * seq
    base = 0 if ctx else N_CTX // t
    per = seq // t
    ptab = DEC_SEQ // t
    kv_dtype = jnp.float32 if ctx else jnp.bfloat16
    out_spec = pl.BlockSpec((t, D), lambda i: (i, 0))
    return pl.pallas_call(
        functools.partial(_attn_pre_kernel, rope=not ctx),
        grid=(rows // t,),
        in_specs=[
            pl.BlockSpec((t, D), lambda i: (i + base, 0)),
            pl.BlockSpec((None, N_MOD, D), lambda i: (_grp_of_block(i + base, t), 0, 0)),
            _const_spec((6, D)), _const_spec((D, 3 * D)), _const_spec((D, D)),
            pl.BlockSpec((t, HD), lambda i: (i % ptab, 0)),
            pl.BlockSpec((t, HD), lambda i: (i % ptab, 0)),
            pl.BlockSpec((HD, t), lambda i: (0, i % ptab)),
            pl.BlockSpec((HD, t), lambda i: (0, i % ptab)),
        ],
        out_specs=[out_spec, pl.BlockSpec((None, D, t), lambda i: (i // per, 0, i % per)), out_spec],
        out_shape=[
            jax.ShapeDtypeStruct((rows, D), jnp.bfloat16),
            jax.ShapeDtypeStruct((nb, D, seq), kv_dtype),
            jax.ShapeDtypeStruct((rows, D), kv_dtype),
        ],
        compiler_params=_cparams(("parallel",)),
        name="attn_pre_ctx" if ctx else "attn_pre_smp",
    )(x, mod_l, g_l, w_in, w_kt, *tables)


def _attn_kernel(*refs, lam_init, cache):
    if cache:
        (x_ref, mod_ref, g_ref, q_ref, k_ref, v_ref, kc_ref, vc_ref,
         lam_ref, sub_ref, w_ref, o_ref, oh_ref, s_ref, e_ref) = refs
    else:
        (x_ref, mod_ref, g_ref, q_ref, k_ref, v_ref,
         lam_ref, sub_ref, w_ref, o_ref, oh_ref, s_ref, e_ref) = refs
    lp = lam_ref[...]
    lam = (jnp.exp(jnp.sum(lp[0:1] * lp[1:2], axis=-1, keepdims=True))
           - jnp.exp(jnp.sum(lp[2:3] * lp[3:4], axis=-1, keepdims=True)) + lam_init)
    tq = q_ref.shape[0]
    lane = lax.broadcasted_iota(jnp.int32, (tq, HD), 1)
    bf = jnp.bfloat16
    n_own = k_ref.shape[1] // KB
    n_kb = n_own + (1 if cache else 0)

    def keys_t(sl, kb):
        if kb < n_own:
            return k_ref[sl, kb * KB:(kb + 1) * KB].astype(bf)
        return kc_ref[sl, :].astype(bf)

    def values(sl, kb):
        if kb < n_own:
            return v_ref[kb * KB:(kb + 1) * KB, sl].astype(bf)
        return vc_ref[:, sl].astype(bf)

    def head_slice(hd):
        return slice(hd * HD, (hd + 1) * HD)

    def kcols(kb):
        return slice(kb * KB, (kb + 1) * KB)

    def stacked_q(hd):
        qh = q_ref[:, head_slice(hd)]
        return jnp.concatenate([jnp.where(lane < DA_DK, qh, jnp.zeros_like(qh)),
                                jnp.where(lane >= DA_DK, qh, jnp.zeros_like(qh))], axis=0)

    st = [dict() for _ in range(HEADS)]
    for slot in range(HEADS + 2):
        h1, h2, h3 = slot, slot - 1, slot - 2
        if h1 < HEADS:
            st[h1]["qq"] = stacked_q(h1)
        if 0 <= h3 < HEADS:
            tot = st[h3]["tot"]
            st[h3]["c"] = jnp.broadcast_to(lam * tot[:tq] / tot[tq:], (tq, KB)).astype(bf)
        for kb in range(n_kb):
            if h1 < HEADS:
                s = _dot(st[h1]["qq"], keys_t(head_slice(h1), kb))
                s_ref[h1 % 2, :, kcols(kb)] = s
                mp = jnp.maximum(s[:, :LANES], s[:, LANES:])
                st[h1]["m"] = mp if kb == 0 else jnp.maximum(st[h1]["m"], mp)
            if 0 <= h2 < HEADS:
                e = jnp.exp2(s_ref[h2 % 2, :, kcols(kb)] - st[h2]["mrow"])
                ep = e[:, :LANES] + e[:, LANES:]
                st[h2]["l"] = ep if kb == 0 else st[h2]["l"] + ep
                e_ref[h2 % 2, :, kcols(kb)] = e.astype(bf)
            if 0 <= h3 < HEADS:
                a = e_ref[h3 % 2, :tq, kcols(kb)] - st[h3]["c"] * e_ref[h3 % 2, tq:, kcols(kb)]
                part = _dot(a, values(head_slice(h3), kb))
                st[h3]["o"] = part if kb == 0 else st[h3]["o"] + part
        if h1 < HEADS:
            st[h1]["mrow"] = jnp.max(st[h1]["m"], axis=-1, keepdims=True)
        if 0 <= h2 < HEADS:
            st[h2]["tot"] = jnp.sum(st[h2]["l"], axis=-1, keepdims=True)
        if 0 <= h3 < HEADS:
            o = st[h3]["o"] * (1.0 / st[h3]["tot"][:tq])
            o = _rms(o, sub_ref[...]) * (1.0 - lam_init)
            oh_ref[:, head_slice(h3)] = o.astype(bf)
            st[h3].clear()
    y = _dot(oh_ref[...], w_ref[...])
    o_ref[...] = _ada_post(x_ref[...], y, mod_ref, g_ref, 1, 3, 1.0)


def _attn(x, mod_l, g_l, q, k, v, kc, vc, lam_p, subln, w_out, *, ctx, lam_init):
    nb, seq = (BATCH, SEQ) if ctx else (DEC_BATCH, DEC_SEQ)
    rows = nb * seq
    base = 0 if ctx else N_CTX // TQ
    nq = seq // TQ
    assert seq % KB == 0 and PAST == KB
    n_keys = seq if ctx else seq + PAST
    row_spec = pl.BlockSpec((TQ, D), lambda b, i: (b * nq + i, 0))
    in_specs = [
        pl.BlockSpec((TQ, D), lambda b, i: (base + b * nq + i, 0)),
        pl.BlockSpec((None, N_MOD, D), lambda b, i: (_grp_of_block(base + b * nq + i, TQ), 0, 0)),
        pl.BlockSpec((6, D), lambda b, i: (0, 0)),
        row_spec,
        pl.BlockSpec((None, D, seq), lambda b, i: (b, 0, 0)),
        pl.BlockSpec((seq, D), lambda b, i: (b, 0)),
    ]
    args = [x, mod_l, g_l, q, k, v]
    if not ctx:
        in_specs += [pl.BlockSpec((None, D, PAST), lambda b, i: (b, 0, 0)),
                     pl.BlockSpec((None, PAST, D), lambda b, i: (b, 0, 0))]
        args += [kc, vc]
    in_specs += [
        pl.BlockSpec((4, DA_DK), lambda b, i: (0, 0)),
        pl.BlockSpec((1, HD), lambda b, i: (0, 0)),
        pl.BlockSpec((D, D), lambda b, i: (0, 0), pipeline_mode=pl.Buffered(1)),
    ]
    args += [lam_p, subln, w_out]
    return pl.pallas_call(
        functools.partial(_attn_kernel, lam_init=lam_init, cache=not ctx),
        grid=(nb, nq),
        in_specs=in_specs,
        out_specs=row_spec,
        out_shape=jax.ShapeDtypeStruct((rows, D), jnp.float32),
        scratch_shapes=[pltpu.VMEM((TQ, D), jnp.bfloat16),
                        pltpu.VMEM((2, 2 * TQ, n_keys), jnp.float32),
                        pltpu.VMEM((2, 2 * TQ, n_keys), jnp.bfloat16)],
        compiler_params=_cparams(("parallel", "parallel")),
        name="attn_ctx" if ctx else "attn_smp",
    )(*args)


def _hgrn_pre_kernel(x_ref, mod_ref, g_ref, w_ref, lb_ref, q_ref, i_ref, gt_ref, lf_ref, *, layer):
    x = x_ref[...]
    h = _ada_pre(x, mod_ref, g_ref, 1, 2).astype(jnp.bfloat16)

    def proj(n):
        return _dot(h, w_ref[:, n * D:(n + 1) * D])

    def log_gate(d, ff):
        raw = [lb_ref[l, d:d + 1, :] for l in range(DEPTH)]
        m = functools.reduce(jnp.maximum, raw)
        ex = [jnp.exp(r - m) for r in raw]
        tot = functools.reduce(lambda a, b: a + b, ex)
        soft = [e / tot for e in ex]
        lb = functools.reduce(lambda a, b: a + b, soft[:layer + 1]) - soft[0]
        return jnp.log(lb + (1.0 - lb) * jax.nn.sigmoid(ff))

    ff = proj(1)
    q_ref[...] = proj(0).astype(q_ref.dtype)
    lf_ref[0] = log_gate(0, ff)
    fb = proj(2)
    i_ref[...] = proj(3).astype(i_ref.dtype)
    lf_ref[1] = log_gate(1, fb)
    gt_ref[...] = proj(4).astype(gt_ref.dtype)


def _hgrn_pre(x, mod_l, g_l, w_in, lb_raw, layer):
    row = pl.BlockSpec((TM, D), lambda i: (i, 0))
    return pl.pallas_call(
        functools.partial(_hgrn_pre_kernel, layer=layer),
        grid=(N_TOK // TM,),
        in_specs=_tok_specs() + [_const_spec((D, 5 * D)), _const_spec((DEPTH, 2, D))],
        out_specs=[row, row, row, pl.BlockSpec((2, TM, D), lambda i: (0, i, 0))],
        out_shape=[
            jax.ShapeDtypeStruct((N_TOK, D), jnp.bfloat16),
            jax.ShapeDtypeStruct((N_TOK, D), jnp.bfloat16),
            jax.ShapeDtypeStruct((N_TOK, D), jnp.bfloat16),
            jax.ShapeDtypeStruct((2, N_TOK, D), jnp.float32),
        ],
        compiler_params=_cparams(("parallel",)),
        name="hgrn_pre",
    )(x, mod_l, g_l, w_in, lb_raw)


def _level_masks(rev):
    t = np.arange(CH)[:, None]
    s = np.arange(CH)[None, :]
    out = []
    for p in range(N_LEVEL):
        same = (t >> (p + 1)) == (s >> (p + 1))
        tb, sb = (t >> p) & 1, (s >> p) & 1
        out.append(same & ((tb == 0) & (sb == 1) if rev else (tb == 1) & (sb == 0)))
    out.append(t == s)
    return np.stack(out).astype(np.float32)


def _row_bcast(x, r):
    return jnp.broadcast_to(x[r:r + 1, :], x.shape)


def _rec_cumsum(lf_ref, b_ref, br_ref, rev):
    sub = lax.broadcasted_iota(jnp.int32, (SUB, D), 0)
    order = range(N_SLAB - 1, -1, -1) if rev else range(N_SLAB)
    edge = 0 if rev else SUB - 1
    carry = None
    for v in order:
        rows = slice(v * SUB, (v + 1) * SUB)
        c = lf_ref[rows, :] * LOG2E
        for k in (1, 2, 4):
            if rev:
                c = c + jnp.where(sub < SUB - k, pltpu.roll(c, SUB - k, 0), 0.0)
            else:
                c = c + jnp.where(sub >= k, pltpu.roll(c, k, 0), 0.0)
        if carry is not None:
            c = c + carry
        carry = _row_bcast(c, edge)
        b_ref[rows, :] = c
        br_ref[rows, :] = carry
    return carry


def _rec_fast(lf_ref, q_ref, i_ref, b_ref, b_last, qs_ref, ks_ref, st_ref, o_ref, rev):
    bf = jnp.bfloat16
    half = 0.5 * b_last
    half2 = jnp.concatenate([half, half], axis=0)
    for v2 in range(N_SLAB // 2):
        rows = slice(v2 * 2 * SUB, (v2 + 1) * 2 * SUB)
        k = 1.0 - jnp.exp2(lf_ref[rows, :] * LOG2E)
        d = b_ref[rows, :] - half2
        qs_ref[rows, :] = (q_ref[rows, :].astype(jnp.float32) * jnp.exp2(d)).astype(bf)
        ks_ref[rows, :] = (k * jnp.exp2(-d)).astype(bf)
    scale = jnp.exp2(half[0:1, :])
    decay = scale * scale
    t = lax.broadcasted_iota(jnp.int32, (CH, CH), 0)
    s = lax.broadcasted_iota(jnp.int32, (CH, CH), 1)
    seen = (s >= t) if rev else (s <= t)
    for hd in range(HEADS):
        sl = slice(hd * HD, (hd + 1) * HD)
        qs, ks, vh = qs_ref[:, sl], ks_ref[:, sl], i_ref[:, sl]
        a = jnp.where(seen, _dot_nt(qs, ks), 0.0)
        st = st_ref[hd]
        o = _dot(a.astype(bf), vh) + _dot_nt(qs, (st * scale[:, sl]).astype(bf))
        o_ref[:, sl] = o.astype(o_ref.dtype)
        st_ref[hd] = st * decay[:, sl] + _dot_tn(vh, ks) * scale[:, sl]


def _rec_prepare(lf_ref, q_ref, b_ref, br_ref, b_last, qt_ref, kt_ref, rev):
    bf = jnp.bfloat16
    sub = lax.broadcasted_iota(jnp.int32, (SUB, D), 0)
    zeros = jnp.zeros((SUB, D), jnp.float32)
    for v2 in range(N_SLAB // 2):
        rows = slice(v2 * 2 * SUB, (v2 + 1) * 2 * SUB)
        f = jnp.exp2(lf_ref[rows, :] * LOG2E)
        k = 1.0 - f
        q = q_ref[rows, :].astype(jnp.float32)
        b = b_ref[rows, :]
        qt_ref[0, rows, :] = (q * f).astype(bf)
        kt_ref[0, rows, :] = k.astype(bf)
        half = (slice(0, SUB), slice(SUB, 2 * SUB))
        for p in range(1, N_LEVEL):
            if p >= 3:
                w = p - 3
                es, qside = [], []
                for n, v in enumerate((2 * v2, 2 * v2 + 1)):
                    hi = v & ~((1 << (w + 1)) - 1)
                    vr = hi | (1 << w) if rev else hi | ((1 << w) - 1)
                    r = br_ref[vr * SUB:(vr + 1) * SUB, :]
                    qside.append(((v >> w) & 1) == (0 if rev else 1))
                    es.append(jnp.exp2(b[half[n], :] - r) if qside[n] else jnp.exp2(r - b[half[n], :]))
                if qside[0] == qside[1]:
                    e = jnp.concatenate(es, axis=0)
                    if qside[0]:
                        qt_ref[p, rows, :] = (q * e).astype(bf)
                    else:
                        kt_ref[p, rows, :] = (k * e).astype(bf)
                else:
                    qe = [q[half[n], :] * es[n] if qside[n] else zeros for n in range(2)]
                    ke = [zeros if qside[n] else k[half[n], :] * es[n] for n in range(2)]
                    qt_ref[p, rows, :] = jnp.concatenate(qe, axis=0).astype(bf)
                    kt_ref[p, rows, :] = jnp.concatenate(ke, axis=0).astype(bf)
            else:
                es = []
                for n in range(2):
                    bv = b[half[n], :]
                    if p == 2:
                        r = _row_bcast(bv, 4 if rev else 3)
                    else:
                        lo, hi_r = (2, 6) if rev else (1, 5)
                        r = jnp.where(sub < 4, _row_bcast(bv, lo), _row_bcast(bv, hi_r))
                    es.append(jnp.exp2(-jnp.abs(bv - r)))
                e = jnp.concatenate(es, axis=0)
                qt_ref[p, rows, :] = (q * e).astype(bf)
                kt_ref[p, rows, :] = (k * e).astype(bf)
        bl = jnp.concatenate([b_last, b_last], axis=0)
        qt_ref[N_LEVEL, rows, :] = (q * jnp.exp2(b)).astype(bf)
        kt_ref[N_LEVEL, rows, :] = (k * jnp.exp2(bl - b)).astype(bf)
    return jnp.exp2(b_last[0:1, :])


def _rec_heads(q_ref, i_ref, qt_ref, kt_ref, msk_ref, st_ref, o_ref, decay):
    bf = jnp.bfloat16
    top = N_LEVEL - 1
    for hd in range(HEADS):
        sl = slice(hd * HD, (hd + 1) * HD)
        a = _dot_nt(qt_ref[top, :, sl], kt_ref[top, :, sl])
        a = a + msk_ref[N_LEVEL] * _dot_nt(q_ref[:, sl], kt_ref[0, :, sl])
        for p in range(top):
            a = a + msk_ref[p] * _dot_nt(qt_ref[p, :, sl], kt_ref[p, :, sl])
        vh = i_ref[:, sl]
        st = st_ref[hd]
        o = _dot(a.astype(bf), vh) + _dot_nt(qt_ref[N_LEVEL, :, sl], st.astype(bf))
        o_ref[:, sl] = o.astype(o_ref.dtype)
        st_ref[hd] = st * decay[:, sl] + _dot_tn(vh, kt_ref[N_LEVEL, :, sl])


def _rec_kernel(blkf_ref, blkb_ref, sid_ref, first_ref, last_ref,
                qf_ref, if_ref, lff_ref, qb_ref, ib_ref, lfb_ref, s0_ref, mskf_ref, mskb_ref,
                of_ref, ob_ref, sfin_ref,
                st_ref, bf_ref, brf_ref, qtf_ref, ktf_ref, bb_ref, brb_ref, qtb_ref, ktb_ref):
    step = pl.program_id(0)

    @pl.when(step == 0)
    def _():
        for ref in (qtf_ref, ktf_ref, qtb_ref, ktb_ref):
            ref[...] = jnp.zeros_like(ref)

    @pl.when(first_ref[step] == 1)
    def _():
        st_ref[...] = jnp.zeros_like(st_ref)

    @pl.when(first_ref[step] == 2)
    def _():
        for d in range(2):
            for hd in range(HEADS):
                st_ref[d, hd] = s0_ref[d, hd].T

    last_f = _rec_cumsum(lff_ref, bf_ref, brf_ref, False)
    last_b = _rec_cumsum(lfb_ref, bb_ref, brb_ref, True)
    moderate = jnp.min(jnp.minimum(last_f, last_b)) >= -MAX_FAST_LOG2_DECAY

    @pl.when(moderate)
    def _():
        _rec_fast(lff_ref, qf_ref, if_ref, bf_ref, last_f, qtf_ref.at[N_LEVEL], ktf_ref.at[N_LEVEL],
                  st_ref.at[0], of_ref, False)
        _rec_fast(lfb_ref, qb_ref, ib_ref, bb_ref, last_b, qtb_ref.at[N_LEVEL], ktb_ref.at[N_LEVEL],
                  st_ref.at[1], ob_ref, True)

    @pl.when(jnp.logical_not(moderate))
    def _():
        decay_f = _rec_prepare(lff_ref, qf_ref, bf_ref, brf_ref, last_f, qtf_ref, ktf_ref, False)
        decay_b = _rec_prepare(lfb_ref, qb_ref, bb_ref, brb_ref, last_b, qtb_ref, ktb_ref, True)
        _rec_heads(qf_ref, if_ref, qtf_ref, ktf_ref, mskf_ref, st_ref.at[0], of_ref, decay_f)
        _rec_heads(qb_ref, ib_ref, qtb_ref, ktb_ref, mskb_ref, st_ref.at[1], ob_ref, decay_b)

    @pl.when(last_ref[step] == 1)
    def _():
        for d in range(2):
            for hd in range(HEADS):
                sfin_ref[d, hd] = st_ref[d, hd].T


def _rec_tables():
    blkf, blkb, sid, first, last = [], [], [], [], []
    seqs = [(b * (SEQ // CH), SEQ // CH) for b in range(BATCH)]
    seqs += [(N_CTX // CH + b * (DEC_SEQ // CH), DEC_SEQ // CH) for b in range(DEC_BATCH)]
    for n, (start, nc) in enumerate(seqs):
        ctx = n < BATCH
        for c in range(nc):
            blkf.append(start + c)
            blkb.append(start + nc - 1 - c)
            sid.append(n)
            first.append((1 if ctx else 2) if c == 0 else 0)
            last.append(int(ctx and c == nc - 1))
    return [jnp.asarray(np.asarray(a, np.int32)) for a in (blkf, blkb, sid, first, last)]


def _rec(q, i, lf, s0):
    row_f = pl.BlockSpec((CH, D), lambda s, bkf, bkb, sid, fi, la: (bkf[s], 0))
    row_b = pl.BlockSpec((CH, D), lambda s, bkf, bkb, sid, fi, la: (bkb[s], 0))
    st_shape = (None, 2, HEADS, HD, HD)
    scratch = [pltpu.VMEM((2, HEADS, HD, HD), jnp.float32)]
    for _ in range(2):
        scratch += [
            pltpu.VMEM((CH, D), jnp.float32),
            pltpu.VMEM((CH, D), jnp.float32),
            pltpu.VMEM((N_LEVEL + 1, CH, D), jnp.bfloat16),
            pltpu.VMEM((N_LEVEL + 1, CH, D), jnp.bfloat16),
        ]
    msk_spec = pl.BlockSpec((N_LEVEL + 1, CH, CH), lambda s, *_: (0, 0, 0))
    grid_spec = pltpu.PrefetchScalarGridSpec(
        num_scalar_prefetch=5,
        grid=(N_TOK // CH,),
        in_specs=[
            row_f, row_f,
            pl.BlockSpec((None, CH, D), lambda s, bkf, bkb, sid, fi, la: (0, bkf[s], 0)),
            row_b, row_b,
            pl.BlockSpec((None, CH, D), lambda s, bkf, bkb, sid, fi, la: (1, bkb[s], 0)),
            pl.BlockSpec(st_shape, lambda s, bkf, bkb, sid, fi, la: (jnp.maximum(sid[s] - BATCH, 0), 0, 0, 0, 0)),
            msk_spec, msk_spec,
        ],
        out_specs=[
            row_f, row_b,
            pl.BlockSpec(st_shape, lambda s, bkf, bkb, sid, fi, la: (jnp.minimum(sid[s], BATCH - 1), 0, 0, 0, 0)),
        ],
        scratch_shapes=scratch,
    )
    return pl.pallas_call(
        _rec_kernel,
        grid_spec=grid_spec,
        out_shape=[
            jax.ShapeDtypeStruct((N_TOK, D), jnp.bfloat16),
            jax.ShapeDtypeStruct((N_TOK, D), jnp.bfloat16),
            jax.ShapeDtypeStruct((BATCH, 2, HEADS, HD, HD), jnp.float32),
        ],
        compiler_params=_cparams(("arbitrary",)),
        name="hgrn_rec",
    )(*_rec_tables(), q, i, lf, q, i, lf, s0, jnp.asarray(_level_masks(False)), jnp.asarray(_level_masks(True)))


def _hgrn_out_kernel(x_ref, mod_ref, g_ref, of_ref, ob_ref, gt_ref, gn_ref, w_ref, o_ref, oh_ref):
    for hd in range(HEADS):
        sl = slice(hd * HD, (hd + 1) * HD)
        o = of_ref[:, sl].astype(jnp.float32) + ob_ref[:, sl].astype(jnp.float32)
        o = _rms(o, gn_ref[...]) * jax.nn.sigmoid(gt_ref[:, sl].astype(jnp.float32))
        oh_ref[:, sl] = o.astype(jnp.bfloat16)
    y = _dot(oh_ref[...], w_ref[...])
    o_ref[...] = _ada_post(x_ref[...], y, mod_ref, g_ref, 1, 3, 1.0)


def _hgrn_out(x, mod_l, g_l, o_f, o_b, gt, gnorm, w_out):
    row = pl.BlockSpec((TM, D), lambda i: (i, 0))
    return pl.pallas_call(
        _hgrn_out_kernel,
        grid=(N_TOK // TM,),
        in_specs=_tok_specs() + [row, row, row, _const_spec((1, HD)), _const_spec((D, D))],
        out_specs=row,
        out_shape=jax.ShapeDtypeStruct((N_TOK, D), jnp.float32),
        scratch_shapes=[pltpu.VMEM((TM, D), jnp.bfloat16)],
        compiler_params=_cparams(("parallel",)),
        name="hgrn_out",
    )(x, mod_l, g_l, o_f, o_b, gt, gnorm, w_out)


def _rope_tables():
    pos = np.arange(DEC_SEQ)
    row = (pos // GRID_W).astype(np.float64)
    col = (pos % GRID_W).astype(np.float64)
    nf = DA_DK // 4
    inv = ROPE_THETA ** (-np.arange(nf, dtype=np.float64) / nf)
    lane = np.arange(HD)
    axis = (lane % DA_DK) // (2 * nf)
    ang = np.where(axis[None, :] == 0, row[:, None], col[:, None]) * inv[lane % nf][None, :]
    sign = np.where((lane % (2 * nf)) < nf, -1.0, 1.0)
    cos, sin = np.cos(ang), np.sin(ang) * sign[None, :]
    return tuple(jnp.asarray(a, jnp.float32) for a in (cos, sin, cos.T, sin.T))


def kernel(x_prompt, x_sample, cache_k, cache_v, state_hgrn, c, c_ctx, w_mod, b_mod, norm_g,
           ffn_w_in, ffn_w_out, attn_w_in, attn_w_out, attn_lambda, attn_subln,
           hgrn_w_in, hgrn_w_out, hgrn_lower_bounds, hgrn_gnorm):
    bf = jnp.bfloat16
    x = (x_prompt.reshape(N_CTX, D), x_sample.reshape(N_SMP, D))
    cc = jnp.concatenate([c_ctx[None, :], c, jnp.zeros((N_GRP - 1 - DEC_BATCH, D), jnp.float32)], axis=0)
    mod = _modulation(cc, w_mod, b_mod).reshape(DEPTH, N_GRP, N_MOD, D)
    tables = _rope_tables()

    ks, vs, states = [], [], []
    for l in range(DEPTH):
        mod_l, g_l = mod[l], norm_g[l]
        x = _ffn(x, mod_l, g_l, ffn_w_in, ffn_w_out, l, 0)
        if l % 2 == 0:
            a = l // 2
            lam_init = 0.8 - 0.6 * math.exp(-0.3 * l)
            w_kt = attn_w_in[a][:, D:2 * D].T
            w_out = attn_w_out[a]
            subln = attn_subln[a].reshape(1, HD)
            qc, kt_new, vc_new = _attn_pre(x, mod_l, g_l, attn_w_in[a], w_kt, tables, ctx=True)
            qs, kt_smp, v_smp = _attn_pre(x, mod_l, g_l, attn_w_in[a], w_kt, tables, ctx=False)
            kct = jnp.transpose(cache_k[:, a], (0, 2, 3, 4, 1)).reshape(DEC_BATCH, D, PAST)
            xc = _attn(x, mod_l, g_l, qc, kt_new, vc_new, None, None, attn_lambda[a], subln, w_out,
                       ctx=True, lam_init=lam_init)
            xs = _attn(x, mod_l, g_l, qs, kt_smp, v_smp, kct, cache_v[:, a].reshape(DEC_BATCH, PAST, D),
                       attn_lambda[a], subln, w_out, ctx=False, lam_init=lam_init)
            x = (xc, xs)
            ks.append(jnp.transpose(kt_new.reshape(BATCH, HEADS, 2, DA_DK, SEQ), (0, 4, 1, 2, 3)))
            vs.append(vc_new.reshape(BATCH, SEQ, HEADS, HD))
        else:
            r = l // 2
            q, i, gt, lf = _hgrn_pre(x, mod_l, g_l, hgrn_w_in[r], hgrn_lower_bounds, l)
            s0 = state_hgrn[:, r].astype(jnp.float32)
            o_f, o_b, s_fin = _rec(q, i, lf, s0)
            x = _hgrn_out(x, mod_l, g_l, o_f, o_b, gt, hgrn_gnorm[r].reshape(1, HD), hgrn_w_out[r])
            states.append(s_fin)
        if l < DEPTH - 1:
            x = _ffn(x, mod_l, g_l, ffn_w_in, ffn_w_out, l, 2)

    assert not isinstance(x, tuple)
    y_prompt = _ffn(x, mod_l, g_l, ffn_w_in, ffn_w_out, DEPTH - 1, 2, base=0, rows=N_CTX)
    y_sample = _ffn(x, mod_l, g_l, ffn_w_in, ffn_w_out, DEPTH - 1, 2, base=N_CTX, rows=N_SMP)
    return (y_prompt.reshape(BATCH, SEQ, D), y_sample.reshape(DEC_BATCH, DEC_SEQ, D),
            jnp.stack(ks, axis=1), jnp.stack(vs, axis=1), jnp.stack(states, axis=1))
```

```python
import functools
import math

import numpy as np
import jax
import jax.numpy as jnp
from jax import lax
from jax.experimental import pallas as pl
from jax.experimental.pallas import tpu as pltpu

D = 1024
BATCH = 16
SEQ = 256
DEPTH = 2
DEC_BATCH = 4
DEC_SEQ = 2048
PAST = 256
GRID_W = 64
N_MOD = 9
EPS = 1e-6
HEADS = 8
HD = 128
DA_DK = 64
ROPE_THETA = 10000.0
FFN_H = 2816
LOG2E = 1.4426950408889634

N_CTX = BATCH * SEQ
N_SMP = DEC_BATCH * DEC_SEQ
N_TOK = N_CTX + N_SMP
N_GRP = 8

VMEM_LIMIT = 56 * 1024 * 1024
LANES = 128
SUB = 8

TM = 512
TH = 256
FFN_HC = 256
TQ = 256
KB = 256
CH = 128
N_SLAB = CH // SUB
N_LEVEL = 7
MAX_FAST_LOG2_DECAY = 200.0


def _cparams(sem):
    return pltpu.CompilerParams(dimension_semantics=sem, vmem_limit_bytes=VMEM_LIMIT)


def _grp_of_block(i, rows):
    nctx = N_CTX // rows
    per = DEC_SEQ // rows
    return jnp.where(i < nctx, 0, 1 + (i - nctx) // per)


def _rms(x, g):
    return (x * lax.rsqrt(jnp.mean(x * x, axis=-1, keepdims=True) + EPS)) * g


def _dot(a, b):
    return lax.dot_general(a, b, (((1,), (0,)), ((), ())), preferred_element_type=jnp.float32)


def _dot_nt(a, b):
    return lax.dot_general(a, b, (((1,), (1,)), ((), ())), preferred_element_type=jnp.float32)


def _dot_tn(a, b):
    return lax.dot_general(a, b, (((0,), (0,)), ((), ())), preferred_element_type=jnp.float32)


def _const_spec(shape):
    nd = len(shape)
    return pl.BlockSpec(shape, lambda *_: (0,) * nd, pipeline_mode=pl.Buffered(1))


def _mod_kernel(c_ref, w_ref, b_ref, o_ref):
    c = c_ref[...]
    s = (c * jax.nn.sigmoid(c)).astype(jnp.bfloat16)
    o_ref[...] = _dot(s, w_ref[...].astype(jnp.bfloat16)) + b_ref[...]


def _modulation(cc, w_mod, b_mod):
    tn = 1024
    nt = (N_MOD * D) // tn
    return pl.pallas_call(
        _mod_kernel,
        grid=(DEPTH, nt),
        in_specs=[
            pl.BlockSpec((N_GRP, D), lambda l, j: (0, 0)),
            pl.BlockSpec((None, D, tn), lambda l, j: (l, 0, j)),
            pl.BlockSpec((None, 1, tn), lambda l, j: (l, 0, j)),
        ],
        out_specs=pl.BlockSpec((None, N_GRP, tn), lambda l, j: (l, 0, j)),
        out_shape=jax.ShapeDtypeStruct((DEPTH, N_GRP, N_MOD * D), jnp.float32),
        compiler_params=_cparams(("parallel", "parallel")),
        name="modulation",
    )(cc, w_mod, b_mod.reshape(DEPTH, 1, N_MOD * D))


def _ada_pre(x, mod_ref, g_ref, j, gi):
    shift = mod_ref[3 * j:3 * j + 1, :]
    scale = mod_ref[3 * j + 1:3 * j + 2, :]
    return _rms(x, g_ref[gi:gi + 1, :]) * (1.0 + scale) + shift


def _ada_post(x, y, mod_ref, g_ref, j, gi, res_w):
    gate = mod_ref[3 * j + 2:3 * j + 3, :]
    return x + res_w * gate * _rms(y, g_ref[gi:gi + 1, :])


def _tok_specs():
    return [
        pl.BlockSpec((TM, D), lambda i: (i, 0)),
        pl.BlockSpec((None, N_MOD, D), lambda i: (_grp_of_block(i, TM), 0, 0)),
        _const_spec((6, D)),
    ]


def _ffn_kernel(*refs, j, pair):
    if pair:
        xa_ref, xb_ref, mod_ref, g_ref, win_ref, wout_ref, o_ref, acc_ref = refs
        x = jnp.where(pl.program_id(0) < N_CTX // TM, xa_ref[...], xb_ref[...])
    else:
        x_ref, mod_ref, g_ref, win_ref, wout_ref, o_ref, acc_ref = refs
        x = x_ref[...]
    h = _ada_pre(x, mod_ref, g_ref, j, 2 * j).astype(jnp.bfloat16)
    for c in range(FFN_H // FFN_HC):
        lo = c * FFN_HC
        a = _dot(h, win_ref[:, lo:lo + FFN_HC])
        b = _dot(h, win_ref[:, FFN_H + lo:FFN_H + lo + FFN_HC])
        u = ((a * jax.nn.sigmoid(a)) * b).astype(jnp.bfloat16)
        part = _dot(u, wout_ref[lo:lo + FFN_HC, :])
        if c == 0:
            acc_ref[...] = part
        else:
            acc_ref[...] += part
    o_ref[...] = _ada_post(x, acc_ref[...], mod_ref, g_ref, j, 2 * j + 1, 0.5)


def _ffn(x, mod_l, g_l, w_in, w_out, l, j, *, base=0, rows=N_TOK):
    pair = isinstance(x, tuple)
    nctx = N_CTX // TM
    b0 = base // TM
    w_idx = (l, j // 2, 0, 0)
    if pair:
        assert base == 0 and rows == N_TOK
        x_specs = [pl.BlockSpec((TM, D), lambda i: (jnp.minimum(i, nctx - 1), 0)),
                   pl.BlockSpec((TM, D), lambda i: (jnp.maximum(i - nctx, 0), 0))]
        x_args = list(x)
    else:
        x_specs = [pl.BlockSpec((TM, D), lambda i: (i + b0, 0))]
        x_args = [x]
    return pl.pallas_call(
        functools.partial(_ffn_kernel, j=j, pair=pair),
        grid=(rows // TM,),
        in_specs=x_specs + [
            pl.BlockSpec((None, N_MOD, D), lambda i: (_grp_of_block(i + b0, TM), 0, 0)),
            _const_spec((6, D)),
            pl.BlockSpec((None, None, D, 2 * FFN_H), lambda i: w_idx, pipeline_mode=pl.Buffered(1)),
            pl.BlockSpec((None, None, FFN_H, D), lambda i: w_idx, pipeline_mode=pl.Buffered(1))],
        out_specs=pl.BlockSpec((TM, D), lambda i: (i, 0)),
        out_shape=jax.ShapeDtypeStruct((rows, D), jnp.float32),
        scratch_shapes=[pltpu.VMEM((TM, D), jnp.float32)],
        compiler_params=_cparams(("parallel",)),
        name=f"ffn{j}",
    )(*x_args, mod_l, g_l, w_in, w_out)


def _rope_lanes(x, cos, sin_signed):
    lane = lax.broadcasted_iota(jnp.int32, x.shape, 1)
    partner = jnp.where((lane % 32) < 16, pltpu.roll(x, LANES - 16, 1), pltpu.roll(x, 16, 1))
    return x * cos + partner * sin_signed


def _rope_rows(x, cos_t, sin_signed_t):
    parts = []
    for g in range(HD // 32):
        parts += [x[g * 32 + 16:g * 32 + 32, :], x[g * 32:g * 32 + 16, :]]
    return x * cos_t + jnp.concatenate(parts, axis=0) * sin_signed_t


def _attn_pre_kernel(x_ref, mod_ref, g_ref, w_ref, wkt_ref, cos_ref, sin_ref, cost_ref, sint_ref,
                     q_ref, kt_ref, v_ref, *, rope):
    x = x_ref[...]
    h = _ada_pre(x, mod_ref, g_ref, 1, 2).astype(jnp.bfloat16)
    qscale = DA_DK ** -0.5 * LOG2E
    zq = _dot(h, w_ref[:, 0:D])
    zkt = _dot_nt(wkt_ref[...], h)
    v_ref[...] = _dot(h, w_ref[:, 2 * D:3 * D]).astype(v_ref.dtype)
    for hd in range(HEADS):
        sl = slice(hd * HD, (hd + 1) * HD)
        q = zq[:, sl]
        k = zkt[sl, :]
        if rope:
            q = _rope_lanes(q, cos_ref[...], sin_ref[...])
            k = _rope_rows(k, cost_ref[...], sint_ref[...])
        q_ref[:, sl] = (q * qscale).astype(q_ref.dtype)
        kt_ref[sl, :] = k.astype(kt_ref.dtype)


def _attn_pre(x, mod_l, g_l, w_in, w_kt, tables, *, ctx):
    nb, seq = (BATCH, SEQ) if ctx else (DEC_BATCH, DEC_SEQ)
    t = min(TM, seq)
    rows = nb * seq
    base = 0 if ctx else N_CTX // t
    per = seq // t
    ptab = DEC_SEQ // t
    kv_dtype = jnp.float32 if ctx else jnp.bfloat16
    out_spec = pl.BlockSpec((t, D), lambda i: (i, 0))
    return pl.pallas_call(
        functools.partial(_attn_pre_kernel, rope=not ctx),
        grid=(rows // t,),
        in_specs=[
            pl.BlockSpec((t, D), lambda i: (i + base, 0)),
            pl.BlockSpec((None, N_MOD, D), lambda i: (_grp_of_block(i + base, t), 0, 0)),
            _const_spec((6, D)), _const_spec((D, 3 * D)), _const_spec((D, D)),
            pl.BlockSpec((t, HD), lambda i: (i % ptab, 0)),
            pl.BlockSpec((t, HD), lambda i: (i % ptab, 0)),
            pl.BlockSpec((HD, t), lambda i: (0, i % ptab)),
            pl.BlockSpec((HD, t), lambda i: (0, i % ptab)),
        ],
        out_specs=[out_spec, pl.BlockSpec((None, D, t), lambda i: (i // per, 0, i % per)), out_spec],
        out_shape=[
            jax.ShapeDtypeStruct((rows, D), jnp.bfloat16),
            jax.ShapeDtypeStruct((nb, D, seq), kv_dtype),
            jax.ShapeDtypeStruct((rows, D), kv_dtype),
        ],
        compiler_params=_cparams(("parallel",)),
        name="attn_pre_ctx" if ctx else "attn_pre_smp",
    )(x, mod_l, g_l, w_in, w_kt, *tables)


def _attn_kernel(*refs, lam_init, cache):
    if cache:
        (x_ref, mod_ref, g_ref, q_ref, k_ref, v_ref, kc_ref, vc_ref,
         lam_ref, sub_ref, w_ref, o_ref, oh_ref, s_ref, e_ref) = refs
    else:
        (x_ref, mod_ref, g_ref, q_ref, k_ref, v_ref,
         lam_ref, sub_ref, w_ref, o_ref, oh_ref, s_ref, e_ref) = refs
    lp = lam_ref[...]
    lam = (jnp.exp(jnp.sum(lp[0:1] * lp[1:2], axis=-1, keepdims=True))
           - jnp.exp(jnp.sum(lp[2:3] * lp[3:4], axis=-1, keepdims=True)) + lam_init)
    tq = q_ref.shape[0]
    lane = lax.broadcasted_iota(jnp.int32, (tq, HD), 1)
    bf = jnp.bfloat16
    n_own = k_ref.shape[1] // KB
    n_kb = n_own + (1 if cache else 0)

    def keys_t(sl, kb):
        if kb < n_own:
            return k_ref[sl, kb * KB:(kb + 1) * KB].astype(bf)
        return kc_ref[sl, :].astype(bf)

    def values(sl, kb):
        if kb < n_own:
            return v_ref[kb * KB:(kb + 1) * KB, sl].astype(bf)
        return vc_ref[:, sl].astype(bf)

    def head_slice(hd):
        return slice(hd * HD, (hd + 1) * HD)

    def kcols(kb):
        return slice(kb * KB, (kb + 1) * KB)

    def stacked_q(hd):
        qh = q_ref[:, head_slice(hd)]
        return jnp.concatenate([jnp.where(lane < DA_DK, qh, jnp.zeros_like(qh)),
                                jnp.where(lane >= DA_DK, qh, jnp.zeros_like(qh))], axis=0)

    st = [dict() for _ in range(HEADS)]
    for slot in range(HEADS + 2):
        h1, h2, h3 = slot, slot - 1, slot - 2
        if h1 < HEADS:
            st[h1]["qq"] = stacked_q(h1)
        if 0 <= h3 < HEADS:
            tot = st[h3]["tot"]
            st[h3]["c"] = jnp.broadcast_to(lam * tot[:tq] / tot[tq:], (tq, KB)).astype(bf)
        for kb in range(n_kb):
            if h1 < HEADS:
                s = _dot(st[h1]["qq"], keys_t(head_slice(h1), kb))
                s_ref[h1 % 2, :, kcols(kb)] = s
                mp = jnp.maximum(s[:, :LANES], s[:, LANES:])
                st[h1]["m"] = mp if kb == 0 else jnp.maximum(st[h1]["m"], mp)
            if 0 <= h2 < HEADS:
                e = jnp.exp2(s_ref[h2 % 2, :, kcols(kb)] - st[h2]["mrow"])
                ep = e[:, :LANES] + e[:, LANES:]
                st[h2]["l"] = ep if kb == 0 else st[h2]["l"] + ep
                e_ref[h2 % 2, :, kcols(kb)] = e.astype(bf)
            if 0 <= h3 < HEADS:
                a = e_ref[h3 % 2, :tq, kcols(kb)] - st[h3]["c"] * e_ref[h3 % 2, tq:, kcols(kb)]
                part = _dot(a, values(head_slice(h3), kb))
                st[h3]["o"] = part if kb == 0 else st[h3]["o"] + part
        if h1 < HEADS:
            st[h1]["mrow"] = jnp.max(st[h1]["m"], axis=-1, keepdims=True)
        if 0 <= h2 < HEADS:
            st[h2]["tot"] = jnp.sum(st[h2]["l"], axis=-1, keepdims=True)
        if 0 <= h3 < HEADS:
            o = st[h3]["o"] * (1.0 / st[h3]["tot"][:tq])
            o = _rms(o, sub_ref[...]) * (1.0 - lam_init)
            oh_ref[:, head_slice(h3)] = o.astype(bf)
            st[h3].clear()
    y = _dot(oh_ref[...], w_ref[...])
    o_ref[...] = _ada_post(x_ref[...], y, mod_ref, g_ref, 1, 3, 1.0)


def _attn(x, mod_l, g_l, q, k, v, kc, vc, lam_p, subln, w_out, *, ctx, lam_init):
    nb, seq = (BATCH, SEQ) if ctx else (DEC_BATCH, DEC_SEQ)
    rows = nb * seq
    base = 0 if ctx else N_CTX // TQ
    nq = seq // TQ
    assert seq % KB == 0 and PAST == KB
    n_keys = seq if ctx else seq + PAST
    row_spec = pl.BlockSpec((TQ, D), lambda b, i: (b * nq + i, 0))
    in_specs = [
        pl.BlockSpec((TQ, D), lambda b, i: (base + b * nq + i, 0)),
        pl.BlockSpec((None, N_MOD, D), lambda b, i: (_grp_of_block(base + b * nq + i, TQ), 0, 0)),
        pl.BlockSpec((6, D), lambda b, i: (0, 0)),
        row_spec,
        pl.BlockSpec((None, D, seq), lambda b, i: (b, 0, 0)),
        pl.BlockSpec((seq, D), lambda b, i: (b, 0)),
    ]
    args = [x, mod_l, g_l, q, k, v]
    if not ctx:
        in_specs += [pl.BlockSpec((None, D, PAST), lambda b, i: (b, 0, 0)),
                     pl.BlockSpec((None, PAST, D), lambda b, i: (b, 0, 0))]
        args += [kc, vc]
    in_specs += [
        pl.BlockSpec((4, DA_DK), lambda b, i: (0, 0)),
        pl.BlockSpec((1, HD), lambda b, i: (0, 0)),
        pl.BlockSpec((D, D), lambda b, i: (0, 0), pipeline_mode=pl.Buffered(1)),
    ]
    args += [lam_p, subln, w_out]
    return pl.pallas_call(
        functools.partial(_attn_kernel, lam_init=lam_init, cache=not ctx),
        grid=(nb, nq),
        in_specs=in_specs,
        out_specs=row_spec,
        out_shape=jax.ShapeDtypeStruct((rows, D), jnp.float32),
        scratch_shapes=[pltpu.VMEM((TQ, D), jnp.bfloat16),
                        pltpu.VMEM((2, 2 * TQ, n_keys), jnp.float32),
                        pltpu.VMEM((2, 2 * TQ, n_keys), jnp.bfloat16)],
        compiler_params=_cparams(("parallel", "parallel")),
        name="attn_ctx" if ctx else "attn_smp",
    )(*args)


def _hgrn_pre_kernel(x_ref, mod_ref, g_ref, w_ref, lb_ref,
                     q_ref, i_ref, gt_ref, lf_ref, qs_ref, ks_ref, bl_ref, qf_ref, b_ref, *, layer):
    x = x_ref[...]
    h = _ada_pre(x, mod_ref, g_ref, 1, 2).astype(jnp.bfloat16)

    def proj(n):
        return _dot(h, w_ref[:, n * D:(n + 1) * D])

    def log_gate(d, ff):
        raw = [lb_ref[l, d:d + 1, :] for l in range(DEPTH)]
        m = functools.reduce(jnp.maximum, raw)
        ex = [jnp.exp(r - m) for r in raw]
        tot = functools.reduce(lambda a, b: a + b, ex)
        soft = [e / tot for e in ex]
        lb = functools.reduce(lambda a, b: a + b, soft[:layer + 1]) - soft[0]
        return jnp.log(lb + (1.0 - lb) * jax.nn.sigmoid(ff))

    ff = proj(1)
    qf_ref[...] = proj(0)
    q_ref[...] = qf_ref[...].astype(q_ref.dtype)
    lf_ref[0] = log_gate(0, ff)
    fb = proj(2)
    i_ref[...] = proj(3).astype(i_ref.dtype)
    lf_ref[1] = log_gate(1, fb)
    gt_ref[...] = proj(4).astype(gt_ref.dtype)

    bf = jnp.bfloat16
    for d in range(2):
        for c in range(x_ref.shape[0] // CH):
            base = c * CH
            last = _rec_cumsum(lf_ref.at[d], b_ref.at[d], None, d == 1, base)
            bl_ref[c, d:d + 1, :] = last[0:1, :]
            half = 0.5 * last
            half2 = jnp.concatenate([half, half], axis=0)
            for v2 in range(N_SLAB // 2):
                rows = slice(base + v2 * 2 * SUB, base + (v2 + 1) * 2 * SUB)
                k = 1.0 - jnp.exp2(lf_ref[d, rows, :] * LOG2E)
                dd = b_ref[d, rows, :] - half2
                qs_ref[d, rows, :] = (qf_ref[rows, :] * jnp.exp2(dd)).astype(bf)
                ks_ref[d, rows, :] = (k * jnp.exp2(-dd)).astype(bf)


def _hgrn_pre(x, mod_l, g_l, w_in, lb_raw, layer):
    t = TH
    row = pl.BlockSpec((t, D), lambda i: (i, 0))
    both = pl.BlockSpec((2, t, D), lambda i: (0, i, 0))
    return pl.pallas_call(
        functools.partial(_hgrn_pre_kernel, layer=layer),
        grid=(N_TOK // t,),
        in_specs=[
            row,
            pl.BlockSpec((None, N_MOD, D), lambda i: (_grp_of_block(i, t), 0, 0)),
            _const_spec((6, D)), _const_spec((D, 5 * D)), _const_spec((DEPTH, 2, D))],
        out_specs=[row, row, row, both, both, both,
                   pl.BlockSpec((t // CH, 2, D), lambda i: (i, 0, 0))],
        out_shape=[
            jax.ShapeDtypeStruct((N_TOK, D), jnp.bfloat16),
            jax.ShapeDtypeStruct((N_TOK, D), jnp.bfloat16),
            jax.ShapeDtypeStruct((N_TOK, D), jnp.bfloat16),
            jax.ShapeDtypeStruct((2, N_TOK, D), jnp.float32),
            jax.ShapeDtypeStruct((2, N_TOK, D), jnp.bfloat16),
            jax.ShapeDtypeStruct((2, N_TOK, D), jnp.bfloat16),
            jax.ShapeDtypeStruct((N_TOK // CH, 2, D), jnp.float32),
        ],
        scratch_shapes=[pltpu.VMEM((t, D), jnp.float32), pltpu.VMEM((2, t, D), jnp.float32)],
        compiler_params=_cparams(("parallel",)),
        name="hgrn_pre",
    )(x, mod_l, g_l, w_in, lb_raw)


def _level_masks(rev):
    t = np.arange(CH)[:, None]
    s = np.arange(CH)[None, :]
    out = []
    for p in range(N_LEVEL):
        same = (t >> (p + 1)) == (s >> (p + 1))
        tb, sb = (t >> p) & 1, (s >> p) & 1
        out.append(same & ((tb == 0) & (sb == 1) if rev else (tb == 1) & (sb == 0)))
    out.append(t == s)
    return np.stack(out).astype(np.float32)


def _row_bcast(x, r):
    return jnp.broadcast_to(x[r:r + 1, :], x.shape)


def _rec_cumsum(lf_ref, b_ref, br_ref, rev, base=0):
    sub = lax.broadcasted_iota(jnp.int32, (SUB, D), 0)
    order = range(N_SLAB - 1, -1, -1) if rev else range(N_SLAB)
    edge = 0 if rev else SUB - 1
    carry = None
    for v in order:
        rows = slice(base + v * SUB, base + (v + 1) * SUB)
        c = lf_ref[rows, :] * LOG2E
        for k in (1, 2, 4):
            if rev:
                c = c + jnp.where(sub < SUB - k, pltpu.roll(c, SUB - k, 0), 0.0)
            else:
                c = c + jnp.where(sub >= k, pltpu.roll(c, k, 0), 0.0)
        if carry is not None:
            c = c + carry
        carry = _row_bcast(c, edge)
        b_ref[rows, :] = c
        if br_ref is not None:
            br_ref[rows, :] = carry
    return carry


def _rec_fast(lf_ref, q_ref, i_ref, b_ref, b_last, qs_ref, ks_ref, st_ref, o_ref, rev):
    bf = jnp.bfloat16
    half = 0.5 * b_last
    half2 = jnp.concatenate([half, half], axis=0)
    for v2 in range(N_SLAB // 2):
        rows = slice(v2 * 2 * SUB, (v2 + 1) * 2 * SUB)
        k = 1.0 - jnp.exp2(lf_ref[rows, :] * LOG2E)
        d = b_ref[rows, :] - half2
        qs_ref[rows, :] = (q_ref[rows, :].astype(jnp.float32) * jnp.exp2(d)).astype(bf)
        ks_ref[rows, :] = (k * jnp.exp2(-d)).astype(bf)
    scale = jnp.exp2(half[0:1, :])
    decay = scale * scale
    t = lax.broadcasted_iota(jnp.int32, (CH, CH), 0)
    s = lax.broadcasted_iota(jnp.int32, (CH, CH), 1)
    seen = (s >= t) if rev else (s <= t)
    for hd in range(HEADS):
        sl = slice(hd * HD, (hd + 1) * HD)
        qs, ks, vh = qs_ref[:, sl], ks_ref[:, sl], i_ref[:, sl]
        a = jnp.where(seen, _dot_nt(qs, ks), 0.0)
        st = st_ref[hd]
        o = _dot(a.astype(bf), vh) + _dot_nt(qs, (st * scale[:, sl]).astype(bf))
        o_ref[:, sl] = o.astype(o_ref.dtype)
        st_ref[hd] = st * decay[:, sl] + _dot_tn(vh, ks) * scale[:, sl]


def _rec_prepare(lf_ref, q_ref, b_ref, br_ref, b_last, qt_ref, kt_ref, rev):
    bf = jnp.bfloat16
    sub = lax.broadcasted_iota(jnp.int32, (SUB, D), 0)
    zeros = jnp.zeros((SUB, D), jnp.float32)
    for v2 in range(N_SLAB // 2):
        rows = slice(v2 * 2 * SUB, (v2 + 1) * 2 * SUB)
        f = jnp.exp2(lf_ref[rows, :] * LOG2E)
        k = 1.0 - f
        q = q_ref[rows, :].astype(jnp.float32)
        b = b_ref[rows, :]
        qt_ref[0, rows, :] = (q * f).astype(bf)
        kt_ref[0, rows, :] = k.astype(bf)
        half = (slice(0, SUB), slice(SUB, 2 * SUB))
        for p in range(1, N_LEVEL):
            if p >= 3:
                w = p - 3
                es, qside = [], []
                for n, v in enumerate((2 * v2, 2 * v2 + 1)):
                    hi = v & ~((1 << (w + 1)) - 1)
                    vr = hi | (1 << w) if rev else hi | ((1 << w) - 1)
                    r = br_ref[vr * SUB:(vr + 1) * SUB, :]
                    qside.append(((v >> w) & 1) == (0 if rev else 1))
                    es.append(jnp.exp2(b[half[n], :] - r) if qside[n] else jnp.exp2(r - b[half[n], :]))
                if qside[0] == qside[1]:
                    e = jnp.concatenate(es, axis=0)
                    if qside[0]:
                        qt_ref[p, rows, :] = (q * e).astype(bf)
                    else:
                        kt_ref[p, rows, :] = (k * e).astype(bf)
                else:
                    qe = [q[half[n], :] * es[n] if qside[n] else zeros for n in range(2)]
                    ke = [zeros if qside[n] else k[half[n], :] * es[n] for n in range(2)]
                    qt_ref[p, rows, :] = jnp.concatenate(qe, axis=0).astype(bf)
                    kt_ref[p, rows, :] = jnp.concatenate(ke, axis=0).astype(bf)
            else:
                es = []
                for n in range(2):
                    bv = b[half[n], :]
                    if p == 2:
                        r = _row_bcast(bv, 4 if rev else 3)
                    else:
                        lo, hi_r = (2, 6) if rev else (1, 5)
                        r = jnp.where(sub < 4, _row_bcast(bv, lo), _row_bcast(bv, hi_r))
                    es.append(jnp.exp2(-jnp.abs(bv - r)))
                e = jnp.concatenate(es, axis=0)
                qt_ref[p, rows, :] = (q * e).astype(bf)
                kt_ref[p, rows, :] = (k * e).astype(bf)
        bl = jnp.concatenate([b_last, b_last], axis=0)
        qt_ref[N_LEVEL, rows, :] = (q * jnp.exp2(b)).astype(bf)
        kt_ref[N_LEVEL, rows, :] = (k * jnp.exp2(bl - b)).astype(bf)
    return jnp.exp2(b_last[0:1, :])


def _rec_heads(q_ref, i_ref, qt_ref, kt_ref, msk_ref, st_ref, o_ref, decay):
    bf = jnp.bfloat16
    top = N_LEVEL - 1
    for hd in range(HEADS):
        sl = slice(hd * HD, (hd + 1) * HD)
        a = _dot_nt(qt_ref[top, :, sl], kt_ref[top, :, sl])
        a = a + msk_ref[N_LEVEL] * _dot_nt(q_ref[:, sl], kt_ref[0, :, sl])
        for p in range(top):
            a = a + msk_ref[p] * _dot_nt(qt_ref[p, :, sl], kt_ref[p, :, sl])
        vh = i_ref[:, sl]
        st = st_ref[hd]
        o = _dot(a.astype(bf), vh) + _dot_nt(qt_ref[N_LEVEL, :, sl], st.astype(bf))
        o_ref[:, sl] = o.astype(o_ref.dtype)
        st_ref[hd] = st * decay[:, sl] + _dot_tn(vh, kt_ref[N_LEVEL, :, sl])


def _rec_kernel(blkf_ref, blkb_ref, sid_ref, first_ref, last_ref,
                qf_ref, if_ref, lff_ref, qb_ref, ib_ref, lfb_ref, s0_ref, mskf_ref, mskb_ref,
                of_ref, ob_ref, sfin_ref,
                st_ref, bf_ref, brf_ref, qtf_ref, ktf_ref, bb_ref, brb_ref, qtb_ref, ktb_ref):
    step = pl.program_id(0)

    @pl.when(step == 0)
    def _():
        for ref in (qtf_ref, ktf_ref, qtb_ref, ktb_ref):
            ref[...] = jnp.zeros_like(ref)

    @pl.when(first_ref[step] == 1)
    def _():
        st_ref[...] = jnp.zeros_like(st_ref)

    @pl.when(first_ref[step] == 2)
    def _():
        for d in range(2):
            for hd in range(HEADS):
                st_ref[d, hd] = s0_ref[d, hd].T

    last_f = _rec_cumsum(lff_ref, bf_ref, brf_ref, False)
    last_b = _rec_cumsum(lfb_ref, bb_ref, brb_ref, True)
    moderate = jnp.min(jnp.minimum(last_f, last_b)) >= -MAX_FAST_LOG2_DECAY

    @pl.when(moderate)
    def _():
        _rec_fast(lff_ref, qf_ref, if_ref, bf_ref, last_f, qtf_ref.at[N_LEVEL], ktf_ref.at[N_LEVEL],
                  st_ref.at[0], of_ref, False)
        _rec_fast(lfb_ref, qb_ref, ib_ref, bb_ref, last_b, qtb_ref.at[N_LEVEL], ktb_ref.at[N_LEVEL],
                  st_ref.at[1], ob_ref, True)

    @pl.when(jnp.logical_not(moderate))
    def _():
        decay_f = _rec_prepare(lff_ref, qf_ref, bf_ref, brf_ref, last_f, qtf_ref, ktf_ref, False)
        decay_b = _rec_prepare(lfb_ref, qb_ref, bb_ref, brb_ref, last_b, qtb_ref, ktb_ref, True)
        _rec_heads(qf_ref, if_ref, qtf_ref, ktf_ref, mskf_ref, st_ref.at[0], of_ref, decay_f)
        _rec_heads(qb_ref, ib_ref, qtb_ref, ktb_ref, mskb_ref, st_ref.at[1], ob_ref, decay_b)

    @pl.when(last_ref[step] == 1)
    def _():
        for d in range(2):
            for hd in range(HEADS):
                sfin_ref[d, hd] = st_ref[d, hd].T


def _rec_tables():
    blkf, blkb, sid, first, last = [], [], [], [], []
    seqs = [(b * (SEQ // CH), SEQ // CH) for b in range(BATCH)]
    seqs += [(N_CTX // CH + b * (DEC_SEQ // CH), DEC_SEQ // CH) for b in range(DEC_BATCH)]
    for n, (start, nc) in enumerate(seqs):
        ctx = n < BATCH
        for c in range(nc):
            blkf.append(start + c)
            blkb.append(start + nc - 1 - c)
            sid.append(n)
            first.append((1 if ctx else 2) if c == 0 else 0)
            last.append(int(ctx and c == nc - 1))
    return [jnp.asarray(np.asarray(a, np.int32)) for a in (blkf, blkb, sid, first, last)]


def _rec(q, i, lf, s0):
    row_f = pl.BlockSpec((CH, D), lambda s, bkf, bkb, sid, fi, la: (bkf[s], 0))
    row_b = pl.BlockSpec((CH, D), lambda s, bkf, bkb, sid, fi, la: (bkb[s], 0))
    st_shape = (None, 2, HEADS, HD, HD)
    scratch = [pltpu.VMEM((2, HEADS, HD, HD), jnp.float32)]
    for _ in range(2):
        scratch += [
            pltpu.VMEM((CH, D), jnp.float32),
            pltpu.VMEM((CH, D), jnp.float32),
            pltpu.VMEM((N_LEVEL + 1, CH, D), jnp.bfloat16),
            pltpu.VMEM((N_LEVEL + 1, CH, D), jnp.bfloat16),
        ]
    msk_spec = pl.BlockSpec((N_LEVEL + 1, CH, CH), lambda s, *_: (0, 0, 0))
    grid_spec = pltpu.PrefetchScalarGridSpec(
        num_scalar_prefetch=5,
        grid=(N_TOK // CH,),
        in_specs=[
            row_f, row_f,
            pl.BlockSpec((None, CH, D), lambda s, bkf, bkb, sid, fi, la: (0, bkf[s], 0)),
            row_b, row_b,
            pl.BlockSpec((None, CH, D), lambda s, bkf, bkb, sid, fi, la: (1, bkb[s], 0)),
            pl.BlockSpec(st_shape, lambda s, bkf, bkb, sid, fi, la: (jnp.maximum(sid[s] - BATCH, 0), 0, 0, 0, 0)),
            msk_spec, msk_spec,
        ],
        out_specs=[
            row_f, row_b,
            pl.BlockSpec(st_shape, lambda s, bkf, bkb, sid, fi, la: (jnp.minimum(sid[s], BATCH - 1), 0, 0, 0, 0)),
        ],
        scratch_shapes=scratch,
    )
    return pl.pallas_call(
        _rec_kernel,
        grid_spec=grid_spec,
        out_shape=[
            jax.ShapeDtypeStruct((N_TOK, D), jnp.bfloat16),
            jax.ShapeDtypeStruct((N_TOK, D), jnp.bfloat16),
            jax.ShapeDtypeStruct((BATCH, 2, HEADS, HD, HD), jnp.float32),
        ],
        compiler_params=_cparams(("arbitrary",)),
        name="hgrn_rec",
    )(*_rec_tables(), q, i, lf, q, i, lf, s0, jnp.asarray(_level_masks(False)), jnp.asarray(_level_masks(True)))


def _rec_scaled_heads(qs_ref, ks_ref, i_ref, bl_ref, d, st_ref, o_ref):
    bf = jnp.bfloat16
    scale = jnp.exp2(0.5 * bl_ref[d:d + 1, :])
    decay = scale * scale
    t = lax.broadcasted_iota(jnp.int32, (CH, CH), 0)
    s = lax.broadcasted_iota(jnp.int32, (CH, CH), 1)
    seen = (s >= t) if d == 1 else (s <= t)
    for hd in range(HEADS):
        sl = slice(hd * HD, (hd + 1) * HD)
        qs, ks, vh = qs_ref[:, sl], ks_ref[:, sl], i_ref[:, sl]
        a = jnp.where(seen, _dot_nt(qs, ks), 0.0)
        st = st_ref[hd]
        o = _dot(a.astype(bf), vh) + _dot_nt(qs, (st * scale[:, sl]).astype(bf))
        o_ref[:, sl] = o.astype(o_ref.dtype)
        st_ref[hd] = st * decay[:, sl] + _dot_tn(vh, ks) * scale[:, sl]


def _rec_scaled_kernel(blkf_ref, blkb_ref, sid_ref, first_ref, last_ref,
                       qsf_ref, ksf_ref, if_ref, blf_ref, qsb_ref, ksb_ref, ib_ref, blb_ref, s0_ref,
                       of_ref, ob_ref, sfin_ref, st_ref):
    step = pl.program_id(0)

    @pl.when(first_ref[step] == 1)
    def _():
        st_ref[...] = jnp.zeros_like(st_ref)

    @pl.when(first_ref[step] == 2)
    def _():
        for d in range(2):
            for hd in range(HEADS):
                st_ref[d, hd] = s0_ref[d, hd].T

    _rec_scaled_heads(qsf_ref, ksf_ref, if_ref, blf_ref, 0, st_ref.at[0], of_ref)
    _rec_scaled_heads(qsb_ref, ksb_ref, ib_ref, blb_ref, 1, st_ref.at[1], ob_ref)

    @pl.when(last_ref[step] == 1)
    def _():
        for d in range(2):
            for hd in range(HEADS):
                sfin_ref[d, hd] = st_ref[d, hd].T


def _rec_scaled(qs, ks, i, bl, s0):
    def at(which, d=None):
        def index(s, bkf, bkb, sid, fi, la):
            blk = (bkf, bkb)[which][s]
            return (blk, 0) if d is None else (d, blk, 0)
        return index

    row_f = pl.BlockSpec((CH, D), at(0))
    row_b = pl.BlockSpec((CH, D), at(1))
    dir_f = pl.BlockSpec((None, CH, D), at(0, 0))
    dir_b = pl.BlockSpec((None, CH, D), at(1, 1))
    bl_f = pl.BlockSpec((None, 2, D), lambda s, bkf, bkb, sid, fi, la: (bkf[s], 0, 0))
    bl_b = pl.BlockSpec((None, 2, D), lambda s, bkf, bkb, sid, fi, la: (bkb[s], 0, 0))
    st_shape = (None, 2, HEADS, HD, HD)
    grid_spec = pltpu.PrefetchScalarGridSpec(
        num_scalar_prefetch=5,
        grid=(N_TOK // CH,),
        in_specs=[
            dir_f, dir_f, row_f, bl_f, dir_b, dir_b, row_b, bl_b,
            pl.BlockSpec(st_shape, lambda s, bkf, bkb, sid, fi, la: (jnp.maximum(sid[s] - BATCH, 0), 0, 0, 0, 0)),
        ],
        out_specs=[
            row_f, row_b,
            pl.BlockSpec(st_shape, lambda s, bkf, bkb, sid, fi, la: (jnp.minimum(sid[s], BATCH - 1), 0, 0, 0, 0)),
        ],
        scratch_shapes=[pltpu.VMEM((2, HEADS, HD, HD), jnp.float32)],
    )
    return pl.pallas_call(
        _rec_scaled_kernel,
        grid_spec=grid_spec,
        out_shape=[
            jax.ShapeDtypeStruct((N_TOK, D), jnp.bfloat16),
            jax.ShapeDtypeStruct((N_TOK, D), jnp.bfloat16),
            jax.ShapeDtypeStruct((BATCH, 2, HEADS, HD, HD), jnp.float32),
        ],
        compiler_params=_cparams(("arbitrary",)),
        name="hgrn_rec_scaled",
    )(*_rec_tables(), qs, ks, i, bl, qs, ks, i, bl, s0)


def _hgrn_out_kernel(x_ref, mod_ref, g_ref, of_ref, ob_ref, gt_ref, gn_ref, w_ref, o_ref, oh_ref):
    for hd in range(HEADS):
        sl = slice(hd * HD, (hd + 1) * HD)
        o = of_ref[:, sl].astype(jnp.float32) + ob_ref[:, sl].astype(jnp.float32)
        o = _rms(o, gn_ref[...]) * jax.nn.sigmoid(gt_ref[:, sl].astype(jnp.float32))
        oh_ref[:, sl] = o.astype(jnp.bfloat16)
    y = _dot(oh_ref[...], w_ref[...])
    o_ref[...] = _ada_post(x_ref[...], y, mod_ref, g_ref, 1, 3, 1.0)


def _hgrn_out(x, mod_l, g_l, o_f, o_b, gt, gnorm, w_out):
    row = pl.BlockSpec((TM, D), lambda i: (i, 0))
    return pl.pallas_call(
        _hgrn_out_kernel,
        grid=(N_TOK // TM,),
        in_specs=_tok_specs() + [row, row, row, _const_spec((1, HD)), _const_spec((D, D))],
        out_specs=row,
        out_shape=jax.ShapeDtypeStruct((N_TOK, D), jnp.float32),
        scratch_shapes=[pltpu.VMEM((TM, D), jnp.bfloat16)],
        compiler_params=_cparams(("parallel",)),
        name="hgrn_out",
    )(x, mod_l, g_l, o_f, o_b, gt, gnorm, w_out)


def _rope_tables():
    pos = np.arange(DEC_SEQ)
    row = (pos // GRID_W).astype(np.float64)
    col = (pos % GRID_W).astype(np.float64)
    nf = DA_DK // 4
    inv = ROPE_THETA ** (-np.arange(nf, dtype=np.float64) / nf)
    lane = np.arange(HD)
    axis = (lane % DA_DK) // (2 * nf)
    ang = np.where(axis[None, :] == 0, row[:, None], col[:, None]) * inv[lane % nf][None, :]
    sign = np.where((lane % (2 * nf)) < nf, -1.0, 1.0)
    cos, sin = np.cos(ang), np.sin(ang) * sign[None, :]
    return tuple(jnp.asarray(a, jnp.float32) for a in (cos, sin, cos.T, sin.T))


def kernel(x_prompt, x_sample, cache_k, cache_v, state_hgrn, c, c_ctx, w_mod, b_mod, norm_g,
           ffn_w_in, ffn_w_out, attn_w_in, attn_w_out, attn_lambda, attn_subln,
           hgrn_w_in, hgrn_w_out, hgrn_lower_bounds, hgrn_gnorm):
    bf = jnp.bfloat16
    x = (x_prompt.reshape(N_CTX, D), x_sample.reshape(N_SMP, D))
    cc = jnp.concatenate([c_ctx[None, :], c, jnp.zeros((N_GRP - 1 - DEC_BATCH, D), jnp.float32)], axis=0)
    mod = _modulation(cc, w_mod, b_mod).reshape(DEPTH, N_GRP, N_MOD, D)
    tables = _rope_tables()

    ks, vs, states = [], [], []
    for l in range(DEPTH):
        mod_l, g_l = mod[l], norm_g[l]
        x = _ffn(x, mod_l, g_l, ffn_w_in, ffn_w_out, l, 0)
        if l % 2 == 0:
            a = l // 2
            lam_init = 0.8 - 0.6 * math.exp(-0.3 * l)
            w_kt = attn_w_in[a][:, D:2 * D].T
            w_out = attn_w_out[a]
            subln = attn_subln[a].reshape(1, HD)
            qc, kt_new, vc_new = _attn_pre(x, mod_l, g_l, attn_w_in[a], w_kt, tables, ctx=True)
            qs, kt_smp, v_smp = _attn_pre(x, mod_l, g_l, attn_w_in[a], w_kt, tables, ctx=False)
            kct = jnp.transpose(cache_k[:, a], (0, 2, 3, 4, 1)).reshape(DEC_BATCH, D, PAST)
            xc = _attn(x, mod_l, g_l, qc, kt_new, vc_new, None, None, attn_lambda[a], subln, w_out,
                       ctx=True, lam_init=lam_init)
            xs = _attn(x, mod_l, g_l, qs, kt_smp, v_smp, kct, cache_v[:, a].reshape(DEC_BATCH, PAST, D),
                       attn_lambda[a], subln, w_out, ctx=False, lam_init=lam_init)
            x = (xc, xs)
            ks.append(jnp.transpose(kt_new.reshape(BATCH, HEADS, 2, DA_DK, SEQ), (0, 4, 1, 2, 3)))
            vs.append(vc_new.reshape(BATCH, SEQ, HEADS, HD))
        else:
            r = l // 2
            q, i, gt, lf, q_sc, k_sc, bl = _hgrn_pre(x, mod_l, g_l, hgrn_w_in[r], hgrn_lower_bounds, l)
            s0 = state_hgrn[:, r].astype(jnp.float32)
            moderate = jnp.min(bl) >= -MAX_FAST_LOG2_DECAY
            o_f, o_b, s_fin = lax.cond(moderate,
                                       lambda: _rec_scaled(q_sc, k_sc, i, bl, s0),
                                       lambda: _rec(q, i, lf, s0))
            x = _hgrn_out(x, mod_l, g_l, o_f, o_b, gt, hgrn_gnorm[r].reshape(1, HD), hgrn_w_out[r])
            states.append(s_fin)
        if l < DEPTH - 1:
            x = _ffn(x, mod_l, g_l, ffn_w_in, ffn_w_out, l, 2)

    assert not isinstance(x, tuple)
    y_prompt = _ffn(x, mod_l, g_l, ffn_w_in, ffn_w_out, DEPTH - 1, 2, base=0, rows=N_CTX)
    y_sample = _ffn(x, mod_l, g_l, ffn_w_in, ffn_w_out, DEPTH - 1, 2, base=N_CTX, rows=N_SMP)
    return (y_prompt.reshape(BATCH, SEQ, D), y_sample.reshape(DEC_BATCH, DEC_SEQ, D),
            jnp.stack(ks, axis=1), jnp.stack(vs, axis=1), jnp.stack(states, axis=1))
```

```python
import functools
import math

import numpy as np
import jax
import jax.numpy as jnp
from jax import lax
from jax.experimental import pallas as pl
from jax.experimental.pallas import tpu as pltpu

D = 1024
BATCH = 16
SEQ = 256
DEPTH = 2
DEC_BATCH = 4
DEC_SEQ = 2048
PAST = 256
GRID_W = 64
N_MOD = 9
EPS = 1e-6
HEADS = 8
HD = 128
DA_DK = 64
ROPE_THETA = 10000.0
FFN_H = 2816
LOG2E = 1.4426950408889634

N_CTX = BATCH * SEQ
N_SMP = DEC_BATCH * DEC_SEQ
N_TOK = N_CTX + N_SMP
N_GRP = 8

VMEM_LIMIT = 56 * 1024 * 1024
LANES = 128
SUB = 8

TM = 512
TH = 256
FFN_HC = 256
TQ = 256
KB = 256
CH = 128
N_SLAB = CH // SUB
N_LEVEL = 7
MAX_FAST_LOG2_DECAY = 200.0


def _cparams(sem):
    return pltpu.CompilerParams(dimension_semantics=sem, vmem_limit_bytes=VMEM_LIMIT)


def _grp_of_block(i, rows):
    nctx = N_CTX // rows
    per = DEC_SEQ // rows
    return jnp.where(i < nctx, 0, 1 + (i - nctx) // per)


def _rms(x, g):
    return (x * lax.rsqrt(jnp.mean(x * x, axis=-1, keepdims=True) + EPS)) * g


def _dot(a, b):
    return lax.dot_general(a, b, (((1,), (0,)), ((), ())), preferred_element_type=jnp.float32)


def _dot_nt(a, b):
    return lax.dot_general(a, b, (((1,), (1,)), ((), ())), preferred_element_type=jnp.float32)


def _dot_tn(a, b):
    return lax.dot_general(a, b, (((0,), (0,)), ((), ())), preferred_element_type=jnp.float32)


def _const_spec(shape):
    nd = len(shape)
    return pl.BlockSpec(shape, lambda *_: (0,) * nd, pipeline_mode=pl.Buffered(1))


def _mod_kernel(c_ref, w_ref, b_ref, o_ref):
    c = c_ref[...]
    s = (c * jax.nn.sigmoid(c)).astype(jnp.bfloat16)
    o_ref[...] = _dot(s, w_ref[...].astype(jnp.bfloat16)) + b_ref[...]


def _modulation(cc, w_mod, b_mod):
    tn = 1024
    nt = (N_MOD * D) // tn
    return pl.pallas_call(
        _mod_kernel,
        grid=(DEPTH, nt),
        in_specs=[
            pl.BlockSpec((N_GRP, D), lambda l, j: (0, 0)),
            pl.BlockSpec((None, D, tn), lambda l, j: (l, 0, j)),
            pl.BlockSpec((None, 1, tn), lambda l, j: (l, 0, j)),
        ],
        out_specs=pl.BlockSpec((None, N_GRP, tn), lambda l, j: (l, 0, j)),
        out_shape=jax.ShapeDtypeStruct((DEPTH, N_GRP, N_MOD * D), jnp.float32),
        compiler_params=_cparams(("parallel", "parallel")),
        name="modulation",
    )(cc, w_mod, b_mod.reshape(DEPTH, 1, N_MOD * D))


def _ada_pre(x, mod_ref, g_ref, j, gi):
    shift = mod_ref[3 * j:3 * j + 1, :]
    scale = mod_ref[3 * j + 1:3 * j + 2, :]
    return _rms(x, g_ref[gi:gi + 1, :]) * (1.0 + scale) + shift


def _ada_post(x, y, mod_ref, g_ref, j, gi, res_w):
    gate = mod_ref[3 * j + 2:3 * j + 3, :]
    return x + res_w * gate * _rms(y, g_ref[gi:gi + 1, :])


def _tok_specs():
    return [
        pl.BlockSpec((TM, D), lambda i: (i, 0)),
        pl.BlockSpec((None, N_MOD, D), lambda i: (_grp_of_block(i, TM), 0, 0)),
        _const_spec((6, D)),
    ]


def _ffn_kernel(*refs, j, pair):
    if pair:
        xa_ref, xb_ref, mod_ref, g_ref, win_ref, wout_ref, o_ref, acc_ref = refs
        x = jnp.where(pl.program_id(0) < N_CTX // TM, xa_ref[...], xb_ref[...])
    else:
        x_ref, mod_ref, g_ref, win_ref, wout_ref, o_ref, acc_ref = refs
        x = x_ref[...]
    h = _ada_pre(x, mod_ref, g_ref, j, 2 * j).astype(jnp.bfloat16)
    for c in range(FFN_H // FFN_HC):
        lo = c * FFN_HC
        a = _dot(h, win_ref[:, lo:lo + FFN_HC])
        b = _dot(h, win_ref[:, FFN_H + lo:FFN_H + lo + FFN_HC])
        u = ((a * jax.nn.sigmoid(a)) * b).astype(jnp.bfloat16)
        part = _dot(u, wout_ref[lo:lo + FFN_HC, :])
        if c == 0:
            acc_ref[...] = part
        else:
            acc_ref[...] += part
    o_ref[...] = _ada_post(x, acc_ref[...], mod_ref, g_ref, j, 2 * j + 1, 0.5)


def _ffn(x, mod_l, g_l, w_in, w_out, l, j, *, base=0, rows=N_TOK):
    pair = isinstance(x, tuple)
    nctx = N_CTX // TM
    b0 = base // TM
    w_idx = (l, j // 2, 0, 0)
    if pair:
        assert base == 0 and rows == N_TOK
        x_specs = [pl.BlockSpec((TM, D), lambda i: (jnp.minimum(i, nctx - 1), 0)),
                   pl.BlockSpec((TM, D), lambda i: (jnp.maximum(i - nctx, 0), 0))]
        x_args = list(x)
    else:
        x_specs = [pl.BlockSpec((TM, D), lambda i: (i + b0, 0))]
        x_args = [x]
    return pl.pallas_call(
        functools.partial(_ffn_kernel, j=j, pair=pair),
        grid=(rows // TM,),
        in_specs=x_specs + [
            pl.BlockSpec((None, N_MOD, D), lambda i: (_grp_of_block(i + b0, TM), 0, 0)),
            _const_spec((6, D)),
            pl.BlockSpec((None, None, D, 2 * FFN_H), lambda i: w_idx, pipeline_mode=pl.Buffered(1)),
            pl.BlockSpec((None, None, FFN_H, D), lambda i: w_idx, pipeline_mode=pl.Buffered(1))],
        out_specs=pl.BlockSpec((TM, D), lambda i: (i, 0)),
        out_shape=jax.ShapeDtypeStruct((rows, D), jnp.float32),
        scratch_shapes=[pltpu.VMEM((TM, D), jnp.float32)],
        compiler_params=_cparams(("parallel",)),
        name=f"ffn{j}",
    )(*x_args, mod_l, g_l, w_in, w_out)


def _rope_lanes(x, cos, sin_signed):
    lane = lax.broadcasted_iota(jnp.int32, x.shape, 1)
    partner = jnp.where((lane % 32) < 16, pltpu.roll(x, LANES - 16, 1), pltpu.roll(x, 16, 1))
    return x * cos + partner * sin_signed


def _rope_rows(x, cos_t, sin_signed_t):
    parts = []
    for g in range(HD // 32):
        parts += [x[g * 32 + 16:g * 32 + 32, :], x[g * 32:g * 32 + 16, :]]
    return x * cos_t + jnp.concatenate(parts, axis=0) * sin_signed_t


def _attn_pre_kernel(x_ref, mod_ref, g_ref, w_ref, wkt_ref, cos_ref, sin_ref, cost_ref, sint_ref,
                     q_ref, kt_ref, v_ref, *, rope):
    x = x_ref[...]
    h = _ada_pre(x, mod_ref, g_ref, 1, 2).astype(jnp.bfloat16)
    qscale = DA_DK ** -0.5 * LOG2E
    zq = _dot(h, w_ref[:, 0:D])
    zkt = _dot_nt(wkt_ref[...], h)
    v_ref[...] = _dot(h, w_ref[:, 2 * D:3 * D]).astype(v_ref.dtype)
    for hd in range(HEADS):
        sl = slice(hd * HD, (hd + 1) * HD)
        q = zq[:, sl]
        k = zkt[sl, :]
        if rope:
            q = _rope_lanes(q, cos_ref[...], sin_ref[...])
            k = _rope_rows(k, cost_ref[...], sint_ref[...])
        q_ref[:, sl] = (q * qscale).astype(q_ref.dtype)
        kt_ref[sl, :] = k.astype(kt_ref.dtype)


def _attn_pre(x, mod_l, g_l, w_in, w_kt, tables, *, ctx):
    nb, seq = (BATCH, SEQ) if ctx else (DEC_BATCH, DEC_SEQ)
    t = min(TM, seq)
    rows = nb * seq
    base = 0 if ctx else N_CTX // t
    per = seq // t
    ptab = DEC_SEQ // t
    kv_dtype = jnp.float32 if ctx else jnp.bfloat16
    out_spec = pl.BlockSpec((t, D), lambda i: (i, 0))
    return pl.pallas_call(
        functools.partial(_attn_pre_kernel, rope=not ctx),
        grid=(rows // t,),
        in_specs=[
            pl.BlockSpec((t, D), lambda i: (i + base, 0)),
            pl.BlockSpec((None, N_MOD, D), lambda i: (_grp_of_block(i + base, t), 0, 0)),
            _const_spec((6, D)), _const_spec((D, 3 * D)), _const_spec((D, D)),
            pl.BlockSpec((t, HD), lambda i: (i % ptab, 0)),
            pl.BlockSpec((t, HD), lambda i: (i % ptab, 0)),
            pl.BlockSpec((HD, t), lambda i: (0, i % ptab)),
            pl.BlockSpec((HD, t), lambda i: (0, i % ptab)),
        ],
        out_specs=[out_spec, pl.BlockSpec((None, D, t), lambda i: (i // per, 0, i % per)), out_spec],
        out_shape=[
            jax.ShapeDtypeStruct((rows, D), jnp.bfloat16),
            jax.ShapeDtypeStruct((nb, D, seq), kv_dtype),
            jax.ShapeDtypeStruct((rows, D), kv_dtype),
        ],
        compiler_params=_cparams(("parallel",)),
        name="attn_pre_ctx" if ctx else "attn_pre_smp",
    )(x, mod_l, g_l, w_in, w_kt, *tables)


def _attn_kernel(*refs, lam_init, cache):
    if cache:
        (x_ref, mod_ref, g_ref, q_ref, k_ref, v_ref, kc_ref, vc_ref,
         lam_ref, sub_ref, w_ref, o_ref, oh_ref, s_ref, e_ref) = refs
    else:
        (x_ref, mod_ref, g_ref, q_ref, k_ref, v_ref,
         lam_ref, sub_ref, w_ref, o_ref, oh_ref, s_ref, e_ref) = refs
    lp = lam_ref[...]
    lam = (jnp.exp(jnp.sum(lp[0:1] * lp[1:2], axis=-1, keepdims=True))
           - jnp.exp(jnp.sum(lp[2:3] * lp[3:4], axis=-1, keepdims=True)) + lam_init)
    tq = q_ref.shape[0]
    lane = lax.broadcasted_iota(jnp.int32, (tq, HD), 1)
    bf = jnp.bfloat16
    n_own = k_ref.shape[1] // KB
    n_kb = n_own + (1 if cache else 0)

    def keys_t(sl, kb):
        if kb < n_own:
            return k_ref[sl, kb * KB:(kb + 1) * KB].astype(bf)
        return kc_ref[sl, :].astype(bf)

    def values(sl, kb):
        if kb < n_own:
            return v_ref[kb * KB:(kb + 1) * KB, sl].astype(bf)
        return vc_ref[:, sl].astype(bf)

    def head_slice(hd):
        return slice(hd * HD, (hd + 1) * HD)

    def kcols(kb):
        return slice(kb * KB, (kb + 1) * KB)

    def stacked_q(hd):
        qh = q_ref[:, head_slice(hd)]
        return jnp.concatenate([jnp.where(lane < DA_DK, qh, jnp.zeros_like(qh)),
                                jnp.where(lane >= DA_DK, qh, jnp.zeros_like(qh))], axis=0)

    st = [dict() for _ in range(HEADS)]
    for slot in range(HEADS + 2):
        h1, h2, h3 = slot, slot - 1, slot - 2
        if h1 < HEADS:
            st[h1]["qq"] = stacked_q(h1)
        if 0 <= h3 < HEADS:
            tot = st[h3]["tot"]
            st[h3]["c"] = jnp.broadcast_to(lam * tot[:tq] / tot[tq:], (tq, KB)).astype(bf)
        for kb in range(n_kb):
            if h1 < HEADS:
                s = _dot(st[h1]["qq"], keys_t(head_slice(h1), kb))
                s_ref[h1 % 2, :, kcols(kb)] = s
                mp = jnp.maximum(s[:, :LANES], s[:, LANES:])
                st[h1]["m"] = mp if kb == 0 else jnp.maximum(st[h1]["m"], mp)
            if 0 <= h2 < HEADS:
                e = jnp.exp2(s_ref[h2 % 2, :, kcols(kb)] - st[h2]["mrow"])
                ep = e[:, :LANES] + e[:, LANES:]
                st[h2]["l"] = ep if kb == 0 else st[h2]["l"] + ep
                e_ref[h2 % 2, :, kcols(kb)] = e.astype(bf)
            if 0 <= h3 < HEADS:
                a = e_ref[h3 % 2, :tq, kcols(kb)] - st[h3]["c"] * e_ref[h3 % 2, tq:, kcols(kb)]
                part = _dot(a, values(head_slice(h3), kb))
                st[h3]["o"] = part if kb == 0 else st[h3]["o"] + part
        if h1 < HEADS:
            st[h1]["mrow"] = jnp.max(st[h1]["m"], axis=-1, keepdims=True)
        if 0 <= h2 < HEADS:
            st[h2]["tot"] = jnp.sum(st[h2]["l"], axis=-1, keepdims=True)
        if 0 <= h3 < HEADS:
            o = st[h3]["o"] * (1.0 / st[h3]["tot"][:tq])
            o = _rms(o, sub_ref[...]) * (1.0 - lam_init)
            oh_ref[:, head_slice(h3)] = o.astype(bf)
            st[h3].clear()
    y = _dot(oh_ref[...], w_ref[...])
    o_ref[...] = _ada_post(x_ref[...], y, mod_ref, g_ref, 1, 3, 1.0)


def _attn(x, mod_l, g_l, q, k, v, kc, vc, lam_p, subln, w_out, *, ctx, lam_init):
    nb, seq = (BATCH, SEQ) if ctx else (DEC_BATCH, DEC_SEQ)
    rows = nb * seq
    base = 0 if ctx else N_CTX // TQ
    nq = seq // TQ
    assert seq % KB == 0 and PAST == KB
    n_keys = seq if ctx else seq + PAST
    row_spec = pl.BlockSpec((TQ, D), lambda b, i: (b * nq + i, 0))
    in_specs = [
        pl.BlockSpec((TQ, D), lambda b, i: (base + b * nq + i, 0)),
        pl.BlockSpec((None, N_MOD, D), lambda b, i: (_grp_of_block(base + b * nq + i, TQ), 0, 0)),
        pl.BlockSpec((6, D), lambda b, i: (0, 0)),
        row_spec,
        pl.BlockSpec((None, D, seq), lambda b, i: (b, 0, 0)),
        pl.BlockSpec((seq, D), lambda b, i: (b, 0)),
    ]
    args = [x, mod_l, g_l, q, k, v]
    if not ctx:
        in_specs += [pl.BlockSpec((None, D, PAST), lambda b, i: (b, 0, 0)),
                     pl.BlockSpec((None, PAST, D), lambda b, i: (b, 0, 0))]
        args += [kc, vc]
    in_specs += [
        pl.BlockSpec((4, DA_DK), lambda b, i: (0, 0)),
        pl.BlockSpec((1, HD), lambda b, i: (0, 0)),
        pl.BlockSpec((D, D), lambda b, i: (0, 0), pipeline_mode=pl.Buffered(1)),
    ]
    args += [lam_p, subln, w_out]
    return pl.pallas_call(
        functools.partial(_attn_kernel, lam_init=lam_init, cache=not ctx),
        grid=(nb, nq),
        in_specs=in_specs,
        out_specs=row_spec,
        out_shape=jax.ShapeDtypeStruct((rows, D), jnp.float32),
        scratch_shapes=[pltpu.VMEM((TQ, D), jnp.bfloat16),
                        pltpu.VMEM((2, 2 * TQ, n_keys), jnp.float32),
                        pltpu.VMEM((2, 2 * TQ, n_keys), jnp.bfloat16)],
        compiler_params=_cparams(("parallel", "parallel")),
        name="attn_ctx" if ctx else "attn_smp",
    )(*args)


def _hgrn_pre_kernel(x_ref, mod_ref, g_ref, w_ref, lb_ref,
                     q_ref, i_ref, gt_ref, lf_ref, qs_ref, ks_ref, bl_ref, qf_ref, b_ref, *, layer):
    x = x_ref[...]
    h = _ada_pre(x, mod_ref, g_ref, 1, 2).astype(jnp.bfloat16)

    def proj(n):
        return _dot(h, w_ref[:, n * D:(n + 1) * D])

    def log_gate(d, ff):
        raw = [lb_ref[l, d:d + 1, :] for l in range(DEPTH)]
        m = functools.reduce(jnp.maximum, raw)
        ex = [jnp.exp(r - m) for r in raw]
        tot = functools.reduce(lambda a, b: a + b, ex)
        soft = [e / tot for e in ex]
        lb = functools.reduce(lambda a, b: a + b, soft[:layer + 1]) - soft[0]
        return jnp.log(lb + (1.0 - lb) * jax.nn.sigmoid(ff))

    bf = jnp.bfloat16

    def scaled_operands(d, c):
        base = c * CH
        last = _rec_cumsum(lf_ref.at[d], b_ref.at[d], None, d == 1, base)
        bl_ref[c, d:d + 1, :] = last[0:1, :]
        half = 0.5 * last
        half2 = jnp.concatenate([half, half], axis=0)
        for v2 in range(N_SLAB // 2):
            rows = slice(base + v2 * 2 * SUB, base + (v2 + 1) * 2 * SUB)
            k = 1.0 - jnp.exp2(lf_ref[d, rows, :] * LOG2E)
            dd = b_ref[d, rows, :] - half2
            qs_ref[d, rows, :] = (qf_ref[rows, :] * jnp.exp2(dd)).astype(bf)
            ks_ref[d, rows, :] = (k * jnp.exp2(-dd)).astype(bf)

    def half_proj(n, part):
        lo = n * D + part * (D // 2)
        return _dot(h, w_ref[:, lo:lo + D // 2])

    assert x_ref.shape[0] // CH == 2
    ff = proj(1)
    qf_ref[...] = proj(0)
    q_ref[...] = qf_ref[...].astype(q_ref.dtype)
    lf_ref[0] = log_gate(0, ff)
    fb = proj(2)
    lf_ref[1] = log_gate(1, fb)
    work = [(d, c) for d in range(2) for c in range(2)]
    for n, out_ref in ((3, i_ref), (4, gt_ref)):
        for part in range(2):
            cols = slice(part * (D // 2), (part + 1) * (D // 2))
            out_ref[:, cols] = half_proj(n, part).astype(out_ref.dtype)
            scaled_operands(*work.pop(0))


def _hgrn_pre(x, mod_l, g_l, w_in, lb_raw, layer):
    t = TH
    row = pl.BlockSpec((t, D), lambda i: (i, 0))
    both = pl.BlockSpec((2, t, D), lambda i: (0, i, 0))
    return pl.pallas_call(
        functools.partial(_hgrn_pre_kernel, layer=layer),
        grid=(N_TOK // t,),
        in_specs=[
            row,
            pl.BlockSpec((None, N_MOD, D), lambda i: (_grp_of_block(i, t), 0, 0)),
            _const_spec((6, D)), _const_spec((D, 5 * D)), _const_spec((DEPTH, 2, D))],
        out_specs=[row, row, row, both, both, both,
                   pl.BlockSpec((t // CH, 2, D), lambda i: (i, 0, 0))],
        out_shape=[
            jax.ShapeDtypeStruct((N_TOK, D), jnp.bfloat16),
            jax.ShapeDtypeStruct((N_TOK, D), jnp.bfloat16),
            jax.ShapeDtypeStruct((N_TOK, D), jnp.bfloat16),
            jax.ShapeDtypeStruct((2, N_TOK, D), jnp.float32),
            jax.ShapeDtypeStruct((2, N_TOK, D), jnp.bfloat16),
            jax.ShapeDtypeStruct((2, N_TOK, D), jnp.bfloat16),
            jax.ShapeDtypeStruct((N_TOK // CH, 2, D), jnp.float32),
        ],
        scratch_shapes=[pltpu.VMEM((t, D), jnp.float32), pltpu.VMEM((2, t, D), jnp.float32)],
        compiler_params=_cparams(("parallel",)),
        name="hgrn_pre",
    )(x, mod_l, g_l, w_in, lb_raw)


def _level_masks(rev):
    t = np.arange(CH)[:, None]
    s = np.arange(CH)[None, :]
    out = []
    for p in range(N_LEVEL):
        same = (t >> (p + 1)) == (s >> (p + 1))
        tb, sb = (t >> p) & 1, (s >> p) & 1
        out.append(same & ((tb == 0) & (sb == 1) if rev else (tb == 1) & (sb == 0)))
    out.append(t == s)
    return np.stack(out).astype(np.float32)


def _row_bcast(x, r):
    return jnp.broadcast_to(x[r:r + 1, :], x.shape)


def _rec_cumsum(lf_ref, b_ref, br_ref, rev, base=0):
    sub = lax.broadcasted_iota(jnp.int32, (SUB, D), 0)
    order = range(N_SLAB - 1, -1, -1) if rev else range(N_SLAB)
    edge = 0 if rev else SUB - 1
    carry = None
    for v in order:
        rows = slice(base + v * SUB, base + (v + 1) * SUB)
        c = lf_ref[rows, :] * LOG2E
        for k in (1, 2, 4):
            if rev:
                c = c + jnp.where(sub < SUB - k, pltpu.roll(c, SUB - k, 0), 0.0)
            else:
                c = c + jnp.where(sub >= k, pltpu.roll(c, k, 0), 0.0)
        if carry is not None:
            c = c + carry
        carry = _row_bcast(c, edge)
        b_ref[rows, :] = c
        if br_ref is not None:
            br_ref[rows, :] = carry
    return carry


def _rec_fast(lf_ref, q_ref, i_ref, b_ref, b_last, qs_ref, ks_ref, st_ref, o_ref, rev):
    bf = jnp.bfloat16
    half = 0.5 * b_last
    half2 = jnp.concatenate([half, half], axis=0)
    for v2 in range(N_SLAB // 2):
        rows = slice(v2 * 2 * SUB, (v2 + 1) * 2 * SUB)
        k = 1.0 - jnp.exp2(lf_ref[rows, :] * LOG2E)
        d = b_ref[rows, :] - half2
        qs_ref[rows, :] = (q_ref[rows, :].astype(jnp.float32) * jnp.exp2(d)).astype(bf)
        ks_ref[rows, :] = (k * jnp.exp2(-d)).astype(bf)
    scale = jnp.exp2(half[0:1, :])
    decay = scale * scale
    t = lax.broadcasted_iota(jnp.int32, (CH, CH), 0)
    s = lax.broadcasted_iota(jnp.int32, (CH, CH), 1)
    seen = (s >= t) if rev else (s <= t)
    for hd in range(HEADS):
        sl = slice(hd * HD, (hd + 1) * HD)
        qs, ks, vh = qs_ref[:, sl], ks_ref[:, sl], i_ref[:, sl]
        a = jnp.where(seen, _dot_nt(qs, ks), 0.0)
        st = st_ref[hd]
        o = _dot(a.astype(bf), vh) + _dot_nt(qs, (st * scale[:, sl]).astype(bf))
        o_ref[:, sl] = o.astype(o_ref.dtype)
        st_ref[hd] = st * decay[:, sl] + _dot_tn(vh, ks) * scale[:, sl]


def _rec_prepare(lf_ref, q_ref, b_ref, br_ref, b_last, qt_ref, kt_ref, rev):
    bf = jnp.bfloat16
    sub = lax.broadcasted_iota(jnp.int32, (SUB, D), 0)
    zeros = jnp.zeros((SUB, D), jnp.float32)
    for v2 in range(N_SLAB // 2):
        rows = slice(v2 * 2 * SUB, (v2 + 1) * 2 * SUB)
        f = jnp.exp2(lf_ref[rows, :] * LOG2E)
        k = 1.0 - f
        q = q_ref[rows, :].astype(jnp.float32)
        b = b_ref[rows, :]
        qt_ref[0, rows, :] = (q * f).astype(bf)
        kt_ref[0, rows, :] = k.astype(bf)
        half = (slice(0, SUB), slice(SUB, 2 * SUB))
        for p in range(1, N_LEVEL):
            if p >= 3:
                w = p - 3
                es, qside = [], []
                for n, v in enumerate((2 * v2, 2 * v2 + 1)):
                    hi = v & ~((1 << (w + 1)) - 1)
                    vr = hi | (1 << w) if rev else hi | ((1 << w) - 1)
                    r = br_ref[vr * SUB:(vr + 1) * SUB, :]
                    qside.append(((v >> w) & 1) == (0 if rev else 1))
                    es.append(jnp.exp2(b[half[n], :] - r) if qside[n] else jnp.exp2(r - b[half[n], :]))
                if qside[0] == qside[1]:
                    e = jnp.concatenate(es, axis=0)
                    if qside[0]:
                        qt_ref[p, rows, :] = (q * e).astype(bf)
                    else:
                        kt_ref[p, rows, :] = (k * e).astype(bf)
                else:
                    qe = [q[half[n], :] * es[n] if qside[n] else zeros for n in range(2)]
                    ke = [zeros if qside[n] else k[half[n], :] * es[n] for n in range(2)]
                    qt_ref[p, rows, :] = jnp.concatenate(qe, axis=0).astype(bf)
                    kt_ref[p, rows, :] = jnp.concatenate(ke, axis=0).astype(bf)
            else:
                es = []
                for n in range(2):
                    bv = b[half[n], :]
                    if p == 2:
                        r = _row_bcast(bv, 4 if rev else 3)
                    else:
                        lo, hi_r = (2, 6) if rev else (1, 5)
                        r = jnp.where(sub < 4, _row_bcast(bv, lo), _row_bcast(bv, hi_r))
                    es.append(jnp.exp2(-jnp.abs(bv - r)))
                e = jnp.concatenate(es, axis=0)
                qt_ref[p, rows, :] = (q * e).astype(bf)
                kt_ref[p, rows, :] = (k * e).astype(bf)
        bl = jnp.concatenate([b_last, b_last], axis=0)
        qt_ref[N_LEVEL, rows, :] = (q * jnp.exp2(b)).astype(bf)
        kt_ref[N_LEVEL, rows, :] = (k * jnp.exp2(bl - b)).astype(bf)
    return jnp.exp2(b_last[0:1, :])


def _rec_heads(q_ref, i_ref, qt_ref, kt_ref, msk_ref, st_ref, o_ref, decay):
    bf = jnp.bfloat16
    top = N_LEVEL - 1
    for hd in range(HEADS):
        sl = slice(hd * HD, (hd + 1) * HD)
        a = _dot_nt(qt_ref[top, :, sl], kt_ref[top, :, sl])
        a = a + msk_ref[N_LEVEL] * _dot_nt(q_ref[:, sl], kt_ref[0, :, sl])
        for p in range(top):
            a = a + msk_ref[p] * _dot_nt(qt_ref[p, :, sl], kt_ref[p, :, sl])
        vh = i_ref[:, sl]
        st = st_ref[hd]
        o = _dot(a.astype(bf), vh) + _dot_nt(qt_ref[N_LEVEL, :, sl], st.astype(bf))
        o_ref[:, sl] = o.astype(o_ref.dtype)
        st_ref[hd] = st * decay[:, sl] + _dot_tn(vh, kt_ref[N_LEVEL, :, sl])


def _rec_kernel(blkf_ref, blkb_ref, sid_ref, first_ref, last_ref,
                qf_ref, if_ref, lff_ref, qb_ref, ib_ref, lfb_ref, s0_ref, mskf_ref, mskb_ref,
                of_ref, ob_ref, sfin_ref,
                st_ref, bf_ref, brf_ref, qtf_ref, ktf_ref, bb_ref, brb_ref, qtb_ref, ktb_ref):
    step = pl.program_id(0)

    @pl.when(step == 0)
    def _():
        for ref in (qtf_ref, ktf_ref, qtb_ref, ktb_ref):
            ref[...] = jnp.zeros_like(ref)

    @pl.when(first_ref[step] == 1)
    def _():
        st_ref[...] = jnp.zeros_like(st_ref)

    @pl.when(first_ref[step] == 2)
    def _():
        for d in range(2):
            for hd in range(HEADS):
                st_ref[d, hd] = s0_ref[d, hd].T

    last_f = _rec_cumsum(lff_ref, bf_ref, brf_ref, False)
    last_b = _rec_cumsum(lfb_ref, bb_ref, brb_ref, True)
    moderate = jnp.min(jnp.minimum(last_f, last_b)) >= -MAX_FAST_LOG2_DECAY

    @pl.when(moderate)
    def _():
        _rec_fast(lff_ref, qf_ref, if_ref, bf_ref, last_f, qtf_ref.at[N_LEVEL], ktf_ref.at[N_LEVEL],
                  st_ref.at[0], of_ref, False)
        _rec_fast(lfb_ref, qb_ref, ib_ref, bb_ref, last_b, qtb_ref.at[N_LEVEL], ktb_ref.at[N_LEVEL],
                  st_ref.at[1], ob_ref, True)

    @pl.when(jnp.logical_not(moderate))
    def _():
        decay_f = _rec_prepare(lff_ref, qf_ref, bf_ref, brf_ref, last_f, qtf_ref, ktf_ref, False)
        decay_b = _rec_prepare(lfb_ref, qb_ref, bb_ref, brb_ref, last_b, qtb_ref, ktb_ref, True)
        _rec_heads(qf_ref, if_ref, qtf_ref, ktf_ref, mskf_ref, st_ref.at[0], of_ref, decay_f)
        _rec_heads(qb_ref, ib_ref, qtb_ref, ktb_ref, mskb_ref, st_ref.at[1], ob_ref, decay_b)

    @pl.when(last_ref[step] == 1)
    def _():
        for d in range(2):
            for hd in range(HEADS):
                sfin_ref[d, hd] = st_ref[d, hd].T


def _rec_tables():
    blkf, blkb, sid, first, last = [], [], [], [], []
    seqs = [(b * (SEQ // CH), SEQ // CH) for b in range(BATCH)]
    seqs += [(N_CTX // CH + b * (DEC_SEQ // CH), DEC_SEQ // CH) for b in range(DEC_BATCH)]
    for n, (start, nc) in enumerate(seqs):
        ctx = n < BATCH
        for c in range(nc):
            blkf.append(start + c)
            blkb.append(start + nc - 1 - c)
            sid.append(n)
            first.append((1 if ctx else 2) if c == 0 else 0)
            last.append(int(ctx and c == nc - 1))
    return [jnp.asarray(np.asarray(a, np.int32)) for a in (blkf, blkb, sid, first, last)]


def _rec(q, i, lf, s0):
    row_f = pl.BlockSpec((CH, D), lambda s, bkf, bkb, sid, fi, la: (bkf[s], 0))
    row_b = pl.BlockSpec((CH, D), lambda s, bkf, bkb, sid, fi, la: (bkb[s], 0))
    st_shape = (None, 2, HEADS, HD, HD)
    scratch = [pltpu.VMEM((2, HEADS, HD, HD), jnp.float32)]
    for _ in range(2):
        scratch += [
            pltpu.VMEM((CH, D), jnp.float32),
            pltpu.VMEM((CH, D), jnp.float32),
            pltpu.VMEM((N_LEVEL + 1, CH, D), jnp.bfloat16),
            pltpu.VMEM((N_LEVEL + 1, CH, D), jnp.bfloat16),
        ]
    msk_spec = pl.BlockSpec((N_LEVEL + 1, CH, CH), lambda s, *_: (0, 0, 0))
    grid_spec = pltpu.PrefetchScalarGridSpec(
        num_scalar_prefetch=5,
        grid=(N_TOK // CH,),
        in_specs=[
            row_f, row_f,
            pl.BlockSpec((None, CH, D), lambda s, bkf, bkb, sid, fi, la: (0, bkf[s], 0)),
            row_b, row_b,
            pl.BlockSpec((None, CH, D), lambda s, bkf, bkb, sid, fi, la: (1, bkb[s], 0)),
            pl.BlockSpec(st_shape, lambda s, bkf, bkb, sid, fi, la: (jnp.maximum(sid[s] - BATCH, 0), 0, 0, 0, 0)),
            msk_spec, msk_spec,
        ],
        out_specs=[
            row_f, row_b,
            pl.BlockSpec(st_shape, lambda s, bkf, bkb, sid, fi, la: (jnp.minimum(sid[s], BATCH - 1), 0, 0, 0, 0)),
        ],
        scratch_shapes=scratch,
    )
    return pl.pallas_call(
        _rec_kernel,
        grid_spec=grid_spec,
        out_shape=[
            jax.ShapeDtypeStruct((N_TOK, D), jnp.bfloat16),
            jax.ShapeDtypeStruct((N_TOK, D), jnp.bfloat16),
            jax.ShapeDtypeStruct((BATCH, 2, HEADS, HD, HD), jnp.float32),
        ],
        compiler_params=_cparams(("arbitrary",)),
        name="hgrn_rec",
    )(*_rec_tables(), q, i, lf, q, i, lf, s0, jnp.asarray(_level_masks(False)), jnp.asarray(_level_masks(True)))


def _rec_scaled_heads(qs_ref, ks_ref, i_ref, bl_ref, d, st_ref, o_ref):
    bf = jnp.bfloat16
    scale = jnp.exp2(0.5 * bl_ref[d:d + 1, :])
    decay = scale * scale
    t = lax.broadcasted_iota(jnp.int32, (CH, CH), 0)
    s = lax.broadcasted_iota(jnp.int32, (CH, CH), 1)
    seen = (s >= t) if d == 1 else (s <= t)
    for hd in range(HEADS):
        sl = slice(hd * HD, (hd + 1) * HD)
        qs, ks, vh = qs_ref[:, sl], ks_ref[:, sl], i_ref[:, sl]
        a = jnp.where(seen, _dot_nt(qs, ks), 0.0)
        st = st_ref[hd]
        o = _dot(a.astype(bf), vh) + _dot_nt(qs, (st * scale[:, sl]).astype(bf))
        o_ref[:, sl] = o.astype(o_ref.dtype)
        st_ref[hd] = st * decay[:, sl] + _dot_tn(vh, ks) * scale[:, sl]


def _rec_scaled_kernel(blkf_ref, blkb_ref, sid_ref, first_ref, last_ref,
                       qsf_ref, ksf_ref, if_ref, blf_ref, qsb_ref, ksb_ref, ib_ref, blb_ref, s0_ref,
                       of_ref, ob_ref, sfin_ref, st_ref):
    step = pl.program_id(0)

    @pl.when(first_ref[step] == 1)
    def _():
        st_ref[...] = jnp.zeros_like(st_ref)

    @pl.when(first_ref[step] == 2)
    def _():
        for d in range(2):
            for hd in range(HEADS):
                st_ref[d, hd] = s0_ref[d, hd].T

    _rec_scaled_heads(qsf_ref, ksf_ref, if_ref, blf_ref, 0, st_ref.at[0], of_ref)
    _rec_scaled_heads(qsb_ref, ksb_ref, ib_ref, blb_ref, 1, st_ref.at[1], ob_ref)

    @pl.when(last_ref[step] == 1)
    def _():
        for d in range(2):
            for hd in range(HEADS):
                sfin_ref[d, hd] = st_ref[d, hd].T


def _rec_scaled(qs, ks, i, bl, s0):
    def at(which, d=None):
        def index(s, bkf, bkb, sid, fi, la):
            blk = (bkf, bkb)[which][s]
            return (blk, 0) if d is None else (d, blk, 0)
        return index

    row_f = pl.BlockSpec((CH, D), at(0))
    row_b = pl.BlockSpec((CH, D), at(1))
    dir_f = pl.BlockSpec((None, CH, D), at(0, 0))
    dir_b = pl.BlockSpec((None, CH, D), at(1, 1))
    bl_f = pl.BlockSpec((None, 2, D), lambda s, bkf, bkb, sid, fi, la: (bkf[s], 0, 0))
    bl_b = pl.BlockSpec((None, 2, D), lambda s, bkf, bkb, sid, fi, la: (bkb[s], 0, 0))
    st_shape = (None, 2, HEADS, HD, HD)
    grid_spec = pltpu.PrefetchScalarGridSpec(
        num_scalar_prefetch=5,
        grid=(N_TOK // CH,),
        in_specs=[
            dir_f, dir_f, row_f, bl_f, dir_b, dir_b, row_b, bl_b,
            pl.BlockSpec(st_shape, lambda s, bkf, bkb, sid, fi, la: (jnp.maximum(sid[s] - BATCH, 0), 0, 0, 0, 0)),
        ],
        out_specs=[
            row_f, row_b,
            pl.BlockSpec(st_shape, lambda s, bkf, bkb, sid, fi, la: (jnp.minimum(sid[s], BATCH - 1), 0, 0, 0, 0)),
        ],
        scratch_shapes=[pltpu.VMEM((2, HEADS, HD, HD), jnp.float32)],
    )
    return pl.pallas_call(
        _rec_scaled_kernel,
        grid_spec=grid_spec,
        out_shape=[
            jax.ShapeDtypeStruct((N_TOK, D), jnp.bfloat16),
            jax.ShapeDtypeStruct((N_TOK, D), jnp.bfloat16),
            jax.ShapeDtypeStruct((BATCH, 2, HEADS, HD, HD), jnp.float32),
        ],
        compiler_params=_cparams(("arbitrary",)),
        name="hgrn_rec_scaled",
    )(*_rec_tables(), qs, ks, i, bl, qs, ks, i, bl, s0)


def _hgrn_out_kernel(x_ref, mod_ref, g_ref, of_ref, ob_ref, gt_ref, gn_ref, w_ref, o_ref, oh_ref):
    for hd in range(HEADS):
        sl = slice(hd * HD, (hd + 1) * HD)
        o = of_ref[:, sl].astype(jnp.float32) + ob_ref[:, sl].astype(jnp.float32)
        o = _rms(o, gn_ref[...]) * jax.nn.sigmoid(gt_ref[:, sl].astype(jnp.float32))
        oh_ref[:, sl] = o.astype(jnp.bfloat16)
    y = _dot(oh_ref[...], w_ref[...])
    o_ref[...] = _ada_post(x_ref[...], y, mod_ref, g_ref, 1, 3, 1.0)


def _hgrn_out(x, mod_l, g_l, o_f, o_b, gt, gnorm, w_out):
    row = pl.BlockSpec((TM, D), lambda i: (i, 0))
    return pl.pallas_call(
        _hgrn_out_kernel,
        grid=(N_TOK // TM,),
        in_specs=_tok_specs() + [row, row, row, _const_spec((1, HD)), _const_spec((D, D))],
        out_specs=row,
        out_shape=jax.ShapeDtypeStruct((N_TOK, D), jnp.float32),
        scratch_shapes=[pltpu.VMEM((TM, D), jnp.bfloat16)],
        compiler_params=_cparams(("parallel",)),
        name="hgrn_out",
    )(x, mod_l, g_l, o_f, o_b, gt, gnorm, w_out)


def _rope_tables():
    pos = np.arange(DEC_SEQ)
    row = (pos // GRID_W).astype(np.float64)
    col = (pos % GRID_W).astype(np.float64)
    nf = DA_DK // 4
    inv = ROPE_THETA ** (-np.arange(nf, dtype=np.float64) / nf)
    lane = np.arange(HD)
    axis = (lane % DA_DK) // (2 * nf)
    ang = np.where(axis[None, :] == 0, row[:, None], col[:, None]) * inv[lane % nf][None, :]
    sign = np.where((lane % (2 * nf)) < nf, -1.0, 1.0)
    cos, sin = np.cos(ang), np.sin(ang) * sign[None, :]
    return tuple(jnp.asarray(a, jnp.float32) for a in (cos, sin, cos.T, sin.T))


def kernel(x_prompt, x_sample, cache_k, cache_v, state_hgrn, c, c_ctx, w_mod, b_mod, norm_g,
           ffn_w_in, ffn_w_out, attn_w_in, attn_w_out, attn_lambda, attn_subln,
           hgrn_w_in, hgrn_w_out, hgrn_lower_bounds, hgrn_gnorm):
    bf = jnp.bfloat16
    x = (x_prompt.reshape(N_CTX, D), x_sample.reshape(N_SMP, D))
    cc = jnp.concatenate([c_ctx[None, :], c, jnp.zeros((N_GRP - 1 - DEC_BATCH, D), jnp.float32)], axis=0)
    mod = _modulation(cc, w_mod, b_mod).reshape(DEPTH, N_GRP, N_MOD, D)
    tables = _rope_tables()

    ks, vs, states = [], [], []
    for l in range(DEPTH):
        mod_l, g_l = mod[l], norm_g[l]
        x = _ffn(x, mod_l, g_l, ffn_w_in, ffn_w_out, l, 0)
        if l % 2 == 0:
            a = l // 2
            lam_init = 0.8 - 0.6 * math.exp(-0.3 * l)
            w_kt = attn_w_in[a][:, D:2 * D].T
            w_out = attn_w_out[a]
            subln = attn_subln[a].reshape(1, HD)
            qc, kt_new, vc_new = _attn_pre(x, mod_l, g_l, attn_w_in[a], w_kt, tables, ctx=True)
            qs, kt_smp, v_smp = _attn_pre(x, mod_l, g_l, attn_w_in[a], w_kt, tables, ctx=False)
            kct = jnp.transpose(cache_k[:, a], (0, 2, 3, 4, 1)).reshape(DEC_BATCH, D, PAST)
            xc = _attn(x, mod_l, g_l, qc, kt_new, vc_new, None, None, attn_lambda[a], subln, w_out,
                       ctx=True, lam_init=lam_init)
            xs = _attn(x, mod_l, g_l, qs, kt_smp, v_smp, kct, cache_v[:, a].reshape(DEC_BATCH, PAST, D),
                       attn_lambda[a], subln, w_out, ctx=False, lam_init=lam_init)
            x = (xc, xs)
            ks.append(jnp.transpose(kt_new.reshape(BATCH, HEADS, 2, DA_DK, SEQ), (0, 4, 1, 2, 3)))
            vs.append(vc_new.reshape(BATCH, SEQ, HEADS, HD))
        else:
            r = l // 2
            q, i, gt, lf, q_sc, k_sc, bl = _hgrn_pre(x, mod_l, g_l, hgrn_w_in[r], hgrn_lower_bounds, l)
            s0 = state_hgrn[:, r].astype(jnp.float32)
            moderate = jnp.min(bl) >= -MAX_FAST_LOG2_DECAY
            o_f, o_b, s_fin = lax.cond(moderate,
                                       lambda: _rec_scaled(q_sc, k_sc, i, bl, s0),
                                       lambda: _rec(q, i, lf, s0))
            x = _hgrn_out(x, mod_l, g_l, o_f, o_b, gt, hgrn_gnorm[r].reshape(1, HD), hgrn_w_out[r])
            states.append(s_fin)
        if l < DEPTH - 1:
            x = _ffn(x, mod_l, g_l, ffn_w_in, ffn_w_out, l, 2)

    assert not isinstance(x, tuple)
    y_prompt = _ffn(x, mod_l, g_l, ffn_w_in, ffn_w_out, DEPTH - 1, 2, base=0, rows=N_CTX)
    y_sample = _ffn(x, mod_l, g_l, ffn_w_in, ffn_w_out, DEPTH - 1, 2, base=N_CTX, rows=N_SMP)
    return (y_prompt.reshape(BATCH, SEQ, D), y_sample.reshape(DEC_BATCH, DEC_SEQ, D),
            jnp.stack(ks, axis=1), jnp.stack(vs, axis=1), jnp.stack(states, axis=1))
```

```python
import functools
import math

import numpy as np
import jax
import jax.numpy as jnp
from jax import lax
from jax.experimental import pallas as pl
from jax.experimental.pallas import tpu as pltpu

D = 1024
BATCH = 16
SEQ = 256
DEPTH = 2
DEC_BATCH = 4
DEC_SEQ = 2048
PAST = 256
GRID_W = 64
N_MOD = 9
EPS = 1e-6
HEADS = 8
HD = 128
DA_DK = 64
ROPE_THETA = 10000.0
FFN_H = 2816
LOG2E = 1.4426950408889634

N_CTX = BATCH * SEQ
N_SMP = DEC_BATCH * DEC_SEQ
N_TOK = N_CTX + N_SMP
N_GRP = 8

VMEM_LIMIT = 56 * 1024 * 1024
LANES = 128
SUB = 8

TM = 512
FFN_HC = 256
TQ = 256
KB = 256
N_QK_GROUP = 2 * HEADS
MAX_UNSHIFTED_LOG2_SCORE = 60.0
CH = 128
N_SLAB = CH // SUB
N_LEVEL = 7
MAX_FAST_LOG2_DECAY = 200.0


def _cparams(sem):
    return pltpu.CompilerParams(dimension_semantics=sem, vmem_limit_bytes=VMEM_LIMIT)


def _grp_of_block(i, rows):
    nctx = N_CTX // rows
    per = DEC_SEQ // rows
    return jnp.where(i < nctx, 0, 1 + (i - nctx) // per)


def _rms(x, g):
    return (x * lax.rsqrt(jnp.mean(x * x, axis=-1, keepdims=True) + EPS)) * g


def _dot(a, b):
    return lax.dot_general(a, b, (((1,), (0,)), ((), ())), preferred_element_type=jnp.float32)


def _dot_nt(a, b):
    return lax.dot_general(a, b, (((1,), (1,)), ((), ())), preferred_element_type=jnp.float32)


def _dot_tn(a, b):
    return lax.dot_general(a, b, (((0,), (0,)), ((), ())), preferred_element_type=jnp.float32)


def _const_spec(shape):
    nd = len(shape)
    return pl.BlockSpec(shape, lambda *_: (0,) * nd, pipeline_mode=pl.Buffered(1))


def _mod_kernel(c_ref, w_ref, b_ref, o_ref):
    c = c_ref[...]
    s = (c * jax.nn.sigmoid(c)).astype(jnp.bfloat16)
    o_ref[...] = _dot(s, w_ref[...].astype(jnp.bfloat16)) + b_ref[...]


def _modulation(cc, w_mod, b_mod):
    tn = 1024
    nt = (N_MOD * D) // tn
    return pl.pallas_call(
        _mod_kernel,
        grid=(DEPTH, nt),
        in_specs=[
            pl.BlockSpec((N_GRP, D), lambda l, j: (0, 0)),
            pl.BlockSpec((None, D, tn), lambda l, j: (l, 0, j)),
            pl.BlockSpec((None, 1, tn), lambda l, j: (l, 0, j)),
        ],
        out_specs=pl.BlockSpec((None, N_GRP, tn), lambda l, j: (l, 0, j)),
        out_shape=jax.ShapeDtypeStruct((DEPTH, N_GRP, N_MOD * D), jnp.float32),
        compiler_params=_cparams(("parallel", "parallel")),
        name="modulation",
    )(cc, w_mod, b_mod.reshape(DEPTH, 1, N_MOD * D))


def _ada_pre(x, mod_ref, g_ref, j, gi):
    shift = mod_ref[3 * j:3 * j + 1, :]
    scale = mod_ref[3 * j + 1:3 * j + 2, :]
    return _rms(x, g_ref[gi:gi + 1, :]) * (1.0 + scale) + shift


def _ada_post(x, y, mod_ref, g_ref, j, gi, res_w):
    gate = mod_ref[3 * j + 2:3 * j + 3, :]
    return x + res_w * gate * _rms(y, g_ref[gi:gi + 1, :])


def _tok_specs():
    return [
        pl.BlockSpec((TM, D), lambda i: (i, 0)),
        pl.BlockSpec((None, N_MOD, D), lambda i: (_grp_of_block(i, TM), 0, 0)),
        _const_spec((6, D)),
    ]


def _ffn_kernel(*refs, j, pair):
    if pair:
        xa_ref, xb_ref, mod_ref, g_ref, win_ref, wout_ref, o_ref, acc_ref = refs
        x = jnp.where(pl.program_id(0) < N_CTX // TM, xa_ref[...], xb_ref[...])
    else:
        x_ref, mod_ref, g_ref, win_ref, wout_ref, o_ref, acc_ref = refs
        x = x_ref[...]
    h = _ada_pre(x, mod_ref, g_ref, j, 2 * j).astype(jnp.bfloat16)
    for c in range(FFN_H // FFN_HC):
        lo = c * FFN_HC
        a = _dot(h, win_ref[:, lo:lo + FFN_HC])
        b = _dot(h, win_ref[:, FFN_H + lo:FFN_H + lo + FFN_HC])
        u = ((a * jax.nn.sigmoid(a)) * b).astype(jnp.bfloat16)
        part = _dot(u, wout_ref[lo:lo + FFN_HC, :])
        if c == 0:
            acc_ref[...] = part
        else:
            acc_ref[...] += part
    o_ref[...] = _ada_post(x, acc_ref[...], mod_ref, g_ref, j, 2 * j + 1, 0.5)


def _ffn(x, mod_l, g_l, w_in, w_out, l, j, *, base=0, rows=N_TOK):
    pair = isinstance(x, tuple)
    nctx = N_CTX // TM
    b0 = base // TM
    w_idx = (l, j // 2, 0, 0)
    if pair:
        assert base == 0 and rows == N_TOK
        x_specs = [pl.BlockSpec((TM, D), lambda i: (jnp.minimum(i, nctx - 1), 0)),
                   pl.BlockSpec((TM, D), lambda i: (jnp.maximum(i - nctx, 0), 0))]
        x_args = list(x)
    else:
        x_specs = [pl.BlockSpec((TM, D), lambda i: (i + b0, 0))]
        x_args = [x]
    return pl.pallas_call(
        functools.partial(_ffn_kernel, j=j, pair=pair),
        grid=(rows // TM,),
        in_specs=x_specs + [
            pl.BlockSpec((None, N_MOD, D), lambda i: (_grp_of_block(i + b0, TM), 0, 0)),
            _const_spec((6, D)),
            pl.BlockSpec((None, None, D, 2 * FFN_H), lambda i: w_idx, pipeline_mode=pl.Buffered(1)),
            pl.BlockSpec((None, None, FFN_H, D), lambda i: w_idx, pipeline_mode=pl.Buffered(1))],
        out_specs=pl.BlockSpec((TM, D), lambda i: (i, 0)),
        out_shape=jax.ShapeDtypeStruct((rows, D), jnp.float32),
        scratch_shapes=[pltpu.VMEM((TM, D), jnp.float32)],
        compiler_params=_cparams(("parallel",)),
        name=f"ffn{j}",
    )(*x_args, mod_l, g_l, w_in, w_out)


def _rope_lanes(x, cos, sin_signed):
    lane = lax.broadcasted_iota(jnp.int32, x.shape, 1)
    partner = jnp.where((lane % 32) < 16, pltpu.roll(x, LANES - 16, 1), pltpu.roll(x, 16, 1))
    return x * cos + partner * sin_signed


def _rope_rows(x, cos_t, sin_signed_t):
    parts = []
    for g in range(HD // 32):
        parts += [x[g * 32 + 16:g * 32 + 32, :], x[g * 32:g * 32 + 16, :]]
    return x * cos_t + jnp.concatenate(parts, axis=0) * sin_signed_t


def _attn_pre_kernel(x_ref, mod_ref, g_ref, w_ref, wkt_ref, grp_ref, cos_ref, sin_ref, cost_ref, sint_ref,
                     q_ref, kt_ref, v_ref, nq_ref, nk_ref, q2_ref, k2_ref, *, rope):
    x = x_ref[...]
    h = _ada_pre(x, mod_ref, g_ref, 1, 2).astype(jnp.bfloat16)
    qscale = DA_DK ** -0.5 * LOG2E
    zq = _dot(h, w_ref[:, 0:D])
    zkt = _dot_nt(wkt_ref[...], h)
    v_ref[...] = _dot(h, w_ref[:, 2 * D:3 * D]).astype(v_ref.dtype)
    for hd in range(HEADS):
        sl = slice(hd * HD, (hd + 1) * HD)
        q = zq[:, sl]
        k = zkt[sl, :]
        if rope:
            q = _rope_lanes(q, cos_ref[...], sin_ref[...])
            k = _rope_rows(k, cost_ref[...], sint_ref[...])
        q = q * qscale
        q_ref[:, sl] = q.astype(q_ref.dtype)
        kt_ref[sl, :] = k.astype(kt_ref.dtype)
        q2_ref[:, sl] = (q * q).astype(jnp.bfloat16)
        k2_ref[sl, :] = (k * k).astype(jnp.bfloat16)
    qn = jnp.max(_dot(q2_ref[...], grp_ref[...]), axis=0, keepdims=True)
    kn = jnp.max(_dot_tn(grp_ref[...], k2_ref[...]), axis=1, keepdims=True)
    nq_ref[...] = jnp.broadcast_to(qn, nq_ref.shape)
    nk_ref[...] = jnp.broadcast_to(kn[:N_QK_GROUP, :], nk_ref.shape)


def _attn_pre(x, mod_l, g_l, w_in, w_kt, grp, tables, *, ctx):
    nb, seq = (BATCH, SEQ) if ctx else (DEC_BATCH, DEC_SEQ)
    t = min(TM, seq)
    rows = nb * seq
    base = 0 if ctx else N_CTX // t
    per = seq // t
    ptab = DEC_SEQ // t
    kv_dtype = jnp.float32 if ctx else jnp.bfloat16
    out_spec = pl.BlockSpec((t, D), lambda i: (i, 0))
    return pl.pallas_call(
        functools.partial(_attn_pre_kernel, rope=not ctx),
        grid=(rows // t,),
        in_specs=[
            pl.BlockSpec((t, D), lambda i: (i + base, 0)),
            pl.BlockSpec((None, N_MOD, D), lambda i: (_grp_of_block(i + base, t), 0, 0)),
            _const_spec((6, D)), _const_spec((D, 3 * D)), _const_spec((D, D)), _const_spec((D, LANES)),
            pl.BlockSpec((t, HD), lambda i: (i % ptab, 0)),
            pl.BlockSpec((t, HD), lambda i: (i % ptab, 0)),
            pl.BlockSpec((HD, t), lambda i: (0, i % ptab)),
            pl.BlockSpec((HD, t), lambda i: (0, i % ptab)),
        ],
        out_specs=[out_spec, pl.BlockSpec((None, D, t), lambda i: (i // per, 0, i % per)), out_spec,
                   pl.BlockSpec((None, SUB, LANES), lambda i: (i, 0, 0)),
                   pl.BlockSpec((None, N_QK_GROUP, LANES), lambda i: (i, 0, 0))],
        out_shape=[
            jax.ShapeDtypeStruct((rows, D), jnp.bfloat16),
            jax.ShapeDtypeStruct((nb, D, seq), kv_dtype),
            jax.ShapeDtypeStruct((rows, D), kv_dtype),
            jax.ShapeDtypeStruct((rows // t, SUB, LANES), jnp.float32),
            jax.ShapeDtypeStruct((rows // t, N_QK_GROUP, LANES), jnp.float32),
        ],
        scratch_shapes=[pltpu.VMEM((t, D), jnp.bfloat16), pltpu.VMEM((D, t), jnp.bfloat16)],
        compiler_params=_cparams(("parallel",)),
        name="attn_pre_ctx" if ctx else "attn_pre_smp",
    )(x, mod_l, g_l, w_in, w_kt, grp, *tables)


def _attn_kernel(*refs, lam_init, cache, shifted):
    refs = list(refs)
    e_ref = refs.pop()
    s_ref = refs.pop() if shifted else None
    if cache:
        (x_ref, mod_ref, g_ref, q_ref, k_ref, v_ref, kc_ref, vc_ref,
         lam_ref, sub_ref, w_ref, o_ref, oh_ref) = refs
    else:
        (x_ref, mod_ref, g_ref, q_ref, k_ref, v_ref,
         lam_ref, sub_ref, w_ref, o_ref, oh_ref) = refs
    lp = lam_ref[...]
    lam = (jnp.exp(jnp.sum(lp[0:1] * lp[1:2], axis=-1, keepdims=True))
           - jnp.exp(jnp.sum(lp[2:3] * lp[3:4], axis=-1, keepdims=True)) + lam_init)
    tq = q_ref.shape[0]
    lane = lax.broadcasted_iota(jnp.int32, (tq, HD), 1)
    bf = jnp.bfloat16
    n_own = k_ref.shape[1] // KB
    n_kb = n_own + (1 if cache else 0)

    def keys_t(sl, kb):
        if kb < n_own:
            return k_ref[sl, kb * KB:(kb + 1) * KB].astype(bf)
        return kc_ref[sl, :].astype(bf)

    def values(sl, kb):
        if kb < n_own:
            return v_ref[kb * KB:(kb + 1) * KB, sl].astype(bf)
        return vc_ref[:, sl].astype(bf)

    def head_slice(hd):
        return slice(hd * HD, (hd + 1) * HD)

    def kcols(kb):
        return slice(kb * KB, (kb + 1) * KB)

    def stacked_q(hd):
        qh = q_ref[:, head_slice(hd)]
        return jnp.concatenate([jnp.where(lane < DA_DK, qh, jnp.zeros_like(qh)),
                                jnp.where(lane >= DA_DK, qh, jnp.zeros_like(qh))], axis=0)

    st = [dict() for _ in range(HEADS)]
    lag = 1 if shifted else 0
    for slot in range(HEADS + lag + 1):
        h1, h2, h3 = slot, slot - lag, slot - lag - 1
        if h1 < HEADS:
            st[h1]["qq"] = stacked_q(h1)
        if 0 <= h3 < HEADS:
            tot = st[h3]["tot"]
            st[h3]["c"] = jnp.broadcast_to(lam * tot[:tq] / tot[tq:], (tq, KB)).astype(bf)
        for kb in range(n_kb):
            if h1 < HEADS:
                s = _dot(st[h1]["qq"], keys_t(head_slice(h1), kb))
                if shifted:
                    s_ref[h1 % 2, :, kcols(kb)] = s
                    mp = jnp.maximum(s[:, :LANES], s[:, LANES:])
                    st[h1]["m"] = mp if kb == 0 else jnp.maximum(st[h1]["m"], mp)
            if 0 <= h2 < HEADS:
                if shifted:
                    e = jnp.exp2(s_ref[h2 % 2, :, kcols(kb)] - st[h2]["mrow"])
                else:
                    e = jnp.exp2(s)
                ep = e[:, :LANES] + e[:, LANES:]
                st[h2]["l"] = ep if kb == 0 else st[h2]["l"] + ep
                e_ref[h2 % 2, :, kcols(kb)] = e.astype(bf)
            if 0 <= h3 < HEADS:
                a = e_ref[h3 % 2, :tq, kcols(kb)] - st[h3]["c"] * e_ref[h3 % 2, tq:, kcols(kb)]
                part = _dot(a, values(head_slice(h3), kb))
                st[h3]["o"] = part if kb == 0 else st[h3]["o"] + part
        if shifted and h1 < HEADS:
            st[h1]["mrow"] = jnp.max(st[h1]["m"], axis=-1, keepdims=True)
        if 0 <= h2 < HEADS:
            st[h2]["tot"] = jnp.sum(st[h2]["l"], axis=-1, keepdims=True)
        if 0 <= h3 < HEADS:
            o = st[h3]["o"] * (1.0 / st[h3]["tot"][:tq])
            o = _rms(o, sub_ref[...]) * (1.0 - lam_init)
            oh_ref[:, head_slice(h3)] = o.astype(bf)
            st[h3].clear()
    y = _dot(oh_ref[...], w_ref[...])
    o_ref[...] = _ada_post(x_ref[...], y, mod_ref, g_ref, 1, 3, 1.0)


def _attn(x, mod_l, g_l, q, k, v, kc, vc, lam_p, subln, w_out, *, ctx, lam_init, shifted):
    nb, seq = (BATCH, SEQ) if ctx else (DEC_BATCH, DEC_SEQ)
    rows = nb * seq
    base = 0 if ctx else N_CTX // TQ
    nq = seq // TQ
    assert seq % KB == 0 and PAST == KB
    n_keys = seq if ctx else seq + PAST
    row_spec = pl.BlockSpec((TQ, D), lambda b, i: (b * nq + i, 0))
    in_specs = [
        pl.BlockSpec((TQ, D), lambda b, i: (base + b * nq + i, 0)),
        pl.BlockSpec((None, N_MOD, D), lambda b, i: (_grp_of_block(base + b * nq + i, TQ), 0, 0)),
        pl.BlockSpec((6, D), lambda b, i: (0, 0)),
        row_spec,
        pl.BlockSpec((None, D, seq), lambda b, i: (b, 0, 0)),
        pl.BlockSpec((seq, D), lambda b, i: (b, 0)),
    ]
    args = [x, mod_l, g_l, q, k, v]
    if not ctx:
        in_specs += [pl.BlockSpec((None, D, PAST), lambda b, i: (b, 0, 0)),
                     pl.BlockSpec((None, PAST, D), lambda b, i: (b, 0, 0))]
        args += [kc, vc]
    in_specs += [
        pl.BlockSpec((4, DA_DK), lambda b, i: (0, 0)),
        pl.BlockSpec((1, HD), lambda b, i: (0, 0)),
        pl.BlockSpec((D, D), lambda b, i: (0, 0), pipeline_mode=pl.Buffered(1)),
    ]
    args += [lam_p, subln, w_out]
    return pl.pallas_call(
        functools.partial(_attn_kernel, lam_init=lam_init, cache=not ctx, shifted=shifted),
        grid=(nb, nq),
        in_specs=in_specs,
        out_specs=row_spec,
        out_shape=jax.ShapeDtypeStruct((rows, D), jnp.float32),
        scratch_shapes=[pltpu.VMEM((TQ, D), jnp.bfloat16)]
        + ([pltpu.VMEM((2, 2 * TQ, n_keys), jnp.float32)] if shifted else [])
        + [pltpu.VMEM((2, 2 * TQ, n_keys), jnp.bfloat16)],
        compiler_params=_cparams(("parallel", "parallel")),
        name=("attn_ctx" if ctx else "attn_smp") + ("_shifted" if shifted else ""),
    )(*args)


def _hgrn_pre_kernel(x_ref, mod_ref, g_ref, w_ref, lb_ref, q_ref, i_ref, gt_ref, lf_ref, *, layer):
    x = x_ref[...]
    h = _ada_pre(x, mod_ref, g_ref, 1, 2).astype(jnp.bfloat16)

    def proj(n):
        return _dot(h, w_ref[:, n * D:(n + 1) * D])

    def log_gate(d, ff):
        raw = [lb_ref[l, d:d + 1, :] for l in range(DEPTH)]
        m = functools.reduce(jnp.maximum, raw)
        ex = [jnp.exp(r - m) for r in raw]
        tot = functools.reduce(lambda a, b: a + b, ex)
        soft = [e / tot for e in ex]
        lb = functools.reduce(lambda a, b: a + b, soft[:layer + 1]) - soft[0]
        return jnp.log(lb + (1.0 - lb) * jax.nn.sigmoid(ff))

    ff = proj(1)
    q_ref[...] = proj(0).astype(q_ref.dtype)
    lf_ref[0] = log_gate(0, ff)
    fb = proj(2)
    i_ref[...] = proj(3).astype(i_ref.dtype)
    lf_ref[1] = log_gate(1, fb)
    gt_ref[...] = proj(4).astype(gt_ref.dtype)


def _hgrn_pre(x, mod_l, g_l, w_in, lb_raw, layer):
    row = pl.BlockSpec((TM, D), lambda i: (i, 0))
    return pl.pallas_call(
        functools.partial(_hgrn_pre_kernel, layer=layer),
        grid=(N_TOK // TM,),
        in_specs=_tok_specs() + [_const_spec((D, 5 * D)), _const_spec((DEPTH, 2, D))],
        out_specs=[row, row, row, pl.BlockSpec((2, TM, D), lambda i: (0, i, 0))],
        out_shape=[
            jax.ShapeDtypeStruct((N_TOK, D), jnp.bfloat16),
            jax.ShapeDtypeStruct((N_TOK, D), jnp.bfloat16),
            jax.ShapeDtypeStruct((N_TOK, D), jnp.bfloat16),
            jax.ShapeDtypeStruct((2, N_TOK, D), jnp.float32),
        ],
        compiler_params=_cparams(("parallel",)),
        name="hgrn_pre",
    )(x, mod_l, g_l, w_in, lb_raw)


def _level_masks(rev):
    t = np.arange(CH)[:, None]
    s = np.arange(CH)[None, :]
    out = []
    for p in range(N_LEVEL):
        same = (t >> (p + 1)) == (s >> (p + 1))
        tb, sb = (t >> p) & 1, (s >> p) & 1
        out.append(same & ((tb == 0) & (sb == 1) if rev else (tb == 1) & (sb == 0)))
    out.append(t == s)
    return np.stack(out).astype(np.float32)


def _row_bcast(x, r):
    return jnp.broadcast_to(x[r:r + 1, :], x.shape)


def _rec_cumsum(lf_ref, b_ref, br_ref, rev):
    sub = lax.broadcasted_iota(jnp.int32, (SUB, D), 0)
    order = range(N_SLAB - 1, -1, -1) if rev else range(N_SLAB)
    edge = 0 if rev else SUB - 1
    carry = None
    for v in order:
        rows = slice(v * SUB, (v + 1) * SUB)
        c = lf_ref[rows, :] * LOG2E
        for k in (1, 2, 4):
            if rev:
                c = c + jnp.where(sub < SUB - k, pltpu.roll(c, SUB - k, 0), 0.0)
            else:
                c = c + jnp.where(sub >= k, pltpu.roll(c, k, 0), 0.0)
        if carry is not None:
            c = c + carry
        carry = _row_bcast(c, edge)
        b_ref[rows, :] = c
        br_ref[rows, :] = carry
    return carry


def _rec_fast(lf_ref, q_ref, i_ref, b_ref, b_last, qs_ref, ks_ref, st_ref, o_ref, rev):
    bf = jnp.bfloat16
    half = 0.5 * b_last
    half2 = jnp.concatenate([half, half], axis=0)
    for v2 in range(N_SLAB // 2):
        rows = slice(v2 * 2 * SUB, (v2 + 1) * 2 * SUB)
        k = 1.0 - jnp.exp2(lf_ref[rows, :] * LOG2E)
        d = b_ref[rows, :] - half2
        qs_ref[rows, :] = (q_ref[rows, :].astype(jnp.float32) * jnp.exp2(d)).astype(bf)
        ks_ref[rows, :] = (k * jnp.exp2(-d)).astype(bf)
    scale = jnp.exp2(half[0:1, :])
    decay = scale * scale
    t = lax.broadcasted_iota(jnp.int32, (CH, CH), 0)
    s = lax.broadcasted_iota(jnp.int32, (CH, CH), 1)
    seen = (s >= t) if rev else (s <= t)
    for hd in range(HEADS):
        sl = slice(hd * HD, (hd + 1) * HD)
        qs, ks, vh = qs_ref[:, sl], ks_ref[:, sl], i_ref[:, sl]
        a = jnp.where(seen, _dot_nt(qs, ks), 0.0)
        st = st_ref[hd]
        o = _dot(a.astype(bf), vh) + _dot_nt(qs, (st * scale[:, sl]).astype(bf))
        o_ref[:, sl] = o.astype(o_ref.dtype)
        st_ref[hd] = st * decay[:, sl] + _dot_tn(vh, ks) * scale[:, sl]


def _rec_prepare(lf_ref, q_ref, b_ref, br_ref, b_last, qt_ref, kt_ref, rev):
    bf = jnp.bfloat16
    sub = lax.broadcasted_iota(jnp.int32, (SUB, D), 0)
    zeros = jnp.zeros((SUB, D), jnp.float32)
    for v2 in range(N_SLAB // 2):
        rows = slice(v2 * 2 * SUB, (v2 + 1) * 2 * SUB)
        f = jnp.exp2(lf_ref[rows, :] * LOG2E)
        k = 1.0 - f
        q = q_ref[rows, :].astype(jnp.float32)
        b = b_ref[rows, :]
        qt_ref[0, rows, :] = (q * f).astype(bf)
        kt_ref[0, rows, :] = k.astype(bf)
        half = (slice(0, SUB), slice(SUB, 2 * SUB))
        for p in range(1, N_LEVEL):
            if p >= 3:
                w = p - 3
                es, qside = [], []
                for n, v in enumerate((2 * v2, 2 * v2 + 1)):
                    hi = v & ~((1 << (w + 1)) - 1)
                    vr = hi | (1 << w) if rev else hi | ((1 << w) - 1)
                    r = br_ref[vr * SUB:(vr + 1) * SUB, :]
                    qside.append(((v >> w) & 1) == (0 if rev else 1))
                    es.append(jnp.exp2(b[half[n], :] - r) if qside[n] else jnp.exp2(r - b[half[n], :]))
                if qside[0] == qside[1]:
                    e = jnp.concatenate(es, axis=0)
                    if qside[0]:
                        qt_ref[p, rows, :] = (q * e).astype(bf)
                    else:
                        kt_ref[p, rows, :] = (k * e).astype(bf)
                else:
                    qe = [q[half[n], :] * es[n] if qside[n] else zeros for n in range(2)]
                    ke = [zeros if qside[n] else k[half[n], :] * es[n] for n in range(2)]
                    qt_ref[p, rows, :] = jnp.concatenate(qe, axis=0).astype(bf)
                    kt_ref[p, rows, :] = jnp.concatenate(ke, axis=0).astype(bf)
            else:
                es = []
                for n in range(2):
                    bv = b[half[n], :]
                    if p == 2:
                        r = _row_bcast(bv, 4 if rev else 3)
                    else:
                        lo, hi_r = (2, 6) if rev else (1, 5)
                        r = jnp.where(sub < 4, _row_bcast(bv, lo), _row_bcast(bv, hi_r))
                    es.append(jnp.exp2(-jnp.abs(bv - r)))
                e = jnp.concatenate(es, axis=0)
                qt_ref[p, rows, :] = (q * e).astype(bf)
                kt_ref[p, rows, :] = (k * e).astype(bf)
        bl = jnp.concatenate([b_last, b_last], axis=0)
        qt_ref[N_LEVEL, rows, :] = (q * jnp.exp2(b)).astype(bf)
        kt_ref[N_LEVEL, rows, :] = (k * jnp.exp2(bl - b)).astype(bf)
    return jnp.exp2(b_last[0:1, :])


def _rec_heads(q_ref, i_ref, qt_ref, kt_ref, msk_ref, st_ref, o_ref, decay):
    bf = jnp.bfloat16
    top = N_LEVEL - 1
    for hd in range(HEADS):
        sl = slice(hd * HD, (hd + 1) * HD)
        a = _dot_nt(qt_ref[top, :, sl], kt_ref[top, :, sl])
        a = a + msk_ref[N_LEVEL] * _dot_nt(q_ref[:, sl], kt_ref[0, :, sl])
        for p in range(top):
            a = a + msk_ref[p] * _dot_nt(qt_ref[p, :, sl], kt_ref[p, :, sl])
        vh = i_ref[:, sl]
        st = st_ref[hd]
        o = _dot(a.astype(bf), vh) + _dot_nt(qt_ref[N_LEVEL, :, sl], st.astype(bf))
        o_ref[:, sl] = o.astype(o_ref.dtype)
        st_ref[hd] = st * decay[:, sl] + _dot_tn(vh, kt_ref[N_LEVEL, :, sl])


def _rec_kernel(blkf_ref, blkb_ref, sid_ref, first_ref, last_ref,
                qf_ref, if_ref, lff_ref, qb_ref, ib_ref, lfb_ref, s0_ref, mskf_ref, mskb_ref,
                of_ref, ob_ref, sfin_ref,
                st_ref, bf_ref, brf_ref, qtf_ref, ktf_ref, bb_ref, brb_ref, qtb_ref, ktb_ref):
    step = pl.program_id(0)

    @pl.when(step == 0)
    def _():
        for ref in (qtf_ref, ktf_ref, qtb_ref, ktb_ref):
            ref[...] = jnp.zeros_like(ref)

    @pl.when(first_ref[step] == 1)
    def _():
        st_ref[...] = jnp.zeros_like(st_ref)

    @pl.when(first_ref[step] == 2)
    def _():
        for d in range(2):
            for hd in range(HEADS):
                st_ref[d, hd] = s0_ref[d, hd].T

    last_f = _rec_cumsum(lff_ref, bf_ref, brf_ref, False)
    last_b = _rec_cumsum(lfb_ref, bb_ref, brb_ref, True)
    moderate = jnp.min(jnp.minimum(last_f, last_b)) >= -MAX_FAST_LOG2_DECAY

    @pl.when(moderate)
    def _():
        _rec_fast(lff_ref, qf_ref, if_ref, bf_ref, last_f, qtf_ref.at[N_LEVEL], ktf_ref.at[N_LEVEL],
                  st_ref.at[0], of_ref, False)
        _rec_fast(lfb_ref, qb_ref, ib_ref, bb_ref, last_b, qtb_ref.at[N_LEVEL], ktb_ref.at[N_LEVEL],
                  st_ref.at[1], ob_ref, True)

    @pl.when(jnp.logical_not(moderate))
    def _():
        decay_f = _rec_prepare(lff_ref, qf_ref, bf_ref, brf_ref, last_f, qtf_ref, ktf_ref, False)
        decay_b = _rec_prepare(lfb_ref, qb_ref, bb_ref, brb_ref, last_b, qtb_ref, ktb_ref, True)
        _rec_heads(qf_ref, if_ref, qtf_ref, ktf_ref, mskf_ref, st_ref.at[0], of_ref, decay_f)
        _rec_heads(qb_ref, ib_ref, qtb_ref, ktb_ref, mskb_ref, st_ref.at[1], ob_ref, decay_b)

    @pl.when(last_ref[step] == 1)
    def _():
        for d in range(2):
            for hd in range(HEADS):
                sfin_ref[d, hd] = st_ref[d, hd].T


def _rec_tables():
    blkf, blkb, sid, first, last = [], [], [], [], []
    seqs = [(b * (SEQ // CH), SEQ // CH) for b in range(BATCH)]
    seqs += [(N_CTX // CH + b * (DEC_SEQ // CH), DEC_SEQ // CH) for b in range(DEC_BATCH)]
    for n, (start, nc) in enumerate(seqs):
        ctx = n < BATCH
        for c in range(nc):
            blkf.append(start + c)
            blkb.append(start + nc - 1 - c)
            sid.append(n)
            first.append((1 if ctx else 2) if c == 0 else 0)
            last.append(int(ctx and c == nc - 1))
    return [jnp.asarray(np.asarray(a, np.int32)) for a in (blkf, blkb, sid, first, last)]


def _rec(q, i, lf, s0):
    row_f = pl.BlockSpec((CH, D), lambda s, bkf, bkb, sid, fi, la: (bkf[s], 0))
    row_b = pl.BlockSpec((CH, D), lambda s, bkf, bkb, sid, fi, la: (bkb[s], 0))
    st_shape = (None, 2, HEADS, HD, HD)
    scratch = [pltpu.VMEM((2, HEADS, HD, HD), jnp.float32)]
    for _ in range(2):
        scratch += [
            pltpu.VMEM((CH, D), jnp.float32),
            pltpu.VMEM((CH, D), jnp.float32),
            pltpu.VMEM((N_LEVEL + 1, CH, D), jnp.bfloat16),
            pltpu.VMEM((N_LEVEL + 1, CH, D), jnp.bfloat16),
        ]
    msk_spec = pl.BlockSpec((N_LEVEL + 1, CH, CH), lambda s, *_: (0, 0, 0))
    grid_spec = pltpu.PrefetchScalarGridSpec(
        num_scalar_prefetch=5,
        grid=(N_TOK // CH,),
        in_specs=[
            row_f, row_f,
            pl.BlockSpec((None, CH, D), lambda s, bkf, bkb, sid, fi, la: (0, bkf[s], 0)),
            row_b, row_b,
            pl.BlockSpec((None, CH, D), lambda s, bkf, bkb, sid, fi, la: (1, bkb[s], 0)),
            pl.BlockSpec(st_shape, lambda s, bkf, bkb, sid, fi, la: (jnp.maximum(sid[s] - BATCH, 0), 0, 0, 0, 0)),
            msk_spec, msk_spec,
        ],
        out_specs=[
            row_f, row_b,
            pl.BlockSpec(st_shape, lambda s, bkf, bkb, sid, fi, la: (jnp.minimum(sid[s], BATCH - 1), 0, 0, 0, 0)),
        ],
        scratch_shapes=scratch,
    )
    return pl.pallas_call(
        _rec_kernel,
        grid_spec=grid_spec,
        out_shape=[
            jax.ShapeDtypeStruct((N_TOK, D), jnp.bfloat16),
            jax.ShapeDtypeStruct((N_TOK, D), jnp.bfloat16),
            jax.ShapeDtypeStruct((BATCH, 2, HEADS, HD, HD), jnp.float32),
        ],
        compiler_params=_cparams(("arbitrary",)),
        name="hgrn_rec",
    )(*_rec_tables(), q, i, lf, q, i, lf, s0, jnp.asarray(_level_masks(False)), jnp.asarray(_level_masks(True)))


def _hgrn_out_kernel(x_ref, mod_ref, g_ref, of_ref, ob_ref, gt_ref, gn_ref, w_ref, o_ref, oh_ref):
    for hd in range(HEADS):
        sl = slice(hd * HD, (hd + 1) * HD)
        o = of_ref[:, sl].astype(jnp.float32) + ob_ref[:, sl].astype(jnp.float32)
        o = _rms(o, gn_ref[...]) * jax.nn.sigmoid(gt_ref[:, sl].astype(jnp.float32))
        oh_ref[:, sl] = o.astype(jnp.bfloat16)
    y = _dot(oh_ref[...], w_ref[...])
    o_ref[...] = _ada_post(x_ref[...], y, mod_ref, g_ref, 1, 3, 1.0)


def _hgrn_out(x, mod_l, g_l, o_f, o_b, gt, gnorm, w_out):
    row = pl.BlockSpec((TM, D), lambda i: (i, 0))
    return pl.pallas_call(
        _hgrn_out_kernel,
        grid=(N_TOK // TM,),
        in_specs=_tok_specs() + [row, row, row, _const_spec((1, HD)), _const_spec((D, D))],
        out_specs=row,
        out_shape=jax.ShapeDtypeStruct((N_TOK, D), jnp.float32),
        scratch_shapes=[pltpu.VMEM((TM, D), jnp.bfloat16)],
        compiler_params=_cparams(("parallel",)),
        name="hgrn_out",
    )(x, mod_l, g_l, o_f, o_b, gt, gnorm, w_out)


def _rope_tables():
    pos = np.arange(DEC_SEQ)
    row = (pos // GRID_W).astype(np.float64)
    col = (pos % GRID_W).astype(np.float64)
    nf = DA_DK // 4
    inv = ROPE_THETA ** (-np.arange(nf, dtype=np.float64) / nf)
    lane = np.arange(HD)
    axis = (lane % DA_DK) // (2 * nf)
    ang = np.where(axis[None, :] == 0, row[:, None], col[:, None]) * inv[lane % nf][None, :]
    sign = np.where((lane % (2 * nf)) < nf, -1.0, 1.0)
    cos, sin = np.cos(ang), np.sin(ang) * sign[None, :]
    return tuple(jnp.asarray(a, jnp.float32) for a in (cos, sin, cos.T, sin.T))


def kernel(x_prompt, x_sample, cache_k, cache_v, state_hgrn, c, c_ctx, w_mod, b_mod, norm_g,
           ffn_w_in, ffn_w_out, attn_w_in, attn_w_out, attn_lambda, attn_subln,
           hgrn_w_in, hgrn_w_out, hgrn_lower_bounds, hgrn_gnorm):
    x = (x_prompt.reshape(N_CTX, D), x_sample.reshape(N_SMP, D))
    cc = jnp.concatenate([c_ctx[None, :], c, jnp.zeros((N_GRP - 1 - DEC_BATCH, D), jnp.float32)], axis=0)
    mod = _modulation(cc, w_mod, b_mod).reshape(DEPTH, N_GRP, N_MOD, D)
    tables = _rope_tables()

    ks, vs, states = [], [], []
    for l in range(DEPTH):
        mod_l, g_l = mod[l], norm_g[l]
        x = _ffn(x, mod_l, g_l, ffn_w_in, ffn_w_out, l, 0)
        if l % 2 == 0:
            a = l // 2
            lam_init = 0.8 - 0.6 * math.exp(-0.3 * l)
            w_kt = attn_w_in[a][:, D:2 * D].T
            w_out = attn_w_out[a]
            subln = attn_subln[a].reshape(1, HD)
            grp = jnp.asarray(np.arange(D)[:, None] // DA_DK == np.arange(LANES)[None, :], jnp.bfloat16)
            qc, kt_new, vc_new, nqc, nkc = _attn_pre(x, mod_l, g_l, attn_w_in[a], w_kt, grp, tables, ctx=True)
            qs, kt_smp, v_smp, nqs, nks = _attn_pre(x, mod_l, g_l, attn_w_in[a], w_kt, grp, tables, ctx=False)
            kct = jnp.transpose(cache_k[:, a], (0, 2, 3, 4, 1)).reshape(DEC_BATCH, D, PAST)
            vcache = cache_v[:, a].reshape(DEC_BATCH, PAST, D)
            lam_p = attn_lambda[a]

            def score_bound_ok(nq, nk, n_seq, extra_k2=None):
                q2 = jnp.max(nq[:, 0, :N_QK_GROUP].reshape(n_seq, -1, N_QK_GROUP), axis=1)
                k2 = jnp.max(nk[:, :, 0].reshape(n_seq, -1, N_QK_GROUP), axis=1)
                if extra_k2 is not None:
                    k2 = jnp.maximum(k2, extra_k2)
                return jnp.max(q2 * k2) <= MAX_UNSHIFTED_LOG2_SCORE ** 2

            def attn_ctx(shifted):
                return lambda: _attn(x, mod_l, g_l, qc, kt_new, vc_new, None, None, lam_p, subln, w_out,
                                     ctx=True, lam_init=lam_init, shifted=shifted)

            def attn_smp(shifted):
                return lambda: _attn(x, mod_l, g_l, qs, kt_smp, v_smp, kct, vcache, lam_p, subln, w_out,
                                     ctx=False, lam_init=lam_init, shifted=shifted)

            cache_k2 = jnp.max(jnp.sum(jnp.square(cache_k[:, a]), axis=-1), axis=1).reshape(DEC_BATCH, N_QK_GROUP)
            xc = lax.cond(score_bound_ok(nqc, nkc, BATCH), attn_ctx(False), attn_ctx(True))
            xs = lax.cond(score_bound_ok(nqs, nks, DEC_BATCH, cache_k2), attn_smp(False), attn_smp(True))
            x = (xc, xs)
            ks.append(jnp.transpose(kt_new.reshape(BATCH, HEADS, 2, DA_DK, SEQ), (0, 4, 1, 2, 3)))
            vs.append(vc_new.reshape(BATCH, SEQ, HEADS, HD))
        else:
            r = l // 2
            q, i, gt, lf = _hgrn_pre(x, mod_l, g_l, hgrn_w_in[r], hgrn_lower_bounds, l)
            s0 = state_hgrn[:, r].astype(jnp.float32)
            o_f, o_b, s_fin = _rec(q, i, lf, s0)
            x = _hgrn_out(x, mod_l, g_l, o_f, o_b, gt, hgrn_gnorm[r].reshape(1, HD), hgrn_w_out[r])
            states.append(s_fin)
        if l < DEPTH - 1:
            x = _ffn(x, mod_l, g_l, ffn_w_in, ffn_w_out, l, 2)

    assert not isinstance(x, tuple)
    y_prompt = _ffn(x, mod_l, g_l, ffn_w_in, ffn_w_out, DEPTH - 1, 2, base=0, rows=N_CTX)
    y_sample = _ffn(x, mod_l, g_l, ffn_w_in, ffn_w_out, DEPTH - 1, 2, base=N_CTX, rows=N_SMP)
    return (y_prompt.reshape(BATCH, SEQ, D), y_sample.reshape(DEC_BATCH, DEC_SEQ, D),
            jnp.stack(ks, axis=1), jnp.stack(vs, axis=1), jnp.stack(states, axis=1))
```

```python
import functools
import math

import numpy as np
import jax
import jax.numpy as jnp
from jax import lax
from jax.experimental import pallas as pl
from jax.experimental.pallas import tpu as pltpu

D = 1024
BATCH = 16
SEQ = 256
DEPTH = 2
DEC_BATCH = 4
DEC_SEQ = 2048
PAST = 256
GRID_W = 64
N_MOD = 9
EPS = 1e-6
HEADS = 8
HD = 128
DA_DK = 64
ROPE_THETA = 10000.0
FFN_H = 2816
LOG2E = 1.4426950408889634

N_CTX = BATCH * SEQ
N_SMP = DEC_BATCH * DEC_SEQ
N_TOK = N_CTX + N_SMP
N_GRP = 8

VMEM_LIMIT = 56 * 1024 * 1024
LANES = 128
SUB = 8

TM = 512
FFN_HC = 256
TQ = 256
KB = 256
N_QK_GROUP = 2 * HEADS
MAX_UNSHIFTED_LOG2_SCORE = 60.0
CH = 128
N_SLAB = CH // SUB
N_LEVEL = 7
MAX_FAST_LOG2_DECAY = 200.0


def _cparams(sem):
    return pltpu.CompilerParams(dimension_semantics=sem, vmem_limit_bytes=VMEM_LIMIT)


def _grp_of_block(i, rows):
    nctx = N_CTX // rows
    per = DEC_SEQ // rows
    return jnp.where(i < nctx, 0, 1 + (i - nctx) // per)


def _rms(x, g):
    return (x * lax.rsqrt(jnp.mean(x * x, axis=-1, keepdims=True) + EPS)) * g


def _dot(a, b):
    return lax.dot_general(a, b, (((1,), (0,)), ((), ())), preferred_element_type=jnp.float32)


def _dot_nt(a, b):
    return lax.dot_general(a, b, (((1,), (1,)), ((), ())), preferred_element_type=jnp.float32)


def _dot_tn(a, b):
    return lax.dot_general(a, b, (((0,), (0,)), ((), ())), preferred_element_type=jnp.float32)


def _const_spec(shape):
    nd = len(shape)
    return pl.BlockSpec(shape, lambda *_: (0,) * nd, pipeline_mode=pl.Buffered(1))


def _mod_kernel(c_ref, w_ref, b_ref, o_ref):
    c = c_ref[...]
    s = (c * jax.nn.sigmoid(c)).astype(jnp.bfloat16)
    o_ref[...] = _dot(s, w_ref[...].astype(jnp.bfloat16)) + b_ref[...]


def _modulation(cc, w_mod, b_mod):
    tn = 1024
    nt = (N_MOD * D) // tn
    return pl.pallas_call(
        _mod_kernel,
        grid=(DEPTH, nt),
        in_specs=[
            pl.BlockSpec((N_GRP, D), lambda l, j: (0, 0)),
            pl.BlockSpec((None, D, tn), lambda l, j: (l, 0, j)),
            pl.BlockSpec((None, 1, tn), lambda l, j: (l, 0, j)),
        ],
        out_specs=pl.BlockSpec((None, N_GRP, tn), lambda l, j: (l, 0, j)),
        out_shape=jax.ShapeDtypeStruct((DEPTH, N_GRP, N_MOD * D), jnp.float32),
        compiler_params=_cparams(("parallel", "parallel")),
        name="modulation",
    )(cc, w_mod, b_mod.reshape(DEPTH, 1, N_MOD * D))


def _ada_pre(x, mod_ref, g_ref, j, gi):
    shift = mod_ref[3 * j:3 * j + 1, :]
    scale = mod_ref[3 * j + 1:3 * j + 2, :]
    return _rms(x, g_ref[gi:gi + 1, :]) * (1.0 + scale) + shift


def _ada_post(x, y, mod_ref, g_ref, j, gi, res_w):
    gate = mod_ref[3 * j + 2:3 * j + 3, :]
    return x + res_w * gate * _rms(y, g_ref[gi:gi + 1, :])


def _tok_specs():
    return [
        pl.BlockSpec((TM, D), lambda i: (i, 0)),
        pl.BlockSpec((None, N_MOD, D), lambda i: (_grp_of_block(i, TM), 0, 0)),
        _const_spec((6, D)),
    ]


def _ffn_kernel(*refs, j, pair):
    if pair:
        xa_ref, xb_ref, mod_ref, g_ref, win_ref, wout_ref, o_ref, acc_ref = refs
        x = jnp.where(pl.program_id(0) < N_CTX // TM, xa_ref[...], xb_ref[...])
    else:
        x_ref, mod_ref, g_ref, win_ref, wout_ref, o_ref, acc_ref = refs
        x = x_ref[...]
    h = _ada_pre(x, mod_ref, g_ref, j, 2 * j).astype(jnp.bfloat16)
    for c in range(FFN_H // FFN_HC):
        lo = c * FFN_HC
        a = _dot(h, win_ref[:, lo:lo + FFN_HC])
        b = _dot(h, win_ref[:, FFN_H + lo:FFN_H + lo + FFN_HC])
        u = ((a * jax.nn.sigmoid(a)) * b).astype(jnp.bfloat16)
        part = _dot(u, wout_ref[lo:lo + FFN_HC, :])
        if c == 0:
            acc_ref[...] = part
        else:
            acc_ref[...] += part
    o_ref[...] = _ada_post(x, acc_ref[...], mod_ref, g_ref, j, 2 * j + 1, 0.5)


def _ffn(x, mod_l, g_l, w_in, w_out, l, j, *, base=0, rows=N_TOK):
    pair = isinstance(x, tuple)
    nctx = N_CTX // TM
    b0 = base // TM
    w_idx = (l, j // 2, 0, 0)
    if pair:
        assert base == 0 and rows == N_TOK
        x_specs = [pl.BlockSpec((TM, D), lambda i: (jnp.minimum(i, nctx - 1), 0)),
                   pl.BlockSpec((TM, D), lambda i: (jnp.maximum(i - nctx, 0), 0))]
        x_args = list(x)
    else:
        x_specs = [pl.BlockSpec((TM, D), lambda i: (i + b0, 0))]
        x_args = [x]
    return pl.pallas_call(
        functools.partial(_ffn_kernel, j=j, pair=pair),
        grid=(rows // TM,),
        in_specs=x_specs + [
            pl.BlockSpec((None, N_MOD, D), lambda i: (_grp_of_block(i + b0, TM), 0, 0)),
            _const_spec((6, D)),
            pl.BlockSpec((None, None, D, 2 * FFN_H), lambda i: w_idx, pipeline_mode=pl.Buffered(1)),
            pl.BlockSpec((None, None, FFN_H, D), lambda i: w_idx, pipeline_mode=pl.Buffered(1))],
        out_specs=pl.BlockSpec((TM, D), lambda i: (i, 0)),
        out_shape=jax.ShapeDtypeStruct((rows, D), jnp.float32),
        scratch_shapes=[pltpu.VMEM((TM, D), jnp.float32)],
        compiler_params=_cparams(("parallel",)),
        name=f"ffn{j}",
    )(*x_args, mod_l, g_l, w_in, w_out)


def _rope_lanes(x, cos, sin_signed):
    lane = lax.broadcasted_iota(jnp.int32, x.shape, 1)
    partner = jnp.where((lane % 32) < 16, pltpu.roll(x, LANES - 16, 1), pltpu.roll(x, 16, 1))
    return x * cos + partner * sin_signed


def _rope_rows(x, cos_t, sin_signed_t):
    parts = []
    for g in range(HD // 32):
        parts += [x[g * 32 + 16:g * 32 + 32, :], x[g * 32:g * 32 + 16, :]]
    return x * cos_t + jnp.concatenate(parts, axis=0) * sin_signed_t


def _attn_pre_kernel(x_ref, mod_ref, g_ref, w_ref, wkt_ref, grp_ref, cos_ref, sin_ref, cost_ref, sint_ref,
                     q_ref, kt_ref, v_ref, nq_ref, nk_ref, q2_ref, k2_ref, *, rope):
    x = x_ref[...]
    h = _ada_pre(x, mod_ref, g_ref, 1, 2).astype(jnp.bfloat16)
    qscale = DA_DK ** -0.5 * LOG2E
    zq = _dot(h, w_ref[:, 0:D])
    zkt = _dot_nt(wkt_ref[...], h)
    v_ref[...] = _dot(h, w_ref[:, 2 * D:3 * D]).astype(v_ref.dtype)
    for hd in range(HEADS):
        sl = slice(hd * HD, (hd + 1) * HD)
        q = zq[:, sl]
        k = zkt[sl, :]
        if rope:
            q = _rope_lanes(q, cos_ref[...], sin_ref[...])
            k = _rope_rows(k, cost_ref[...], sint_ref[...])
        q = q * qscale
        q_ref[:, sl] = q.astype(q_ref.dtype)
        kt_ref[sl, :] = k.astype(kt_ref.dtype)
        q2_ref[:, sl] = (q * q).astype(jnp.bfloat16)
        k2_ref[sl, :] = (k * k).astype(jnp.bfloat16)
    qn = jnp.max(_dot(q2_ref[...], grp_ref[...]), axis=0, keepdims=True)
    kn = jnp.max(_dot_tn(grp_ref[...], k2_ref[...]), axis=1, keepdims=True)
    nq_ref[...] = jnp.broadcast_to(qn, nq_ref.shape)
    nk_ref[...] = jnp.broadcast_to(kn[:N_QK_GROUP, :], nk_ref.shape)


def _attn_pre(x, mod_l, g_l, w_in, w_kt, grp, tables, *, ctx):
    nb, seq = (BATCH, SEQ) if ctx else (DEC_BATCH, DEC_SEQ)
    t = min(TM, seq)
    rows = nb * seq
    base = 0 if ctx else N_CTX // t
    per = seq // t
    ptab = DEC_SEQ // t
    kv_dtype = jnp.float32 if ctx else jnp.bfloat16
    out_spec = pl.BlockSpec((t, D), lambda i: (i, 0))
    return pl.pallas_call(
        functools.partial(_attn_pre_kernel, rope=not ctx),
        grid=(rows // t,),
        in_specs=[
            pl.BlockSpec((t, D), lambda i: (i + base, 0)),
            pl.BlockSpec((None, N_MOD, D), lambda i: (_grp_of_block(i + base, t), 0, 0)),
            _const_spec((6, D)), _const_spec((D, 3 * D)), _const_spec((D, D)), _const_spec((D, LANES)),
            pl.BlockSpec((t, HD), lambda i: (i % ptab, 0)),
            pl.BlockSpec((t, HD), lambda i: (i % ptab, 0)),
            pl.BlockSpec((HD, t), lambda i: (0, i % ptab)),
            pl.BlockSpec((HD, t), lambda i: (0, i % ptab)),
        ],
        out_specs=[out_spec, pl.BlockSpec((None, D, t), lambda i: (i // per, 0, i % per)), out_spec,
                   pl.BlockSpec((None, SUB, LANES), lambda i: (i, 0, 0)),
                   pl.BlockSpec((None, N_QK_GROUP, LANES), lambda i: (i, 0, 0))],
        out_shape=[
            jax.ShapeDtypeStruct((rows, D), jnp.bfloat16),
            jax.ShapeDtypeStruct((nb, D, seq), kv_dtype),
            jax.ShapeDtypeStruct((rows, D), kv_dtype),
            jax.ShapeDtypeStruct((rows // t, SUB, LANES), jnp.float32),
            jax.ShapeDtypeStruct((rows // t, N_QK_GROUP, LANES), jnp.float32),
        ],
        scratch_shapes=[pltpu.VMEM((t, D), jnp.bfloat16), pltpu.VMEM((D, t), jnp.bfloat16)],
        compiler_params=_cparams(("parallel",)),
        name="attn_pre_ctx" if ctx else "attn_pre_smp",
    )(x, mod_l, g_l, w_in, w_kt, grp, *tables)


def _attn_kernel(*refs, lam_init, cache, shifted):
    refs = list(refs)
    e_ref = refs.pop()
    s_ref = refs.pop() if shifted else None
    if cache:
        (x_ref, mod_ref, g_ref, q_ref, k_ref, v_ref, kc_ref, vc_ref,
         lam_ref, sub_ref, w_ref, o_ref, oh_ref) = refs
    else:
        (x_ref, mod_ref, g_ref, q_ref, k_ref, v_ref,
         lam_ref, sub_ref, w_ref, o_ref, oh_ref) = refs
    lp = lam_ref[...]
    lam = (jnp.exp(jnp.sum(lp[0:1] * lp[1:2], axis=-1, keepdims=True))
           - jnp.exp(jnp.sum(lp[2:3] * lp[3:4], axis=-1, keepdims=True)) + lam_init)
    tq = q_ref.shape[0]
    lane = lax.broadcasted_iota(jnp.int32, (tq, HD), 1)
    bf = jnp.bfloat16
    n_own = k_ref.shape[1] // KB
    n_kb = n_own + (1 if cache else 0)

    def keys_t(sl, kb):
        if kb < n_own:
            return k_ref[sl, kb * KB:(kb + 1) * KB].astype(bf)
        return kc_ref[sl, :].astype(bf)

    def values(sl, kb):
        if kb < n_own:
            return v_ref[kb * KB:(kb + 1) * KB, sl].astype(bf)
        return vc_ref[:, sl].astype(bf)

    def head_slice(hd):
        return slice(hd * HD, (hd + 1) * HD)

    def kcols(kb):
        return slice(kb * KB, (kb + 1) * KB)

    def stacked_q(hd):
        qh = q_ref[:, head_slice(hd)]
        return jnp.concatenate([jnp.where(lane < DA_DK, qh, jnp.zeros_like(qh)),
                                jnp.where(lane >= DA_DK, qh, jnp.zeros_like(qh))], axis=0)

    st = [dict() for _ in range(HEADS)]
    lag = 1 if shifted else 0
    for slot in range(HEADS + lag + 1):
        h1, h2, h3 = slot, slot - lag, slot - lag - 1
        if h1 < HEADS:
            st[h1]["qq"] = stacked_q(h1)
        if 0 <= h3 < HEADS:
            tot = st[h3]["tot"]
            st[h3]["c"] = jnp.broadcast_to(lam * tot[:tq] / tot[tq:], (tq, KB)).astype(bf)
        for kb in range(n_kb):
            if h1 < HEADS:
                s = _dot(st[h1]["qq"], keys_t(head_slice(h1), kb))
                if shifted:
                    s_ref[h1 % 2, :, kcols(kb)] = s
                    mp = jnp.maximum(s[:, :LANES], s[:, LANES:])
                    st[h1]["m"] = mp if kb == 0 else jnp.maximum(st[h1]["m"], mp)
            if 0 <= h2 < HEADS:
                if shifted:
                    e = jnp.exp2(s_ref[h2 % 2, :, kcols(kb)] - st[h2]["mrow"])
                else:
                    e = jnp.exp2(s)
                ep = e[:, :LANES] + e[:, LANES:]
                st[h2]["l"] = ep if kb == 0 else st[h2]["l"] + ep
                e_ref[h2 % 2, :, kcols(kb)] = e.astype(bf)
            if 0 <= h3 < HEADS:
                a = e_ref[h3 % 2, :tq, kcols(kb)] - st[h3]["c"] * e_ref[h3 % 2, tq:, kcols(kb)]
                part = _dot(a, values(head_slice(h3), kb))
                st[h3]["o"] = part if kb == 0 else st[h3]["o"] + part
        if shifted and h1 < HEADS:
            st[h1]["mrow"] = jnp.max(st[h1]["m"], axis=-1, keepdims=True)
        if 0 <= h2 < HEADS:
            st[h2]["tot"] = jnp.sum(st[h2]["l"], axis=-1, keepdims=True)
        if 0 <= h3 < HEADS:
            o = st[h3]["o"] * (1.0 / st[h3]["tot"][:tq])
            o = _rms(o, sub_ref[...]) * (1.0 - lam_init)
            oh_ref[:, head_slice(h3)] = o.astype(bf)
            st[h3].clear()
    y = _dot(oh_ref[...], w_ref[...])
    o_ref[...] = _ada_post(x_ref[...], y, mod_ref, g_ref, 1, 3, 1.0)


def _attn(x, mod_l, g_l, q, k, v, kc, vc, lam_p, subln, w_out, *, ctx, lam_init, shifted):
    nb, seq = (BATCH, SEQ) if ctx else (DEC_BATCH, DEC_SEQ)
    rows = nb * seq
    tq = min(TQ if shifted else 2 * TQ, seq)
    base = 0 if ctx else N_CTX // tq
    nq = seq // tq
    assert seq % KB == 0 and PAST == KB
    n_keys = seq if ctx else seq + PAST
    row_spec = pl.BlockSpec((tq, D), lambda b, i: (b * nq + i, 0))
    in_specs = [
        pl.BlockSpec((tq, D), lambda b, i: (base + b * nq + i, 0)),
        pl.BlockSpec((None, N_MOD, D), lambda b, i: (_grp_of_block(base + b * nq + i, tq), 0, 0)),
        pl.BlockSpec((6, D), lambda b, i: (0, 0)),
        row_spec,
        pl.BlockSpec((None, D, seq), lambda b, i: (b, 0, 0)),
        pl.BlockSpec((seq, D), lambda b, i: (b, 0)),
    ]
    args = [x, mod_l, g_l, q, k, v]
    if not ctx:
        in_specs += [pl.BlockSpec((None, D, PAST), lambda b, i: (b, 0, 0)),
                     pl.BlockSpec((None, PAST, D), lambda b, i: (b, 0, 0))]
        args += [kc, vc]
    in_specs += [
        pl.BlockSpec((4, DA_DK), lambda b, i: (0, 0)),
        pl.BlockSpec((1, HD), lambda b, i: (0, 0)),
        pl.BlockSpec((D, D), lambda b, i: (0, 0), pipeline_mode=pl.Buffered(1)),
    ]
    args += [lam_p, subln, w_out]
    return pl.pallas_call(
        functools.partial(_attn_kernel, lam_init=lam_init, cache=not ctx, shifted=shifted),
        grid=(nb, nq),
        in_specs=in_specs,
        out_specs=row_spec,
        out_shape=jax.ShapeDtypeStruct((rows, D), jnp.float32),
        scratch_shapes=[pltpu.VMEM((tq, D), jnp.bfloat16)]
        + ([pltpu.VMEM((2, 2 * tq, n_keys), jnp.float32)] if shifted else [])
        + [pltpu.VMEM((2, 2 * tq, n_keys), jnp.bfloat16)],
        compiler_params=_cparams(("parallel", "parallel")),
        name=("attn_ctx" if ctx else "attn_smp") + ("_shifted" if shifted else ""),
    )(*args)


def _hgrn_pre_kernel(x_ref, mod_ref, g_ref, w_ref, lb_ref, q_ref, i_ref, gt_ref, lf_ref, *, layer):
    x = x_ref[...]
    h = _ada_pre(x, mod_ref, g_ref, 1, 2).astype(jnp.bfloat16)

    def proj(n):
        return _dot(h, w_ref[:, n * D:(n + 1) * D])

    def log_gate(d, ff):
        raw = [lb_ref[l, d:d + 1, :] for l in range(DEPTH)]
        m = functools.reduce(jnp.maximum, raw)
        ex = [jnp.exp(r - m) for r in raw]
        tot = functools.reduce(lambda a, b: a + b, ex)
        soft = [e / tot for e in ex]
        lb = functools.reduce(lambda a, b: a + b, soft[:layer + 1]) - soft[0]
        return jnp.log(lb + (1.0 - lb) * jax.nn.sigmoid(ff))

    ff = proj(1)
    q_ref[...] = proj(0).astype(q_ref.dtype)
    lf_ref[0] = log_gate(0, ff)
    fb = proj(2)
    i_ref[...] = proj(3).astype(i_ref.dtype)
    lf_ref[1] = log_gate(1, fb)
    gt_ref[...] = proj(4).astype(gt_ref.dtype)


def _hgrn_pre(x, mod_l, g_l, w_in, lb_raw, layer):
    row = pl.BlockSpec((TM, D), lambda i: (i, 0))
    return pl.pallas_call(
        functools.partial(_hgrn_pre_kernel, layer=layer),
        grid=(N_TOK // TM,),
        in_specs=_tok_specs() + [_const_spec((D, 5 * D)), _const_spec((DEPTH, 2, D))],
        out_specs=[row, row, row, pl.BlockSpec((2, TM, D), lambda i: (0, i, 0))],
        out_shape=[
            jax.ShapeDtypeStruct((N_TOK, D), jnp.bfloat16),
            jax.ShapeDtypeStruct((N_TOK, D), jnp.bfloat16),
            jax.ShapeDtypeStruct((N_TOK, D), jnp.bfloat16),
            jax.ShapeDtypeStruct((2, N_TOK, D), jnp.float32),
        ],
        compiler_params=_cparams(("parallel",)),
        name="hgrn_pre",
    )(x, mod_l, g_l, w_in, lb_raw)


def _level_masks(rev):
    t = np.arange(CH)[:, None]
    s = np.arange(CH)[None, :]
    out = []
    for p in range(N_LEVEL):
        same = (t >> (p + 1)) == (s >> (p + 1))
        tb, sb = (t >> p) & 1, (s >> p) & 1
        out.append(same & ((tb == 0) & (sb == 1) if rev else (tb == 1) & (sb == 0)))
    out.append(t == s)
    return np.stack(out).astype(np.float32)


def _row_bcast(x, r):
    return jnp.broadcast_to(x[r:r + 1, :], x.shape)


def _rec_cumsum(lf_ref, b_ref, br_ref, rev):
    sub = lax.broadcasted_iota(jnp.int32, (SUB, D), 0)
    order = range(N_SLAB - 1, -1, -1) if rev else range(N_SLAB)
    edge = 0 if rev else SUB - 1
    carry = None
    for v in order:
        rows = slice(v * SUB, (v + 1) * SUB)
        c = lf_ref[rows, :] * LOG2E
        for k in (1, 2, 4):
            if rev:
                c = c + jnp.where(sub < SUB - k, pltpu.roll(c, SUB - k, 0), 0.0)
            else:
                c = c + jnp.where(sub >= k, pltpu.roll(c, k, 0), 0.0)
        if carry is not None:
            c = c + carry
        carry = _row_bcast(c, edge)
        b_ref[rows, :] = c
        br_ref[rows, :] = carry
    return carry


def _rec_fast(lf_ref, q_ref, i_ref, b_ref, b_last, qs_ref, ks_ref, st_ref, o_ref, rev):
    bf = jnp.bfloat16
    half = 0.5 * b_last
    half2 = jnp.concatenate([half, half], axis=0)
    for v2 in range(N_SLAB // 2):
        rows = slice(v2 * 2 * SUB, (v2 + 1) * 2 * SUB)
        k = 1.0 - jnp.exp2(lf_ref[rows, :] * LOG2E)
        d = b_ref[rows, :] - half2
        qs_ref[rows, :] = (q_ref[rows, :].astype(jnp.float32) * jnp.exp2(d)).astype(bf)
        ks_ref[rows, :] = (k * jnp.exp2(-d)).astype(bf)
    scale = jnp.exp2(half[0:1, :])
    decay = scale * scale
    t = lax.broadcasted_iota(jnp.int32, (CH, CH), 0)
    s = lax.broadcasted_iota(jnp.int32, (CH, CH), 1)
    seen = (s >= t) if rev else (s <= t)
    for hd in range(HEADS):
        sl = slice(hd * HD, (hd + 1) * HD)
        qs, ks, vh = qs_ref[:, sl], ks_ref[:, sl], i_ref[:, sl]
        a = jnp.where(seen, _dot_nt(qs, ks), 0.0)
        st = st_ref[hd]
        o = _dot(a.astype(bf), vh) + _dot_nt(qs, (st * scale[:, sl]).astype(bf))
        o_ref[:, sl] = o.astype(o_ref.dtype)
        st_ref[hd] = st * decay[:, sl] + _dot_tn(vh, ks) * scale[:, sl]


def _rec_prepare(lf_ref, q_ref, b_ref, br_ref, b_last, qt_ref, kt_ref, rev):
    bf = jnp.bfloat16
    sub = lax.broadcasted_iota(jnp.int32, (SUB, D), 0)
    zeros = jnp.zeros((SUB, D), jnp.float32)
    for v2 in range(N_SLAB // 2):
        rows = slice(v2 * 2 * SUB, (v2 + 1) * 2 * SUB)
        f = jnp.exp2(lf_ref[rows, :] * LOG2E)
        k = 1.0 - f
        q = q_ref[rows, :].astype(jnp.float32)
        b = b_ref[rows, :]
        qt_ref[0, rows, :] = (q * f).astype(bf)
        kt_ref[0, rows, :] = k.astype(bf)
        half = (slice(0, SUB), slice(SUB, 2 * SUB))
        for p in range(1, N_LEVEL):
            if p >= 3:
                w = p - 3
                es, qside = [], []
                for n, v in enumerate((2 * v2, 2 * v2 + 1)):
                    hi = v & ~((1 << (w + 1)) - 1)
                    vr = hi | (1 << w) if rev else hi | ((1 << w) - 1)
                    r = br_ref[vr * SUB:(vr + 1) * SUB, :]
                    qside.append(((v >> w) & 1) == (0 if rev else 1))
                    es.append(jnp.exp2(b[half[n], :] - r) if qside[n] else jnp.exp2(r - b[half[n], :]))
                if qside[0] == qside[1]:
                    e = jnp.concatenate(es, axis=0)
                    if qside[0]:
                        qt_ref[p, rows, :] = (q * e).astype(bf)
                    else:
                        kt_ref[p, rows, :] = (k * e).astype(bf)
                else:
                    qe = [q[half[n], :] * es[n] if qside[n] else zeros for n in range(2)]
                    ke = [zeros if qside[n] else k[half[n], :] * es[n] for n in range(2)]
                    qt_ref[p, rows, :] = jnp.concatenate(qe, axis=0).astype(bf)
                    kt_ref[p, rows, :] = jnp.concatenate(ke, axis=0).astype(bf)
            else:
                es = []
                for n in range(2):
                    bv = b[half[n], :]
                    if p == 2:
                        r = _row_bcast(bv, 4 if rev else 3)
                    else:
                        lo, hi_r = (2, 6) if rev else (1, 5)
                        r = jnp.where(sub < 4, _row_bcast(bv, lo), _row_bcast(bv, hi_r))
                    es.append(jnp.exp2(-jnp.abs(bv - r)))
                e = jnp.concatenate(es, axis=0)
                qt_ref[p, rows, :] = (q * e).astype(bf)
                kt_ref[p, rows, :] = (k * e).astype(bf)
        bl = jnp.concatenate([b_last, b_last], axis=0)
        qt_ref[N_LEVEL, rows, :] = (q * jnp.exp2(b)).astype(bf)
        kt_ref[N_LEVEL, rows, :] = (k * jnp.exp2(bl - b)).astype(bf)
    return jnp.exp2(b_last[0:1, :])


def _rec_heads(q_ref, i_ref, qt_ref, kt_ref, msk_ref, st_ref, o_ref, decay):
    bf = jnp.bfloat16
    top = N_LEVEL - 1
    for hd in range(HEADS):
        sl = slice(hd * HD, (hd + 1) * HD)
        a = _dot_nt(qt_ref[top, :, sl], kt_ref[top, :, sl])
        a = a + msk_ref[N_LEVEL] * _dot_nt(q_ref[:, sl], kt_ref[0, :, sl])
        for p in range(top):
            a = a + msk_ref[p] * _dot_nt(qt_ref[p, :, sl], kt_ref[p, :, sl])
        vh = i_ref[:, sl]
        st = st_ref[hd]
        o = _dot(a.astype(bf), vh) + _dot_nt(qt_ref[N_LEVEL, :, sl], st.astype(bf))
        o_ref[:, sl] = o.astype(o_ref.dtype)
        st_ref[hd] = st * decay[:, sl] + _dot_tn(vh, kt_ref[N_LEVEL, :, sl])


def _rec_kernel(blkf_ref, blkb_ref, sid_ref, first_ref, last_ref,
                qf_ref, if_ref, lff_ref, qb_ref, ib_ref, lfb_ref, s0_ref, mskf_ref, mskb_ref,
                of_ref, ob_ref, sfin_ref,
                st_ref, bf_ref, brf_ref, qtf_ref, ktf_ref, bb_ref, brb_ref, qtb_ref, ktb_ref):
    step = pl.program_id(0)

    @pl.when(step == 0)
    def _():
        for ref in (qtf_ref, ktf_ref, qtb_ref, ktb_ref):
            ref[...] = jnp.zeros_like(ref)

    @pl.when(first_ref[step] == 1)
    def _():
        st_ref[...] = jnp.zeros_like(st_ref)

    @pl.when(first_ref[step] == 2)
    def _():
        for d in range(2):
            for hd in range(HEADS):
                st_ref[d, hd] = s0_ref[d, hd].T

    last_f = _rec_cumsum(lff_ref, bf_ref, brf_ref, False)
    last_b = _rec_cumsum(lfb_ref, bb_ref, brb_ref, True)
    moderate = jnp.min(jnp.minimum(last_f, last_b)) >= -MAX_FAST_LOG2_DECAY

    @pl.when(moderate)
    def _():
        _rec_fast(lff_ref, qf_ref, if_ref, bf_ref, last_f, qtf_ref.at[N_LEVEL], ktf_ref.at[N_LEVEL],
                  st_ref.at[0], of_ref, False)
        _rec_fast(lfb_ref, qb_ref, ib_ref, bb_ref, last_b, qtb_ref.at[N_LEVEL], ktb_ref.at[N_LEVEL],
                  st_ref.at[1], ob_ref, True)

    @pl.when(jnp.logical_not(moderate))
    def _():
        decay_f = _rec_prepare(lff_ref, qf_ref, bf_ref, brf_ref, last_f, qtf_ref, ktf_ref, False)
        decay_b = _rec_prepare(lfb_ref, qb_ref, bb_ref, brb_ref, last_b, qtb_ref, ktb_ref, True)
        _rec_heads(qf_ref, if_ref, qtf_ref, ktf_ref, mskf_ref, st_ref.at[0], of_ref, decay_f)
        _rec_heads(qb_ref, ib_ref, qtb_ref, ktb_ref, mskb_ref, st_ref.at[1], ob_ref, decay_b)

    @pl.when(last_ref[step] == 1)
    def _():
        for d in range(2):
            for hd in range(HEADS):
                sfin_ref[d, hd] = st_ref[d, hd].T


def _rec_tables():
    blkf, blkb, sid, first, last = [], [], [], [], []
    seqs = [(b * (SEQ // CH), SEQ // CH) for b in range(BATCH)]
    seqs += [(N_CTX // CH + b * (DEC_SEQ // CH), DEC_SEQ // CH) for b in range(DEC_BATCH)]
    for n, (start, nc) in enumerate(seqs):
        ctx = n < BATCH
        for c in range(nc):
            blkf.append(start + c)
            blkb.append(start + nc - 1 - c)
            sid.append(n)
            first.append((1 if ctx else 2) if c == 0 else 0)
            last.append(int(ctx and c == nc - 1))
    return [jnp.asarray(np.asarray(a, np.int32)) for a in (blkf, blkb, sid, first, last)]


def _rec(q, i, lf, s0):
    row_f = pl.BlockSpec((CH, D), lambda s, bkf, bkb, sid, fi, la: (bkf[s], 0))
    row_b = pl.BlockSpec((CH, D), lambda s, bkf, bkb, sid, fi, la: (bkb[s], 0))
    st_shape = (None, 2, HEADS, HD, HD)
    scratch = [pltpu.VMEM((2, HEADS, HD, HD), jnp.float32)]
    for _ in range(2):
        scratch += [
            pltpu.VMEM((CH, D), jnp.float32),
            pltpu.VMEM((CH, D), jnp.float32),
            pltpu.VMEM((N_LEVEL + 1, CH, D), jnp.bfloat16),
            pltpu.VMEM((N_LEVEL + 1, CH, D), jnp.bfloat16),
        ]
    msk_spec = pl.BlockSpec((N_LEVEL + 1, CH, CH), lambda s, *_: (0, 0, 0))
    grid_spec = pltpu.PrefetchScalarGridSpec(
        num_scalar_prefetch=5,
        grid=(N_TOK // CH,),
        in_specs=[
            row_f, row_f,
            pl.BlockSpec((None, CH, D), lambda s, bkf, bkb, sid, fi, la: (0, bkf[s], 0)),
            row_b, row_b,
            pl.BlockSpec((None, CH, D), lambda s, bkf, bkb, sid, fi, la: (1, bkb[s], 0)),
            pl.BlockSpec(st_shape, lambda s, bkf, bkb, sid, fi, la: (jnp.maximum(sid[s] - BATCH, 0), 0, 0, 0, 0)),
            msk_spec, msk_spec,
        ],
        out_specs=[
            row_f, row_b,
            pl.BlockSpec(st_shape, lambda s, bkf, bkb, sid, fi, la: (jnp.minimum(sid[s], BATCH - 1), 0, 0, 0, 0)),
        ],
        scratch_shapes=scratch,
    )
    return pl.pallas_call(
        _rec_kernel,
        grid_spec=grid_spec,
        out_shape=[
            jax.ShapeDtypeStruct((N_TOK, D), jnp.bfloat16),
            jax.ShapeDtypeStruct((N_TOK, D), jnp.bfloat16),
            jax.ShapeDtypeStruct((BATCH, 2, HEADS, HD, HD), jnp.float32),
        ],
        compiler_params=_cparams(("arbitrary",)),
        name="hgrn_rec",
    )(*_rec_tables(), q, i, lf, q, i, lf, s0, jnp.asarray(_level_masks(False)), jnp.asarray(_level_masks(True)))


def _hgrn_out_kernel(x_ref, mod_ref, g_ref, of_ref, ob_ref, gt_ref, gn_ref, w_ref, o_ref, oh_ref):
    for hd in range(HEADS):
        sl = slice(hd * HD, (hd + 1) * HD)
        o = of_ref[:, sl].astype(jnp.float32) + ob_ref[:, sl].astype(jnp.float32)
        o = _rms(o, gn_ref[...]) * jax.nn.sigmoid(gt_ref[:, sl].astype(jnp.float32))
        oh_ref[:, sl] = o.astype(jnp.bfloat16)
    y = _dot(oh_ref[...], w_ref[...])
    o_ref[...] = _ada_post(x_ref[...], y, mod_ref, g_ref, 1, 3, 1.0)


def _hgrn_out(x, mod_l, g_l, o_f, o_b, gt, gnorm, w_out):
    row = pl.BlockSpec((TM, D), lambda i: (i, 0))
    return pl.pallas_call(
        _hgrn_out_kernel,
        grid=(N_TOK // TM,),
        in_specs=_tok_specs() + [row, row, row, _const_spec((1, HD)), _const_spec((D, D))],
        out_specs=row,
        out_shape=jax.ShapeDtypeStruct((N_TOK, D), jnp.float32),
        scratch_shapes=[pltpu.VMEM((TM, D), jnp.bfloat16)],
        compiler_params=_cparams(("parallel",)),
        name="hgrn_out",
    )(x, mod_l, g_l, o_f, o_b, gt, gnorm, w_out)


def _rope_tables():
    pos = np.arange(DEC_SEQ)
    row = (pos // GRID_W).astype(np.float64)
    col = (pos % GRID_W).astype(np.float64)
    nf = DA_DK // 4
    inv = ROPE_THETA ** (-np.arange(nf, dtype=np.float64) / nf)
    lane = np.arange(HD)
    axis = (lane % DA_DK) // (2 * nf)
    ang = np.where(axis[None, :] == 0, row[:, None], col[:, None]) * inv[lane % nf][None, :]
    sign = np.where((lane % (2 * nf)) < nf, -1.0, 1.0)
    cos, sin = np.cos(ang), np.sin(ang) * sign[None, :]
    return tuple(jnp.asarray(a, jnp.float32) for a in (cos, sin, cos.T, sin.T))


def kernel(x_prompt, x_sample, cache_k, cache_v, state_hgrn, c, c_ctx, w_mod, b_mod, norm_g,
           ffn_w_in, ffn_w_out, attn_w_in, attn_w_out, attn_lambda, attn_subln,
           hgrn_w_in, hgrn_w_out, hgrn_lower_bounds, hgrn_gnorm):
    x = (x_prompt.reshape(N_CTX, D), x_sample.reshape(N_SMP, D))
    cc = jnp.concatenate([c_ctx[None, :], c, jnp.zeros((N_GRP - 1 - DEC_BATCH, D), jnp.float32)], axis=0)
    mod = _modulation(cc, w_mod, b_mod).reshape(DEPTH, N_GRP, N_MOD, D)
    tables = _rope_tables()

    ks, vs, states = [], [], []
    for l in range(DEPTH):
        mod_l, g_l = mod[l], norm_g[l]
        x = _ffn(x, mod_l, g_l, ffn_w_in, ffn_w_out, l, 0)
        if l % 2 == 0:
            a = l // 2
            lam_init = 0.8 - 0.6 * math.exp(-0.3 * l)
            w_kt = attn_w_in[a][:, D:2 * D].T
            w_out = attn_w_out[a]
            subln = attn_subln[a].reshape(1, HD)
            grp = jnp.asarray(np.arange(D)[:, None] // DA_DK == np.arange(LANES)[None, :], jnp.bfloat16)
            qc, kt_new, vc_new, nqc, nkc = _attn_pre(x, mod_l, g_l, attn_w_in[a], w_kt, grp, tables, ctx=True)
            qs, kt_smp, v_smp, nqs, nks = _attn_pre(x, mod_l, g_l, attn_w_in[a], w_kt, grp, tables, ctx=False)
            kct = jnp.transpose(cache_k[:, a], (0, 2, 3, 4, 1)).reshape(DEC_BATCH, D, PAST)
            vcache = cache_v[:, a].reshape(DEC_BATCH, PAST, D)
            lam_p = attn_lambda[a]

            def score_bound_ok(nq, nk, n_seq, extra_k2=None):
                q2 = jnp.max(nq[:, 0, :N_QK_GROUP].reshape(n_seq, -1, N_QK_GROUP), axis=1)
                k2 = jnp.max(nk[:, :, 0].reshape(n_seq, -1, N_QK_GROUP), axis=1)
                if extra_k2 is not None:
                    k2 = jnp.maximum(k2, extra_k2)
                return jnp.max(q2 * k2) <= MAX_UNSHIFTED_LOG2_SCORE ** 2

            def attn_ctx(shifted):
                return lambda: _attn(x, mod_l, g_l, qc, kt_new, vc_new, None, None, lam_p, subln, w_out,
                                     ctx=True, lam_init=lam_init, shifted=shifted)

            def attn_smp(shifted):
                return lambda: _attn(x, mod_l, g_l, qs, kt_smp, v_smp, kct, vcache, lam_p, subln, w_out,
                                     ctx=False, lam_init=lam_init, shifted=shifted)

            cache_k2 = jnp.max(jnp.sum(jnp.square(cache_k[:, a]), axis=-1), axis=1).reshape(DEC_BATCH, N_QK_GROUP)
            xc = lax.cond(score_bound_ok(nqc, nkc, BATCH), attn_ctx(False), attn_ctx(True))
            xs = lax.cond(score_bound_ok(nqs, nks, DEC_BATCH, cache_k2), attn_smp(False), attn_smp(True))
            x = (xc, xs)
            ks.append(jnp.transpose(kt_new.reshape(BATCH, HEADS, 2, DA_DK, SEQ), (0, 4, 1, 2, 3)))
            vs.append(vc_new.reshape(BATCH, SEQ, HEADS, HD))
        else:
            r = l // 2
            q, i, gt, lf = _hgrn_pre(x, mod_l, g_l, hgrn_w_in[r], hgrn_lower_bounds, l)
            s0 = state_hgrn[:, r].astype(jnp.float32)
            o_f, o_b, s_fin = _rec(q, i, lf, s0)
            x = _hgrn_out(x, mod_l, g_l, o_f, o_b, gt, hgrn_gnorm[r].reshape(1, HD), hgrn_w_out[r])
            states.append(s_fin)
        if l < DEPTH - 1:
            x = _ffn(x, mod_l, g_l, ffn_w_in, ffn_w_out, l, 2)

    assert not isinstance(x, tuple)
    y_prompt = _ffn(x, mod_l, g_l, ffn_w_in, ffn_w_out, DEPTH - 1, 2, base=0, rows=N_CTX)
    y_sample = _ffn(x, mod_l, g_l, ffn_w_in, ffn_w_out, DEPTH - 1, 2, base=N_CTX, rows=N_SMP)
    return (y_prompt.reshape(BATCH, SEQ, D), y_sample.reshape(DEC_BATCH, DEC_SEQ, D),
            jnp.stack(ks, axis=1), jnp.stack(vs, axis=1), jnp.stack(states, axis=1))
```

```python
import functools
import math

import numpy as np
import jax
import jax.numpy as jnp
from jax import lax
from jax.experimental import pallas as pl
from jax.experimental.pallas import tpu as pltpu

D = 1024
BATCH = 16
SEQ = 256
DEPTH = 2
DEC_BATCH = 4
DEC_SEQ = 2048
PAST = 256
GRID_W = 64
N_MOD = 9
EPS = 1e-6
HEADS = 8
HD = 128
DA_DK = 64
ROPE_THETA = 10000.0
FFN_H = 2816
LOG2E = 1.4426950408889634

N_CTX = BATCH * SEQ
N_SMP = DEC_BATCH * DEC_SEQ
N_TOK = N_CTX + N_SMP
N_GRP = 8

VMEM_LIMIT = 56 * 1024 * 1024
LANES = 128
SUB = 8

TM = 512
FFN_HC = 256
TQ = 256
KB = 256
N_QK_GROUP = 2 * HEADS
MAX_UNSHIFTED_LOG2_SCORE = 60.0
CH = 128
N_SLAB = CH // SUB
N_LEVEL = 7
MAX_FAST_LOG2_DECAY = 200.0


def _cparams(sem):
    return pltpu.CompilerParams(dimension_semantics=sem, vmem_limit_bytes=VMEM_LIMIT)


def _grp_of_block(i, rows):
    nctx = N_CTX // rows
    per = DEC_SEQ // rows
    return jnp.where(i < nctx, 0, 1 + (i - nctx) // per)


def _rms(x, g):
    return (x * lax.rsqrt(jnp.mean(x * x, axis=-1, keepdims=True) + EPS)) * g


def _dot(a, b):
    return lax.dot_general(a, b, (((1,), (0,)), ((), ())), preferred_element_type=jnp.float32)


def _dot_nt(a, b):
    return lax.dot_general(a, b, (((1,), (1,)), ((), ())), preferred_element_type=jnp.float32)


def _dot_tn(a, b):
    return lax.dot_general(a, b, (((0,), (0,)), ((), ())), preferred_element_type=jnp.float32)


def _const_spec(shape):
    nd = len(shape)
    return pl.BlockSpec(shape, lambda *_: (0,) * nd, pipeline_mode=pl.Buffered(1))


def _mod_kernel(c_ref, w_ref, b_ref, o_ref):
    c = c_ref[...]
    s = (c * jax.nn.sigmoid(c)).astype(jnp.bfloat16)
    o_ref[...] = _dot(s, w_ref[...].astype(jnp.bfloat16)) + b_ref[...]


def _modulation(cc, w_mod, b_mod):
    tn = 1024
    nt = (N_MOD * D) // tn
    return pl.pallas_call(
        _mod_kernel,
        grid=(DEPTH, nt),
        in_specs=[
            pl.BlockSpec((N_GRP, D), lambda l, j: (0, 0)),
            pl.BlockSpec((None, D, tn), lambda l, j: (l, 0, j)),
            pl.BlockSpec((None, 1, tn), lambda l, j: (l, 0, j)),
        ],
        out_specs=pl.BlockSpec((None, N_GRP, tn), lambda l, j: (l, 0, j)),
        out_shape=jax.ShapeDtypeStruct((DEPTH, N_GRP, N_MOD * D), jnp.float32),
        compiler_params=_cparams(("parallel", "parallel")),
        name="modulation",
    )(cc, w_mod, b_mod.reshape(DEPTH, 1, N_MOD * D))


def _ada_pre(x, mod_ref, g_ref, j, gi):
    shift = mod_ref[3 * j:3 * j + 1, :]
    scale = mod_ref[3 * j + 1:3 * j + 2, :]
    return _rms(x, g_ref[gi:gi + 1, :]) * (1.0 + scale) + shift


def _ada_post(x, y, mod_ref, g_ref, j, gi, res_w):
    gate = mod_ref[3 * j + 2:3 * j + 3, :]
    return x + res_w * gate * _rms(y, g_ref[gi:gi + 1, :])


def _tok_specs():
    return [
        pl.BlockSpec((TM, D), lambda i: (i, 0)),
        pl.BlockSpec((None, N_MOD, D), lambda i: (_grp_of_block(i, TM), 0, 0)),
        _const_spec((6, D)),
    ]


def _ffn_kernel(*refs, j, pair):
    if pair:
        xa_ref, xb_ref, mod_ref, g_ref, win_ref, wout_ref, o_ref, acc_ref = refs
        x = jnp.where(pl.program_id(0) < N_CTX // TM, xa_ref[...], xb_ref[...])
    else:
        x_ref, mod_ref, g_ref, win_ref, wout_ref, o_ref, acc_ref = refs
        x = x_ref[...]
    h = _ada_pre(x, mod_ref, g_ref, j, 2 * j).astype(jnp.bfloat16)
    for c in range(FFN_H // FFN_HC):
        lo = c * FFN_HC
        a = _dot(h, win_ref[:, lo:lo + FFN_HC])
        b = _dot(h, win_ref[:, FFN_H + lo:FFN_H + lo + FFN_HC])
        u = ((a * jax.nn.sigmoid(a)) * b).astype(jnp.bfloat16)
        part = _dot(u, wout_ref[lo:lo + FFN_HC, :])
        if c == 0:
            acc_ref[...] = part
        else:
            acc_ref[...] += part
    o_ref[...] = _ada_post(x, acc_ref[...], mod_ref, g_ref, j, 2 * j + 1, 0.5)


def _ffn(x, mod_l, g_l, w_in, w_out, l, j, *, base=0, rows=N_TOK):
    pair = isinstance(x, tuple)
    nctx = N_CTX // TM
    b0 = base // TM
    w_idx = (l, j // 2, 0, 0)
    if pair:
        assert base == 0 and rows == N_TOK
        x_specs = [pl.BlockSpec((TM, D), lambda i: (jnp.minimum(i, nctx - 1), 0)),
                   pl.BlockSpec((TM, D), lambda i: (jnp.maximum(i - nctx, 0), 0))]
        x_args = list(x)
    else:
        x_specs = [pl.BlockSpec((TM, D), lambda i: (i + b0, 0))]
        x_args = [x]
    return pl.pallas_call(
        functools.partial(_ffn_kernel, j=j, pair=pair),
        grid=(rows // TM,),
        in_specs=x_specs + [
            pl.BlockSpec((None, N_MOD, D), lambda i: (_grp_of_block(i + b0, TM), 0, 0)),
            _const_spec((6, D)),
            pl.BlockSpec((None, None, D, 2 * FFN_H), lambda i: w_idx, pipeline_mode=pl.Buffered(1)),
            pl.BlockSpec((None, None, FFN_H, D), lambda i: w_idx, pipeline_mode=pl.Buffered(1))],
        out_specs=pl.BlockSpec((TM, D), lambda i: (i, 0)),
        out_shape=jax.ShapeDtypeStruct((rows, D), jnp.float32),
        scratch_shapes=[pltpu.VMEM((TM, D), jnp.float32)],
        compiler_params=_cparams(("parallel",)),
        name=f"ffn{j}",
    )(*x_args, mod_l, g_l, w_in, w_out)


def _rope_lanes(x, cos, sin_signed):
    lane = lax.broadcasted_iota(jnp.int32, x.shape, 1)
    partner = jnp.where((lane % 32) < 16, pltpu.roll(x, LANES - 16, 1), pltpu.roll(x, 16, 1))
    return x * cos + partner * sin_signed


def _rope_rows(x, cos_t, sin_signed_t):
    parts = []
    for g in range(HD // 32):
        parts += [x[g * 32 + 16:g * 32 + 32, :], x[g * 32:g * 32 + 16, :]]
    return x * cos_t + jnp.concatenate(parts, axis=0) * sin_signed_t


def _attn_pre_kernel(x_ref, mod_ref, g_ref, w_ref, wkt_ref, grp_ref, cos_ref, sin_ref, cost_ref, sint_ref,
                     q_ref, kt_ref, v_ref, nq_ref, nk_ref, q2_ref, k2_ref, *, rope):
    x = x_ref[...]
    h = _ada_pre(x, mod_ref, g_ref, 1, 2).astype(jnp.bfloat16)
    qscale = DA_DK ** -0.5 * LOG2E
    zq = _dot(h, w_ref[:, 0:D])
    zkt = _dot_nt(wkt_ref[...], h)
    v_ref[...] = _dot(h, w_ref[:, 2 * D:3 * D]).astype(v_ref.dtype)
    for hd in range(HEADS):
        sl = slice(hd * HD, (hd + 1) * HD)
        q = zq[:, sl]
        k = zkt[sl, :]
        if rope:
            q = _rope_lanes(q, cos_ref[...], sin_ref[...])
            k = _rope_rows(k, cost_ref[...], sint_ref[...])
        q = q * qscale
        q_ref[:, sl] = q.astype(q_ref.dtype)
        kt_ref[sl, :] = k.astype(kt_ref.dtype)
        q2_ref[:, sl] = (q * q).astype(jnp.bfloat16)
        k2_ref[sl, :] = (k * k).astype(jnp.bfloat16)
    qn = jnp.max(_dot(q2_ref[...], grp_ref[...]), axis=0, keepdims=True)
    kn = jnp.max(_dot_tn(grp_ref[...], k2_ref[...]), axis=1, keepdims=True)
    nq_ref[...] = jnp.broadcast_to(qn, nq_ref.shape)
    nk_ref[...] = jnp.broadcast_to(kn[:N_QK_GROUP, :], nk_ref.shape)


def _attn_pre(x, mod_l, g_l, w_in, w_kt, grp, tables, *, ctx):
    nb, seq = (BATCH, SEQ) if ctx else (DEC_BATCH, DEC_SEQ)
    t = min(TM, seq)
    rows = nb * seq
    base = 0 if ctx else N_CTX // t
    per = seq // t
    ptab = DEC_SEQ // t
    kv_dtype = jnp.float32 if ctx else jnp.bfloat16
    out_spec = pl.BlockSpec((t, D), lambda i: (i, 0))
    return pl.pallas_call(
        functools.partial(_attn_pre_kernel, rope=not ctx),
        grid=(rows // t,),
        in_specs=[
            pl.BlockSpec((t, D), lambda i: (i + base, 0)),
            pl.BlockSpec((None, N_MOD, D), lambda i: (_grp_of_block(i + base, t), 0, 0)),
            _const_spec((6, D)), _const_spec((D, 3 * D)), _const_spec((D, D)), _const_spec((D, LANES)),
            pl.BlockSpec((t, HD), lambda i: (i % ptab, 0)),
            pl.BlockSpec((t, HD), lambda i: (i % ptab, 0)),
            pl.BlockSpec((HD, t), lambda i: (0, i % ptab)),
            pl.BlockSpec((HD, t), lambda i: (0, i % ptab)),
        ],
        out_specs=[out_spec, pl.BlockSpec((None, D, t), lambda i: (i // per, 0, i % per)), out_spec,
                   pl.BlockSpec((None, SUB, LANES), lambda i: (i, 0, 0)),
                   pl.BlockSpec((None, N_QK_GROUP, LANES), lambda i: (i, 0, 0))],
        out_shape=[
            jax.ShapeDtypeStruct((rows, D), jnp.bfloat16),
            jax.ShapeDtypeStruct((nb, D, seq), kv_dtype),
            jax.ShapeDtypeStruct((rows, D), kv_dtype),
            jax.ShapeDtypeStruct((rows // t, SUB, LANES), jnp.float32),
            jax.ShapeDtypeStruct((rows // t, N_QK_GROUP, LANES), jnp.float32),
        ],
        scratch_shapes=[pltpu.VMEM((t, D), jnp.bfloat16), pltpu.VMEM((D, t), jnp.bfloat16)],
        compiler_params=_cparams(("parallel",)),
        name="attn_pre_ctx" if ctx else "attn_pre_smp",
    )(x, mod_l, g_l, w_in, w_kt, grp, *tables)


def _attn_kernel(*refs, lam_init, cache, shifted):
    refs = list(refs)
    e_ref = refs.pop()
    s_ref = refs.pop() if shifted else None
    if cache:
        (x_ref, mod_ref, g_ref, q_ref, k_ref, v_ref, kc_ref, vc_ref,
         lam_ref, sub_ref, w_ref, o_ref, oh_ref) = refs
    else:
        (x_ref, mod_ref, g_ref, q_ref, k_ref, v_ref,
         lam_ref, sub_ref, w_ref, o_ref, oh_ref) = refs
    lp = lam_ref[...]
    lam = (jnp.exp(jnp.sum(lp[0:1] * lp[1:2], axis=-1, keepdims=True))
           - jnp.exp(jnp.sum(lp[2:3] * lp[3:4], axis=-1, keepdims=True)) + lam_init)
    tq = q_ref.shape[0]
    lane = lax.broadcasted_iota(jnp.int32, (tq, HD), 1)
    bf = jnp.bfloat16
    kbw = KB if shifted else min(2 * KB, k_ref.shape[1])
    n_own = k_ref.shape[1] // kbw
    n_kb = n_own + (1 if cache else 0)

    def kcols(kb):
        if kb < n_own:
            return slice(kb * kbw, (kb + 1) * kbw)
        return slice(n_own * kbw, n_own * kbw + PAST)

    def keys_t(sl, kb):
        if kb < n_own:
            return k_ref[sl, kcols(kb)].astype(bf)
        return kc_ref[sl, :].astype(bf)

    def values(sl, kb):
        if kb < n_own:
            return v_ref[kcols(kb), sl].astype(bf)
        return vc_ref[:, sl].astype(bf)

    def head_slice(hd):
        return slice(hd * HD, (hd + 1) * HD)

    def lane_tiles(x, op):
        parts = [x[:, j * LANES:(j + 1) * LANES] for j in range(x.shape[1] // LANES)]
        return functools.reduce(op, parts)

    def stacked_q(hd):
        qh = q_ref[:, head_slice(hd)]
        return jnp.concatenate([jnp.where(lane < DA_DK, qh, jnp.zeros_like(qh)),
                                jnp.where(lane >= DA_DK, qh, jnp.zeros_like(qh))], axis=0)

    st = [dict() for _ in range(HEADS)]
    lag = 1 if shifted else 0
    for slot in range(HEADS + lag + 1):
        h1, h2, h3 = slot, slot - lag, slot - lag - 1
        if h1 < HEADS:
            st[h1]["qq"] = stacked_q(h1)
        if 0 <= h3 < HEADS:
            tot = st[h3]["tot"]
            c_col = lam * tot[:tq] / tot[tq:]
            st[h3]["c"] = {w: jnp.broadcast_to(c_col, (tq, w)).astype(bf)
                           for w in {kbw, PAST if cache else kbw}}
        for kb in range(n_kb):
            if h1 < HEADS:
                s = _dot(st[h1]["qq"], keys_t(head_slice(h1), kb))
                if shifted:
                    s_ref[h1 % 2, :, kcols(kb)] = s
                    mp = lane_tiles(s, jnp.maximum)
                    st[h1]["m"] = mp if kb == 0 else jnp.maximum(st[h1]["m"], mp)
            if 0 <= h2 < HEADS:
                if shifted:
                    e = jnp.exp2(s_ref[h2 % 2, :, kcols(kb)] - st[h2]["mrow"])
                else:
                    e = jnp.exp2(s)
                ep = lane_tiles(e, lambda a, b: a + b)
                st[h2]["l"] = ep if kb == 0 else st[h2]["l"] + ep
                e_ref[h2 % 2, :, kcols(kb)] = e.astype(bf)
            if 0 <= h3 < HEADS:
                cols = kcols(kb)
                c = st[h3]["c"][cols.stop - cols.start]
                a = e_ref[h3 % 2, :tq, cols] - c * e_ref[h3 % 2, tq:, cols]
                part = _dot(a, values(head_slice(h3), kb))
                st[h3]["o"] = part if kb == 0 else st[h3]["o"] + part
        if shifted and h1 < HEADS:
            st[h1]["mrow"] = jnp.max(st[h1]["m"], axis=-1, keepdims=True)
        if 0 <= h2 < HEADS:
            st[h2]["tot"] = jnp.sum(st[h2]["l"], axis=-1, keepdims=True)
        if 0 <= h3 < HEADS:
            o = st[h3]["o"] * (1.0 / st[h3]["tot"][:tq])
            o = _rms(o, sub_ref[...]) * (1.0 - lam_init)
            oh_ref[:, head_slice(h3)] = o.astype(bf)
            st[h3].clear()
    y = _dot(oh_ref[...], w_ref[...])
    o_ref[...] = _ada_post(x_ref[...], y, mod_ref, g_ref, 1, 3, 1.0)


def _attn(x, mod_l, g_l, q, k, v, kc, vc, lam_p, subln, w_out, *, ctx, lam_init, shifted):
    nb, seq = (BATCH, SEQ) if ctx else (DEC_BATCH, DEC_SEQ)
    rows = nb * seq
    tq = min(TQ if shifted else 2 * TQ, seq)
    base = 0 if ctx else N_CTX // tq
    nq = seq // tq
    assert seq % KB == 0 and PAST == KB
    n_keys = seq if ctx else seq + PAST
    row_spec = pl.BlockSpec((tq, D), lambda b, i: (b * nq + i, 0))
    in_specs = [
        pl.BlockSpec((tq, D), lambda b, i: (base + b * nq + i, 0)),
        pl.BlockSpec((None, N_MOD, D), lambda b, i: (_grp_of_block(base + b * nq + i, tq), 0, 0)),
        pl.BlockSpec((6, D), lambda b, i: (0, 0)),
        row_spec,
        pl.BlockSpec((None, D, seq), lambda b, i: (b, 0, 0)),
        pl.BlockSpec((seq, D), lambda b, i: (b, 0)),
    ]
    args = [x, mod_l, g_l, q, k, v]
    if not ctx:
        in_specs += [pl.BlockSpec((None, D, PAST), lambda b, i: (b, 0, 0)),
                     pl.BlockSpec((None, PAST, D), lambda b, i: (b, 0, 0))]
        args += [kc, vc]
    in_specs += [
        pl.BlockSpec((4, DA_DK), lambda b, i: (0, 0)),
        pl.BlockSpec((1, HD), lambda b, i: (0, 0)),
        pl.BlockSpec((D, D), lambda b, i: (0, 0), pipeline_mode=pl.Buffered(1)),
    ]
    args += [lam_p, subln, w_out]
    return pl.pallas_call(
        functools.partial(_attn_kernel, lam_init=lam_init, cache=not ctx, shifted=shifted),
        grid=(nb, nq),
        in_specs=in_specs,
        out_specs=row_spec,
        out_shape=jax.ShapeDtypeStruct((rows, D), jnp.float32),
        scratch_shapes=[pltpu.VMEM((tq, D), jnp.bfloat16)]
        + ([pltpu.VMEM((2, 2 * tq, n_keys), jnp.float32)] if shifted else [])
        + [pltpu.VMEM((2, 2 * tq, n_keys), jnp.bfloat16)],
        compiler_params=_cparams(("parallel", "parallel")),
        name=("attn_ctx" if ctx else "attn_smp") + ("_shifted" if shifted else ""),
    )(*args)


def _hgrn_pre_kernel(x_ref, mod_ref, g_ref, w_ref, lb_ref, q_ref, i_ref, gt_ref, lf_ref, *, layer):
    x = x_ref[...]
    h = _ada_pre(x, mod_ref, g_ref, 1, 2).astype(jnp.bfloat16)

    def proj(n):
        return _dot(h, w_ref[:, n * D:(n + 1) * D])

    def log_gate(d, ff):
        raw = [lb_ref[l, d:d + 1, :] for l in range(DEPTH)]
        m = functools.reduce(jnp.maximum, raw)
        ex = [jnp.exp(r - m) for r in raw]
        tot = functools.reduce(lambda a, b: a + b, ex)
        soft = [e / tot for e in ex]
        lb = functools.reduce(lambda a, b: a + b, soft[:layer + 1]) - soft[0]
        return jnp.log(lb + (1.0 - lb) * jax.nn.sigmoid(ff))

    ff = proj(1)
    q_ref[...] = proj(0).astype(q_ref.dtype)
    lf_ref[0] = log_gate(0, ff)
    fb = proj(2)
    i_ref[...] = proj(3).astype(i_ref.dtype)
    lf_ref[1] = log_gate(1, fb)
    gt_ref[...] = proj(4).astype(gt_ref.dtype)


def _hgrn_pre(x, mod_l, g_l, w_in, lb_raw, layer):
    row = pl.BlockSpec((TM, D), lambda i: (i, 0))
    return pl.pallas_call(
        functools.partial(_hgrn_pre_kernel, layer=layer),
        grid=(N_TOK // TM,),
        in_specs=_tok_specs() + [_const_spec((D, 5 * D)), _const_spec((DEPTH, 2, D))],
        out_specs=[row, row, row, pl.BlockSpec((2, TM, D), lambda i: (0, i, 0))],
        out_shape=[
            jax.ShapeDtypeStruct((N_TOK, D), jnp.bfloat16),
            jax.ShapeDtypeStruct((N_TOK, D), jnp.bfloat16),
            jax.ShapeDtypeStruct((N_TOK, D), jnp.bfloat16),
            jax.ShapeDtypeStruct((2, N_TOK, D), jnp.float32),
        ],
        compiler_params=_cparams(("parallel",)),
        name="hgrn_pre",
    )(x, mod_l, g_l, w_in, lb_raw)


def _level_masks(rev):
    t = np.arange(CH)[:, None]
    s = np.arange(CH)[None, :]
    out = []
    for p in range(N_LEVEL):
        same = (t >> (p + 1)) == (s >> (p + 1))
        tb, sb = (t >> p) & 1, (s >> p) & 1
        out.append(same & ((tb == 0) & (sb == 1) if rev else (tb == 1) & (sb == 0)))
    out.append(t == s)
    return np.stack(out).astype(np.float32)


def _row_bcast(x, r):
    return jnp.broadcast_to(x[r:r + 1, :], x.shape)


def _rec_cumsum(lf_ref, b_ref, br_ref, rev):
    sub = lax.broadcasted_iota(jnp.int32, (SUB, D), 0)
    order = range(N_SLAB - 1, -1, -1) if rev else range(N_SLAB)
    edge = 0 if rev else SUB - 1
    carry = None
    for v in order:
        rows = slice(v * SUB, (v + 1) * SUB)
        c = lf_ref[rows, :] * LOG2E
        for k in (1, 2, 4):
            if rev:
                c = c + jnp.where(sub < SUB - k, pltpu.roll(c, SUB - k, 0), 0.0)
            else:
                c = c + jnp.where(sub >= k, pltpu.roll(c, k, 0), 0.0)
        if carry is not None:
            c = c + carry
        carry = _row_bcast(c, edge)
        b_ref[rows, :] = c
        br_ref[rows, :] = carry
    return carry


def _rec_fast(lf_ref, q_ref, i_ref, b_ref, b_last, qs_ref, ks_ref, st_ref, o_ref, rev):
    bf = jnp.bfloat16
    half = 0.5 * b_last
    half2 = jnp.concatenate([half, half], axis=0)
    for v2 in range(N_SLAB // 2):
        rows = slice(v2 * 2 * SUB, (v2 + 1) * 2 * SUB)
        k = 1.0 - jnp.exp2(lf_ref[rows, :] * LOG2E)
        d = b_ref[rows, :] - half2
        qs_ref[rows, :] = (q_ref[rows, :].astype(jnp.float32) * jnp.exp2(d)).astype(bf)
        ks_ref[rows, :] = (k * jnp.exp2(-d)).astype(bf)
    scale = jnp.exp2(half[0:1, :])
    decay = scale * scale
    t = lax.broadcasted_iota(jnp.int32, (CH, CH), 0)
    s = lax.broadcasted_iota(jnp.int32, (CH, CH), 1)
    seen = (s >= t) if rev else (s <= t)
    for hd in range(HEADS):
        sl = slice(hd * HD, (hd + 1) * HD)
        qs, ks, vh = qs_ref[:, sl], ks_ref[:, sl], i_ref[:, sl]
        a = jnp.where(seen, _dot_nt(qs, ks), 0.0)
        st = st_ref[hd]
        o = _dot(a.astype(bf), vh) + _dot_nt(qs, (st * scale[:, sl]).astype(bf))
        o_ref[:, sl] = o.astype(o_ref.dtype)
        st_ref[hd] = st * decay[:, sl] + _dot_tn(vh, ks) * scale[:, sl]


def _rec_prepare(lf_ref, q_ref, b_ref, br_ref, b_last, qt_ref, kt_ref, rev):
    bf = jnp.bfloat16
    sub = lax.broadcasted_iota(jnp.int32, (SUB, D), 0)
    zeros = jnp.zeros((SUB, D), jnp.float32)
    for v2 in range(N_SLAB // 2):
        rows = slice(v2 * 2 * SUB, (v2 + 1) * 2 * SUB)
        f = jnp.exp2(lf_ref[rows, :] * LOG2E)
        k = 1.0 - f
        q = q_ref[rows, :].astype(jnp.float32)
        b = b_ref[rows, :]
        qt_ref[0, rows, :] = (q * f).astype(bf)
        kt_ref[0, rows, :] = k.astype(bf)
        half = (slice(0, SUB), slice(SUB, 2 * SUB))
        for p in range(1, N_LEVEL):
            if p >= 3:
                w = p - 3
                es, qside = [], []
                for n, v in enumerate((2 * v2, 2 * v2 + 1)):
                    hi = v & ~((1 << (w + 1)) - 1)
                    vr = hi | (1 << w) if rev else hi | ((1 << w) - 1)
                    r = br_ref[vr * SUB:(vr + 1) * SUB, :]
                    qside.append(((v >> w) & 1) == (0 if rev else 1))
                    es.append(jnp.exp2(b[half[n], :] - r) if qside[n] else jnp.exp2(r - b[half[n], :]))
                if qside[0] == qside[1]:
                    e = jnp.concatenate(es, axis=0)
                    if qside[0]:
                        qt_ref[p, rows, :] = (q * e).astype(bf)
                    else:
                        kt_ref[p, rows, :] = (k * e).astype(bf)
                else:
                    qe = [q[half[n], :] * es[n] if qside[n] else zeros for n in range(2)]
                    ke = [zeros if qside[n] else k[half[n], :] * es[n] for n in range(2)]
                    qt_ref[p, rows, :] = jnp.concatenate(qe, axis=0).astype(bf)
                    kt_ref[p, rows, :] = jnp.concatenate(ke, axis=0).astype(bf)
            else:
                es = []
                for n in range(2):
                    bv = b[half[n], :]
                    if p == 2:
                        r = _row_bcast(bv, 4 if rev else 3)
                    else:
                        lo, hi_r = (2, 6) if rev else (1, 5)
                        r = jnp.where(sub < 4, _row_bcast(bv, lo), _row_bcast(bv, hi_r))
                    es.append(jnp.exp2(-jnp.abs(bv - r)))
                e = jnp.concatenate(es, axis=0)
                qt_ref[p, rows, :] = (q * e).astype(bf)
                kt_ref[p, rows, :] = (k * e).astype(bf)
        bl = jnp.concatenate([b_last, b_last], axis=0)
        qt_ref[N_LEVEL, rows, :] = (q * jnp.exp2(b)).astype(bf)
        kt_ref[N_LEVEL, rows, :] = (k * jnp.exp2(bl - b)).astype(bf)
    return jnp.exp2(b_last[0:1, :])


def _rec_heads(q_ref, i_ref, qt_ref, kt_ref, msk_ref, st_ref, o_ref, decay):
    bf = jnp.bfloat16
    top = N_LEVEL - 1
    for hd in range(HEADS):
        sl = slice(hd * HD, (hd + 1) * HD)
        a = _dot_nt(qt_ref[top, :, sl], kt_ref[top, :, sl])
        a = a + msk_ref[N_LEVEL] * _dot_nt(q_ref[:, sl], kt_ref[0, :, sl])
        for p in range(top):
            a = a + msk_ref[p] * _dot_nt(qt_ref[p, :, sl], kt_ref[p, :, sl])
        vh = i_ref[:, sl]
        st = st_ref[hd]
        o = _dot(a.astype(bf), vh) + _dot_nt(qt_ref[N_LEVEL, :, sl], st.astype(bf))
        o_ref[:, sl] = o.astype(o_ref.dtype)
        st_ref[hd] = st * decay[:, sl] + _dot_tn(vh, kt_ref[N_LEVEL, :, sl])


def _rec_kernel(blkf_ref, blkb_ref, sid_ref, first_ref, last_ref,
                qf_ref, if_ref, lff_ref, qb_ref, ib_ref, lfb_ref, s0_ref, mskf_ref, mskb_ref,
                of_ref, ob_ref, sfin_ref,
                st_ref, bf_ref, brf_ref, qtf_ref, ktf_ref, bb_ref, brb_ref, qtb_ref, ktb_ref):
    step = pl.program_id(0)

    @pl.when(step == 0)
    def _():
        for ref in (qtf_ref, ktf_ref, qtb_ref, ktb_ref):
            ref[...] = jnp.zeros_like(ref)

    @pl.when(first_ref[step] == 1)
    def _():
        st_ref[...] = jnp.zeros_like(st_ref)

    @pl.when(first_ref[step] == 2)
    def _():
        for d in range(2):
            for hd in range(HEADS):
                st_ref[d, hd] = s0_ref[d, hd].T

    last_f = _rec_cumsum(lff_ref, bf_ref, brf_ref, False)
    last_b = _rec_cumsum(lfb_ref, bb_ref, brb_ref, True)
    moderate = jnp.min(jnp.minimum(last_f, last_b)) >= -MAX_FAST_LOG2_DECAY

    @pl.when(moderate)
    def _():
        _rec_fast(lff_ref, qf_ref, if_ref, bf_ref, last_f, qtf_ref.at[N_LEVEL], ktf_ref.at[N_LEVEL],
                  st_ref.at[0], of_ref, False)
        _rec_fast(lfb_ref, qb_ref, ib_ref, bb_ref, last_b, qtb_ref.at[N_LEVEL], ktb_ref.at[N_LEVEL],
                  st_ref.at[1], ob_ref, True)

    @pl.when(jnp.logical_not(moderate))
    def _():
        decay_f = _rec_prepare(lff_ref, qf_ref, bf_ref, brf_ref, last_f, qtf_ref, ktf_ref, False)
        decay_b = _rec_prepare(lfb_ref, qb_ref, bb_ref, brb_ref, last_b, qtb_ref, ktb_ref, True)
        _rec_heads(qf_ref, if_ref, qtf_ref, ktf_ref, mskf_ref, st_ref.at[0], of_ref, decay_f)
        _rec_heads(qb_ref, ib_ref, qtb_ref, ktb_ref, mskb_ref, st_ref.at[1], ob_ref, decay_b)

    @pl.when(last_ref[step] == 1)
    def _():
        for d in range(2):
            for hd in range(HEADS):
                sfin_ref[d, hd] = st_ref[d, hd].T


def _rec_tables():
    blkf, blkb, sid, first, last = [], [], [], [], []
    seqs = [(b * (SEQ // CH), SEQ // CH) for b in range(BATCH)]
    seqs += [(N_CTX // CH + b * (DEC_SEQ // CH), DEC_SEQ // CH) for b in range(DEC_BATCH)]
    for n, (start, nc) in enumerate(seqs):
        ctx = n < BATCH
        for c in range(nc):
            blkf.append(start + c)
            blkb.append(start + nc - 1 - c)
            sid.append(n)
            first.append((1 if ctx else 2) if c == 0 else 0)
            last.append(int(ctx and c == nc - 1))
    return [jnp.asarray(np.asarray(a, np.int32)) for a in (blkf, blkb, sid, first, last)]


def _rec(q, i, lf, s0):
    row_f = pl.BlockSpec((CH, D), lambda s, bkf, bkb, sid, fi, la: (bkf[s], 0))
    row_b = pl.BlockSpec((CH, D), lambda s, bkf, bkb, sid, fi, la: (bkb[s], 0))
    st_shape = (None, 2, HEADS, HD, HD)
    scratch = [pltpu.VMEM((2, HEADS, HD, HD), jnp.float32)]
    for _ in range(2):
        scratch += [
            pltpu.VMEM((CH, D), jnp.float32),
            pltpu.VMEM((CH, D), jnp.float32),
            pltpu.VMEM((N_LEVEL + 1, CH, D), jnp.bfloat16),
            pltpu.VMEM((N_LEVEL + 1, CH, D), jnp.bfloat16),
        ]
    msk_spec = pl.BlockSpec((N_LEVEL + 1, CH, CH), lambda s, *_: (0, 0, 0))
    grid_spec = pltpu.PrefetchScalarGridSpec(
        num_scalar_prefetch=5,
        grid=(N_TOK // CH,),
        in_specs=[
            row_f, row_f,
            pl.BlockSpec((None, CH, D), lambda s, bkf, bkb, sid, fi, la: (0, bkf[s], 0)),
            row_b, row_b,
            pl.BlockSpec((None, CH, D), lambda s, bkf, bkb, sid, fi, la: (1, bkb[s], 0)),
            pl.BlockSpec(st_shape, lambda s, bkf, bkb, sid, fi, la: (jnp.maximum(sid[s] - BATCH, 0), 0, 0, 0, 0)),
            msk_spec, msk_spec,
        ],
        out_specs=[
            row_f, row_b,
            pl.BlockSpec(st_shape, lambda s, bkf, bkb, sid, fi, la: (jnp.minimum(sid[s], BATCH - 1), 0, 0, 0, 0)),
        ],
        scratch_shapes=scratch,
    )
    return pl.pallas_call(
        _rec_kernel,
        grid_spec=grid_spec,
        out_shape=[
            jax.ShapeDtypeStruct((N_TOK, D), jnp.bfloat16),
            jax.ShapeDtypeStruct((N_TOK, D), jnp.bfloat16),
            jax.ShapeDtypeStruct((BATCH, 2, HEADS, HD, HD), jnp.float32),
        ],
        compiler_params=_cparams(("arbitrary",)),
        name="hgrn_rec",
    )(*_rec_tables(), q, i, lf, q, i, lf, s0, jnp.asarray(_level_masks(False)), jnp.asarray(_level_masks(True)))


def _hgrn_out_kernel(x_ref, mod_ref, g_ref, of_ref, ob_ref, gt_ref, gn_ref, w_ref, o_ref, oh_ref):
    for hd in range(HEADS):
        sl = slice(hd * HD, (hd + 1) * HD)
        o = of_ref[:, sl].astype(jnp.float32) + ob_ref[:, sl].astype(jnp.float32)
        o = _rms(o, gn_ref[...]) * jax.nn.sigmoid(gt_ref[:, sl].astype(jnp.float32))
        oh_ref[:, sl] = o.astype(jnp.bfloat16)
    y = _dot(oh_ref[...], w_ref[...])
    o_ref[...] = _ada_post(x_ref[...], y, mod_ref, g_ref, 1, 3, 1.0)


def _hgrn_out(x, mod_l, g_l, o_f, o_b, gt, gnorm, w_out):
    row = pl.BlockSpec((TM, D), lambda i: (i, 0))
    return pl.pallas_call(
        _hgrn_out_kernel,
        grid=(N_TOK // TM,),
        in_specs=_tok_specs() + [row, row, row, _const_spec((1, HD)), _const_spec((D, D))],
        out_specs=row,
        out_shape=jax.ShapeDtypeStruct((N_TOK, D), jnp.float32),
        scratch_shapes=[pltpu.VMEM((TM, D), jnp.bfloat16)],
        compiler_params=_cparams(("parallel",)),
        name="hgrn_out",
    )(x, mod_l, g_l, o_f, o_b, gt, gnorm, w_out)


def _rope_tables():
    pos = np.arange(DEC_SEQ)
    row = (pos // GRID_W).astype(np.float64)
    col = (pos % GRID_W).astype(np.float64)
    nf = DA_DK // 4
    inv = ROPE_THETA ** (-np.arange(nf, dtype=np.float64) / nf)
    lane = np.arange(HD)
    axis = (lane % DA_DK) // (2 * nf)
    ang = np.where(axis[None, :] == 0, row[:, None], col[:, None]) * inv[lane % nf][None, :]
    sign = np.where((lane % (2 * nf)) < nf, -1.0, 1.0)
    cos, sin = np.cos(ang), np.sin(ang) * sign[None, :]
    return tuple(jnp.asarray(a, jnp.float32) for a in (cos, sin, cos.T, sin.T))


def kernel(x_prompt, x_sample, cache_k, cache_v, state_hgrn, c, c_ctx, w_mod, b_mod, norm_g,
           ffn_w_in, ffn_w_out, attn_w_in, attn_w_out, attn_lambda, attn_subln,
           hgrn_w_in, hgrn_w_out, hgrn_lower_bounds, hgrn_gnorm):
    x = (x_prompt.reshape(N_CTX, D), x_sample.reshape(N_SMP, D))
    cc = jnp.concatenate([c_ctx[None, :], c, jnp.zeros((N_GRP - 1 - DEC_BATCH, D), jnp.float32)], axis=0)
    mod = _modulation(cc, w_mod, b_mod).reshape(DEPTH, N_GRP, N_MOD, D)
    tables = _rope_tables()

    ks, vs, states = [], [], []
    for l in range(DEPTH):
        mod_l, g_l = mod[l], norm_g[l]
        x = _ffn(x, mod_l, g_l, ffn_w_in, ffn_w_out, l, 0)
        if l % 2 == 0:
            a = l // 2
            lam_init = 0.8 - 0.6 * math.exp(-0.3 * l)
            w_kt = attn_w_in[a][:, D:2 * D].T
            w_out = attn_w_out[a]
            subln = attn_subln[a].reshape(1, HD)
            grp = jnp.asarray(np.arange(D)[:, None] // DA_DK == np.arange(LANES)[None, :], jnp.bfloat16)
            qc, kt_new, vc_new, nqc, nkc = _attn_pre(x, mod_l, g_l, attn_w_in[a], w_kt, grp, tables, ctx=True)
            qs, kt_smp, v_smp, nqs, nks = _attn_pre(x, mod_l, g_l, attn_w_in[a], w_kt, grp, tables, ctx=False)
            kct = jnp.transpose(cache_k[:, a], (0, 2, 3, 4, 1)).reshape(DEC_BATCH, D, PAST)
            vcache = cache_v[:, a].reshape(DEC_BATCH, PAST, D)
            lam_p = attn_lambda[a]

            def score_bound_ok(nq, nk, n_seq, extra_k2=None):
                q2 = jnp.max(nq[:, 0, :N_QK_GROUP].reshape(n_seq, -1, N_QK_GROUP), axis=1)
                k2 = jnp.max(nk[:, :, 0].reshape(n_seq, -1, N_QK_GROUP), axis=1)
                if extra_k2 is not None:
                    k2 = jnp.maximum(k2, extra_k2)
                return jnp.max(q2 * k2) <= MAX_UNSHIFTED_LOG2_SCORE ** 2

            def attn_ctx(shifted):
                return lambda: _attn(x, mod_l, g_l, qc, kt_new, vc_new, None, None, lam_p, subln, w_out,
                                     ctx=True, lam_init=lam_init, shifted=shifted)

            def attn_smp(shifted):
                return lambda: _attn(x, mod_l, g_l, qs, kt_smp, v_smp, kct, vcache, lam_p, subln, w_out,
                                     ctx=False, lam_init=lam_init, shifted=shifted)

            cache_k2 = jnp.max(jnp.sum(jnp.square(cache_k[:, a]), axis=-1), axis=1).reshape(DEC_BATCH, N_QK_GROUP)
            xc = lax.cond(score_bound_ok(nqc, nkc, BATCH), attn_ctx(False), attn_ctx(True))
            xs = lax.cond(score_bound_ok(nqs, nks, DEC_BATCH, cache_k2), attn_smp(False), attn_smp(True))
            x = (xc, xs)
            ks.append(jnp.transpose(kt_new.reshape(BATCH, HEADS, 2, DA_DK, SEQ), (0, 4, 1, 2, 3)))
            vs.append(vc_new.reshape(BATCH, SEQ, HEADS, HD))
        else:
            r = l // 2
            q, i, gt, lf = _hgrn_pre(x, mod_l, g_l, hgrn_w_in[r], hgrn_lower_bounds, l)
            s0 = state_hgrn[:, r].astype(jnp.float32)
            o_f, o_b, s_fin = _rec(q, i, lf, s0)
            x = _hgrn_out(x, mod_l, g_l, o_f, o_b, gt, hgrn_gnorm[r].reshape(1, HD), hgrn_w_out[r])
            states.append(s_fin)
        if l < DEPTH - 1:
            x = _ffn(x, mod_l, g_l, ffn_w_in, ffn_w_out, l, 2)

    assert not isinstance(x, tuple)
    y_prompt = _ffn(x, mod_l, g_l, ffn_w_in, ffn_w_out, DEPTH - 1, 2, base=0, rows=N_CTX)
    y_sample = _ffn(x, mod_l, g_l, ffn_w_in, ffn_w_out, DEPTH - 1, 2, base=N_CTX, rows=N_SMP)
    return (y_prompt.reshape(BATCH, SEQ, D), y_sample.reshape(DEC_BATCH, DEC_SEQ, D),
            jnp.stack(ks, axis=1), jnp.stack(vs, axis=1), jnp.stack(states, axis=1))
```

```python
import functools
import math

import numpy as np
import jax
import jax.numpy as jnp
from jax import lax
from jax.experimental import pallas as pl
from jax.experimental.pallas import tpu as pltpu

D = 1024
BATCH = 16
SEQ = 256
DEPTH = 2
DEC_BATCH = 4
DEC_SEQ = 2048
PAST = 256
GRID_W = 64
N_MOD = 9
EPS = 1e-6
HEADS = 8
HD = 128
DA_DK = 64
ROPE_THETA = 10000.0
FFN_H = 2816
LOG2E = 1.4426950408889634

N_CTX = BATCH * SEQ
N_SMP = DEC_BATCH * DEC_SEQ
N_TOK = N_CTX + N_SMP
N_GRP = 8

VMEM_LIMIT = 56 * 1024 * 1024
LANES = 128
SUB = 8

TM = 512
FFN_HC = 256
TQ = 256
KB = 256
N_QK_GROUP = 2 * HEADS
MAX_UNSHIFTED_LOG2_SCORE = 60.0
CH = 128
N_SLAB = CH // SUB
N_LEVEL = 7
MAX_FAST_LOG2_DECAY = 200.0


def _cparams(sem):
    return pltpu.CompilerParams(dimension_semantics=sem, vmem_limit_bytes=VMEM_LIMIT)


def _grp_of_block(i, rows):
    nctx = N_CTX // rows
    per = DEC_SEQ // rows
    return jnp.where(i < nctx, 0, 1 + (i - nctx) // per)


def _rms(x, g):
    return (x * lax.rsqrt(jnp.mean(x * x, axis=-1, keepdims=True) + EPS)) * g


def _dot(a, b):
    return lax.dot_general(a, b, (((1,), (0,)), ((), ())), preferred_element_type=jnp.float32)


def _dot_nt(a, b):
    return lax.dot_general(a, b, (((1,), (1,)), ((), ())), preferred_element_type=jnp.float32)


def _dot_tn(a, b):
    return lax.dot_general(a, b, (((0,), (0,)), ((), ())), preferred_element_type=jnp.float32)


def _const_spec(shape):
    nd = len(shape)
    return pl.BlockSpec(shape, lambda *_: (0,) * nd, pipeline_mode=pl.Buffered(1))


def _mod_kernel(c_ref, w_ref, b_ref, o_ref):
    c = c_ref[...]
    s = (c * jax.nn.sigmoid(c)).astype(jnp.bfloat16)
    o_ref[...] = _dot(s, w_ref[...].astype(jnp.bfloat16)) + b_ref[...]


def _modulation(cc, w_mod, b_mod):
    tn = 1024
    nt = (N_MOD * D) // tn
    return pl.pallas_call(
        _mod_kernel,
        grid=(DEPTH, nt),
        in_specs=[
            pl.BlockSpec((N_GRP, D), lambda l, j: (0, 0)),
            pl.BlockSpec((None, D, tn), lambda l, j: (l, 0, j)),
            pl.BlockSpec((None, 1, tn), lambda l, j: (l, 0, j)),
        ],
        out_specs=pl.BlockSpec((None, N_GRP, tn), lambda l, j: (l, 0, j)),
        out_shape=jax.ShapeDtypeStruct((DEPTH, N_GRP, N_MOD * D), jnp.float32),
        compiler_params=_cparams(("parallel", "parallel")),
        name="modulation",
    )(cc, w_mod, b_mod.reshape(DEPTH, 1, N_MOD * D))


def _ada_pre(x, mod_ref, g_ref, j, gi):
    shift = mod_ref[3 * j:3 * j + 1, :]
    scale = mod_ref[3 * j + 1:3 * j + 2, :]
    return _rms(x, g_ref[gi:gi + 1, :]) * (1.0 + scale) + shift


def _ada_post(x, y, mod_ref, g_ref, j, gi, res_w):
    gate = mod_ref[3 * j + 2:3 * j + 3, :]
    return x + res_w * gate * _rms(y, g_ref[gi:gi + 1, :])


def _tok_specs():
    return [
        pl.BlockSpec((TM, D), lambda i: (i, 0)),
        pl.BlockSpec((None, N_MOD, D), lambda i: (_grp_of_block(i, TM), 0, 0)),
        _const_spec((6, D)),
    ]


def _ffn_kernel(*refs, j, pair, split):
    refs = list(refs)
    acc_ref = refs.pop()
    out_refs = [refs.pop() for _ in range(2 if split else 1)][::-1]
    is_ctx = pl.program_id(0) < N_CTX // TM
    if pair:
        xa_ref, xb_ref, mod_ref, g_ref, win_ref, wout_ref = refs
        x = jnp.where(is_ctx, xa_ref[...], xb_ref[...])
    else:
        x_ref, mod_ref, g_ref, win_ref, wout_ref = refs
        x = x_ref[...]
    h = _ada_pre(x, mod_ref, g_ref, j, 2 * j).astype(jnp.bfloat16)
    for c in range(FFN_H // FFN_HC):
        lo = c * FFN_HC
        a = _dot(h, win_ref[:, lo:lo + FFN_HC])
        b = _dot(h, win_ref[:, FFN_H + lo:FFN_H + lo + FFN_HC])
        u = ((a * jax.nn.sigmoid(a)) * b).astype(jnp.bfloat16)
        part = _dot(u, wout_ref[lo:lo + FFN_HC, :])
        if c == 0:
            acc_ref[...] = part
        else:
            acc_ref[...] += part
    res = _ada_post(x, acc_ref[...], mod_ref, g_ref, j, 2 * j + 1, 0.5)
    if split:
        @pl.when(is_ctx)
        def _():
            out_refs[0][...] = res

        @pl.when(jnp.logical_not(is_ctx))
        def _():
            out_refs[1][...] = res
    else:
        out_refs[0][...] = res


def _ffn(x, mod_l, g_l, w_in, w_out, l, j, *, split=False):
    pair = isinstance(x, tuple)
    nctx = N_CTX // TM
    rows = N_TOK
    w_idx = (l, j // 2, 0, 0)
    if split:
        out_specs = [pl.BlockSpec((TM, D), lambda i: (jnp.minimum(i, nctx - 1), 0)),
                     pl.BlockSpec((TM, D), lambda i: (jnp.maximum(i - nctx, 0), 0))]
        out_shape = [jax.ShapeDtypeStruct((N_CTX, D), jnp.float32),
                     jax.ShapeDtypeStruct((N_SMP, D), jnp.float32)]
    else:
        out_specs = pl.BlockSpec((TM, D), lambda i: (i, 0))
        out_shape = jax.ShapeDtypeStruct((rows, D), jnp.float32)
    if pair:
        x_specs = [pl.BlockSpec((TM, D), lambda i: (jnp.minimum(i, nctx - 1), 0)),
                   pl.BlockSpec((TM, D), lambda i: (jnp.maximum(i - nctx, 0), 0))]
        x_args = list(x)
    else:
        x_specs = [pl.BlockSpec((TM, D), lambda i: (i, 0))]
        x_args = [x]
    return pl.pallas_call(
        functools.partial(_ffn_kernel, j=j, pair=pair, split=split),
        grid=(rows // TM,),
        in_specs=x_specs + [
            pl.BlockSpec((None, N_MOD, D), lambda i: (_grp_of_block(i, TM), 0, 0)),
            _const_spec((6, D)),
            pl.BlockSpec((None, None, D, 2 * FFN_H), lambda i: w_idx, pipeline_mode=pl.Buffered(1)),
            pl.BlockSpec((None, None, FFN_H, D), lambda i: w_idx, pipeline_mode=pl.Buffered(1))],
        out_specs=out_specs,
        out_shape=out_shape,
        scratch_shapes=[pltpu.VMEM((TM, D), jnp.float32)],
        compiler_params=_cparams(("arbitrary" if split else "parallel",)),
        name=f"ffn{j}",
    )(*x_args, mod_l, g_l, w_in, w_out)


def _rope_lanes(x, cos, sin_signed):
    lane = lax.broadcasted_iota(jnp.int32, x.shape, 1)
    partner = jnp.where((lane % 32) < 16, pltpu.roll(x, LANES - 16, 1), pltpu.roll(x, 16, 1))
    return x * cos + partner * sin_signed


def _rope_rows(x, cos_t, sin_signed_t):
    parts = []
    for g in range(HD // 32):
        parts += [x[g * 32 + 16:g * 32 + 32, :], x[g * 32:g * 32 + 16, :]]
    return x * cos_t + jnp.concatenate(parts, axis=0) * sin_signed_t


def _attn_pre_kernel(x_ref, mod_ref, g_ref, w_ref, wkt_ref, grp_ref, cos_ref, sin_ref, cost_ref, sint_ref,
                     q_ref, kt_ref, v_ref, nq_ref, nk_ref, q2_ref, k2_ref, *, rope):
    x = x_ref[...]
    h = _ada_pre(x, mod_ref, g_ref, 1, 2).astype(jnp.bfloat16)
    qscale = DA_DK ** -0.5 * LOG2E
    zq = _dot(h, w_ref[:, 0:D])
    zkt = _dot_nt(wkt_ref[...], h)
    v_ref[...] = _dot(h, w_ref[:, 2 * D:3 * D]).astype(v_ref.dtype)
    for hd in range(HEADS):
        sl = slice(hd * HD, (hd + 1) * HD)
        q = zq[:, sl]
        k = zkt[sl, :]
        if rope:
            q = _rope_lanes(q, cos_ref[...], sin_ref[...])
            k = _rope_rows(k, cost_ref[...], sint_ref[...])
        q = q * qscale
        q_ref[:, sl] = q.astype(q_ref.dtype)
        kt_ref[sl, :] = k.astype(kt_ref.dtype)
        q2_ref[:, sl] = (q * q).astype(jnp.bfloat16)
        k2_ref[sl, :] = (k * k).astype(jnp.bfloat16)
    qn = jnp.max(_dot(q2_ref[...], grp_ref[...]), axis=0, keepdims=True)
    kn = jnp.max(_dot_tn(grp_ref[...], k2_ref[...]), axis=1, keepdims=True)
    nq_ref[...] = jnp.broadcast_to(qn, nq_ref.shape)
    nk_ref[...] = jnp.broadcast_to(kn[:N_QK_GROUP, :], nk_ref.shape)


def _attn_pre(x, mod_l, g_l, w_in, w_kt, grp, tables, *, ctx):
    nb, seq = (BATCH, SEQ) if ctx else (DEC_BATCH, DEC_SEQ)
    t = min(TM, seq)
    rows = nb * seq
    base = 0 if ctx else N_CTX // t
    per = seq // t
    ptab = DEC_SEQ // t
    kv_dtype = jnp.float32 if ctx else jnp.bfloat16
    out_spec = pl.BlockSpec((t, D), lambda i: (i, 0))
    return pl.pallas_call(
        functools.partial(_attn_pre_kernel, rope=not ctx),
        grid=(rows // t,),
        in_specs=[
            pl.BlockSpec((t, D), lambda i: (i + base, 0)),
            pl.BlockSpec((None, N_MOD, D), lambda i: (_grp_of_block(i + base, t), 0, 0)),
            _const_spec((6, D)), _const_spec((D, 3 * D)), _const_spec((D, D)), _const_spec((D, LANES)),
            pl.BlockSpec((t, HD), lambda i: (i % ptab, 0)),
            pl.BlockSpec((t, HD), lambda i: (i % ptab, 0)),
            pl.BlockSpec((HD, t), lambda i: (0, i % ptab)),
            pl.BlockSpec((HD, t), lambda i: (0, i % ptab)),
        ],
        out_specs=[out_spec, pl.BlockSpec((None, D, t), lambda i: (i // per, 0, i % per)), out_spec,
                   pl.BlockSpec((None, SUB, LANES), lambda i: (i, 0, 0)),
                   pl.BlockSpec((None, N_QK_GROUP, LANES), lambda i: (i, 0, 0))],
        out_shape=[
            jax.ShapeDtypeStruct((rows, D), jnp.bfloat16),
            jax.ShapeDtypeStruct((nb, D, seq), kv_dtype),
            jax.ShapeDtypeStruct((rows, D), kv_dtype),
            jax.ShapeDtypeStruct((rows // t, SUB, LANES), jnp.float32),
            jax.ShapeDtypeStruct((rows // t, N_QK_GROUP, LANES), jnp.float32),
        ],
        scratch_shapes=[pltpu.VMEM((t, D), jnp.bfloat16), pltpu.VMEM((D, t), jnp.bfloat16)],
        compiler_params=_cparams(("parallel",)),
        name="attn_pre_ctx" if ctx else "attn_pre_smp",
    )(x, mod_l, g_l, w_in, w_kt, grp, *tables)


def _attn_kernel(*refs, lam_init, cache, shifted):
    refs = list(refs)
    e_ref = refs.pop()
    s_ref = refs.pop() if shifted else None
    if cache:
        (x_ref, mod_ref, g_ref, q_ref, k_ref, v_ref, kc_ref, vc_ref,
         lam_ref, sub_ref, w_ref, o_ref, oh_ref) = refs
    else:
        (x_ref, mod_ref, g_ref, q_ref, k_ref, v_ref,
         lam_ref, sub_ref, w_ref, o_ref, oh_ref) = refs
    lp = lam_ref[...]
    lam = (jnp.exp(jnp.sum(lp[0:1] * lp[1:2], axis=-1, keepdims=True))
           - jnp.exp(jnp.sum(lp[2:3] * lp[3:4], axis=-1, keepdims=True)) + lam_init)
    tq = q_ref.shape[0]
    lane = lax.broadcasted_iota(jnp.int32, (tq, HD), 1)
    bf = jnp.bfloat16
    kbw = KB if shifted else min(2 * KB, k_ref.shape[1])
    n_own = k_ref.shape[1] // kbw
    n_kb = n_own + (1 if cache else 0)

    def kcols(kb):
        if kb < n_own:
            return slice(kb * kbw, (kb + 1) * kbw)
        return slice(n_own * kbw, n_own * kbw + PAST)

    def keys_t(sl, kb):
        if kb < n_own:
            return k_ref[sl, kcols(kb)].astype(bf)
        return kc_ref[sl, :].astype(bf)

    def values(sl, kb):
        if kb < n_own:
            return v_ref[kcols(kb), sl].astype(bf)
        return vc_ref[:, sl].astype(bf)

    def head_slice(hd):
        return slice(hd * HD, (hd + 1) * HD)

    def lane_tiles(x, op):
        parts = [x[:, j * LANES:(j + 1) * LANES] for j in range(x.shape[1] // LANES)]
        return functools.reduce(op, parts)

    def stacked_q(hd):
        qh = q_ref[:, head_slice(hd)]
        return jnp.concatenate([jnp.where(lane < DA_DK, qh, jnp.zeros_like(qh)),
                                jnp.where(lane >= DA_DK, qh, jnp.zeros_like(qh))], axis=0)

    st = [dict() for _ in range(HEADS)]
    lag = 1 if shifted else 0
    for slot in range(HEADS + lag + 1):
        h1, h2, h3 = slot, slot - lag, slot - lag - 1
        if h1 < HEADS:
            st[h1]["qq"] = stacked_q(h1)
        if 0 <= h3 < HEADS:
            tot = st[h3]["tot"]
            c_col = lam * tot[:tq] / tot[tq:]
            st[h3]["c"] = {w: jnp.broadcast_to(c_col, (tq, w)).astype(bf)
                           for w in {kbw, PAST if cache else kbw}}
        for kb in range(n_kb):
            if h1 < HEADS:
                s = _dot(st[h1]["qq"], keys_t(head_slice(h1), kb))
                if shifted:
                    s_ref[h1 % 2, :, kcols(kb)] = s
                    mp = lane_tiles(s, jnp.maximum)
                    st[h1]["m"] = mp if kb == 0 else jnp.maximum(st[h1]["m"], mp)
            if 0 <= h2 < HEADS:
                if shifted:
                    e = jnp.exp2(s_ref[h2 % 2, :, kcols(kb)] - st[h2]["mrow"])
                else:
                    e = jnp.exp2(s)
                ep = lane_tiles(e, lambda a, b: a + b)
                st[h2]["l"] = ep if kb == 0 else st[h2]["l"] + ep
                e_ref[h2 % 2, :, kcols(kb)] = e.astype(bf)
            if 0 <= h3 < HEADS:
                cols = kcols(kb)
                c = st[h3]["c"][cols.stop - cols.start]
                a = e_ref[h3 % 2, :tq, cols] - c * e_ref[h3 % 2, tq:, cols]
                part = _dot(a, values(head_slice(h3), kb))
                st[h3]["o"] = part if kb == 0 else st[h3]["o"] + part
        if shifted and h1 < HEADS:
            st[h1]["mrow"] = jnp.max(st[h1]["m"], axis=-1, keepdims=True)
        if 0 <= h2 < HEADS:
            st[h2]["tot"] = jnp.sum(st[h2]["l"], axis=-1, keepdims=True)
        if 0 <= h3 < HEADS:
            o = st[h3]["o"] * (1.0 / st[h3]["tot"][:tq])
            o = _rms(o, sub_ref[...]) * (1.0 - lam_init)
            oh_ref[:, head_slice(h3)] = o.astype(bf)
            st[h3].clear()
    y = _dot(oh_ref[...], w_ref[...])
    o_ref[...] = _ada_post(x_ref[...], y, mod_ref, g_ref, 1, 3, 1.0)


def _attn(x, mod_l, g_l, q, k, v, kc, vc, lam_p, subln, w_out, *, ctx, lam_init, shifted):
    nb, seq = (BATCH, SEQ) if ctx else (DEC_BATCH, DEC_SEQ)
    rows = nb * seq
    tq = min(TQ if shifted else 2 * TQ, seq)
    base = 0 if ctx else N_CTX // tq
    nq = seq // tq
    assert seq % KB == 0 and PAST == KB
    n_keys = seq if ctx else seq + PAST
    row_spec = pl.BlockSpec((tq, D), lambda b, i: (b * nq + i, 0))
    in_specs = [
        pl.BlockSpec((tq, D), lambda b, i: (base + b * nq + i, 0)),
        pl.BlockSpec((None, N_MOD, D), lambda b, i: (_grp_of_block(base + b * nq + i, tq), 0, 0)),
        pl.BlockSpec((6, D), lambda b, i: (0, 0)),
        row_spec,
        pl.BlockSpec((None, D, seq), lambda b, i: (b, 0, 0)),
        pl.BlockSpec((seq, D), lambda b, i: (b, 0)),
    ]
    args = [x, mod_l, g_l, q, k, v]
    if not ctx:
        in_specs += [pl.BlockSpec((None, D, PAST), lambda b, i: (b, 0, 0)),
                     pl.BlockSpec((None, PAST, D), lambda b, i: (b, 0, 0))]
        args += [kc, vc]
    in_specs += [
        pl.BlockSpec((4, DA_DK), lambda b, i: (0, 0)),
        pl.BlockSpec((1, HD), lambda b, i: (0, 0)),
        pl.BlockSpec((D, D), lambda b, i: (0, 0), pipeline_mode=pl.Buffered(1)),
    ]
    args += [lam_p, subln, w_out]
    return pl.pallas_call(
        functools.partial(_attn_kernel, lam_init=lam_init, cache=not ctx, shifted=shifted),
        grid=(nb, nq),
        in_specs=in_specs,
        out_specs=row_spec,
        out_shape=jax.ShapeDtypeStruct((rows, D), jnp.float32),
        scratch_shapes=[pltpu.VMEM((tq, D), jnp.bfloat16)]
        + ([pltpu.VMEM((2, 2 * tq, n_keys), jnp.float32)] if shifted else [])
        + [pltpu.VMEM((2, 2 * tq, n_keys), jnp.bfloat16)],
        compiler_params=_cparams(("parallel", "parallel")),
        name=("attn_ctx" if ctx else "attn_smp") + ("_shifted" if shifted else ""),
    )(*args)


def _hgrn_pre_kernel(x_ref, mod_ref, g_ref, w_ref, lb_ref, q_ref, i_ref, gt_ref, lf_ref, *, layer):
    x = x_ref[...]
    h = _ada_pre(x, mod_ref, g_ref, 1, 2).astype(jnp.bfloat16)

    def proj(n):
        return _dot(h, w_ref[:, n * D:(n + 1) * D])

    def log_gate(d, ff):
        raw = [lb_ref[l, d:d + 1, :] for l in range(DEPTH)]
        m = functools.reduce(jnp.maximum, raw)
        ex = [jnp.exp(r - m) for r in raw]
        tot = functools.reduce(lambda a, b: a + b, ex)
        soft = [e / tot for e in ex]
        lb = functools.reduce(lambda a, b: a + b, soft[:layer + 1]) - soft[0]
        return jnp.log(lb + (1.0 - lb) * jax.nn.sigmoid(ff))

    ff = proj(1)
    q_ref[...] = proj(0).astype(q_ref.dtype)
    lf_ref[0] = log_gate(0, ff)
    fb = proj(2)
    i_ref[...] = proj(3).astype(i_ref.dtype)
    lf_ref[1] = log_gate(1, fb)
    gt_ref[...] = proj(4).astype(gt_ref.dtype)


def _hgrn_pre(x, mod_l, g_l, w_in, lb_raw, layer):
    row = pl.BlockSpec((TM, D), lambda i: (i, 0))
    return pl.pallas_call(
        functools.partial(_hgrn_pre_kernel, layer=layer),
        grid=(N_TOK // TM,),
        in_specs=_tok_specs() + [_const_spec((D, 5 * D)), _const_spec((DEPTH, 2, D))],
        out_specs=[row, row, row, pl.BlockSpec((2, TM, D), lambda i: (0, i, 0))],
        out_shape=[
            jax.ShapeDtypeStruct((N_TOK, D), jnp.bfloat16),
            jax.ShapeDtypeStruct((N_TOK, D), jnp.bfloat16),
            jax.ShapeDtypeStruct((N_TOK, D), jnp.bfloat16),
            jax.ShapeDtypeStruct((2, N_TOK, D), jnp.float32),
        ],
        compiler_params=_cparams(("parallel",)),
        name="hgrn_pre",
    )(x, mod_l, g_l, w_in, lb_raw)


def _level_masks(rev):
    t = np.arange(CH)[:, None]
    s = np.arange(CH)[None, :]
    out = []
    for p in range(N_LEVEL):
        same = (t >> (p + 1)) == (s >> (p + 1))
        tb, sb = (t >> p) & 1, (s >> p) & 1
        out.append(same & ((tb == 0) & (sb == 1) if rev else (tb == 1) & (sb == 0)))
    out.append(t == s)
    return np.stack(out).astype(np.float32)


def _row_bcast(x, r):
    return jnp.broadcast_to(x[r:r + 1, :], x.shape)


def _rec_cumsum(lf_ref, b_ref, br_ref, rev):
    sub = lax.broadcasted_iota(jnp.int32, (SUB, D), 0)
    order = range(N_SLAB - 1, -1, -1) if rev else range(N_SLAB)
    edge = 0 if rev else SUB - 1
    carry = None
    for v in order:
        rows = slice(v * SUB, (v + 1) * SUB)
        c = lf_ref[rows, :] * LOG2E
        for k in (1, 2, 4):
            if rev:
                c = c + jnp.where(sub < SUB - k, pltpu.roll(c, SUB - k, 0), 0.0)
            else:
                c = c + jnp.where(sub >= k, pltpu.roll(c, k, 0), 0.0)
        if carry is not None:
            c = c + carry
        carry = _row_bcast(c, edge)
        b_ref[rows, :] = c
        br_ref[rows, :] = carry
    return carry


def _rec_fast(lf_ref, q_ref, i_ref, b_ref, b_last, qs_ref, ks_ref, st_ref, o_ref, rev):
    bf = jnp.bfloat16
    half = 0.5 * b_last
    half2 = jnp.concatenate([half, half], axis=0)
    for v2 in range(N_SLAB // 2):
        rows = slice(v2 * 2 * SUB, (v2 + 1) * 2 * SUB)
        k = 1.0 - jnp.exp2(lf_ref[rows, :] * LOG2E)
        d = b_ref[rows, :] - half2
        qs_ref[rows, :] = (q_ref[rows, :].astype(jnp.float32) * jnp.exp2(d)).astype(bf)
        ks_ref[rows, :] = (k * jnp.exp2(-d)).astype(bf)
    scale = jnp.exp2(half[0:1, :])
    decay = scale * scale
    t = lax.broadcasted_iota(jnp.int32, (CH, CH), 0)
    s = lax.broadcasted_iota(jnp.int32, (CH, CH), 1)
    seen = (s >= t) if rev else (s <= t)
    for hd in range(HEADS):
        sl = slice(hd * HD, (hd + 1) * HD)
        qs, ks, vh = qs_ref[:, sl], ks_ref[:, sl], i_ref[:, sl]
        a = jnp.where(seen, _dot_nt(qs, ks), 0.0)
        st = st_ref[hd]
        o = _dot(a.astype(bf), vh) + _dot_nt(qs, (st * scale[:, sl]).astype(bf))
        o_ref[:, sl] = o.astype(o_ref.dtype)
        st_ref[hd] = st * decay[:, sl] + _dot_tn(vh, ks) * scale[:, sl]


def _rec_prepare(lf_ref, q_ref, b_ref, br_ref, b_last, qt_ref, kt_ref, rev):
    bf = jnp.bfloat16
    sub = lax.broadcasted_iota(jnp.int32, (SUB, D), 0)
    zeros = jnp.zeros((SUB, D), jnp.float32)
    for v2 in range(N_SLAB // 2):
        rows = slice(v2 * 2 * SUB, (v2 + 1) * 2 * SUB)
        f = jnp.exp2(lf_ref[rows, :] * LOG2E)
        k = 1.0 - f
        q = q_ref[rows, :].astype(jnp.float32)
        b = b_ref[rows, :]
        qt_ref[0, rows, :] = (q * f).astype(bf)
        kt_ref[0, rows, :] = k.astype(bf)
        half = (slice(0, SUB), slice(SUB, 2 * SUB))
        for p in range(1, N_LEVEL):
            if p >= 3:
                w = p - 3
                es, qside = [], []
                for n, v in enumerate((2 * v2, 2 * v2 + 1)):
                    hi = v & ~((1 << (w + 1)) - 1)
                    vr = hi | (1 << w) if rev else hi | ((1 << w) - 1)
                    r = br_ref[vr * SUB:(vr + 1) * SUB, :]
                    qside.append(((v >> w) & 1) == (0 if rev else 1))
                    es.append(jnp.exp2(b[half[n], :] - r) if qside[n] else jnp.exp2(r - b[half[n], :]))
                if qside[0] == qside[1]:
                    e = jnp.concatenate(es, axis=0)
                    if qside[0]:
                        qt_ref[p, rows, :] = (q * e).astype(bf)
                    else:
                        kt_ref[p, rows, :] = (k * e).astype(bf)
                else:
                    qe = [q[half[n], :] * es[n] if qside[n] else zeros for n in range(2)]
                    ke = [zeros if qside[n] else k[half[n], :] * es[n] for n in range(2)]
                    qt_ref[p, rows, :] = jnp.concatenate(qe, axis=0).astype(bf)
                    kt_ref[p, rows, :] = jnp.concatenate(ke, axis=0).astype(bf)
            else:
                es = []
                for n in range(2):
                    bv = b[half[n], :]
                    if p == 2:
                        r = _row_bcast(bv, 4 if rev else 3)
                    else:
                        lo, hi_r = (2, 6) if rev else (1, 5)
                        r = jnp.where(sub < 4, _row_bcast(bv, lo), _row_bcast(bv, hi_r))
                    es.append(jnp.exp2(-jnp.abs(bv - r)))
                e = jnp.concatenate(es, axis=0)
                qt_ref[p, rows, :] = (q * e).astype(bf)
                kt_ref[p, rows, :] = (k * e).astype(bf)
        bl = jnp.concatenate([b_last, b_last], axis=0)
        qt_ref[N_LEVEL, rows, :] = (q * jnp.exp2(b)).astype(bf)
        kt_ref[N_LEVEL, rows, :] = (k * jnp.exp2(bl - b)).astype(bf)
    return jnp.exp2(b_last[0:1, :])


def _rec_heads(q_ref, i_ref, qt_ref, kt_ref, msk_ref, st_ref, o_ref, decay):
    bf = jnp.bfloat16
    top = N_LEVEL - 1
    for hd in range(HEADS):
        sl = slice(hd * HD, (hd + 1) * HD)
        a = _dot_nt(qt_ref[top, :, sl], kt_ref[top, :, sl])
        a = a + msk_ref[N_LEVEL] * _dot_nt(q_ref[:, sl], kt_ref[0, :, sl])
        for p in range(top):
            a = a + msk_ref[p] * _dot_nt(qt_ref[p, :, sl], kt_ref[p, :, sl])
        vh = i_ref[:, sl]
        st = st_ref[hd]
        o = _dot(a.astype(bf), vh) + _dot_nt(qt_ref[N_LEVEL, :, sl], st.astype(bf))
        o_ref[:, sl] = o.astype(o_ref.dtype)
        st_ref[hd] = st * decay[:, sl] + _dot_tn(vh, kt_ref[N_LEVEL, :, sl])


def _rec_kernel(blkf_ref, blkb_ref, sid_ref, first_ref, last_ref,
                qf_ref, if_ref, lff_ref, qb_ref, ib_ref, lfb_ref, s0_ref, mskf_ref, mskb_ref,
                of_ref, ob_ref, sfin_ref,
                st_ref, bf_ref, brf_ref, qtf_ref, ktf_ref, bb_ref, brb_ref, qtb_ref, ktb_ref):
    step = pl.program_id(0)

    @pl.when(step == 0)
    def _():
        for ref in (qtf_ref, ktf_ref, qtb_ref, ktb_ref):
            ref[...] = jnp.zeros_like(ref)

    @pl.when(first_ref[step] == 1)
    def _():
        st_ref[...] = jnp.zeros_like(st_ref)

    @pl.when(first_ref[step] == 2)
    def _():
        for d in range(2):
            for hd in range(HEADS):
                st_ref[d, hd] = s0_ref[d, hd].T

    last_f = _rec_cumsum(lff_ref, bf_ref, brf_ref, False)
    last_b = _rec_cumsum(lfb_ref, bb_ref, brb_ref, True)
    moderate = jnp.min(jnp.minimum(last_f, last_b)) >= -MAX_FAST_LOG2_DECAY

    @pl.when(moderate)
    def _():
        _rec_fast(lff_ref, qf_ref, if_ref, bf_ref, last_f, qtf_ref.at[N_LEVEL], ktf_ref.at[N_LEVEL],
                  st_ref.at[0], of_ref, False)
        _rec_fast(lfb_ref, qb_ref, ib_ref, bb_ref, last_b, qtb_ref.at[N_LEVEL], ktb_ref.at[N_LEVEL],
                  st_ref.at[1], ob_ref, True)

    @pl.when(jnp.logical_not(moderate))
    def _():
        decay_f = _rec_prepare(lff_ref, qf_ref, bf_ref, brf_ref, last_f, qtf_ref, ktf_ref, False)
        decay_b = _rec_prepare(lfb_ref, qb_ref, bb_ref, brb_ref, last_b, qtb_ref, ktb_ref, True)
        _rec_heads(qf_ref, if_ref, qtf_ref, ktf_ref, mskf_ref, st_ref.at[0], of_ref, decay_f)
        _rec_heads(qb_ref, ib_ref, qtb_ref, ktb_ref, mskb_ref, st_ref.at[1], ob_ref, decay_b)

    @pl.when(last_ref[step] == 1)
    def _():
        for d in range(2):
            for hd in range(HEADS):
                sfin_ref[d, hd] = st_ref[d, hd].T


def _rec_tables():
    blkf, blkb, sid, first, last = [], [], [], [], []
    seqs = [(b * (SEQ // CH), SEQ // CH) for b in range(BATCH)]
    seqs += [(N_CTX // CH + b * (DEC_SEQ // CH), DEC_SEQ // CH) for b in range(DEC_BATCH)]
    for n, (start, nc) in enumerate(seqs):
        ctx = n < BATCH
        for c in range(nc):
            blkf.append(start + c)
            blkb.append(start + nc - 1 - c)
            sid.append(n)
            first.append((1 if ctx else 2) if c == 0 else 0)
            last.append(int(ctx and c == nc - 1))
    return [jnp.asarray(np.asarray(a, np.int32)) for a in (blkf, blkb, sid, first, last)]


def _rec(q, i, lf, s0):
    row_f = pl.BlockSpec((CH, D), lambda s, bkf, bkb, sid, fi, la: (bkf[s], 0))
    row_b = pl.BlockSpec((CH, D), lambda s, bkf, bkb, sid, fi, la: (bkb[s], 0))
    st_shape = (None, 2, HEADS, HD, HD)
    scratch = [pltpu.VMEM((2, HEADS, HD, HD), jnp.float32)]
    for _ in range(2):
        scratch += [
            pltpu.VMEM((CH, D), jnp.float32),
            pltpu.VMEM((CH, D), jnp.float32),
            pltpu.VMEM((N_LEVEL + 1, CH, D), jnp.bfloat16),
            pltpu.VMEM((N_LEVEL + 1, CH, D), jnp.bfloat16),
        ]
    msk_spec = pl.BlockSpec((N_LEVEL + 1, CH, CH), lambda s, *_: (0, 0, 0))
    grid_spec = pltpu.PrefetchScalarGridSpec(
        num_scalar_prefetch=5,
        grid=(N_TOK // CH,),
        in_specs=[
            row_f, row_f,
            pl.BlockSpec((None, CH, D), lambda s, bkf, bkb, sid, fi, la: (0, bkf[s], 0)),
            row_b, row_b,
            pl.BlockSpec((None, CH, D), lambda s, bkf, bkb, sid, fi, la: (1, bkb[s], 0)),
            pl.BlockSpec(st_shape, lambda s, bkf, bkb, sid, fi, la: (jnp.maximum(sid[s] - BATCH, 0), 0, 0, 0, 0)),
            msk_spec, msk_spec,
        ],
        out_specs=[
            row_f, row_b,
            pl.BlockSpec(st_shape, lambda s, bkf, bkb, sid, fi, la: (jnp.minimum(sid[s], BATCH - 1), 0, 0, 0, 0)),
        ],
        scratch_shapes=scratch,
    )
    return pl.pallas_call(
        _rec_kernel,
        grid_spec=grid_spec,
        out_shape=[
            jax.ShapeDtypeStruct((N_TOK, D), jnp.bfloat16),
            jax.ShapeDtypeStruct((N_TOK, D), jnp.bfloat16),
            jax.ShapeDtypeStruct((BATCH, 2, HEADS, HD, HD), jnp.float32),
        ],
        compiler_params=_cparams(("arbitrary",)),
        name="hgrn_rec",
    )(*_rec_tables(), q, i, lf, q, i, lf, s0, jnp.asarray(_level_masks(False)), jnp.asarray(_level_masks(True)))


def _hgrn_out_kernel(x_ref, mod_ref, g_ref, of_ref, ob_ref, gt_ref, gn_ref, w_ref, o_ref, oh_ref):
    for hd in range(HEADS):
        sl = slice(hd * HD, (hd + 1) * HD)
        o = of_ref[:, sl].astype(jnp.float32) + ob_ref[:, sl].astype(jnp.float32)
        o = _rms(o, gn_ref[...]) * jax.nn.sigmoid(gt_ref[:, sl].astype(jnp.float32))
        oh_ref[:, sl] = o.astype(jnp.bfloat16)
    y = _dot(oh_ref[...], w_ref[...])
    o_ref[...] = _ada_post(x_ref[...], y, mod_ref, g_ref, 1, 3, 1.0)


def _hgrn_out(x, mod_l, g_l, o_f, o_b, gt, gnorm, w_out):
    row = pl.BlockSpec((TM, D), lambda i: (i, 0))
    return pl.pallas_call(
        _hgrn_out_kernel,
        grid=(N_TOK // TM,),
        in_specs=_tok_specs() + [row, row, row, _const_spec((1, HD)), _const_spec((D, D))],
        out_specs=row,
        out_shape=jax.ShapeDtypeStruct((N_TOK, D), jnp.float32),
        scratch_shapes=[pltpu.VMEM((TM, D), jnp.bfloat16)],
        compiler_params=_cparams(("parallel",)),
        name="hgrn_out",
    )(x, mod_l, g_l, o_f, o_b, gt, gnorm, w_out)


def _rope_tables():
    pos = np.arange(DEC_SEQ)
    row = (pos // GRID_W).astype(np.float64)
    col = (pos % GRID_W).astype(np.float64)
    nf = DA_DK // 4
    inv = ROPE_THETA ** (-np.arange(nf, dtype=np.float64) / nf)
    lane = np.arange(HD)
    axis = (lane % DA_DK) // (2 * nf)
    ang = np.where(axis[None, :] == 0, row[:, None], col[:, None]) * inv[lane % nf][None, :]
    sign = np.where((lane % (2 * nf)) < nf, -1.0, 1.0)
    cos, sin = np.cos(ang), np.sin(ang) * sign[None, :]
    return tuple(jnp.asarray(a, jnp.float32) for a in (cos, sin, cos.T, sin.T))


def kernel(x_prompt, x_sample, cache_k, cache_v, state_hgrn, c, c_ctx, w_mod, b_mod, norm_g,
           ffn_w_in, ffn_w_out, attn_w_in, attn_w_out, attn_lambda, attn_subln,
           hgrn_w_in, hgrn_w_out, hgrn_lower_bounds, hgrn_gnorm):
    x = (x_prompt.reshape(N_CTX, D), x_sample.reshape(N_SMP, D))
    cc = jnp.concatenate([c_ctx[None, :], c, jnp.zeros((N_GRP - 1 - DEC_BATCH, D), jnp.float32)], axis=0)
    mod = _modulation(cc, w_mod, b_mod).reshape(DEPTH, N_GRP, N_MOD, D)
    tables = _rope_tables()

    ks, vs, states = [], [], []
    for l in range(DEPTH):
        mod_l, g_l = mod[l], norm_g[l]
        x = _ffn(x, mod_l, g_l, ffn_w_in, ffn_w_out, l, 0)
        if l % 2 == 0:
            a = l // 2
            lam_init = 0.8 - 0.6 * math.exp(-0.3 * l)
            w_kt = attn_w_in[a][:, D:2 * D].T
            w_out = attn_w_out[a]
            subln = attn_subln[a].reshape(1, HD)
            grp = jnp.asarray(np.arange(D)[:, None] // DA_DK == np.arange(LANES)[None, :], jnp.bfloat16)
            qc, kt_new, vc_new, nqc, nkc = _attn_pre(x, mod_l, g_l, attn_w_in[a], w_kt, grp, tables, ctx=True)
            qs, kt_smp, v_smp, nqs, nks = _attn_pre(x, mod_l, g_l, attn_w_in[a], w_kt, grp, tables, ctx=False)
            kct = jnp.transpose(cache_k[:, a], (0, 2, 3, 4, 1)).reshape(DEC_BATCH, D, PAST)
            vcache = cache_v[:, a].reshape(DEC_BATCH, PAST, D)
            lam_p = attn_lambda[a]

            def score_bound_ok(nq, nk, n_seq, extra_k2=None):
                q2 = jnp.max(nq[:, 0, :N_QK_GROUP].reshape(n_seq, -1, N_QK_GROUP), axis=1)
                k2 = jnp.max(nk[:, :, 0].reshape(n_seq, -1, N_QK_GROUP), axis=1)
                if extra_k2 is not None:
                    k2 = jnp.maximum(k2, extra_k2)
                return jnp.max(q2 * k2) <= MAX_UNSHIFTED_LOG2_SCORE ** 2

            def attn_ctx(shifted):
                return lambda: _attn(x, mod_l, g_l, qc, kt_new, vc_new, None, None, lam_p, subln, w_out,
                                     ctx=True, lam_init=lam_init, shifted=shifted)

            def attn_smp(shifted):
                return lambda: _attn(x, mod_l, g_l, qs, kt_smp, v_smp, kct, vcache, lam_p, subln, w_out,
                                     ctx=False, lam_init=lam_init, shifted=shifted)

            cache_k2 = jnp.max(jnp.sum(jnp.square(cache_k[:, a]), axis=-1), axis=1).reshape(DEC_BATCH, N_QK_GROUP)
            xc = lax.cond(score_bound_ok(nqc, nkc, BATCH), attn_ctx(False), attn_ctx(True))
            xs = lax.cond(score_bound_ok(nqs, nks, DEC_BATCH, cache_k2), attn_smp(False), attn_smp(True))
            x = (xc, xs)
            ks.append(jnp.transpose(kt_new.reshape(BATCH, HEADS, 2, DA_DK, SEQ), (0, 4, 1, 2, 3)))
            vs.append(vc_new.reshape(BATCH, SEQ, HEADS, HD))
        else:
            r = l // 2
            q, i, gt, lf = _hgrn_pre(x, mod_l, g_l, hgrn_w_in[r], hgrn_lower_bounds, l)
            s0 = state_hgrn[:, r].astype(jnp.float32)
            o_f, o_b, s_fin = _rec(q, i, lf, s0)
            x = _hgrn_out(x, mod_l, g_l, o_f, o_b, gt, hgrn_gnorm[r].reshape(1, HD), hgrn_w_out[r])
            states.append(s_fin)
        if l < DEPTH - 1:
            x = _ffn(x, mod_l, g_l, ffn_w_in, ffn_w_out, l, 2)

    assert not isinstance(x, tuple)
    y_prompt, y_sample = _ffn(x, mod_l, g_l, ffn_w_in, ffn_w_out, DEPTH - 1, 2, split=True)
    return (y_prompt.reshape(BATCH, SEQ, D), y_sample.reshape(DEC_BATCH, DEC_SEQ, D),
            jnp.stack(ks, axis=1), jnp.stack(vs, axis=1), jnp.stack(states, axis=1))
```

```python
import functools
import math

import numpy as np
import jax
import jax.numpy as jnp
from jax import lax
from jax.experimental import pallas as pl
from jax.experimental.pallas import tpu as pltpu

D = 1024
BATCH = 16
SEQ = 256
DEPTH = 2
DEC_BATCH = 4
DEC_SEQ = 2048
PAST = 256
GRID_W = 64
N_MOD = 9
EPS = 1e-6
HEADS = 8
HD = 128
DA_DK = 64
ROPE_THETA = 10000.0
FFN_H = 2816
LOG2E = 1.4426950408889634

N_CTX = BATCH * SEQ
N_SMP = DEC_BATCH * DEC_SEQ
N_TOK = N_CTX + N_SMP
N_GRP = 8

VMEM_LIMIT = 56 * 1024 * 1024
LANES = 128
SUB = 8

TM = 512
FFN_HC = 256
TQ = 256
KB = 256
N_QK_GROUP = 2 * HEADS
MAX_UNSHIFTED_LOG2_SCORE = 60.0
CH = 128
N_SLAB = CH // SUB
N_LEVEL = 7
MAX_FAST_LOG2_DECAY = 200.0


def _cparams(sem):
    return pltpu.CompilerParams(dimension_semantics=sem, vmem_limit_bytes=VMEM_LIMIT)


def _grp_of_block(i, rows):
    nctx = N_CTX // rows
    per = DEC_SEQ // rows
    return jnp.where(i < nctx, 0, 1 + (i - nctx) // per)


def _rms(x, g):
    return (x * lax.rsqrt(jnp.mean(x * x, axis=-1, keepdims=True) + EPS)) * g


def _dot(a, b):
    return lax.dot_general(a, b, (((1,), (0,)), ((), ())), preferred_element_type=jnp.float32)


def _dot_nt(a, b):
    return lax.dot_general(a, b, (((1,), (1,)), ((), ())), preferred_element_type=jnp.float32)


def _dot_tn(a, b):
    return lax.dot_general(a, b, (((0,), (0,)), ((), ())), preferred_element_type=jnp.float32)


def _const_spec(shape):
    nd = len(shape)
    return pl.BlockSpec(shape, lambda *_: (0,) * nd, pipeline_mode=pl.Buffered(1))


def _mod_kernel(c_ref, w_ref, b_ref, o_ref):
    c = c_ref[...]
    s = (c * jax.nn.sigmoid(c)).astype(jnp.bfloat16)
    o_ref[...] = _dot(s, w_ref[...].astype(jnp.bfloat16)) + b_ref[...]


def _modulation(cc, w_mod, b_mod):
    tn = 1024
    nt = (N_MOD * D) // tn
    return pl.pallas_call(
        _mod_kernel,
        grid=(DEPTH, nt),
        in_specs=[
            pl.BlockSpec((N_GRP, D), lambda l, j: (0, 0)),
            pl.BlockSpec((None, D, tn), lambda l, j: (l, 0, j)),
            pl.BlockSpec((None, 1, tn), lambda l, j: (l, 0, j)),
        ],
        out_specs=pl.BlockSpec((None, N_GRP, tn), lambda l, j: (l, 0, j)),
        out_shape=jax.ShapeDtypeStruct((DEPTH, N_GRP, N_MOD * D), jnp.float32),
        compiler_params=_cparams(("parallel", "parallel")),
        name="modulation",
    )(cc, w_mod, b_mod.reshape(DEPTH, 1, N_MOD * D))


def _ada_pre(x, mod_ref, g_ref, j, gi):
    shift = mod_ref[3 * j:3 * j + 1, :]
    scale = mod_ref[3 * j + 1:3 * j + 2, :]
    return _rms(x, g_ref[gi:gi + 1, :]) * (1.0 + scale) + shift


def _ada_post(x, y, mod_ref, g_ref, j, gi, res_w):
    gate = mod_ref[3 * j + 2:3 * j + 3, :]
    return x + res_w * gate * _rms(y, g_ref[gi:gi + 1, :])


def _tok_specs():
    return [
        pl.BlockSpec((TM, D), lambda i: (i, 0)),
        pl.BlockSpec((None, N_MOD, D), lambda i: (_grp_of_block(i, TM), 0, 0)),
        _const_spec((6, D)),
    ]


def _ffn_kernel(*refs, j, pair, split):
    refs = list(refs)
    acc_ref = refs.pop()
    out_refs = [refs.pop() for _ in range(2 if split else 1)][::-1]
    is_ctx = pl.program_id(0) < N_CTX // TM
    if pair:
        xa_ref, xb_ref, mod_ref, g_ref, win_ref, wout_ref = refs
        x = jnp.where(is_ctx, xa_ref[...], xb_ref[...])
    else:
        x_ref, mod_ref, g_ref, win_ref, wout_ref = refs
        x = x_ref[...]
    h = _ada_pre(x, mod_ref, g_ref, j, 2 * j).astype(jnp.bfloat16)
    for c in range(FFN_H // FFN_HC):
        lo = c * FFN_HC
        a = _dot(h, win_ref[:, lo:lo + FFN_HC])
        b = _dot(h, win_ref[:, FFN_H + lo:FFN_H + lo + FFN_HC])
        u = ((a * jax.nn.sigmoid(a)) * b).astype(jnp.bfloat16)
        part = _dot(u, wout_ref[lo:lo + FFN_HC, :])
        if c == 0:
            acc_ref[...] = part
        else:
            acc_ref[...] += part
    res = _ada_post(x, acc_ref[...], mod_ref, g_ref, j, 2 * j + 1, 0.5)
    if split:
        @pl.when(is_ctx)
        def _():
            out_refs[0][...] = res

        @pl.when(jnp.logical_not(is_ctx))
        def _():
            out_refs[1][...] = res
    else:
        out_refs[0][...] = res


def _ffn(x, mod_l, g_l, w_in, w_out, l, j, *, split=False):
    pair = isinstance(x, tuple)
    nctx = N_CTX // TM
    rows = N_TOK
    w_idx = (l, j // 2, 0, 0)
    if split:
        out_specs = [pl.BlockSpec((TM, D), lambda i: (jnp.minimum(i, nctx - 1), 0)),
                     pl.BlockSpec((TM, D), lambda i: (jnp.maximum(i - nctx, 0), 0))]
        out_shape = [jax.ShapeDtypeStruct((N_CTX, D), jnp.float32),
                     jax.ShapeDtypeStruct((N_SMP, D), jnp.float32)]
    else:
        out_specs = pl.BlockSpec((TM, D), lambda i: (i, 0))
        out_shape = jax.ShapeDtypeStruct((rows, D), jnp.float32)
    if pair:
        x_specs = [pl.BlockSpec((TM, D), lambda i: (jnp.minimum(i, nctx - 1), 0)),
                   pl.BlockSpec((TM, D), lambda i: (jnp.maximum(i - nctx, 0), 0))]
        x_args = list(x)
    else:
        x_specs = [pl.BlockSpec((TM, D), lambda i: (i, 0))]
        x_args = [x]
    return pl.pallas_call(
        functools.partial(_ffn_kernel, j=j, pair=pair, split=split),
        grid=(rows // TM,),
        in_specs=x_specs + [
            pl.BlockSpec((None, N_MOD, D), lambda i: (_grp_of_block(i, TM), 0, 0)),
            _const_spec((6, D)),
            pl.BlockSpec((None, None, D, 2 * FFN_H), lambda i: w_idx, pipeline_mode=pl.Buffered(1)),
            pl.BlockSpec((None, None, FFN_H, D), lambda i: w_idx, pipeline_mode=pl.Buffered(1))],
        out_specs=out_specs,
        out_shape=out_shape,
        scratch_shapes=[pltpu.VMEM((TM, D), jnp.float32)],
        compiler_params=_cparams(("arbitrary" if split else "parallel",)),
        name=f"ffn{j}",
    )(*x_args, mod_l, g_l, w_in, w_out)


def _rope_lanes(x, cos, sin_signed):
    lane = lax.broadcasted_iota(jnp.int32, x.shape, 1)
    partner = jnp.where((lane % 32) < 16, pltpu.roll(x, LANES - 16, 1), pltpu.roll(x, 16, 1))
    return x * cos + partner * sin_signed


def _rope_rows(x, cos_t, sin_signed_t):
    parts = []
    for g in range(HD // 32):
        parts += [x[g * 32 + 16:g * 32 + 32, :], x[g * 32:g * 32 + 16, :]]
    return x * cos_t + jnp.concatenate(parts, axis=0) * sin_signed_t


def _attn_pre_kernel(x_ref, mod_ref, g_ref, w_ref, wkt_ref, grp_ref, cos_ref, sin_ref, cost_ref, sint_ref,
                     q_ref, kt_ref, v_ref, nq_ref, nk_ref, q2_ref, k2_ref, wb_ref, wktb_ref, *, rope):
    @pl.when(pl.program_id(0) == 0)
    def _():
        wb_ref[...] = w_ref[...].astype(jnp.bfloat16)
        wktb_ref[...] = wkt_ref[...].astype(jnp.bfloat16)

    x = x_ref[...]
    h = _ada_pre(x, mod_ref, g_ref, 1, 2).astype(jnp.bfloat16)
    qscale = DA_DK ** -0.5 * LOG2E
    zq = _dot(h, wb_ref[:, 0:D])
    zkt = _dot_nt(wktb_ref[...], h)
    v_ref[...] = _dot(h, wb_ref[:, 2 * D:3 * D]).astype(v_ref.dtype)
    for hd in range(HEADS):
        sl = slice(hd * HD, (hd + 1) * HD)
        q = zq[:, sl]
        k = zkt[sl, :]
        if rope:
            q = _rope_lanes(q, cos_ref[...], sin_ref[...])
            k = _rope_rows(k, cost_ref[...], sint_ref[...])
        q = q * qscale
        q_ref[:, sl] = q.astype(q_ref.dtype)
        kt_ref[sl, :] = k.astype(kt_ref.dtype)
        q2_ref[:, sl] = (q * q).astype(jnp.bfloat16)
        k2_ref[sl, :] = (k * k).astype(jnp.bfloat16)
    qn = jnp.max(_dot(q2_ref[...], grp_ref[...]), axis=0, keepdims=True)
    kn = jnp.max(_dot_tn(grp_ref[...], k2_ref[...]), axis=1, keepdims=True)
    nq_ref[...] = jnp.broadcast_to(qn, nq_ref.shape)
    nk_ref[...] = jnp.broadcast_to(kn[:N_QK_GROUP, :], nk_ref.shape)


def _attn_pre(x, mod_l, g_l, w_in, w_kt, grp, tables, *, ctx):
    nb, seq = (BATCH, SEQ) if ctx else (DEC_BATCH, DEC_SEQ)
    t = min(TM, seq)
    rows = nb * seq
    base = 0 if ctx else N_CTX // t
    per = seq // t
    ptab = DEC_SEQ // t
    kv_dtype = jnp.float32 if ctx else jnp.bfloat16
    out_spec = pl.BlockSpec((t, D), lambda i: (i, 0))
    return pl.pallas_call(
        functools.partial(_attn_pre_kernel, rope=not ctx),
        grid=(rows // t,),
        in_specs=[
            pl.BlockSpec((t, D), lambda i: (i + base, 0)),
            pl.BlockSpec((None, N_MOD, D), lambda i: (_grp_of_block(i + base, t), 0, 0)),
            _const_spec((6, D)), _const_spec((D, 3 * D)), _const_spec((D, D)), _const_spec((D, LANES)),
            pl.BlockSpec((t, HD), lambda i: (i % ptab, 0)),
            pl.BlockSpec((t, HD), lambda i: (i % ptab, 0)),
            pl.BlockSpec((HD, t), lambda i: (0, i % ptab)),
            pl.BlockSpec((HD, t), lambda i: (0, i % ptab)),
        ],
        out_specs=[out_spec, pl.BlockSpec((None, D, t), lambda i: (i // per, 0, i % per)), out_spec,
                   pl.BlockSpec((None, SUB, LANES), lambda i: (i, 0, 0)),
                   pl.BlockSpec((None, N_QK_GROUP, LANES), lambda i: (i, 0, 0))],
        out_shape=[
            jax.ShapeDtypeStruct((rows, D), jnp.bfloat16),
            jax.ShapeDtypeStruct((nb, D, seq), kv_dtype),
            jax.ShapeDtypeStruct((rows, D), kv_dtype),
            jax.ShapeDtypeStruct((rows // t, SUB, LANES), jnp.float32),
            jax.ShapeDtypeStruct((rows // t, N_QK_GROUP, LANES), jnp.float32),
        ],
        scratch_shapes=[pltpu.VMEM((t, D), jnp.bfloat16), pltpu.VMEM((D, t), jnp.bfloat16),
                        pltpu.VMEM((D, 3 * D), jnp.bfloat16), pltpu.VMEM((D, D), jnp.bfloat16)],
        compiler_params=_cparams(("arbitrary",)),
        name="attn_pre_ctx" if ctx else "attn_pre_smp",
    )(x, mod_l, g_l, w_in, w_kt, grp, *tables)


def _attn_kernel(*refs, lam_init, cache, shifted):
    refs = list(refs)
    e_ref = refs.pop()
    s_ref = refs.pop() if shifted else None
    if cache:
        (x_ref, mod_ref, g_ref, q_ref, k_ref, v_ref, kc_ref, vc_ref,
         lam_ref, sub_ref, w_ref, o_ref, oh_ref) = refs
    else:
        (x_ref, mod_ref, g_ref, q_ref, k_ref, v_ref,
         lam_ref, sub_ref, w_ref, o_ref, oh_ref) = refs
    lp = lam_ref[...]
    lam = (jnp.exp(jnp.sum(lp[0:1] * lp[1:2], axis=-1, keepdims=True))
           - jnp.exp(jnp.sum(lp[2:3] * lp[3:4], axis=-1, keepdims=True)) + lam_init)
    tq = q_ref.shape[0]
    lane = lax.broadcasted_iota(jnp.int32, (tq, HD), 1)
    bf = jnp.bfloat16
    kbw = KB if shifted else min(2 * KB, k_ref.shape[1])
    n_own = k_ref.shape[1] // kbw
    n_kb = n_own + (1 if cache else 0)

    def kcols(kb):
        if kb < n_own:
            return slice(kb * kbw, (kb + 1) * kbw)
        return slice(n_own * kbw, n_own * kbw + PAST)

    def keys_t(sl, kb):
        if kb < n_own:
            return k_ref[sl, kcols(kb)].astype(bf)
        return kc_ref[sl, :].astype(bf)

    def values(sl, kb):
        if kb < n_own:
            return v_ref[kcols(kb), sl].astype(bf)
        return vc_ref[:, sl].astype(bf)

    def head_slice(hd):
        return slice(hd * HD, (hd + 1) * HD)

    def lane_tiles(x, op):
        parts = [x[:, j * LANES:(j + 1) * LANES] for j in range(x.shape[1] // LANES)]
        return functools.reduce(op, parts)

    def stacked_q(hd):
        qh = q_ref[:, head_slice(hd)]
        return jnp.concatenate([jnp.where(lane < DA_DK, qh, jnp.zeros_like(qh)),
                                jnp.where(lane >= DA_DK, qh, jnp.zeros_like(qh))], axis=0)

    st = [dict() for _ in range(HEADS)]
    lag = 1 if shifted else 0
    for slot in range(HEADS + lag + 1):
        h1, h2, h3 = slot, slot - lag, slot - lag - 1
        if h1 < HEADS:
            st[h1]["qq"] = stacked_q(h1)
        if 0 <= h3 < HEADS:
            tot = st[h3]["tot"]
            c_col = lam * tot[:tq] / tot[tq:]
            st[h3]["c"] = {w: jnp.broadcast_to(c_col, (tq, w)).astype(bf)
                           for w in {kbw, PAST if cache else kbw}}
        for kb in range(n_kb):
            if h1 < HEADS:
                s = _dot(st[h1]["qq"], keys_t(head_slice(h1), kb))
                if shifted:
                    s_ref[h1 % 2, :, kcols(kb)] = s
                    mp = lane_tiles(s, jnp.maximum)
                    st[h1]["m"] = mp if kb == 0 else jnp.maximum(st[h1]["m"], mp)
            if 0 <= h2 < HEADS:
                if shifted:
                    e = jnp.exp2(s_ref[h2 % 2, :, kcols(kb)] - st[h2]["mrow"])
                else:
                    e = jnp.exp2(s)
                ep = lane_tiles(e, lambda a, b: a + b)
                st[h2]["l"] = ep if kb == 0 else st[h2]["l"] + ep
                e_ref[h2 % 2, :, kcols(kb)] = e.astype(bf)
            if 0 <= h3 < HEADS:
                cols = kcols(kb)
                c = st[h3]["c"][cols.stop - cols.start]
                a = e_ref[h3 % 2, :tq, cols] - c * e_ref[h3 % 2, tq:, cols]
                part = _dot(a, values(head_slice(h3), kb))
                st[h3]["o"] = part if kb == 0 else st[h3]["o"] + part
        if shifted and h1 < HEADS:
            st[h1]["mrow"] = jnp.max(st[h1]["m"], axis=-1, keepdims=True)
        if 0 <= h2 < HEADS:
            st[h2]["tot"] = jnp.sum(st[h2]["l"], axis=-1, keepdims=True)
        if 0 <= h3 < HEADS:
            o = st[h3]["o"] * (1.0 / st[h3]["tot"][:tq])
            o = _rms(o, sub_ref[...]) * (1.0 - lam_init)
            oh_ref[:, head_slice(h3)] = o.astype(bf)
            st[h3].clear()
    y = _dot(oh_ref[...], w_ref[...])
    o_ref[...] = _ada_post(x_ref[...], y, mod_ref, g_ref, 1, 3, 1.0)


def _attn(x, mod_l, g_l, q, k, v, kc, vc, lam_p, subln, w_out, *, ctx, lam_init, shifted):
    nb, seq = (BATCH, SEQ) if ctx else (DEC_BATCH, DEC_SEQ)
    rows = nb * seq
    tq = min(TQ if shifted else 2 * TQ, seq)
    base = 0 if ctx else N_CTX // tq
    nq = seq // tq
    assert seq % KB == 0 and PAST == KB
    n_keys = seq if ctx else seq + PAST
    row_spec = pl.BlockSpec((tq, D), lambda b, i: (b * nq + i, 0))
    in_specs = [
        pl.BlockSpec((tq, D), lambda b, i: (base + b * nq + i, 0)),
        pl.BlockSpec((None, N_MOD, D), lambda b, i: (_grp_of_block(base + b * nq + i, tq), 0, 0)),
        pl.BlockSpec((6, D), lambda b, i: (0, 0)),
        row_spec,
        pl.BlockSpec((None, D, seq), lambda b, i: (b, 0, 0)),
        pl.BlockSpec((seq, D), lambda b, i: (b, 0)),
    ]
    args = [x, mod_l, g_l, q, k, v]
    if not ctx:
        in_specs += [pl.BlockSpec((None, D, PAST), lambda b, i: (b, 0, 0)),
                     pl.BlockSpec((None, PAST, D), lambda b, i: (b, 0, 0))]
        args += [kc, vc]
    in_specs += [
        pl.BlockSpec((4, DA_DK), lambda b, i: (0, 0)),
        pl.BlockSpec((1, HD), lambda b, i: (0, 0)),
        pl.BlockSpec((D, D), lambda b, i: (0, 0), pipeline_mode=pl.Buffered(1)),
    ]
    args += [lam_p, subln, w_out]
    return pl.pallas_call(
        functools.partial(_attn_kernel, lam_init=lam_init, cache=not ctx, shifted=shifted),
        grid=(nb, nq),
        in_specs=in_specs,
        out_specs=row_spec,
        out_shape=jax.ShapeDtypeStruct((rows, D), jnp.float32),
        scratch_shapes=[pltpu.VMEM((tq, D), jnp.bfloat16)]
        + ([pltpu.VMEM((2, 2 * tq, n_keys), jnp.float32)] if shifted else [])
        + [pltpu.VMEM((2, 2 * tq, n_keys), jnp.bfloat16)],
        compiler_params=_cparams(("parallel", "parallel")),
        name=("attn_ctx" if ctx else "attn_smp") + ("_shifted" if shifted else ""),
    )(*args)


def _hgrn_pre_kernel(x_ref, mod_ref, g_ref, w_ref, lb_ref, q_ref, i_ref, gt_ref, lf_ref, *, layer):
    x = x_ref[...]
    h = _ada_pre(x, mod_ref, g_ref, 1, 2).astype(jnp.bfloat16)

    def proj(n):
        return _dot(h, w_ref[:, n * D:(n + 1) * D])

    def log_gate(d, ff):
        raw = [lb_ref[l, d:d + 1, :] for l in range(DEPTH)]
        m = functools.reduce(jnp.maximum, raw)
        ex = [jnp.exp(r - m) for r in raw]
        tot = functools.reduce(lambda a, b: a + b, ex)
        soft = [e / tot for e in ex]
        lb = functools.reduce(lambda a, b: a + b, soft[:layer + 1]) - soft[0]
        return jnp.log(lb + (1.0 - lb) * jax.nn.sigmoid(ff))

    ff = proj(1)
    q_ref[...] = proj(0).astype(q_ref.dtype)
    lf_ref[0] = log_gate(0, ff)
    fb = proj(2)
    i_ref[...] = proj(3).astype(i_ref.dtype)
    lf_ref[1] = log_gate(1, fb)
    gt_ref[...] = proj(4).astype(gt_ref.dtype)


def _hgrn_pre(x, mod_l, g_l, w_in, lb_raw, layer):
    row = pl.BlockSpec((TM, D), lambda i: (i, 0))
    return pl.pallas_call(
        functools.partial(_hgrn_pre_kernel, layer=layer),
        grid=(N_TOK // TM,),
        in_specs=_tok_specs() + [_const_spec((D, 5 * D)), _const_spec((DEPTH, 2, D))],
        out_specs=[row, row, row, pl.BlockSpec((2, TM, D), lambda i: (0, i, 0))],
        out_shape=[
            jax.ShapeDtypeStruct((N_TOK, D), jnp.bfloat16),
            jax.ShapeDtypeStruct((N_TOK, D), jnp.bfloat16),
            jax.ShapeDtypeStruct((N_TOK, D), jnp.bfloat16),
            jax.ShapeDtypeStruct((2, N_TOK, D), jnp.float32),
        ],
        compiler_params=_cparams(("parallel",)),
        name="hgrn_pre",
    )(x, mod_l, g_l, w_in, lb_raw)


def _level_masks(rev):
    t = np.arange(CH)[:, None]
    s = np.arange(CH)[None, :]
    out = []
    for p in range(N_LEVEL):
        same = (t >> (p + 1)) == (s >> (p + 1))
        tb, sb = (t >> p) & 1, (s >> p) & 1
        out.append(same & ((tb == 0) & (sb == 1) if rev else (tb == 1) & (sb == 0)))
    out.append(t == s)
    return np.stack(out).astype(np.float32)


def _row_bcast(x, r):
    return jnp.broadcast_to(x[r:r + 1, :], x.shape)


def _rec_cumsum(lf_ref, b_ref, br_ref, rev):
    sub = lax.broadcasted_iota(jnp.int32, (SUB, D), 0)
    order = range(N_SLAB - 1, -1, -1) if rev else range(N_SLAB)
    edge = 0 if rev else SUB - 1
    carry = None
    for v in order:
        rows = slice(v * SUB, (v + 1) * SUB)
        c = lf_ref[rows, :] * LOG2E
        for k in (1, 2, 4):
            if rev:
                c = c + jnp.where(sub < SUB - k, pltpu.roll(c, SUB - k, 0), 0.0)
            else:
                c = c + jnp.where(sub >= k, pltpu.roll(c, k, 0), 0.0)
        if carry is not None:
            c = c + carry
        carry = _row_bcast(c, edge)
        b_ref[rows, :] = c
        br_ref[rows, :] = carry
    return carry


def _rec_fast(lf_ref, q_ref, i_ref, b_ref, b_last, qs_ref, ks_ref, st_ref, o_ref, rev):
    bf = jnp.bfloat16
    half = 0.5 * b_last
    half2 = jnp.concatenate([half, half], axis=0)
    for v2 in range(N_SLAB // 2):
        rows = slice(v2 * 2 * SUB, (v2 + 1) * 2 * SUB)
        k = 1.0 - jnp.exp2(lf_ref[rows, :] * LOG2E)
        d = b_ref[rows, :] - half2
        qs_ref[rows, :] = (q_ref[rows, :].astype(jnp.float32) * jnp.exp2(d)).astype(bf)
        ks_ref[rows, :] = (k * jnp.exp2(-d)).astype(bf)
    scale = jnp.exp2(half[0:1, :])
    decay = scale * scale
    t = lax.broadcasted_iota(jnp.int32, (CH, CH), 0)
    s = lax.broadcasted_iota(jnp.int32, (CH, CH), 1)
    seen = (s >= t) if rev else (s <= t)
    for hd in range(HEADS):
        sl = slice(hd * HD, (hd + 1) * HD)
        qs, ks, vh = qs_ref[:, sl], ks_ref[:, sl], i_ref[:, sl]
        a = jnp.where(seen, _dot_nt(qs, ks), 0.0)
        st = st_ref[hd]
        o = _dot(a.astype(bf), vh) + _dot_nt(qs, (st * scale[:, sl]).astype(bf))
        o_ref[:, sl] = o.astype(o_ref.dtype)
        st_ref[hd] = st * decay[:, sl] + _dot_tn(vh, ks) * scale[:, sl]


def _rec_prepare(lf_ref, q_ref, b_ref, br_ref, b_last, qt_ref, kt_ref, rev):
    bf = jnp.bfloat16
    sub = lax.broadcasted_iota(jnp.int32, (SUB, D), 0)
    zeros = jnp.zeros((SUB, D), jnp.float32)
    for v2 in range(N_SLAB // 2):
        rows = slice(v2 * 2 * SUB, (v2 + 1) * 2 * SUB)
        f = jnp.exp2(lf_ref[rows, :] * LOG2E)
        k = 1.0 - f
        q = q_ref[rows, :].astype(jnp.float32)
        b = b_ref[rows, :]
        qt_ref[0, rows, :] = (q * f).astype(bf)
        kt_ref[0, rows, :] = k.astype(bf)
        half = (slice(0, SUB), slice(SUB, 2 * SUB))
        for p in range(1, N_LEVEL):
            if p >= 3:
                w = p - 3
                es, qside = [], []
                for n, v in enumerate((2 * v2, 2 * v2 + 1)):
                    hi = v & ~((1 << (w + 1)) - 1)
                    vr = hi | (1 << w) if rev else hi | ((1 << w) - 1)
                    r = br_ref[vr * SUB:(vr + 1) * SUB, :]
                    qside.append(((v >> w) & 1) == (0 if rev else 1))
                    es.append(jnp.exp2(b[half[n], :] - r) if qside[n] else jnp.exp2(r - b[half[n], :]))
                if qside[0] == qside[1]:
                    e = jnp.concatenate(es, axis=0)
                    if qside[0]:
                        qt_ref[p, rows, :] = (q * e).astype(bf)
                    else:
                        kt_ref[p, rows, :] = (k * e).astype(bf)
                else:
                    qe = [q[half[n], :] * es[n] if qside[n] else zeros for n in range(2)]
                    ke = [zeros if qside[n] else k[half[n], :] * es[n] for n in range(2)]
                    qt_ref[p, rows, :] = jnp.concatenate(qe, axis=0).astype(bf)
                    kt_ref[p, rows, :] = jnp.concatenate(ke, axis=0).astype(bf)
            else:
                es = []
                for n in range(2):
                    bv = b[half[n], :]
                    if p == 2:
                        r = _row_bcast(bv, 4 if rev else 3)
                    else:
                        lo, hi_r = (2, 6) if rev else (1, 5)
                        r = jnp.where(sub < 4, _row_bcast(bv, lo), _row_bcast(bv, hi_r))
                    es.append(jnp.exp2(-jnp.abs(bv - r)))
                e = jnp.concatenate(es, axis=0)
                qt_ref[p, rows, :] = (q * e).astype(bf)
                kt_ref[p, rows, :] = (k * e).astype(bf)
        bl = jnp.concatenate([b_last, b_last], axis=0)
        qt_ref[N_LEVEL, rows, :] = (q * jnp.exp2(b)).astype(bf)
        kt_ref[N_LEVEL, rows, :] = (k * jnp.exp2(bl - b)).astype(bf)
    return jnp.exp2(b_last[0:1, :])


def _rec_heads(q_ref, i_ref, qt_ref, kt_ref, msk_ref, st_ref, o_ref, decay):
    bf = jnp.bfloat16
    top = N_LEVEL - 1
    for hd in range(HEADS):
        sl = slice(hd * HD, (hd + 1) * HD)
        a = _dot_nt(qt_ref[top, :, sl], kt_ref[top, :, sl])
        a = a + msk_ref[N_LEVEL] * _dot_nt(q_ref[:, sl], kt_ref[0, :, sl])
        for p in range(top):
            a = a + msk_ref[p] * _dot_nt(qt_ref[p, :, sl], kt_ref[p, :, sl])
        vh = i_ref[:, sl]
        st = st_ref[hd]
        o = _dot(a.astype(bf), vh) + _dot_nt(qt_ref[N_LEVEL, :, sl], st.astype(bf))
        o_ref[:, sl] = o.astype(o_ref.dtype)
        st_ref[hd] = st * decay[:, sl] + _dot_tn(vh, kt_ref[N_LEVEL, :, sl])


def _rec_kernel(blkf_ref, blkb_ref, sid_ref, first_ref, last_ref,
                qf_ref, if_ref, lff_ref, qb_ref, ib_ref, lfb_ref, s0_ref, mskf_ref, mskb_ref,
                of_ref, ob_ref, sfin_ref,
                st_ref, bf_ref, brf_ref, qtf_ref, ktf_ref, bb_ref, brb_ref, qtb_ref, ktb_ref):
    step = pl.program_id(0)

    @pl.when(step == 0)
    def _():
        for ref in (qtf_ref, ktf_ref, qtb_ref, ktb_ref):
            ref[...] = jnp.zeros_like(ref)

    @pl.when(first_ref[step] == 1)
    def _():
        st_ref[...] = jnp.zeros_like(st_ref)

    @pl.when(first_ref[step] == 2)
    def _():
        for d in range(2):
            for hd in range(HEADS):
                st_ref[d, hd] = s0_ref[d, hd].T

    last_f = _rec_cumsum(lff_ref, bf_ref, brf_ref, False)
    last_b = _rec_cumsum(lfb_ref, bb_ref, brb_ref, True)
    moderate = jnp.min(jnp.minimum(last_f, last_b)) >= -MAX_FAST_LOG2_DECAY

    @pl.when(moderate)
    def _():
        _rec_fast(lff_ref, qf_ref, if_ref, bf_ref, last_f, qtf_ref.at[N_LEVEL], ktf_ref.at[N_LEVEL],
                  st_ref.at[0], of_ref, False)
        _rec_fast(lfb_ref, qb_ref, ib_ref, bb_ref, last_b, qtb_ref.at[N_LEVEL], ktb_ref.at[N_LEVEL],
                  st_ref.at[1], ob_ref, True)

    @pl.when(jnp.logical_not(moderate))
    def _():
        decay_f = _rec_prepare(lff_ref, qf_ref, bf_ref, brf_ref, last_f, qtf_ref, ktf_ref, False)
        decay_b = _rec_prepare(lfb_ref, qb_ref, bb_ref, brb_ref, last_b, qtb_ref, ktb_ref, True)
        _rec_heads(qf_ref, if_ref, qtf_ref, ktf_ref, mskf_ref, st_ref.at[0], of_ref, decay_f)
        _rec_heads(qb_ref, ib_ref, qtb_ref, ktb_ref, mskb_ref, st_ref.at[1], ob_ref, decay_b)

    @pl.when(last_ref[step] == 1)
    def _():
        for d in range(2):
            for hd in range(HEADS):
                sfin_ref[d, hd] = st_ref[d, hd].T


def _rec_tables():
    blkf, blkb, sid, first, last = [], [], [], [], []
    seqs = [(b * (SEQ // CH), SEQ // CH) for b in range(BATCH)]
    seqs += [(N_CTX // CH + b * (DEC_SEQ // CH), DEC_SEQ // CH) for b in range(DEC_BATCH)]
    for n, (start, nc) in enumerate(seqs):
        ctx = n < BATCH
        for c in range(nc):
            blkf.append(start + c)
            blkb.append(start + nc - 1 - c)
            sid.append(n)
            first.append((1 if ctx else 2) if c == 0 else 0)
            last.append(int(ctx and c == nc - 1))
    return [jnp.asarray(np.asarray(a, np.int32)) for a in (blkf, blkb, sid, first, last)]


def _rec(q, i, lf, s0):
    row_f = pl.BlockSpec((CH, D), lambda s, bkf, bkb, sid, fi, la: (bkf[s], 0))
    row_b = pl.BlockSpec((CH, D), lambda s, bkf, bkb, sid, fi, la: (bkb[s], 0))
    st_shape = (None, 2, HEADS, HD, HD)
    scratch = [pltpu.VMEM((2, HEADS, HD, HD), jnp.float32)]
    for _ in range(2):
        scratch += [
            pltpu.VMEM((CH, D), jnp.float32),
            pltpu.VMEM((CH, D), jnp.float32),
            pltpu.VMEM((N_LEVEL + 1, CH, D), jnp.bfloat16),
            pltpu.VMEM((N_LEVEL + 1, CH, D), jnp.bfloat16),
        ]
    msk_spec = pl.BlockSpec((N_LEVEL + 1, CH, CH), lambda s, *_: (0, 0, 0))
    grid_spec = pltpu.PrefetchScalarGridSpec(
        num_scalar_prefetch=5,
        grid=(N_TOK // CH,),
        in_specs=[
            row_f, row_f,
            pl.BlockSpec((None, CH, D), lambda s, bkf, bkb, sid, fi, la: (0, bkf[s], 0)),
            row_b, row_b,
            pl.BlockSpec((None, CH, D), lambda s, bkf, bkb, sid, fi, la: (1, bkb[s], 0)),
            pl.BlockSpec(st_shape, lambda s, bkf, bkb, sid, fi, la: (jnp.maximum(sid[s] - BATCH, 0), 0, 0, 0, 0)),
            msk_spec, msk_spec,
        ],
        out_specs=[
            row_f, row_b,
            pl.BlockSpec(st_shape, lambda s, bkf, bkb, sid, fi, la: (jnp.minimum(sid[s], BATCH - 1), 0, 0, 0, 0)),
        ],
        scratch_shapes=scratch,
    )
    return pl.pallas_call(
        _rec_kernel,
        grid_spec=grid_spec,
        out_shape=[
            jax.ShapeDtypeStruct((N_TOK, D), jnp.bfloat16),
            jax.ShapeDtypeStruct((N_TOK, D), jnp.bfloat16),
            jax.ShapeDtypeStruct((BATCH, 2, HEADS, HD, HD), jnp.float32),
        ],
        compiler_params=_cparams(("arbitrary",)),
        name="hgrn_rec",
    )(*_rec_tables(), q, i, lf, q, i, lf, s0, jnp.asarray(_level_masks(False)), jnp.asarray(_level_masks(True)))


def _hgrn_out_kernel(x_ref, mod_ref, g_ref, of_ref, ob_ref, gt_ref, gn_ref, w_ref, o_ref, oh_ref):
    for hd in range(HEADS):
        sl = slice(hd * HD, (hd + 1) * HD)
        o = of_ref[:, sl].astype(jnp.float32) + ob_ref[:, sl].astype(jnp.float32)
        o = _rms(o, gn_ref[...]) * jax.nn.sigmoid(gt_ref[:, sl].astype(jnp.float32))
        oh_ref[:, sl] = o.astype(jnp.bfloat16)
    y = _dot(oh_ref[...], w_ref[...])
    o_ref[...] = _ada_post(x_ref[...], y, mod_ref, g_ref, 1, 3, 1.0)


def _hgrn_out(x, mod_l, g_l, o_f, o_b, gt, gnorm, w_out):
    row = pl.BlockSpec((TM, D), lambda i: (i, 0))
    return pl.pallas_call(
        _hgrn_out_kernel,
        grid=(N_TOK // TM,),
        in_specs=_tok_specs() + [row, row, row, _const_spec((1, HD)), _const_spec((D, D))],
        out_specs=row,
        out_shape=jax.ShapeDtypeStruct((N_TOK, D), jnp.float32),
        scratch_shapes=[pltpu.VMEM((TM, D), jnp.bfloat16)],
        compiler_params=_cparams(("parallel",)),
        name="hgrn_out",
    )(x, mod_l, g_l, o_f, o_b, gt, gnorm, w_out)


def _rope_tables():
    pos = np.arange(DEC_SEQ)
    row = (pos // GRID_W).astype(np.float64)
    col = (pos % GRID_W).astype(np.float64)
    nf = DA_DK // 4
    inv = ROPE_THETA ** (-np.arange(nf, dtype=np.float64) / nf)
    lane = np.arange(HD)
    axis = (lane % DA_DK) // (2 * nf)
    ang = np.where(axis[None, :] == 0, row[:, None], col[:, None]) * inv[lane % nf][None, :]
    sign = np.where((lane % (2 * nf)) < nf, -1.0, 1.0)
    cos, sin = np.cos(ang), np.sin(ang) * sign[None, :]
    return tuple(jnp.asarray(a, jnp.float32) for a in (cos, sin, cos.T, sin.T))


def kernel(x_prompt, x_sample, cache_k, cache_v, state_hgrn, c, c_ctx, w_mod, b_mod, norm_g,
           ffn_w_in, ffn_w_out, attn_w_in, attn_w_out, attn_lambda, attn_subln,
           hgrn_w_in, hgrn_w_out, hgrn_lower_bounds, hgrn_gnorm):
    x = (x_prompt.reshape(N_CTX, D), x_sample.reshape(N_SMP, D))
    cc = jnp.concatenate([c_ctx[None, :], c, jnp.zeros((N_GRP - 1 - DEC_BATCH, D), jnp.float32)], axis=0)
    mod = _modulation(cc, w_mod, b_mod).reshape(DEPTH, N_GRP, N_MOD, D)
    tables = _rope_tables()

    ks, vs, states = [], [], []
    for l in range(DEPTH):
        mod_l, g_l = mod[l], norm_g[l]
        x = _ffn(x, mod_l, g_l, ffn_w_in, ffn_w_out, l, 0)
        if l % 2 == 0:
            a = l // 2
            lam_init = 0.8 - 0.6 * math.exp(-0.3 * l)
            w_kt = attn_w_in[a][:, D:2 * D].T
            w_out = attn_w_out[a]
            subln = attn_subln[a].reshape(1, HD)
            grp = jnp.asarray(np.arange(D)[:, None] // DA_DK == np.arange(LANES)[None, :], jnp.bfloat16)
            qc, kt_new, vc_new, nqc, nkc = _attn_pre(x, mod_l, g_l, attn_w_in[a], w_kt, grp, tables, ctx=True)
            qs, kt_smp, v_smp, nqs, nks = _attn_pre(x, mod_l, g_l, attn_w_in[a], w_kt, grp, tables, ctx=False)
            kct = jnp.transpose(cache_k[:, a], (0, 2, 3, 4, 1)).reshape(DEC_BATCH, D, PAST)
            vcache = cache_v[:, a].reshape(DEC_BATCH, PAST, D)
            lam_p = attn_lambda[a]

            def score_bound_ok(nq, nk, n_seq, extra_k2=None):
                q2 = jnp.max(nq[:, 0, :N_QK_GROUP].reshape(n_seq, -1, N_QK_GROUP), axis=1)
                k2 = jnp.max(nk[:, :, 0].reshape(n_seq, -1, N_QK_GROUP), axis=1)
                if extra_k2 is not None:
                    k2 = jnp.maximum(k2, extra_k2)
                return jnp.max(q2 * k2) <= MAX_UNSHIFTED_LOG2_SCORE ** 2

            def attn_ctx(shifted):
                return lambda: _attn(x, mod_l, g_l, qc, kt_new, vc_new, None, None, lam_p, subln, w_out,
                                     ctx=True, lam_init=lam_init, shifted=shifted)

            def attn_smp(shifted):
                return lambda: _attn(x, mod_l, g_l, qs, kt_smp, v_smp, kct, vcache, lam_p, subln, w_out,
                                     ctx=False, lam_init=lam_init, shifted=shifted)

            cache_k2 = jnp.max(jnp.sum(jnp.square(cache_k[:, a]), axis=-1), axis=1).reshape(DEC_BATCH, N_QK_GROUP)
            xc = lax.cond(score_bound_ok(nqc, nkc, BATCH), attn_ctx(False), attn_ctx(True))
            xs = lax.cond(score_bound_ok(nqs, nks, DEC_BATCH, cache_k2), attn_smp(False), attn_smp(True))
            x = (xc, xs)
            ks.append(jnp.transpose(kt_new.reshape(BATCH, HEADS, 2, DA_DK, SEQ), (0, 4, 1, 2, 3)))
            vs.append(vc_new.reshape(BATCH, SEQ, HEADS, HD))
        else:
            r = l // 2
            q, i, gt, lf = _hgrn_pre(x, mod_l, g_l, hgrn_w_in[r], hgrn_lower_bounds, l)
            s0 = state_hgrn[:, r].astype(jnp.float32)
            o_f, o_b, s_fin = _rec(q, i, lf, s0)
            x = _hgrn_out(x, mod_l, g_l, o_f, o_b, gt, hgrn_gnorm[r].reshape(1, HD), hgrn_w_out[r])
            states.append(s_fin)
        if l < DEPTH - 1:
            x = _ffn(x, mod_l, g_l, ffn_w_in, ffn_w_out, l, 2)

    assert not isinstance(x, tuple)
    y_prompt, y_sample = _ffn(x, mod_l, g_l, ffn_w_in, ffn_w_out, DEPTH - 1, 2, split=True)
    return (y_prompt.reshape(BATCH, SEQ, D), y_sample.reshape(DEC_BATCH, DEC_SEQ, D),
            jnp.stack(ks, axis=1), jnp.stack(vs, axis=1), jnp.stack(states, axis=1))
```

```python
import functools
import math

import numpy as np
import jax
import jax.numpy as jnp
from jax import lax
from jax.experimental import pallas as pl
from jax.experimental.pallas import tpu as pltpu

D = 1024
BATCH = 16
SEQ = 256
DEPTH = 2
DEC_BATCH = 4
DEC_SEQ = 2048
PAST = 256
GRID_W = 64
N_MOD = 9
EPS = 1e-6
HEADS = 8
HD = 128
DA_DK = 64
ROPE_THETA = 10000.0
FFN_H = 2816
LOG2E = 1.4426950408889634

N_CTX = BATCH * SEQ
N_SMP = DEC_BATCH * DEC_SEQ
N_TOK = N_CTX + N_SMP
N_GRP = 8

VMEM_LIMIT = 56 * 1024 * 1024
LANES = 128
SUB = 8

TM = 512
FFN_HC = 256
TQ = 256
KB = 256
N_QK_GROUP = 2 * HEADS
MAX_UNSHIFTED_LOG2_SCORE = 60.0
CH = 128
N_SLAB = CH // SUB
N_LEVEL = 7
MAX_FAST_LOG2_DECAY = 200.0


def _cparams(sem):
    return pltpu.CompilerParams(dimension_semantics=sem, vmem_limit_bytes=VMEM_LIMIT)


def _grp_of_block(i, rows):
    nctx = N_CTX // rows
    per = DEC_SEQ // rows
    return jnp.where(i < nctx, 0, 1 + (i - nctx) // per)


def _rms(x, g):
    return (x * lax.rsqrt(jnp.mean(x * x, axis=-1, keepdims=True) + EPS)) * g


def _dot(a, b):
    return lax.dot_general(a, b, (((1,), (0,)), ((), ())), preferred_element_type=jnp.float32)


def _dot_nt(a, b):
    return lax.dot_general(a, b, (((1,), (1,)), ((), ())), preferred_element_type=jnp.float32)


def _dot_tn(a, b):
    return lax.dot_general(a, b, (((0,), (0,)), ((), ())), preferred_element_type=jnp.float32)


def _const_spec(shape):
    nd = len(shape)
    return pl.BlockSpec(shape, lambda *_: (0,) * nd, pipeline_mode=pl.Buffered(1))


def _mod_kernel(c_ref, w_ref, b_ref, o_ref):
    c = c_ref[...]
    s = (c * jax.nn.sigmoid(c)).astype(jnp.bfloat16)
    o_ref[...] = _dot(s, w_ref[...].astype(jnp.bfloat16)) + b_ref[...]


def _modulation(cc, w_mod, b_mod):
    tn = 1024
    nt = (N_MOD * D) // tn
    return pl.pallas_call(
        _mod_kernel,
        grid=(DEPTH, nt),
        in_specs=[
            pl.BlockSpec((N_GRP, D), lambda l, j: (0, 0)),
            pl.BlockSpec((None, D, tn), lambda l, j: (l, 0, j)),
            pl.BlockSpec((None, 1, tn), lambda l, j: (l, 0, j)),
        ],
        out_specs=pl.BlockSpec((None, N_GRP, tn), lambda l, j: (l, 0, j)),
        out_shape=jax.ShapeDtypeStruct((DEPTH, N_GRP, N_MOD * D), jnp.float32),
        compiler_params=_cparams(("parallel", "parallel")),
        name="modulation",
    )(cc, w_mod, b_mod.reshape(DEPTH, 1, N_MOD * D))


def _ada_pre(x, mod_ref, g_ref, j, gi):
    shift = mod_ref[3 * j:3 * j + 1, :]
    scale = mod_ref[3 * j + 1:3 * j + 2, :]
    return _rms(x, g_ref[gi:gi + 1, :]) * (1.0 + scale) + shift


def _ada_post(x, y, mod_ref, g_ref, j, gi, res_w):
    gate = mod_ref[3 * j + 2:3 * j + 3, :]
    return x + res_w * gate * _rms(y, g_ref[gi:gi + 1, :])


def _tok_specs():
    return [
        pl.BlockSpec((TM, D), lambda i: (i, 0)),
        pl.BlockSpec((None, N_MOD, D), lambda i: (_grp_of_block(i, TM), 0, 0)),
        _const_spec((6, D)),
    ]


def _ffn_kernel(*refs, j, pair, split):
    refs = list(refs)
    acc_ref = refs.pop()
    out_refs = [refs.pop() for _ in range(2 if split else 1)][::-1]
    is_ctx = pl.program_id(0) < N_CTX // TM
    if pair:
        xa_ref, xb_ref, mod_ref, g_ref, win_ref, wout_ref = refs
        x = jnp.where(is_ctx, xa_ref[...], xb_ref[...])
    else:
        x_ref, mod_ref, g_ref, win_ref, wout_ref = refs
        x = x_ref[...]
    h = _ada_pre(x, mod_ref, g_ref, j, 2 * j).astype(jnp.bfloat16)
    for c in range(FFN_H // FFN_HC):
        lo = c * FFN_HC
        a = _dot(h, win_ref[:, lo:lo + FFN_HC])
        b = _dot(h, win_ref[:, FFN_H + lo:FFN_H + lo + FFN_HC])
        u = ((a * jax.nn.sigmoid(a)) * b).astype(jnp.bfloat16)
        part = _dot(u, wout_ref[lo:lo + FFN_HC, :])
        if c == 0:
            acc_ref[...] = part
        else:
            acc_ref[...] += part
    res = _ada_post(x, acc_ref[...], mod_ref, g_ref, j, 2 * j + 1, 0.5)
    if split:
        @pl.when(is_ctx)
        def _():
            out_refs[0][...] = res

        @pl.when(jnp.logical_not(is_ctx))
        def _():
            out_refs[1][...] = res
    else:
        out_refs[0][...] = res


def _ffn(x, mod_l, g_l, w_in, w_out, l, j, *, split=False):
    pair = isinstance(x, tuple)
    nctx = N_CTX // TM
    rows = N_TOK
    w_idx = (l, j // 2, 0, 0)
    if split:
        out_specs = [pl.BlockSpec((TM, D), lambda i: (jnp.minimum(i, nctx - 1), 0)),
                     pl.BlockSpec((TM, D), lambda i: (jnp.maximum(i - nctx, 0), 0))]
        out_shape = [jax.ShapeDtypeStruct((N_CTX, D), jnp.float32),
                     jax.ShapeDtypeStruct((N_SMP, D), jnp.float32)]
    else:
        out_specs = pl.BlockSpec((TM, D), lambda i: (i, 0))
        out_shape = jax.ShapeDtypeStruct((rows, D), jnp.float32)
    if pair:
        x_specs = [pl.BlockSpec((TM, D), lambda i: (jnp.minimum(i, nctx - 1), 0)),
                   pl.BlockSpec((TM, D), lambda i: (jnp.maximum(i - nctx, 0), 0))]
        x_args = list(x)
    else:
        x_specs = [pl.BlockSpec((TM, D), lambda i: (i, 0))]
        x_args = [x]
    return pl.pallas_call(
        functools.partial(_ffn_kernel, j=j, pair=pair, split=split),
        grid=(rows // TM,),
        in_specs=x_specs + [
            pl.BlockSpec((None, N_MOD, D), lambda i: (_grp_of_block(i, TM), 0, 0)),
            _const_spec((6, D)),
            pl.BlockSpec((None, None, D, 2 * FFN_H), lambda i: w_idx, pipeline_mode=pl.Buffered(1)),
            pl.BlockSpec((None, None, FFN_H, D), lambda i: w_idx, pipeline_mode=pl.Buffered(1))],
        out_specs=out_specs,
        out_shape=out_shape,
        scratch_shapes=[pltpu.VMEM((TM, D), jnp.float32)],
        compiler_params=_cparams(("arbitrary" if split else "parallel",)),
        name=f"ffn{j}",
    )(*x_args, mod_l, g_l, w_in, w_out)


def _rope_lanes(x, cos, sin_signed):
    lane = lax.broadcasted_iota(jnp.int32, x.shape, 1)
    partner = jnp.where((lane % 32) < 16, pltpu.roll(x, LANES - 16, 1), pltpu.roll(x, 16, 1))
    return x * cos + partner * sin_signed


def _rope_rows(x, cos_t, sin_signed_t):
    parts = []
    for g in range(HD // 32):
        parts += [x[g * 32 + 16:g * 32 + 32, :], x[g * 32:g * 32 + 16, :]]
    return x * cos_t + jnp.concatenate(parts, axis=0) * sin_signed_t


def _attn_pre_kernel(x_ref, mod_ref, g_ref, w_ref, wkt_ref, grp_ref, cos_ref, sin_ref, cost_ref, sint_ref,
                     q_ref, kt_ref, v_ref, nq_ref, nk_ref, q2_ref, k2_ref, *, rope):
    x = x_ref[...]
    h = _ada_pre(x, mod_ref, g_ref, 1, 2).astype(jnp.bfloat16)
    qscale = DA_DK ** -0.5 * LOG2E
    zq = _dot(h, w_ref[:, 0:D])
    zkt = _dot_nt(wkt_ref[...], h)
    v_ref[...] = _dot(h, w_ref[:, 2 * D:3 * D]).astype(v_ref.dtype)
    for hd in range(HEADS):
        sl = slice(hd * HD, (hd + 1) * HD)
        q = zq[:, sl]
        k = zkt[sl, :]
        if rope:
            q = _rope_lanes(q, cos_ref[...], sin_ref[...])
            k = _rope_rows(k, cost_ref[...], sint_ref[...])
        q = q * qscale
        q_ref[:, sl] = q.astype(q_ref.dtype)
        kt_ref[sl, :] = k.astype(kt_ref.dtype)
        q2_ref[:, sl] = (q * q).astype(jnp.bfloat16)
        k2_ref[sl, :] = (k * k).astype(jnp.bfloat16)
    qn = jnp.max(_dot(q2_ref[...], grp_ref[...]), axis=0, keepdims=True)
    kn = jnp.max(_dot_tn(grp_ref[...], k2_ref[...]), axis=1, keepdims=True)
    nq_ref[...] = jnp.broadcast_to(qn, nq_ref.shape)
    nk_ref[...] = jnp.broadcast_to(kn[:N_QK_GROUP, :], nk_ref.shape)


def _attn_pre(x, mod_l, g_l, w_in, w_kt, grp, tables, *, ctx):
    nb, seq = (BATCH, SEQ) if ctx else (DEC_BATCH, DEC_SEQ)
    t = min(TM, seq)
    rows = nb * seq
    base = 0 if ctx else N_CTX // t
    per = seq // t
    ptab = DEC_SEQ // t
    kv_dtype = jnp.float32 if ctx else jnp.bfloat16
    out_spec = pl.BlockSpec((t, D), lambda i: (i, 0))
    return pl.pallas_call(
        functools.partial(_attn_pre_kernel, rope=not ctx),
        grid=(rows // t,),
        in_specs=[
            pl.BlockSpec((t, D), lambda i: (i + base, 0)),
            pl.BlockSpec((None, N_MOD, D), lambda i: (_grp_of_block(i + base, t), 0, 0)),
            _const_spec((6, D)), _const_spec((D, 3 * D)), _const_spec((D, D)), _const_spec((D, LANES)),
            pl.BlockSpec((t, HD), lambda i: (i % ptab, 0)),
            pl.BlockSpec((t, HD), lambda i: (i % ptab, 0)),
            pl.BlockSpec((HD, t), lambda i: (0, i % ptab)),
            pl.BlockSpec((HD, t), lambda i: (0, i % ptab)),
        ],
        out_specs=[out_spec, pl.BlockSpec((None, D, t), lambda i: (i // per, 0, i % per)), out_spec,
                   pl.BlockSpec((None, SUB, LANES), lambda i: (i, 0, 0)),
                   pl.BlockSpec((None, N_QK_GROUP, LANES), lambda i: (i, 0, 0))],
        out_shape=[
            jax.ShapeDtypeStruct((rows, D), jnp.bfloat16),
            jax.ShapeDtypeStruct((nb, D, seq), kv_dtype),
            jax.ShapeDtypeStruct((rows, D), kv_dtype),
            jax.ShapeDtypeStruct((rows // t, SUB, LANES), jnp.float32),
            jax.ShapeDtypeStruct((rows // t, N_QK_GROUP, LANES), jnp.float32),
        ],
        scratch_shapes=[pltpu.VMEM((t, D), jnp.bfloat16), pltpu.VMEM((D, t), jnp.bfloat16)],
        compiler_params=_cparams(("parallel",)),
        name="attn_pre_ctx" if ctx else "attn_pre_smp",
    )(x, mod_l, g_l, w_in, w_kt, grp, *tables)


def _attn_kernel(*refs, lam_init, cache, shifted):
    refs = list(refs)
    e_ref = refs.pop()
    s_ref = refs.pop() if shifted else None
    if cache:
        (x_ref, mod_ref, g_ref, q_ref, k_ref, v_ref, kc_ref, vc_ref,
         lam_ref, sub_ref, w_ref, o_ref, oh_ref) = refs
    else:
        (x_ref, mod_ref, g_ref, q_ref, k_ref, v_ref,
         lam_ref, sub_ref, w_ref, o_ref, oh_ref) = refs
    lp = lam_ref[...]
    lam = (jnp.exp(jnp.sum(lp[0:1] * lp[1:2], axis=-1, keepdims=True))
           - jnp.exp(jnp.sum(lp[2:3] * lp[3:4], axis=-1, keepdims=True)) + lam_init)
    tq = q_ref.shape[0]
    lane = lax.broadcasted_iota(jnp.int32, (tq, HD), 1)
    bf = jnp.bfloat16
    kbw = KB if shifted else min(2 * KB, k_ref.shape[1])
    n_own = k_ref.shape[1] // kbw
    n_kb = n_own + (1 if cache else 0)

    def kcols(kb):
        if kb < n_own:
            return slice(kb * kbw, (kb + 1) * kbw)
        return slice(n_own * kbw, n_own * kbw + PAST)

    def keys_t(sl, kb):
        if kb < n_own:
            return k_ref[sl, kcols(kb)].astype(bf)
        return kc_ref[sl, :].astype(bf)

    def values(sl, kb):
        if kb < n_own:
            return v_ref[kcols(kb), sl].astype(bf)
        return vc_ref[:, sl].astype(bf)

    def head_slice(hd):
        return slice(hd * HD, (hd + 1) * HD)

    def lane_tiles(x, op):
        parts = [x[:, j * LANES:(j + 1) * LANES] for j in range(x.shape[1] // LANES)]
        return functools.reduce(op, parts)

    def stacked_q(hd):
        qh = q_ref[:, head_slice(hd)]
        return jnp.concatenate([jnp.where(lane < DA_DK, qh, jnp.zeros_like(qh)),
                                jnp.where(lane >= DA_DK, qh, jnp.zeros_like(qh))], axis=0)

    st = [dict() for _ in range(HEADS)]
    lag = 1 if shifted else 0
    for slot in range(HEADS + lag + 1):
        h1, h2, h3 = slot, slot - lag, slot - lag - 1
        if h1 < HEADS:
            st[h1]["qq"] = stacked_q(h1)
        if 0 <= h3 < HEADS:
            tot = st[h3]["tot"]
            c_col = lam * tot[:tq] / tot[tq:]
            st[h3]["c"] = {w: jnp.broadcast_to(c_col, (tq, w)).astype(bf)
                           for w in {kbw, PAST if cache else kbw}}
        for kb in range(n_kb):
            if h1 < HEADS:
                s = _dot(st[h1]["qq"], keys_t(head_slice(h1), kb))
                if shifted:
                    s_ref[h1 % 2, :, kcols(kb)] = s
                    mp = lane_tiles(s, jnp.maximum)
                    st[h1]["m"] = mp if kb == 0 else jnp.maximum(st[h1]["m"], mp)
            if 0 <= h2 < HEADS:
                if shifted:
                    e = jnp.exp2(s_ref[h2 % 2, :, kcols(kb)] - st[h2]["mrow"])
                else:
                    e = jnp.exp2(s)
                ep = lane_tiles(e, lambda a, b: a + b)
                st[h2]["l"] = ep if kb == 0 else st[h2]["l"] + ep
                e_ref[h2 % 2, :, kcols(kb)] = e.astype(bf)
            if 0 <= h3 < HEADS:
                cols = kcols(kb)
                c = st[h3]["c"][cols.stop - cols.start]
                a = e_ref[h3 % 2, :tq, cols] - c * e_ref[h3 % 2, tq:, cols]
                part = _dot(a, values(head_slice(h3), kb))
                st[h3]["o"] = part if kb == 0 else st[h3]["o"] + part
        if shifted and h1 < HEADS:
            st[h1]["mrow"] = jnp.max(st[h1]["m"], axis=-1, keepdims=True)
        if 0 <= h2 < HEADS:
            st[h2]["tot"] = jnp.sum(st[h2]["l"], axis=-1, keepdims=True)
        if 0 <= h3 < HEADS:
            o = st[h3]["o"] * (1.0 / st[h3]["tot"][:tq])
            o = _rms(o, sub_ref[...]) * (1.0 - lam_init)
            oh_ref[:, head_slice(h3)] = o.astype(bf)
            st[h3].clear()
    y = _dot(oh_ref[...], w_ref[...])
    o_ref[...] = _ada_post(x_ref[...], y, mod_ref, g_ref, 1, 3, 1.0)


def _attn(x, mod_l, g_l, q, k, v, kc, vc, lam_p, subln, w_out, *, ctx, lam_init, shifted):
    nb, seq = (BATCH, SEQ) if ctx else (DEC_BATCH, DEC_SEQ)
    rows = nb * seq
    tq = min(TQ if shifted else 2 * TQ, seq)
    base = 0 if ctx else N_CTX // tq
    nq = seq // tq
    assert seq % KB == 0 and PAST == KB
    n_keys = seq if ctx else seq + PAST
    row_spec = pl.BlockSpec((tq, D), lambda b, i: (b * nq + i, 0))
    in_specs = [
        pl.BlockSpec((tq, D), lambda b, i: (base + b * nq + i, 0)),
        pl.BlockSpec((None, N_MOD, D), lambda b, i: (_grp_of_block(base + b * nq + i, tq), 0, 0)),
        pl.BlockSpec((6, D), lambda b, i: (0, 0)),
        row_spec,
        pl.BlockSpec((None, D, seq), lambda b, i: (b, 0, 0)),
        pl.BlockSpec((seq, D), lambda b, i: (b, 0)),
    ]
    args = [x, mod_l, g_l, q, k, v]
    if not ctx:
        in_specs += [pl.BlockSpec((None, D, PAST), lambda b, i: (b, 0, 0)),
                     pl.BlockSpec((None, PAST, D), lambda b, i: (b, 0, 0))]
        args += [kc, vc]
    in_specs += [
        pl.BlockSpec((4, DA_DK), lambda b, i: (0, 0)),
        pl.BlockSpec((1, HD), lambda b, i: (0, 0)),
        pl.BlockSpec((D, D), lambda b, i: (0, 0), pipeline_mode=pl.Buffered(1)),
    ]
    args += [lam_p, subln, w_out]
    return pl.pallas_call(
        functools.partial(_attn_kernel, lam_init=lam_init, cache=not ctx, shifted=shifted),
        grid=(nb, nq),
        in_specs=in_specs,
        out_specs=row_spec,
        out_shape=jax.ShapeDtypeStruct((rows, D), jnp.float32),
        scratch_shapes=[pltpu.VMEM((tq, D), jnp.bfloat16)]
        + ([pltpu.VMEM((2, 2 * tq, n_keys), jnp.float32)] if shifted else [])
        + [pltpu.VMEM((2, 2 * tq, n_keys), jnp.bfloat16)],
        compiler_params=_cparams(("parallel", "parallel")),
        name=("attn_ctx" if ctx else "attn_smp") + ("_shifted" if shifted else ""),
    )(*args)


def _hgrn_pre_kernel(x_ref, mod_ref, g_ref, w_ref, lb_ref, q_ref, i_ref, gt_ref, lf_ref, cs_ref, *, layer):
    x = x_ref[...]
    h = _ada_pre(x, mod_ref, g_ref, 1, 2).astype(jnp.bfloat16)

    def proj(n):
        return _dot(h, w_ref[:, n * D:(n + 1) * D])

    def log_gate(d, ff):
        raw = [lb_ref[l, d:d + 1, :] for l in range(DEPTH)]
        m = functools.reduce(jnp.maximum, raw)
        ex = [jnp.exp(r - m) for r in raw]
        tot = functools.reduce(lambda a, b: a + b, ex)
        soft = [e / tot for e in ex]
        lb = functools.reduce(lambda a, b: a + b, soft[:layer + 1]) - soft[0]
        return jnp.log(lb + (1.0 - lb) * jax.nn.sigmoid(ff))

    ff = proj(1)
    q_ref[...] = proj(0).astype(q_ref.dtype)
    lf_ref[0] = log_gate(0, ff)
    fb = proj(2)
    i_ref[...] = proj(3).astype(i_ref.dtype)
    lf_ref[1] = log_gate(1, fb)
    gt_ref[...] = proj(4).astype(gt_ref.dtype)
    for d in range(2):
        for c in range(x_ref.shape[0] // CH):
            cs_ref[c, d:d + 1, :] = jnp.sum(lf_ref[d, c * CH:(c + 1) * CH, :], axis=0, keepdims=True)


def _hgrn_pre(x, mod_l, g_l, w_in, lb_raw, layer):
    row = pl.BlockSpec((TM, D), lambda i: (i, 0))
    return pl.pallas_call(
        functools.partial(_hgrn_pre_kernel, layer=layer),
        grid=(N_TOK // TM,),
        in_specs=_tok_specs() + [_const_spec((D, 5 * D)), _const_spec((DEPTH, 2, D))],
        out_specs=[row, row, row, pl.BlockSpec((2, TM, D), lambda i: (0, i, 0)),
                   pl.BlockSpec((TM // CH, 2, D), lambda i: (i, 0, 0))],
        out_shape=[
            jax.ShapeDtypeStruct((N_TOK, D), jnp.bfloat16),
            jax.ShapeDtypeStruct((N_TOK, D), jnp.bfloat16),
            jax.ShapeDtypeStruct((N_TOK, D), jnp.bfloat16),
            jax.ShapeDtypeStruct((2, N_TOK, D), jnp.float32),
            jax.ShapeDtypeStruct((N_TOK // CH, 2, D), jnp.float32),
        ],
        compiler_params=_cparams(("parallel",)),
        name="hgrn_pre",
    )(x, mod_l, g_l, w_in, lb_raw)


def _level_masks(rev):
    t = np.arange(CH)[:, None]
    s = np.arange(CH)[None, :]
    out = []
    for p in range(N_LEVEL):
        same = (t >> (p + 1)) == (s >> (p + 1))
        tb, sb = (t >> p) & 1, (s >> p) & 1
        out.append(same & ((tb == 0) & (sb == 1) if rev else (tb == 1) & (sb == 0)))
    out.append(t == s)
    return np.stack(out).astype(np.float32)


def _row_bcast(x, r):
    return jnp.broadcast_to(x[r:r + 1, :], x.shape)


def _rec_cumsum(lf_ref, b_ref, br_ref, rev):
    sub = lax.broadcasted_iota(jnp.int32, (SUB, D), 0)
    order = range(N_SLAB - 1, -1, -1) if rev else range(N_SLAB)
    edge = 0 if rev else SUB - 1
    carry = None
    for v in order:
        rows = slice(v * SUB, (v + 1) * SUB)
        c = lf_ref[rows, :] * LOG2E
        for k in (1, 2, 4):
            if rev:
                c = c + jnp.where(sub < SUB - k, pltpu.roll(c, SUB - k, 0), 0.0)
            else:
                c = c + jnp.where(sub >= k, pltpu.roll(c, k, 0), 0.0)
        if carry is not None:
            c = c + carry
        carry = _row_bcast(c, edge)
        b_ref[rows, :] = c
        br_ref[rows, :] = carry
    return carry


def _rec_fast(lf_ref, q_ref, i_ref, b_ref, b_last, qs_ref, ks_ref, st_ref, o_ref, rev):
    bf = jnp.bfloat16
    half = 0.5 * b_last
    half2 = jnp.concatenate([half, half], axis=0)
    for v2 in range(N_SLAB // 2):
        rows = slice(v2 * 2 * SUB, (v2 + 1) * 2 * SUB)
        k = 1.0 - jnp.exp2(lf_ref[rows, :] * LOG2E)
        d = b_ref[rows, :] - half2
        qs_ref[rows, :] = (q_ref[rows, :].astype(jnp.float32) * jnp.exp2(d)).astype(bf)
        ks_ref[rows, :] = (k * jnp.exp2(-d)).astype(bf)
    scale = jnp.exp2(half[0:1, :])
    decay = scale * scale
    t = lax.broadcasted_iota(jnp.int32, (CH, CH), 0)
    s = lax.broadcasted_iota(jnp.int32, (CH, CH), 1)
    seen = (s >= t) if rev else (s <= t)
    for hd in range(HEADS):
        sl = slice(hd * HD, (hd + 1) * HD)
        qs, ks, vh = qs_ref[:, sl], ks_ref[:, sl], i_ref[:, sl]
        a = jnp.where(seen, _dot_nt(qs, ks), 0.0)
        st = st_ref[hd]
        o = _dot(a.astype(bf), vh) + _dot_nt(qs, (st * scale[:, sl]).astype(bf))
        o_ref[:, sl] = o.astype(o_ref.dtype)
        st_ref[hd] = st * decay[:, sl] + _dot_tn(vh, ks) * scale[:, sl]


def _rec_prepare(lf_ref, q_ref, b_ref, br_ref, b_last, qt_ref, kt_ref, rev):
    bf = jnp.bfloat16
    sub = lax.broadcasted_iota(jnp.int32, (SUB, D), 0)
    zeros = jnp.zeros((SUB, D), jnp.float32)
    for v2 in range(N_SLAB // 2):
        rows = slice(v2 * 2 * SUB, (v2 + 1) * 2 * SUB)
        f = jnp.exp2(lf_ref[rows, :] * LOG2E)
        k = 1.0 - f
        q = q_ref[rows, :].astype(jnp.float32)
        b = b_ref[rows, :]
        qt_ref[0, rows, :] = (q * f).astype(bf)
        kt_ref[0, rows, :] = k.astype(bf)
        half = (slice(0, SUB), slice(SUB, 2 * SUB))
        for p in range(1, N_LEVEL):
            if p >= 3:
                w = p - 3
                es, qside = [], []
                for n, v in enumerate((2 * v2, 2 * v2 + 1)):
                    hi = v & ~((1 << (w + 1)) - 1)
                    vr = hi | (1 << w) if rev else hi | ((1 << w) - 1)
                    r = br_ref[vr * SUB:(vr + 1) * SUB, :]
                    qside.append(((v >> w) & 1) == (0 if rev else 1))
                    es.append(jnp.exp2(b[half[n], :] - r) if qside[n] else jnp.exp2(r - b[half[n], :]))
                if qside[0] == qside[1]:
                    e = jnp.concatenate(es, axis=0)
                    if qside[0]:
                        qt_ref[p, rows, :] = (q * e).astype(bf)
                    else:
                        kt_ref[p, rows, :] = (k * e).astype(bf)
                else:
                    qe = [q[half[n], :] * es[n] if qside[n] else zeros for n in range(2)]
                    ke = [zeros if qside[n] else k[half[n], :] * es[n] for n in range(2)]
                    qt_ref[p, rows, :] = jnp.concatenate(qe, axis=0).astype(bf)
                    kt_ref[p, rows, :] = jnp.concatenate(ke, axis=0).astype(bf)
            else:
                es = []
                for n in range(2):
                    bv = b[half[n], :]
                    if p == 2:
                        r = _row_bcast(bv, 4 if rev else 3)
                    else:
                        lo, hi_r = (2, 6) if rev else (1, 5)
                        r = jnp.where(sub < 4, _row_bcast(bv, lo), _row_bcast(bv, hi_r))
                    es.append(jnp.exp2(-jnp.abs(bv - r)))
                e = jnp.concatenate(es, axis=0)
                qt_ref[p, rows, :] = (q * e).astype(bf)
                kt_ref[p, rows, :] = (k * e).astype(bf)
        bl = jnp.concatenate([b_last, b_last], axis=0)
        qt_ref[N_LEVEL, rows, :] = (q * jnp.exp2(b)).astype(bf)
        kt_ref[N_LEVEL, rows, :] = (k * jnp.exp2(bl - b)).astype(bf)
    return jnp.exp2(b_last[0:1, :])


def _rec_heads(q_ref, i_ref, qt_ref, kt_ref, msk_ref, st_ref, o_ref, decay):
    bf = jnp.bfloat16
    top = N_LEVEL - 1
    for hd in range(HEADS):
        sl = slice(hd * HD, (hd + 1) * HD)
        a = _dot_nt(qt_ref[top, :, sl], kt_ref[top, :, sl])
        a = a + msk_ref[N_LEVEL] * _dot_nt(q_ref[:, sl], kt_ref[0, :, sl])
        for p in range(top):
            a = a + msk_ref[p] * _dot_nt(qt_ref[p, :, sl], kt_ref[p, :, sl])
        vh = i_ref[:, sl]
        st = st_ref[hd]
        o = _dot(a.astype(bf), vh) + _dot_nt(qt_ref[N_LEVEL, :, sl], st.astype(bf))
        o_ref[:, sl] = o.astype(o_ref.dtype)
        st_ref[hd] = st * decay[:, sl] + _dot_tn(vh, kt_ref[N_LEVEL, :, sl])


def _rec_kernel(blkf_ref, blkb_ref, sid_ref, first_ref, last_ref, fast_ref,
                qf_ref, if_ref, lff_ref, qb_ref, ib_ref, lfb_ref, s0_ref, mskf_ref, mskb_ref,
                of_ref, ob_ref, sfin_ref,
                st_ref, bf_ref, brf_ref, qtf_ref, ktf_ref, bb_ref, brb_ref, qtb_ref, ktb_ref):
    step = pl.program_id(0)

    @pl.when(step == 0)
    def _():
        for ref in (qtf_ref, ktf_ref, qtb_ref, ktb_ref):
            ref[...] = jnp.zeros_like(ref)

    @pl.when(first_ref[step] == 1)
    def _():
        st_ref[...] = jnp.zeros_like(st_ref)

    @pl.when(first_ref[step] == 2)
    def _():
        for d in range(2):
            for hd in range(HEADS):
                st_ref[d, hd] = s0_ref[d, hd].T

    moderate = fast_ref[step] == 1

    @pl.when(moderate)
    def _():
        last_f = _rec_cumsum(lff_ref, bf_ref, brf_ref, False)
        last_b = _rec_cumsum(lfb_ref, bb_ref, brb_ref, True)
        _rec_fast(lff_ref, qf_ref, if_ref, bf_ref, last_f, qtf_ref.at[N_LEVEL], ktf_ref.at[N_LEVEL],
                  st_ref.at[0], of_ref, False)
        _rec_fast(lfb_ref, qb_ref, ib_ref, bb_ref, last_b, qtb_ref.at[N_LEVEL], ktb_ref.at[N_LEVEL],
                  st_ref.at[1], ob_ref, True)

    @pl.when(jnp.logical_not(moderate))
    def _():
        last_f = _rec_cumsum(lff_ref, bf_ref, brf_ref, False)
        last_b = _rec_cumsum(lfb_ref, bb_ref, brb_ref, True)
        decay_f = _rec_prepare(lff_ref, qf_ref, bf_ref, brf_ref, last_f, qtf_ref, ktf_ref, False)
        decay_b = _rec_prepare(lfb_ref, qb_ref, bb_ref, brb_ref, last_b, qtb_ref, ktb_ref, True)
        _rec_heads(qf_ref, if_ref, qtf_ref, ktf_ref, mskf_ref, st_ref.at[0], of_ref, decay_f)
        _rec_heads(qb_ref, ib_ref, qtb_ref, ktb_ref, mskb_ref, st_ref.at[1], ob_ref, decay_b)

    @pl.when(last_ref[step] == 1)
    def _():
        for d in range(2):
            for hd in range(HEADS):
                sfin_ref[d, hd] = st_ref[d, hd].T


def _rec_tables():
    blkf, blkb, sid, first, last = [], [], [], [], []
    seqs = [(b * (SEQ // CH), SEQ // CH) for b in range(BATCH)]
    seqs += [(N_CTX // CH + b * (DEC_SEQ // CH), DEC_SEQ // CH) for b in range(DEC_BATCH)]
    for n, (start, nc) in enumerate(seqs):
        ctx = n < BATCH
        for c in range(nc):
            blkf.append(start + c)
            blkb.append(start + nc - 1 - c)
            sid.append(n)
            first.append((1 if ctx else 2) if c == 0 else 0)
            last.append(int(ctx and c == nc - 1))
    return [jnp.asarray(np.asarray(a, np.int32)) for a in (blkf, blkb, sid, first, last)]


def _rec(q, i, lf, chunk_sums, s0):
    tables = _rec_tables()
    low = jnp.min(chunk_sums, axis=-1) * LOG2E
    fast = (low[tables[0], 0] >= -MAX_FAST_LOG2_DECAY) & (low[tables[1], 1] >= -MAX_FAST_LOG2_DECAY)
    tables.append(fast.astype(jnp.int32))
    row_f = pl.BlockSpec((CH, D), lambda s, bkf, bkb, sid, fi, la, fa: (bkf[s], 0))
    row_b = pl.BlockSpec((CH, D), lambda s, bkf, bkb, sid, fi, la, fa: (bkb[s], 0))
    st_shape = (None, 2, HEADS, HD, HD)
    scratch = [pltpu.VMEM((2, HEADS, HD, HD), jnp.float32)]
    for _ in range(2):
        scratch += [
            pltpu.VMEM((CH, D), jnp.float32),
            pltpu.VMEM((CH, D), jnp.float32),
            pltpu.VMEM((N_LEVEL + 1, CH, D), jnp.bfloat16),
            pltpu.VMEM((N_LEVEL + 1, CH, D), jnp.bfloat16),
        ]
    msk_spec = pl.BlockSpec((N_LEVEL + 1, CH, CH), lambda s, *_: (0, 0, 0))
    grid_spec = pltpu.PrefetchScalarGridSpec(
        num_scalar_prefetch=6,
        grid=(N_TOK // CH,),
        in_specs=[
            row_f, row_f,
            pl.BlockSpec((None, CH, D), lambda s, bkf, bkb, sid, fi, la, fa: (0, bkf[s], 0)),
            row_b, row_b,
            pl.BlockSpec((None, CH, D), lambda s, bkf, bkb, sid, fi, la, fa: (1, bkb[s], 0)),
            pl.BlockSpec(st_shape, lambda s, bkf, bkb, sid, fi, la, fa: (jnp.maximum(sid[s] - BATCH, 0), 0, 0, 0, 0)),
            msk_spec, msk_spec,
        ],
        out_specs=[
            row_f, row_b,
            pl.BlockSpec(st_shape, lambda s, bkf, bkb, sid, fi, la, fa: (jnp.minimum(sid[s], BATCH - 1), 0, 0, 0, 0)),
        ],
        scratch_shapes=scratch,
    )
    return pl.pallas_call(
        _rec_kernel,
        grid_spec=grid_spec,
        out_shape=[
            jax.ShapeDtypeStruct((N_TOK, D), jnp.bfloat16),
            jax.ShapeDtypeStruct((N_TOK, D), jnp.bfloat16),
            jax.ShapeDtypeStruct((BATCH, 2, HEADS, HD, HD), jnp.float32),
        ],
        compiler_params=_cparams(("arbitrary",)),
        name="hgrn_rec",
    )(*tables, q, i, lf, q, i, lf, s0, jnp.asarray(_level_masks(False)), jnp.asarray(_level_masks(True)))


def _hgrn_out_kernel(x_ref, mod_ref, g_ref, of_ref, ob_ref, gt_ref, gn_ref, w_ref, o_ref, oh_ref):
    for hd in range(HEADS):
        sl = slice(hd * HD, (hd + 1) * HD)
        o = of_ref[:, sl].astype(jnp.float32) + ob_ref[:, sl].astype(jnp.float32)
        o = _rms(o, gn_ref[...]) * jax.nn.sigmoid(gt_ref[:, sl].astype(jnp.float32))
        oh_ref[:, sl] = o.astype(jnp.bfloat16)
    y = _dot(oh_ref[...], w_ref[...])
    o_ref[...] = _ada_post(x_ref[...], y, mod_ref, g_ref, 1, 3, 1.0)


def _hgrn_out(x, mod_l, g_l, o_f, o_b, gt, gnorm, w_out):
    row = pl.BlockSpec((TM, D), lambda i: (i, 0))
    return pl.pallas_call(
        _hgrn_out_kernel,
        grid=(N_TOK // TM,),
        in_specs=_tok_specs() + [row, row, row, _const_spec((1, HD)), _const_spec((D, D))],
        out_specs=row,
        out_shape=jax.ShapeDtypeStruct((N_TOK, D), jnp.float32),
        scratch_shapes=[pltpu.VMEM((TM, D), jnp.bfloat16)],
        compiler_params=_cparams(("parallel",)),
        name="hgrn_out",
    )(x, mod_l, g_l, o_f, o_b, gt, gnorm, w_out)


def _rope_tables():
    pos = np.arange(DEC_SEQ)
    row = (pos // GRID_W).astype(np.float64)
    col = (pos % GRID_W).astype(np.float64)
    nf = DA_DK // 4
    inv = ROPE_THETA ** (-np.arange(nf, dtype=np.float64) / nf)
    lane = np.arange(HD)
    axis = (lane % DA_DK) // (2 * nf)
    ang = np.where(axis[None, :] == 0, row[:, None], col[:, None]) * inv[lane % nf][None, :]
    sign = np.where((lane % (2 * nf)) < nf, -1.0, 1.0)
    cos, sin = np.cos(ang), np.sin(ang) * sign[None, :]
    return tuple(jnp.asarray(a, jnp.float32) for a in (cos, sin, cos.T, sin.T))


def kernel(x_prompt, x_sample, cache_k, cache_v, state_hgrn, c, c_ctx, w_mod, b_mod, norm_g,
           ffn_w_in, ffn_w_out, attn_w_in, attn_w_out, attn_lambda, attn_subln,
           hgrn_w_in, hgrn_w_out, hgrn_lower_bounds, hgrn_gnorm):
    x = (x_prompt.reshape(N_CTX, D), x_sample.reshape(N_SMP, D))
    cc = jnp.concatenate([c_ctx[None, :], c, jnp.zeros((N_GRP - 1 - DEC_BATCH, D), jnp.float32)], axis=0)
    mod = _modulation(cc, w_mod, b_mod).reshape(DEPTH, N_GRP, N_MOD, D)
    tables = _rope_tables()

    ks, vs, states = [], [], []
    for l in range(DEPTH):
        mod_l, g_l = mod[l], norm_g[l]
        x = _ffn(x, mod_l, g_l, ffn_w_in, ffn_w_out, l, 0)
        if l % 2 == 0:
            a = l // 2
            lam_init = 0.8 - 0.6 * math.exp(-0.3 * l)
            w_kt = attn_w_in[a][:, D:2 * D].T
            w_out = attn_w_out[a]
            subln = attn_subln[a].reshape(1, HD)
            grp = jnp.asarray(np.arange(D)[:, None] // DA_DK == np.arange(LANES)[None, :], jnp.bfloat16)
            qc, kt_new, vc_new, nqc, nkc = _attn_pre(x, mod_l, g_l, attn_w_in[a], w_kt, grp, tables, ctx=True)
            qs, kt_smp, v_smp, nqs, nks = _attn_pre(x, mod_l, g_l, attn_w_in[a], w_kt, grp, tables, ctx=False)
            kct = jnp.transpose(cache_k[:, a], (0, 2, 3, 4, 1)).reshape(DEC_BATCH, D, PAST)
            vcache = cache_v[:, a].reshape(DEC_BATCH, PAST, D)
            lam_p = attn_lambda[a]

            def score_bound_ok(nq, nk, n_seq, extra_k2=None):
                q2 = jnp.max(nq[:, 0, :N_QK_GROUP].reshape(n_seq, -1, N_QK_GROUP), axis=1)
                k2 = jnp.max(nk[:, :, 0].reshape(n_seq, -1, N_QK_GROUP), axis=1)
                if extra_k2 is not None:
                    k2 = jnp.maximum(k2, extra_k2)
                return jnp.max(q2 * k2) <= MAX_UNSHIFTED_LOG2_SCORE ** 2

            def attn_ctx(shifted):
                return lambda: _attn(x, mod_l, g_l, qc, kt_new, vc_new, None, None, lam_p, subln, w_out,
                                     ctx=True, lam_init=lam_init, shifted=shifted)

            def attn_smp(shifted):
                return lambda: _attn(x, mod_l, g_l, qs, kt_smp, v_smp, kct, vcache, lam_p, subln, w_out,
                                     ctx=False, lam_init=lam_init, shifted=shifted)

            cache_k2 = jnp.max(jnp.sum(jnp.square(cache_k[:, a]), axis=-1), axis=1).reshape(DEC_BATCH, N_QK_GROUP)
            xc = lax.cond(score_bound_ok(nqc, nkc, BATCH), attn_ctx(False), attn_ctx(True))
            xs = lax.cond(score_bound_ok(nqs, nks, DEC_BATCH, cache_k2), attn_smp(False), attn_smp(True))
            x = (xc, xs)
            ks.append(jnp.transpose(kt_new.reshape(BATCH, HEADS, 2, DA_DK, SEQ), (0, 4, 1, 2, 3)))
            vs.append(vc_new.reshape(BATCH, SEQ, HEADS, HD))
        else:
            r = l // 2
            q, i, gt, lf, chunk_sums = _hgrn_pre(x, mod_l, g_l, hgrn_w_in[r], hgrn_lower_bounds, l)
            s0 = state_hgrn[:, r].astype(jnp.float32)
            o_f, o_b, s_fin = _rec(q, i, lf, chunk_sums, s0)
            x = _hgrn_out(x, mod_l, g_l, o_f, o_b, gt, hgrn_gnorm[r].reshape(1, HD), hgrn_w_out[r])
            states.append(s_fin)
        if l < DEPTH - 1:
            x = _ffn(x, mod_l, g_l, ffn_w_in, ffn_w_out, l, 2)

    assert not isinstance(x, tuple)
    y_prompt, y_sample = _ffn(x, mod_l, g_l, ffn_w_in, ffn_w_out, DEPTH - 1, 2, split=True)
    return (y_prompt.reshape(BATCH, SEQ, D), y_sample.reshape(DEC_BATCH, DEC_SEQ, D),
            jnp.stack(ks, axis=1), jnp.stack(vs, axis=1), jnp.stack(states, axis=1))
```

```python
import functools
import math

import numpy as np
import jax
import jax.numpy as jnp
from jax import lax
from jax.experimental import pallas as pl
from jax.experimental.pallas import tpu as pltpu

D = 1024
BATCH = 16
SEQ = 256
DEPTH = 2
DEC_BATCH = 4
DEC_SEQ = 2048
PAST = 256
GRID_W = 64
N_MOD = 9
EPS = 1e-6
HEADS = 8
HD = 128
DA_DK = 64
ROPE_THETA = 10000.0
FFN_H = 2816
LOG2E = 1.4426950408889634

N_CTX = BATCH * SEQ
N_SMP = DEC_BATCH * DEC_SEQ
N_TOK = N_CTX + N_SMP
N_GRP = 8

VMEM_LIMIT = 56 * 1024 * 1024
LANES = 128
SUB = 8

TM = 512
FFN_HC = 256
TQ = 256
KB = 256
N_QK_GROUP = 2 * HEADS
MAX_UNSHIFTED_LOG2_SCORE = 60.0
CH = 128
N_SLAB = CH // SUB
N_LEVEL = 7
MAX_FAST_LOG2_DECAY = 200.0


def _cparams(sem):
    return pltpu.CompilerParams(dimension_semantics=sem, vmem_limit_bytes=VMEM_LIMIT)


def _grp_of_block(i, rows):
    nctx = N_CTX // rows
    per = DEC_SEQ // rows
    return jnp.where(i < nctx, 0, 1 + (i - nctx) // per)


def _rms(x, g):
    return (x * lax.rsqrt(jnp.mean(x * x, axis=-1, keepdims=True) + EPS)) * g


def _dot(a, b):
    return lax.dot_general(a, b, (((1,), (0,)), ((), ())), preferred_element_type=jnp.float32)


def _dot_nt(a, b):
    return lax.dot_general(a, b, (((1,), (1,)), ((), ())), preferred_element_type=jnp.float32)


def _dot_tn(a, b):
    return lax.dot_general(a, b, (((0,), (0,)), ((), ())), preferred_element_type=jnp.float32)


def _const_spec(shape):
    nd = len(shape)
    return pl.BlockSpec(shape, lambda *_: (0,) * nd, pipeline_mode=pl.Buffered(1))


def _mod_kernel(c_ref, w_ref, b_ref, o_ref):
    c = c_ref[...]
    s = (c * jax.nn.sigmoid(c)).astype(jnp.bfloat16)
    o_ref[...] = _dot(s, w_ref[...].astype(jnp.bfloat16)) + b_ref[...]


def _modulation(cc, w_mod, b_mod):
    tn = 1024
    nt = (N_MOD * D) // tn
    return pl.pallas_call(
        _mod_kernel,
        grid=(DEPTH, nt),
        in_specs=[
            pl.BlockSpec((N_GRP, D), lambda l, j: (0, 0)),
            pl.BlockSpec((None, D, tn), lambda l, j: (l, 0, j)),
            pl.BlockSpec((None, 1, tn), lambda l, j: (l, 0, j)),
        ],
        out_specs=pl.BlockSpec((None, N_GRP, tn), lambda l, j: (l, 0, j)),
        out_shape=jax.ShapeDtypeStruct((DEPTH, N_GRP, N_MOD * D), jnp.float32),
        compiler_params=_cparams(("parallel", "parallel")),
        name="modulation",
    )(cc, w_mod, b_mod.reshape(DEPTH, 1, N_MOD * D))


def _ada_pre(x, mod_ref, g_ref, j, gi):
    shift = mod_ref[3 * j:3 * j + 1, :]
    scale = mod_ref[3 * j + 1:3 * j + 2, :]
    return _rms(x, g_ref[gi:gi + 1, :]) * (1.0 + scale) + shift


def _ada_post(x, y, mod_ref, g_ref, j, gi, res_w):
    gate = mod_ref[3 * j + 2:3 * j + 3, :]
    return x + res_w * gate * _rms(y, g_ref[gi:gi + 1, :])


def _tok_specs():
    return [
        pl.BlockSpec((TM, D), lambda i: (i, 0)),
        pl.BlockSpec((None, N_MOD, D), lambda i: (_grp_of_block(i, TM), 0, 0)),
        _const_spec((6, D)),
    ]


def _ffn_kernel(*refs, j, pair, split):
    refs = list(refs)
    acc_ref = refs.pop()
    out_refs = [refs.pop() for _ in range(2 if split else 1)][::-1]
    is_ctx = pl.program_id(0) < N_CTX // TM
    if pair:
        xa_ref, xb_ref, mod_ref, g_ref, win_ref, wout_ref = refs
        x = jnp.where(is_ctx, xa_ref[...], xb_ref[...])
    else:
        x_ref, mod_ref, g_ref, win_ref, wout_ref = refs
        x = x_ref[...]
    h = _ada_pre(x, mod_ref, g_ref, j, 2 * j).astype(jnp.bfloat16)
    for c in range(FFN_H // FFN_HC):
        lo = c * FFN_HC
        a = _dot(h, win_ref[:, lo:lo + FFN_HC])
        b = _dot(h, win_ref[:, FFN_H + lo:FFN_H + lo + FFN_HC])
        u = ((a * jax.nn.sigmoid(a)) * b).astype(jnp.bfloat16)
        part = _dot(u, wout_ref[lo:lo + FFN_HC, :])
        if c == 0:
            acc_ref[...] = part
        else:
            acc_ref[...] += part
    res = _ada_post(x, acc_ref[...], mod_ref, g_ref, j, 2 * j + 1, 0.5)
    if split:
        @pl.when(is_ctx)
        def _():
            out_refs[0][...] = res

        @pl.when(jnp.logical_not(is_ctx))
        def _():
            out_refs[1][...] = res
    else:
        out_refs[0][...] = res


def _ffn(x, mod_l, g_l, w_in, w_out, l, j, *, split=False):
    pair = isinstance(x, tuple)
    nctx = N_CTX // TM
    rows = N_TOK
    w_idx = (l, j // 2, 0, 0)
    if split:
        out_specs = [pl.BlockSpec((TM, D), lambda i: (jnp.minimum(i, nctx - 1), 0)),
                     pl.BlockSpec((TM, D), lambda i: (jnp.maximum(i - nctx, 0), 0))]
        out_shape = [jax.ShapeDtypeStruct((N_CTX, D), jnp.float32),
                     jax.ShapeDtypeStruct((N_SMP, D), jnp.float32)]
    else:
        out_specs = pl.BlockSpec((TM, D), lambda i: (i, 0))
        out_shape = jax.ShapeDtypeStruct((rows, D), jnp.float32)
    if pair:
        x_specs = [pl.BlockSpec((TM, D), lambda i: (jnp.minimum(i, nctx - 1), 0)),
                   pl.BlockSpec((TM, D), lambda i: (jnp.maximum(i - nctx, 0), 0))]
        x_args = list(x)
    else:
        x_specs = [pl.BlockSpec((TM, D), lambda i: (i, 0))]
        x_args = [x]
    return pl.pallas_call(
        functools.partial(_ffn_kernel, j=j, pair=pair, split=split),
        grid=(rows // TM,),
        in_specs=x_specs + [
            pl.BlockSpec((None, N_MOD, D), lambda i: (_grp_of_block(i, TM), 0, 0)),
            _const_spec((6, D)),
            pl.BlockSpec((None, None, D, 2 * FFN_H), lambda i: w_idx, pipeline_mode=pl.Buffered(1)),
            pl.BlockSpec((None, None, FFN_H, D), lambda i: w_idx, pipeline_mode=pl.Buffered(1))],
        out_specs=out_specs,
        out_shape=out_shape,
        scratch_shapes=[pltpu.VMEM((TM, D), jnp.float32)],
        compiler_params=_cparams(("arbitrary" if split else "parallel",)),
        name=f"ffn{j}",
    )(*x_args, mod_l, g_l, w_in, w_out)


def _rope_lanes(x, cos, sin_signed):
    lane = lax.broadcasted_iota(jnp.int32, x.shape, 1)
    partner = jnp.where((lane % 32) < 16, pltpu.roll(x, LANES - 16, 1), pltpu.roll(x, 16, 1))
    return x * cos + partner * sin_signed


def _rope_rows(x, cos_t, sin_signed_t):
    parts = []
    for g in range(HD // 32):
        parts += [x[g * 32 + 16:g * 32 + 32, :], x[g * 32:g * 32 + 16, :]]
    return x * cos_t + jnp.concatenate(parts, axis=0) * sin_signed_t


def _attn_pre_kernel(x_ref, mod_ref, g_ref, w_ref, wkt_ref, grp_ref, cos_ref, sin_ref, cost_ref, sint_ref,
                     q_ref, kt_ref, v_ref, nq_ref, nk_ref, q2_ref, k2_ref, *, rope):
    x = x_ref[...]
    h = _ada_pre(x, mod_ref, g_ref, 1, 2).astype(jnp.bfloat16)
    qscale = DA_DK ** -0.5 * LOG2E
    zq = _dot(h, w_ref[:, 0:D])
    zkt = _dot_nt(wkt_ref[...], h)
    v_ref[...] = _dot(h, w_ref[:, 2 * D:3 * D]).astype(v_ref.dtype)
    for hd in range(HEADS):
        sl = slice(hd * HD, (hd + 1) * HD)
        q = zq[:, sl]
        k = zkt[sl, :]
        if rope:
            q = _rope_lanes(q, cos_ref[...], sin_ref[...])
            k = _rope_rows(k, cost_ref[...], sint_ref[...])
        q = q * qscale
        q_ref[:, sl] = q.astype(q_ref.dtype)
        kt_ref[sl, :] = k.astype(kt_ref.dtype)
        q2_ref[:, sl] = (q * q).astype(jnp.bfloat16)
        k2_ref[sl, :] = (k * k).astype(jnp.bfloat16)
    qn = jnp.max(_dot(q2_ref[...], grp_ref[...]), axis=0, keepdims=True)
    kn = jnp.max(_dot_tn(grp_ref[...], k2_ref[...]), axis=1, keepdims=True)
    nq_ref[...] = jnp.broadcast_to(qn, nq_ref.shape)
    nk_ref[...] = jnp.broadcast_to(kn[:N_QK_GROUP, :], nk_ref.shape)


def _attn_pre(x, mod_l, g_l, w_in, w_kt, grp, tables, *, ctx):
    nb, seq = (BATCH, SEQ) if ctx else (DEC_BATCH, DEC_SEQ)
    t = min(TM, seq)
    rows = nb * seq
    base = 0 if ctx else N_CTX // t
    per = seq // t
    ptab = DEC_SEQ // t
    kv_dtype = jnp.float32 if ctx else jnp.bfloat16
    out_spec = pl.BlockSpec((t, D), lambda i: (i, 0))
    return pl.pallas_call(
        functools.partial(_attn_pre_kernel, rope=not ctx),
        grid=(rows // t,),
        in_specs=[
            pl.BlockSpec((t, D), lambda i: (i + base, 0)),
            pl.BlockSpec((None, N_MOD, D), lambda i: (_grp_of_block(i + base, t), 0, 0)),
            _const_spec((6, D)), _const_spec((D, 3 * D)), _const_spec((D, D)), _const_spec((D, LANES)),
            pl.BlockSpec((t, HD), lambda i: (i % ptab, 0)),
            pl.BlockSpec((t, HD), lambda i: (i % ptab, 0)),
            pl.BlockSpec((HD, t), lambda i: (0, i % ptab)),
            pl.BlockSpec((HD, t), lambda i: (0, i % ptab)),
        ],
        out_specs=[out_spec, pl.BlockSpec((None, D, t), lambda i: (i // per, 0, i % per)), out_spec,
                   pl.BlockSpec((None, SUB, LANES), lambda i: (i, 0, 0)),
                   pl.BlockSpec((None, N_QK_GROUP, LANES), lambda i: (i, 0, 0))],
        out_shape=[
            jax.ShapeDtypeStruct((rows, D), jnp.bfloat16),
            jax.ShapeDtypeStruct((nb, D, seq), kv_dtype),
            jax.ShapeDtypeStruct((rows, D), kv_dtype),
            jax.ShapeDtypeStruct((rows // t, SUB, LANES), jnp.float32),
            jax.ShapeDtypeStruct((rows // t, N_QK_GROUP, LANES), jnp.float32),
        ],
        scratch_shapes=[pltpu.VMEM((t, D), jnp.bfloat16), pltpu.VMEM((D, t), jnp.bfloat16)],
        compiler_params=_cparams(("parallel",)),
        name="attn_pre_ctx" if ctx else "attn_pre_smp",
    )(x, mod_l, g_l, w_in, w_kt, grp, *tables)


def _attn_kernel(*refs, lam_init, cache, shifted):
    refs = list(refs)
    e_ref = refs.pop()
    s_ref = refs.pop() if shifted else None
    if cache:
        (x_ref, mod_ref, g_ref, q_ref, k_ref, v_ref, kc_ref, vc_ref,
         lam_ref, sub_ref, w_ref, o_ref, oh_ref) = refs
    else:
        (x_ref, mod_ref, g_ref, q_ref, k_ref, v_ref,
         lam_ref, sub_ref, w_ref, o_ref, oh_ref) = refs
    lp = lam_ref[...]
    lam = (jnp.exp(jnp.sum(lp[0:1] * lp[1:2], axis=-1, keepdims=True))
           - jnp.exp(jnp.sum(lp[2:3] * lp[3:4], axis=-1, keepdims=True)) + lam_init)
    tq = q_ref.shape[0]
    lane = lax.broadcasted_iota(jnp.int32, (tq, HD), 1)
    bf = jnp.bfloat16
    kbw = KB if shifted else min(2 * KB, k_ref.shape[1])
    n_own = k_ref.shape[1] // kbw
    n_kb = n_own + (1 if cache else 0)

    def kcols(kb):
        if kb < n_own:
            return slice(kb * kbw, (kb + 1) * kbw)
        return slice(n_own * kbw, n_own * kbw + PAST)

    def keys_t(sl, kb):
        if kb < n_own:
            return k_ref[sl, kcols(kb)].astype(bf)
        return kc_ref[sl, :].astype(bf)

    def values(sl, kb):
        if kb < n_own:
            return v_ref[kcols(kb), sl].astype(bf)
        return vc_ref[:, sl].astype(bf)

    def head_slice(hd):
        return slice(hd * HD, (hd + 1) * HD)

    def lane_tiles(x, op):
        parts = [x[:, j * LANES:(j + 1) * LANES] for j in range(x.shape[1] // LANES)]
        return functools.reduce(op, parts)

    def stacked_q(hd):
        qh = q_ref[:, head_slice(hd)]
        return jnp.concatenate([jnp.where(lane < DA_DK, qh, jnp.zeros_like(qh)),
                                jnp.where(lane >= DA_DK, qh, jnp.zeros_like(qh))], axis=0)

    st = [dict() for _ in range(HEADS)]
    lag = 1 if shifted else 0
    for slot in range(HEADS + lag + 1):
        h1, h2, h3 = slot, slot - lag, slot - lag - 1
        if h1 < HEADS:
            st[h1]["qq"] = stacked_q(h1)
        if 0 <= h3 < HEADS:
            tot = st[h3]["tot"]
            c_col = lam * tot[:tq] / tot[tq:]
            st[h3]["c"] = {w: jnp.broadcast_to(c_col, (tq, w)).astype(bf)
                           for w in {kbw, PAST if cache else kbw}}
        for kb in range(n_kb):
            if h1 < HEADS:
                s = _dot(st[h1]["qq"], keys_t(head_slice(h1), kb))
                if shifted:
                    s_ref[h1 % 2, :, kcols(kb)] = s
                    mp = lane_tiles(s, jnp.maximum)
                    st[h1]["m"] = mp if kb == 0 else jnp.maximum(st[h1]["m"], mp)
            if 0 <= h2 < HEADS:
                if shifted:
                    e = jnp.exp2(s_ref[h2 % 2, :, kcols(kb)] - st[h2]["mrow"])
                else:
                    e = jnp.exp2(s)
                ep = lane_tiles(e, lambda a, b: a + b)
                st[h2]["l"] = ep if kb == 0 else st[h2]["l"] + ep
                e_ref[h2 % 2, :, kcols(kb)] = e.astype(bf)
            if 0 <= h3 < HEADS:
                cols = kcols(kb)
                c = st[h3]["c"][cols.stop - cols.start]
                a = e_ref[h3 % 2, :tq, cols] - c * e_ref[h3 % 2, tq:, cols]
                part = _dot(a, values(head_slice(h3), kb))
                st[h3]["o"] = part if kb == 0 else st[h3]["o"] + part
        if shifted and h1 < HEADS:
            st[h1]["mrow"] = jnp.max(st[h1]["m"], axis=-1, keepdims=True)
        if 0 <= h2 < HEADS:
            st[h2]["tot"] = jnp.sum(st[h2]["l"], axis=-1, keepdims=True)
        if 0 <= h3 < HEADS:
            o = st[h3]["o"] * (1.0 / st[h3]["tot"][:tq])
            o = _rms(o, sub_ref[...]) * (1.0 - lam_init)
            oh_ref[:, head_slice(h3)] = o.astype(bf)
            st[h3].clear()
    y = _dot(oh_ref[...], w_ref[...])
    o_ref[...] = _ada_post(x_ref[...], y, mod_ref, g_ref, 1, 3, 1.0)


def _attn(x, mod_l, g_l, q, k, v, kc, vc, lam_p, subln, w_out, *, ctx, lam_init, shifted):
    nb, seq = (BATCH, SEQ) if ctx else (DEC_BATCH, DEC_SEQ)
    rows = nb * seq
    tq = min(TQ if shifted else 2 * TQ, seq)
    base = 0 if ctx else N_CTX // tq
    nq = seq // tq
    assert seq % KB == 0 and PAST == KB
    n_keys = seq if ctx else seq + PAST
    row_spec = pl.BlockSpec((tq, D), lambda b, i: (b * nq + i, 0))
    in_specs = [
        pl.BlockSpec((tq, D), lambda b, i: (base + b * nq + i, 0)),
        pl.BlockSpec((None, N_MOD, D), lambda b, i: (_grp_of_block(base + b * nq + i, tq), 0, 0)),
        pl.BlockSpec((6, D), lambda b, i: (0, 0)),
        row_spec,
        pl.BlockSpec((None, D, seq), lambda b, i: (b, 0, 0)),
        pl.BlockSpec((seq, D), lambda b, i: (b, 0)),
    ]
    args = [x, mod_l, g_l, q, k, v]
    if not ctx:
        in_specs += [pl.BlockSpec((None, D, PAST), lambda b, i: (b, 0, 0)),
                     pl.BlockSpec((None, PAST, D), lambda b, i: (b, 0, 0))]
        args += [kc, vc]
    in_specs += [
        pl.BlockSpec((4, DA_DK), lambda b, i: (0, 0)),
        pl.BlockSpec((1, HD), lambda b, i: (0, 0)),
        pl.BlockSpec((D, D), lambda b, i: (0, 0), pipeline_mode=pl.Buffered(1)),
    ]
    args += [lam_p, subln, w_out]
    return pl.pallas_call(
        functools.partial(_attn_kernel, lam_init=lam_init, cache=not ctx, shifted=shifted),
        grid=(nb, nq),
        in_specs=in_specs,
        out_specs=row_spec,
        out_shape=jax.ShapeDtypeStruct((rows, D), jnp.float32),
        scratch_shapes=[pltpu.VMEM((tq, D), jnp.bfloat16)]
        + ([pltpu.VMEM((2, 2 * tq, n_keys), jnp.float32)] if shifted else [])
        + [pltpu.VMEM((2, 2 * tq, n_keys), jnp.bfloat16)],
        compiler_params=_cparams(("parallel", "parallel")),
        name=("attn_ctx" if ctx else "attn_smp") + ("_shifted" if shifted else ""),
    )(*args)


def _hgrn_pre_kernel(x_ref, mod_ref, g_ref, w_ref, lb_ref, q_ref, i_ref, gt_ref, lf_ref, cs_ref, *, layer):
    x = x_ref[...]
    h = _ada_pre(x, mod_ref, g_ref, 1, 2).astype(jnp.bfloat16)

    def proj(n):
        return _dot(h, w_ref[:, n * D:(n + 1) * D])

    def log_gate(d, ff):
        raw = [lb_ref[l, d:d + 1, :] for l in range(DEPTH)]
        m = functools.reduce(jnp.maximum, raw)
        ex = [jnp.exp(r - m) for r in raw]
        tot = functools.reduce(lambda a, b: a + b, ex)
        soft = [e / tot for e in ex]
        lb = functools.reduce(lambda a, b: a + b, soft[:layer + 1]) - soft[0]
        return jnp.log(lb + (1.0 - lb) * jax.nn.sigmoid(ff))

    ff = proj(1)
    q_ref[...] = proj(0).astype(q_ref.dtype)
    lf_ref[0] = log_gate(0, ff)
    fb = proj(2)
    i_ref[...] = proj(3).astype(i_ref.dtype)
    lf_ref[1] = log_gate(1, fb)
    gt_ref[...] = proj(4).astype(gt_ref.dtype)
    for d in range(2):
        for c in range(x_ref.shape[0] // CH):
            cs_ref[c, d:d + 1, :] = jnp.sum(lf_ref[d, c * CH:(c + 1) * CH, :], axis=0, keepdims=True)


def _hgrn_pre(x, mod_l, g_l, w_in, lb_raw, layer):
    row = pl.BlockSpec((TM, D), lambda i: (i, 0))
    return pl.pallas_call(
        functools.partial(_hgrn_pre_kernel, layer=layer),
        grid=(N_TOK // TM,),
        in_specs=_tok_specs() + [_const_spec((D, 5 * D)), _const_spec((DEPTH, 2, D))],
        out_specs=[row, row, row, pl.BlockSpec((2, TM, D), lambda i: (0, i, 0)),
                   pl.BlockSpec((TM // CH, 2, D), lambda i: (i, 0, 0))],
        out_shape=[
            jax.ShapeDtypeStruct((N_TOK, D), jnp.bfloat16),
            jax.ShapeDtypeStruct((N_TOK, D), jnp.bfloat16),
            jax.ShapeDtypeStruct((N_TOK, D), jnp.bfloat16),
            jax.ShapeDtypeStruct((2, N_TOK, D), jnp.float32),
            jax.ShapeDtypeStruct((N_TOK // CH, 2, D), jnp.float32),
        ],
        compiler_params=_cparams(("parallel",)),
        name="hgrn_pre",
    )(x, mod_l, g_l, w_in, lb_raw)


def _level_masks(rev):
    t = np.arange(CH)[:, None]
    s = np.arange(CH)[None, :]
    out = []
    for p in range(N_LEVEL):
        same = (t >> (p + 1)) == (s >> (p + 1))
        tb, sb = (t >> p) & 1, (s >> p) & 1
        out.append(same & ((tb == 0) & (sb == 1) if rev else (tb == 1) & (sb == 0)))
    out.append(t == s)
    return np.stack(out).astype(np.float32)


def _row_bcast(x, r):
    return jnp.broadcast_to(x[r:r + 1, :], x.shape)


def _rec_cumsum(lf_ref, b_ref, br_ref, rev):
    sub = lax.broadcasted_iota(jnp.int32, (SUB, D), 0)
    order = range(N_SLAB - 1, -1, -1) if rev else range(N_SLAB)
    edge = 0 if rev else SUB - 1
    carry = None
    for v in order:
        rows = slice(v * SUB, (v + 1) * SUB)
        c = lf_ref[rows, :] * LOG2E
        for k in (1, 2, 4):
            if rev:
                c = c + jnp.where(sub < SUB - k, pltpu.roll(c, SUB - k, 0), 0.0)
            else:
                c = c + jnp.where(sub >= k, pltpu.roll(c, k, 0), 0.0)
        if carry is not None:
            c = c + carry
        carry = _row_bcast(c, edge)
        b_ref[rows, :] = c
        if br_ref is not None:
            br_ref[rows, :] = carry
    return carry


def _rec_fast(lf_ref, q_ref, i_ref, b_ref, b_last, qs_ref, ks_ref, st_ref, o_ref, rev):
    bf = jnp.bfloat16
    half = 0.5 * b_last
    half2 = jnp.concatenate([half, half], axis=0)
    for v2 in range(N_SLAB // 2):
        rows = slice(v2 * 2 * SUB, (v2 + 1) * 2 * SUB)
        k = 1.0 - jnp.exp2(lf_ref[rows, :] * LOG2E)
        d = b_ref[rows, :] - half2
        qs_ref[rows, :] = (q_ref[rows, :].astype(jnp.float32) * jnp.exp2(d)).astype(bf)
        ks_ref[rows, :] = (k * jnp.exp2(-d)).astype(bf)
    scale = jnp.exp2(half[0:1, :])
    decay = scale * scale
    t = lax.broadcasted_iota(jnp.int32, (CH, CH), 0)
    s = lax.broadcasted_iota(jnp.int32, (CH, CH), 1)
    seen = (s >= t) if rev else (s <= t)
    for hd in range(HEADS):
        sl = slice(hd * HD, (hd + 1) * HD)
        qs, ks, vh = qs_ref[:, sl], ks_ref[:, sl], i_ref[:, sl]
        a = jnp.where(seen, _dot_nt(qs, ks), 0.0)
        st = st_ref[hd]
        o = _dot(a.astype(bf), vh) + _dot_nt(qs, (st * scale[:, sl]).astype(bf))
        o_ref[:, sl] = o.astype(o_ref.dtype)
        st_ref[hd] = st * decay[:, sl] + _dot_tn(vh, ks) * scale[:, sl]


def _rec_prepare(lf_ref, q_ref, b_ref, br_ref, b_last, qt_ref, kt_ref, rev):
    bf = jnp.bfloat16
    sub = lax.broadcasted_iota(jnp.int32, (SUB, D), 0)
    zeros = jnp.zeros((SUB, D), jnp.float32)
    for v2 in range(N_SLAB // 2):
        rows = slice(v2 * 2 * SUB, (v2 + 1) * 2 * SUB)
        f = jnp.exp2(lf_ref[rows, :] * LOG2E)
        k = 1.0 - f
        q = q_ref[rows, :].astype(jnp.float32)
        b = b_ref[rows, :]
        qt_ref[0, rows, :] = (q * f).astype(bf)
        kt_ref[0, rows, :] = k.astype(bf)
        half = (slice(0, SUB), slice(SUB, 2 * SUB))
        for p in range(1, N_LEVEL):
            if p >= 3:
                w = p - 3
                es, qside = [], []
                for n, v in enumerate((2 * v2, 2 * v2 + 1)):
                    hi = v & ~((1 << (w + 1)) - 1)
                    vr = hi | (1 << w) if rev else hi | ((1 << w) - 1)
                    r = br_ref[vr * SUB:(vr + 1) * SUB, :]
                    qside.append(((v >> w) & 1) == (0 if rev else 1))
                    es.append(jnp.exp2(b[half[n], :] - r) if qside[n] else jnp.exp2(r - b[half[n], :]))
                if qside[0] == qside[1]:
                    e = jnp.concatenate(es, axis=0)
                    if qside[0]:
                        qt_ref[p, rows, :] = (q * e).astype(bf)
                    else:
                        kt_ref[p, rows, :] = (k * e).astype(bf)
                else:
                    qe = [q[half[n], :] * es[n] if qside[n] else zeros for n in range(2)]
                    ke = [zeros if qside[n] else k[half[n], :] * es[n] for n in range(2)]
                    qt_ref[p, rows, :] = jnp.concatenate(qe, axis=0).astype(bf)
                    kt_ref[p, rows, :] = jnp.concatenate(ke, axis=0).astype(bf)
            else:
                es = []
                for n in range(2):
                    bv = b[half[n], :]
                    if p == 2:
                        r = _row_bcast(bv, 4 if rev else 3)
                    else:
                        lo, hi_r = (2, 6) if rev else (1, 5)
                        r = jnp.where(sub < 4, _row_bcast(bv, lo), _row_bcast(bv, hi_r))
                    es.append(jnp.exp2(-jnp.abs(bv - r)))
                e = jnp.concatenate(es, axis=0)
                qt_ref[p, rows, :] = (q * e).astype(bf)
                kt_ref[p, rows, :] = (k * e).astype(bf)
        bl = jnp.concatenate([b_last, b_last], axis=0)
        qt_ref[N_LEVEL, rows, :] = (q * jnp.exp2(b)).astype(bf)
        kt_ref[N_LEVEL, rows, :] = (k * jnp.exp2(bl - b)).astype(bf)
    return jnp.exp2(b_last[0:1, :])


def _rec_heads(q_ref, i_ref, qt_ref, kt_ref, msk_ref, st_ref, o_ref, decay):
    bf = jnp.bfloat16
    top = N_LEVEL - 1
    for hd in range(HEADS):
        sl = slice(hd * HD, (hd + 1) * HD)
        a = _dot_nt(qt_ref[top, :, sl], kt_ref[top, :, sl])
        a = a + msk_ref[N_LEVEL] * _dot_nt(q_ref[:, sl], kt_ref[0, :, sl])
        for p in range(top):
            a = a + msk_ref[p] * _dot_nt(qt_ref[p, :, sl], kt_ref[p, :, sl])
        vh = i_ref[:, sl]
        st = st_ref[hd]
        o = _dot(a.astype(bf), vh) + _dot_nt(qt_ref[N_LEVEL, :, sl], st.astype(bf))
        o_ref[:, sl] = o.astype(o_ref.dtype)
        st_ref[hd] = st * decay[:, sl] + _dot_tn(vh, kt_ref[N_LEVEL, :, sl])


def _rec_kernel(blkf_ref, blkb_ref, sid_ref, first_ref, last_ref, fast_ref,
                qf_ref, if_ref, lff_ref, qb_ref, ib_ref, lfb_ref, s0_ref, mskf_ref, mskb_ref,
                of_ref, ob_ref, sfin_ref,
                st_ref, bf_ref, brf_ref, qtf_ref, ktf_ref, bb_ref, brb_ref, qtb_ref, ktb_ref):
    step = pl.program_id(0)

    @pl.when(step == 0)
    def _():
        for ref in (qtf_ref, ktf_ref, qtb_ref, ktb_ref):
            ref[...] = jnp.zeros_like(ref)

    @pl.when(first_ref[step] == 1)
    def _():
        st_ref[...] = jnp.zeros_like(st_ref)

    @pl.when(first_ref[step] == 2)
    def _():
        for d in range(2):
            for hd in range(HEADS):
                st_ref[d, hd] = s0_ref[d, hd].T

    moderate = fast_ref[step] == 1

    @pl.when(moderate)
    def _():
        last_f = _rec_cumsum(lff_ref, bf_ref, None, False)
        last_b = _rec_cumsum(lfb_ref, bb_ref, None, True)
        _rec_fast(lff_ref, qf_ref, if_ref, bf_ref, last_f, qtf_ref.at[N_LEVEL], ktf_ref.at[N_LEVEL],
                  st_ref.at[0], of_ref, False)
        _rec_fast(lfb_ref, qb_ref, ib_ref, bb_ref, last_b, qtb_ref.at[N_LEVEL], ktb_ref.at[N_LEVEL],
                  st_ref.at[1], ob_ref, True)

    @pl.when(jnp.logical_not(moderate))
    def _():
        last_f = _rec_cumsum(lff_ref, bf_ref, brf_ref, False)
        last_b = _rec_cumsum(lfb_ref, bb_ref, brb_ref, True)
        decay_f = _rec_prepare(lff_ref, qf_ref, bf_ref, brf_ref, last_f, qtf_ref, ktf_ref, False)
        decay_b = _rec_prepare(lfb_ref, qb_ref, bb_ref, brb_ref, last_b, qtb_ref, ktb_ref, True)
        _rec_heads(qf_ref, if_ref, qtf_ref, ktf_ref, mskf_ref, st_ref.at[0], of_ref, decay_f)
        _rec_heads(qb_ref, ib_ref, qtb_ref, ktb_ref, mskb_ref, st_ref.at[1], ob_ref, decay_b)

    @pl.when(last_ref[step] == 1)
    def _():
        for d in range(2):
            for hd in range(HEADS):
                sfin_ref[d, hd] = st_ref[d, hd].T


def _rec_tables():
    blkf, blkb, sid, first, last = [], [], [], [], []
    seqs = [(b * (SEQ // CH), SEQ // CH) for b in range(BATCH)]
    seqs += [(N_CTX // CH + b * (DEC_SEQ // CH), DEC_SEQ // CH) for b in range(DEC_BATCH)]
    for n, (start, nc) in enumerate(seqs):
        ctx = n < BATCH
        for c in range(nc):
            blkf.append(start + c)
            blkb.append(start + nc - 1 - c)
            sid.append(n)
            first.append((1 if ctx else 2) if c == 0 else 0)
            last.append(int(ctx and c == nc - 1))
    return [jnp.asarray(np.asarray(a, np.int32)) for a in (blkf, blkb, sid, first, last)]


def _rec(q, i, lf, chunk_sums, s0):
    tables = _rec_tables()
    low = jnp.min(chunk_sums, axis=-1) * LOG2E
    fast = (low[tables[0], 0] >= -MAX_FAST_LOG2_DECAY) & (low[tables[1], 1] >= -MAX_FAST_LOG2_DECAY)
    tables.append(fast.astype(jnp.int32))
    row_f = pl.BlockSpec((CH, D), lambda s, bkf, bkb, sid, fi, la, fa: (bkf[s], 0))
    row_b = pl.BlockSpec((CH, D), lambda s, bkf, bkb, sid, fi, la, fa: (bkb[s], 0))
    st_shape = (None, 2, HEADS, HD, HD)
    scratch = [pltpu.VMEM((2, HEADS, HD, HD), jnp.float32)]
    for _ in range(2):
        scratch += [
            pltpu.VMEM((CH, D), jnp.float32),
            pltpu.VMEM((CH, D), jnp.float32),
            pltpu.VMEM((N_LEVEL + 1, CH, D), jnp.bfloat16),
            pltpu.VMEM((N_LEVEL + 1, CH, D), jnp.bfloat16),
        ]
    msk_spec = pl.BlockSpec((N_LEVEL + 1, CH, CH), lambda s, *_: (0, 0, 0))
    grid_spec = pltpu.PrefetchScalarGridSpec(
        num_scalar_prefetch=6,
        grid=(N_TOK // CH,),
        in_specs=[
            row_f, row_f,
            pl.BlockSpec((None, CH, D), lambda s, bkf, bkb, sid, fi, la, fa: (0, bkf[s], 0)),
            row_b, row_b,
            pl.BlockSpec((None, CH, D), lambda s, bkf, bkb, sid, fi, la, fa: (1, bkb[s], 0)),
            pl.BlockSpec(st_shape, lambda s, bkf, bkb, sid, fi, la, fa: (jnp.maximum(sid[s] - BATCH, 0), 0, 0, 0, 0)),
            msk_spec, msk_spec,
        ],
        out_specs=[
            row_f, row_b,
            pl.BlockSpec(st_shape, lambda s, bkf, bkb, sid, fi, la, fa: (jnp.minimum(sid[s], BATCH - 1), 0, 0, 0, 0)),
        ],
        scratch_shapes=scratch,
    )
    return pl.pallas_call(
        _rec_kernel,
        grid_spec=grid_spec,
        out_shape=[
            jax.ShapeDtypeStruct((N_TOK, D), jnp.bfloat16),
            jax.ShapeDtypeStruct((N_TOK, D), jnp.bfloat16),
            jax.ShapeDtypeStruct((BATCH, 2, HEADS, HD, HD), jnp.float32),
        ],
        compiler_params=_cparams(("arbitrary",)),
        name="hgrn_rec",
    )(*tables, q, i, lf, q, i, lf, s0, jnp.asarray(_level_masks(False)), jnp.asarray(_level_masks(True)))


def _hgrn_out_kernel(x_ref, mod_ref, g_ref, of_ref, ob_ref, gt_ref, gn_ref, w_ref, o_ref, oh_ref):
    for hd in range(HEADS):
        sl = slice(hd * HD, (hd + 1) * HD)
        o = of_ref[:, sl].astype(jnp.float32) + ob_ref[:, sl].astype(jnp.float32)
        o = _rms(o, gn_ref[...]) * jax.nn.sigmoid(gt_ref[:, sl].astype(jnp.float32))
        oh_ref[:, sl] = o.astype(jnp.bfloat16)
    y = _dot(oh_ref[...], w_ref[...])
    o_ref[...] = _ada_post(x_ref[...], y, mod_ref, g_ref, 1, 3, 1.0)


def _hgrn_out(x, mod_l, g_l, o_f, o_b, gt, gnorm, w_out):
    row = pl.BlockSpec((TM, D), lambda i: (i, 0))
    return pl.pallas_call(
        _hgrn_out_kernel,
        grid=(N_TOK // TM,),
        in_specs=_tok_specs() + [row, row, row, _const_spec((1, HD)), _const_spec((D, D))],
        out_specs=row,
        out_shape=jax.ShapeDtypeStruct((N_TOK, D), jnp.float32),
        scratch_shapes=[pltpu.VMEM((TM, D), jnp.bfloat16)],
        compiler_params=_cparams(("parallel",)),
        name="hgrn_out",
    )(x, mod_l, g_l, o_f, o_b, gt, gnorm, w_out)


def _rope_tables():
    pos = np.arange(DEC_SEQ)
    row = (pos // GRID_W).astype(np.float64)
    col = (pos % GRID_W).astype(np.float64)
    nf = DA_DK // 4
    inv = ROPE_THETA ** (-np.arange(nf, dtype=np.float64) / nf)
    lane = np.arange(HD)
    axis = (lane % DA_DK) // (2 * nf)
    ang = np.where(axis[None, :] == 0, row[:, None], col[:, None]) * inv[lane % nf][None, :]
    sign = np.where((lane % (2 * nf)) < nf, -1.0, 1.0)
    cos, sin = np.cos(ang), np.sin(ang) * sign[None, :]
    return tuple(jnp.asarray(a, jnp.float32) for a in (cos, sin, cos.T, sin.T))


def kernel(x_prompt, x_sample, cache_k, cache_v, state_hgrn, c, c_ctx, w_mod, b_mod, norm_g,
           ffn_w_in, ffn_w_out, attn_w_in, attn_w_out, attn_lambda, attn_subln,
           hgrn_w_in, hgrn_w_out, hgrn_lower_bounds, hgrn_gnorm):
    x = (x_prompt.reshape(N_CTX, D), x_sample.reshape(N_SMP, D))
    cc = jnp.concatenate([c_ctx[None, :], c, jnp.zeros((N_GRP - 1 - DEC_BATCH, D), jnp.float32)], axis=0)
    mod = _modulation(cc, w_mod, b_mod).reshape(DEPTH, N_GRP, N_MOD, D)
    tables = _rope_tables()

    ks, vs, states = [], [], []
    for l in range(DEPTH):
        mod_l, g_l = mod[l], norm_g[l]
        x = _ffn(x, mod_l, g_l, ffn_w_in, ffn_w_out, l, 0)
        if l % 2 == 0:
            a = l // 2
            lam_init = 0.8 - 0.6 * math.exp(-0.3 * l)
            w_kt = attn_w_in[a][:, D:2 * D].T
            w_out = attn_w_out[a]
            subln = attn_subln[a].reshape(1, HD)
            grp = jnp.asarray(np.arange(D)[:, None] // DA_DK == np.arange(LANES)[None, :], jnp.bfloat16)
            qc, kt_new, vc_new, nqc, nkc = _attn_pre(x, mod_l, g_l, attn_w_in[a], w_kt, grp, tables, ctx=True)
            qs, kt_smp, v_smp, nqs, nks = _attn_pre(x, mod_l, g_l, attn_w_in[a], w_kt, grp, tables, ctx=False)
            kct = jnp.transpose(cache_k[:, a], (0, 2, 3, 4, 1)).reshape(DEC_BATCH, D, PAST)
            vcache = cache_v[:, a].reshape(DEC_BATCH, PAST, D)
            lam_p = attn_lambda[a]

            def score_bound_ok(nq, nk, n_seq, extra_k2=None):
                q2 = jnp.max(nq[:, 0, :N_QK_GROUP].reshape(n_seq, -1, N_QK_GROUP), axis=1)
                k2 = jnp.max(nk[:, :, 0].reshape(n_seq, -1, N_QK_GROUP), axis=1)
                if extra_k2 is not None:
                    k2 = jnp.maximum(k2, extra_k2)
                return jnp.max(q2 * k2) <= MAX_UNSHIFTED_LOG2_SCORE ** 2

            def attn_ctx(shifted):
                return lambda: _attn(x, mod_l, g_l, qc, kt_new, vc_new, None, None, lam_p, subln, w_out,
                                     ctx=True, lam_init=lam_init, shifted=shifted)

            def attn_smp(shifted):
                return lambda: _attn(x, mod_l, g_l, qs, kt_smp, v_smp, kct, vcache, lam_p, subln, w_out,
                                     ctx=False, lam_init=lam_init, shifted=shifted)

            cache_k2 = jnp.max(jnp.sum(jnp.square(cache_k[:, a]), axis=-1), axis=1).reshape(DEC_BATCH, N_QK_GROUP)
            xc = lax.cond(score_bound_ok(nqc, nkc, BATCH), attn_ctx(False), attn_ctx(True))
            xs = lax.cond(score_bound_ok(nqs, nks, DEC_BATCH, cache_k2), attn_smp(False), attn_smp(True))
            x = (xc, xs)
            ks.append(jnp.transpose(kt_new.reshape(BATCH, HEADS, 2, DA_DK, SEQ), (0, 4, 1, 2, 3)))
            vs.append(vc_new.reshape(BATCH, SEQ, HEADS, HD))
        else:
            r = l // 2
            q, i, gt, lf, chunk_sums = _hgrn_pre(x, mod_l, g_l, hgrn_w_in[r], hgrn_lower_bounds, l)
            s0 = state_hgrn[:, r].astype(jnp.float32)
            o_f, o_b, s_fin = _rec(q, i, lf, chunk_sums, s0)
            x = _hgrn_out(x, mod_l, g_l, o_f, o_b, gt, hgrn_gnorm[r].reshape(1, HD), hgrn_w_out[r])
            states.append(s_fin)
        if l < DEPTH - 1:
            x = _ffn(x, mod_l, g_l, ffn_w_in, ffn_w_out, l, 2)

    assert not isinstance(x, tuple)
    y_prompt, y_sample = _ffn(x, mod_l, g_l, ffn_w_in, ffn_w_out, DEPTH - 1, 2, split=True)
    return (y_prompt.reshape(BATCH, SEQ, D), y_sample.reshape(DEC_BATCH, DEC_SEQ, D),
            jnp.stack(ks, axis=1), jnp.stack(vs, axis=1), jnp.stack(states, axis=1))
```
